```python
import math
import jax
import jax.numpy as jnp
from jax import lax
import numpy as np

D_MODEL = 2048
BATCH = 2
SEQ = 4096
DEPTH = 2

CTX_LEN = 256
GRID_W = 64
EPS = 1e-6
ROPE_THETA = 10000.0
Q_BLOCK = 128

DN_HEADS = 8
DN_HEAD_DIM = 128
DN_CONV = 5
DN_CHUNK = 64
MLA_HEADS = 8
MLA_Q_RANK = 768
MLA_KV_RANK = 512
MLA_NOPE = 128
MLA_ROPE = 64
MLA_V = 128
GQA_HEADS = 32
GQA_KV_HEADS = 4
GQA_HEAD_DIM = 64
WINDOW = 128
D_FF = 7168
N_EXPERTS = 8
TOP_K = 2
MOE_BLOCK = 128

DN_QKV = 3 * DN_HEADS * DN_HEAD_DIM
DN_IN = DN_QKV + DN_HEADS * DN_HEAD_DIM + 4 * DN_HEADS
MLA_IN = MLA_Q_RANK + MLA_KV_RANK + MLA_ROPE
AB_IN = DN_IN + MLA_IN
AB_OUT = DN_HEADS * DN_HEAD_DIM + MLA_HEADS * MLA_V
GQA_Q = GQA_HEADS * GQA_HEAD_DIM
GQA_IN = GQA_Q + 2 * GQA_KV_HEADS * GQA_HEAD_DIM

kernel_name = 'hybrid_deltanet_mla_swa_moe_flow_block'


def rms_norm(x, g):
    x32 = x.astype(jnp.float32)
    y = x32 * lax.rsqrt(jnp.mean(x32 * x32, axis=-1, keepdims=True) + EPS)
    return (y * g.astype(jnp.float32)).astype(x.dtype)


def l2_normalize(x):
    x32 = x.astype(jnp.float32)
    return (x32 * lax.rsqrt(jnp.sum(x32 * x32, axis=-1, keepdims=True) + EPS)).astype(x.dtype)


def modulate(x, g, shift, scale):
    return rms_norm(x, g) * (1 + scale) + shift


def axial_rope(rows, rot_dim):
    row = jnp.repeat(jnp.arange(rows), GRID_W).astype(jnp.float32)
    col = jnp.tile(jnp.arange(GRID_W), rows).astype(jnp.float32)
    n_freq = rot_dim // 4
    inv_freq = ROPE_THETA ** (-jnp.arange(n_freq, dtype=jnp.float32) / n_freq)
    ang = jnp.concatenate([row[:, None] * inv_freq, col[:, None] * inv_freq], axis=-1)
    return jnp.cos(ang), jnp.sin(ang)


def apply_rope(x, cos, sin):
    half = x.shape[-1] // 2
    shape = (1, cos.shape[0]) + (1,) * (x.ndim - 3) + (half,)
    c = cos.reshape(shape).astype(x.dtype)
    s = sin.reshape(shape).astype(x.dtype)
    x1, x2 = x[..., :half], x[..., half:]
    return jnp.concatenate([x1 * c - x2 * s, x2 * c + x1 * s], axis=-1)


def short_conv(x, w):
    pad = DN_CONV // 2
    return lax.conv_general_dilated(x, w[:, None, :].astype(x.dtype), window_strides=(1,),
                                    padding=[(pad, pad)], dimension_numbers=('NWC', 'WIO', 'NWC'),
                                    feature_group_count=x.shape[-1])


def swiglu(h, w_gate, w_up, w_down):
    return jnp.dot(jax.nn.silu(jnp.dot(h, w_gate)) * jnp.dot(h, w_up), w_down)


def block_attention(q, k, v, scale, sink=None):
    B, T, Hk, G, Dq = q.shape
    n_keys = k.shape[1]
    qb = jnp.moveaxis(q.reshape(B, T // Q_BLOCK, Q_BLOCK, Hk, G, Dq), 1, 0)

    def one(qi):
        s = jnp.einsum('bqhgd,bshd->bhgqs', qi, k).astype(jnp.float32) * scale
        if sink is not None:
            s_sink = jnp.broadcast_to(sink.astype(jnp.float32)[None, :, :, None, None], s.shape[:-1] + (1,))
            s = jnp.concatenate([s, s_sink], axis=-1)
        p = jax.nn.softmax(s, axis=-1)[..., :n_keys].astype(v.dtype)
        return jnp.einsum('bhgqs,bshd->bqhgd', p, v)

    o = lax.map(one, qb)
    return jnp.moveaxis(o, 0, 1).reshape(B, T, Hk, G, v.shape[-1])


def window_sink_attention(q, k, v, k_ctx, v_ctx, sink, scale):
    B, T, Hk, G, D = q.shape
    nb = T // Q_BLOCK
    span = Q_BLOCK + 2 * WINDOW
    n_ctx = k_ctx.shape[1]
    kp = jnp.pad(k, ((0, 0), (WINDOW, WINDOW), (0, 0), (0, 0)))
    vp = jnp.pad(v, ((0, 0), (WINDOW, WINDOW), (0, 0), (0, 0)))
    qb = jnp.moveaxis(q.reshape(B, nb, Q_BLOCK, Hk, G, D), 1, 0)
    rel = jnp.arange(span)[None, :] - WINDOW - jnp.arange(Q_BLOCK)[:, None]
    band = jnp.abs(rel) <= WINDOW
    sink_logit = sink.astype(jnp.float32)[None, :, :, None, None]

    def one(args):
        i, qi = args
        start = i * Q_BLOCK
        ki = lax.dynamic_slice_in_dim(kp, start, span, axis=1)
        vi = lax.dynamic_slice_in_dim(vp, start, span, axis=1)
        kpos = start - WINDOW + jnp.arange(span)
        valid = band & ((kpos >= 0) & (kpos < T))[None, :]
        s_ctx = jnp.einsum('bqhgd,bshd->bhgqs', qi, k_ctx).astype(jnp.float32) * scale
        s_lat = jnp.einsum('bqhgd,bshd->bhgqs', qi, ki).astype(jnp.float32) * scale
        s_lat = jnp.where(valid, s_lat, -jnp.inf)
        s_sink = jnp.broadcast_to(sink_logit, s_ctx.shape[:-1] + (1,))
        p = jax.nn.softmax(jnp.concatenate([s_ctx, s_lat, s_sink], axis=-1), axis=-1).astype(v.dtype)
        return (jnp.einsum('bhgqs,bshd->bqhgd', p[..., :n_ctx], v_ctx)
                + jnp.einsum('bhgqs,bshd->bqhgd', p[..., n_ctx:n_ctx + span], vi))

    o = lax.map(one, (jnp.arange(nb), qb))
    return jnp.moveaxis(o, 0, 1).reshape(B, T, Hk, G, D)


def gated_delta_chunked(q, k, v, beta, g, s0):
    B, T, H, Dk = q.shape
    Dv = v.shape[-1]
    C = DN_CHUNK
    n = T // C

    def chunk(t):
        t = t.astype(jnp.float32).reshape((B, n, C) + t.shape[2:])
        return jnp.moveaxis(jnp.swapaxes(t, 2, 3), 1, 0)

    q, k, v, beta, g = chunk(q), chunk(k), chunk(v), chunk(beta), chunk(g)
    gc = jnp.cumsum(g, axis=-1)
    idx = jnp.arange(C)
    incl = idx[:, None] >= idx[None, :]
    strict = idx[:, None] > idx[None, :]
    gamma = jnp.exp(jnp.where(incl, gc[..., :, None] - gc[..., None, :], -jnp.inf))
    a_mat = jnp.where(strict, beta[..., :, None] * jnp.einsum('nbhik,nbhjk->nbhij', k, k) * gamma, 0.0)
    rhs = jnp.concatenate([v * beta[..., None], k * (beta * jnp.exp(gc))[..., None]], axis=-1)
    sol = lax.linalg.triangular_solve(a_mat + jnp.eye(C, dtype=jnp.float32), rhs,
                                      left_side=True, lower=True, unit_diagonal=True)
    u, w = sol[..., :Dv], sol[..., Dv:]
    qk = jnp.einsum('nbhik,nbhjk->nbhij', q, k) * gamma
    q_dec = q * jnp.exp(gc)[..., None]
    k_dec = k * jnp.exp(gc[..., -1:] - gc)[..., None]
    last = jnp.exp(gc[..., -1])[..., None, None]

    def step(s, xs):
        u_i, w_i, qk_i, qd_i, kd_i, l_i = xs
        v_new = u_i - jnp.einsum('bhck,bhkv->bhcv', w_i, s)
        o_i = jnp.einsum('bhck,bhkv->bhcv', qd_i, s) + jnp.einsum('bhij,bhjv->bhiv', qk_i, v_new)
        s = s * l_i + jnp.einsum('bhck,bhcv->bhkv', kd_i, v_new)
        return s, o_i

    s_fin, o = lax.scan(step, s0.astype(jnp.float32), (u, w, qk, q_dec, k_dec, last))
    o = jnp.swapaxes(jnp.moveaxis(o, 0, 1), 2, 3).reshape(B, T, H, Dv)
    return o, s_fin


def deltanet_inputs(p, conv_w, a_log, dt_bias):
    B, T, _ = p.shape
    H, Dh = DN_HEADS, DN_HEAD_DIM
    qkv = jax.nn.silu(short_conv(p[..., :DN_QKV], conv_w)).reshape(B, T, 3, H, Dh)
    q = l2_normalize(qkv[:, :, 0]) * (Dh ** -0.5)
    k = l2_normalize(qkv[:, :, 1])
    v = qkv[:, :, 2]
    z = p[..., DN_QKV:DN_QKV + H * Dh].reshape(B, T, H, Dh)
    ba = p[..., DN_QKV + H * Dh:].astype(jnp.float32).reshape(B, T, 2, 2, H)
    beta = jax.nn.sigmoid(ba[:, :, 0])
    g = -jnp.exp(a_log.astype(jnp.float32)) * jax.nn.softplus(ba[:, :, 1] + dt_bias.astype(jnp.float32))
    return q, k, v, z, beta, g


def bidir_deltanet(q, k, v, beta, g, s_fwd, s_bwd):
    rev = lambda t: jnp.flip(t, axis=1)
    o_f, s_fwd = gated_delta_chunked(q, k, v, beta[:, :, 0], g[:, :, 0], s_fwd)
    o_b, s_bwd = gated_delta_chunked(rev(q), rev(k), rev(v), rev(beta[:, :, 1]), rev(g[:, :, 1]), s_bwd)
    return o_f + rev(o_b), s_fwd, s_bwd


def deltanet_out(o, z, norm_g):
    B, T = z.shape[:2]
    y = rms_norm(o, norm_g) * jax.nn.silu(z.astype(jnp.float32))
    return y.astype(z.dtype).reshape(B, T, DN_HEADS * DN_HEAD_DIM)


def mla_project(p, q_norm_g, w_qb, kv_norm_g, w_kvb, cos, sin):
    B, T, _ = p.shape
    cq = rms_norm(p[..., :MLA_Q_RANK], q_norm_g)
    ckv = rms_norm(p[..., MLA_Q_RANK:MLA_Q_RANK + MLA_KV_RANK], kv_norm_g)
    k_rope = p[..., MLA_Q_RANK + MLA_KV_RANK:]
    q = jnp.dot(cq, w_qb).reshape(B, T, MLA_HEADS, MLA_NOPE + MLA_ROPE)
    kv = jnp.dot(ckv, w_kvb).reshape(B, T, MLA_HEADS, MLA_NOPE + MLA_V)
    q_nope, q_rope = q[..., :MLA_NOPE], q[..., MLA_NOPE:]
    if cos is not None:
        q_rope = apply_rope(q_rope, cos, sin)
        k_rope = apply_rope(k_rope, cos, sin)
    k_rope = jnp.broadcast_to(k_rope[:, :, None, :], (B, T, MLA_HEADS, MLA_ROPE))
    q = jnp.concatenate([q_nope, q_rope], axis=-1)
    k = jnp.concatenate([kv[..., :MLA_NOPE], k_rope], axis=-1)
    return q, k, kv[..., MLA_NOPE:]


def mixer_ab(u, uc, w_in, conv_w, a_log, dt_bias, dn_norm_g, q_norm_g, w_qb, kv_norm_g, w_kvb, w_out,
             cos, sin, need_ctx):
    B, T, _ = u.shape
    p, pc = jnp.dot(u, w_in), jnp.dot(uc, w_in)
    qa, ka, va, za, beta, g = deltanet_inputs(p[..., :DN_IN], conv_w, a_log, dt_bias)
    qac, kac, vac, zac, beta_c, g_c = deltanet_inputs(pc[..., :DN_IN], conv_w, a_log, dt_bias)
    s0 = jnp.zeros((B, DN_HEADS, DN_HEAD_DIM, DN_HEAD_DIM), jnp.float32)
    oac, s_fwd, s_bwd = bidir_deltanet(qac, kac, vac, beta_c, g_c, s0, s0)
    oa, _, _ = bidir_deltanet(qa, ka, va, beta, g, s_fwd, s_bwd)
    qb, kb, vb = mla_project(p[..., DN_IN:], q_norm_g, w_qb, kv_norm_g, w_kvb, cos, sin)
    qbc, kbc, vbc = mla_project(pc[..., DN_IN:], q_norm_g, w_qb, kv_norm_g, w_kvb, None, None)
    scale = (MLA_NOPE + MLA_ROPE) ** -0.5
    ob = block_attention(qb[:, :, :, None], jnp.concatenate([kbc, kb], axis=1),
                         jnp.concatenate([vbc, vb], axis=1), scale)
    y = jnp.dot(jnp.concatenate([deltanet_out(oa, za, dn_norm_g),
                                 ob.reshape(B, T, MLA_HEADS * MLA_V)], axis=-1), w_out)
    yc = None
    if need_ctx:
        obc = block_attention(qbc[:, :, :, None], kbc, vbc, scale)
        yc = jnp.dot(jnp.concatenate([deltanet_out(oac, zac, dn_norm_g),
                                      obc.reshape(B, uc.shape[1], MLA_HEADS * MLA_V)], axis=-1), w_out)
    return y, yc


def gqa_project(h, w_qkv, b_qkv):
    B, T, _ = h.shape
    groups = GQA_HEADS // GQA_KV_HEADS
    p = jnp.dot(h, w_qkv) + b_qkv
    q = p[..., :GQA_Q].reshape(B, T, GQA_KV_HEADS, groups, GQA_HEAD_DIM)
    kv = p[..., GQA_Q:].reshape(B, T, 2, GQA_KV_HEADS, GQA_HEAD_DIM)
    return q, kv[:, :, 0], kv[:, :, 1]


def mixer_c(u, uc, w_qkv, b_qkv, sink, w_out, b_out, cos, sin, need_ctx):
    B, T, _ = u.shape
    sink_r = sink.reshape(GQA_KV_HEADS, GQA_HEADS // GQA_KV_HEADS)
    scale = GQA_HEAD_DIM ** -0.5
    q, k, v = gqa_project(u, w_qkv, b_qkv)
    q, k = apply_rope(q, cos, sin), apply_rope(k, cos, sin)
    qc, kc, vc = gqa_project(uc, w_qkv, b_qkv)
    o = window_sink_attention(q, k, v, kc, vc, sink_r, scale)
    y = jnp.dot(o.reshape(B, T, GQA_Q), w_out) + b_out
    yc = None
    if need_ctx:
        oc = block_attention(qc, kc, vc, scale, sink_r)
        yc = jnp.dot(oc.reshape(B, uc.shape[1], GQA_Q), w_out) + b_out
    return y, yc


def moe_swiglu(h, w_router, w_gate, w_up, w_down):
    n_tok, d = h.shape
    logits = jnp.dot(h, w_router).astype(jnp.float32)
    top_logit, top_idx = lax.top_k(logits, TOP_K)
    top_w = jax.nn.softmax(top_logit, axis=-1)
    n_assign = n_tok * TOP_K
    flat_e = top_idx.reshape(-1).astype(jnp.int32)
    flat_tok = jnp.repeat(jnp.arange(n_tok, dtype=jnp.int32), TOP_K)
    flat_w = top_w.reshape(-1)
    order = jnp.argsort(flat_e)
    sorted_e = flat_e[order]
    counts = jnp.zeros((N_EXPERTS,), jnp.int32).at[flat_e].add(1)
    padded = (counts + MOE_BLOCK - 1) // MOE_BLOCK * MOE_BLOCK
    start = jnp.cumsum(counts) - counts
    pad_end = jnp.cumsum(padded)
    pad_start = pad_end - padded
    dest = pad_start[sorted_e] + jnp.arange(n_assign, dtype=jnp.int32) - start[sorted_e]
    n_blocks = -(-n_assign // MOE_BLOCK) + N_EXPERTS
    slots = n_blocks * MOE_BLOCK
    slot_tok = jnp.full((slots,), n_tok, jnp.int32).at[dest].set(flat_tok[order])
    slot_w = jnp.zeros((slots,), jnp.float32).at[dest].set(flat_w[order])
    block_e = jnp.searchsorted(pad_end, jnp.arange(n_blocks, dtype=jnp.int32) * MOE_BLOCK, side='right')
    block_e = jnp.minimum(block_e, N_EXPERTS - 1)
    h_pad = jnp.concatenate([h, jnp.zeros((1, d), h.dtype)], axis=0)
    xb = h_pad[slot_tok].reshape(n_blocks, MOE_BLOCK, d)

    def expert_block(args):
        xi, e = args
        return swiglu(xi, w_gate[e], w_up[e], w_down[e])

    yb = lax.map(expert_block, (xb, block_e)).reshape(slots, d)
    yb = yb * slot_w[:, None].astype(yb.dtype)
    return jnp.zeros((n_tok + 1, d), yb.dtype).at[slot_tok].add(yb)[:n_tok]


def setup_inputs(seed: int = 0) -> dict:
    key = jax.random.key(seed)
    ks = iter(jax.random.split(key, 40))
    f32 = jnp.float32
    D = D_MODEL
    ne, no = (DEPTH + 1) // 2, DEPTH // 2

    def nrm(shape, scale):
        return jax.random.normal(next(ks), shape, f32) * scale

    x = nrm((BATCH, SEQ, D), 1.0)
    c = nrm((BATCH, D), 1.0)
    ctx = nrm((BATCH, CTX_LEN, D), 1.0)
    c_ctx = nrm((D,), 1.0)
    ada_w = nrm((DEPTH, D, 6 * D), 0.5 * D ** -0.5)
    ada_b = nrm((DEPTH, 6 * D), 0.01)
    norm_g = 1.0 + nrm((DEPTH, 4, D), 0.05)
    ab_w_in = nrm((ne, D, AB_IN), D ** -0.5)
    dn_conv_w = nrm((ne, DN_CONV, DN_QKV), DN_CONV ** -0.5)
    dn_a_log = jnp.log(jax.random.uniform(next(ks), (ne, 2, DN_HEADS), f32, 1.0, 16.0))
    dt = jnp.exp(jax.random.uniform(next(ks), (ne, 2, DN_HEADS), f32, math.log(1e-3), math.log(1e-1)))
    dn_dt_bias = dt + jnp.log(-jnp.expm1(-dt))
    dn_norm_g = 1.0 + nrm((ne, DN_HEAD_DIM), 0.05)
    mla_q_norm_g = 1.0 + nrm((ne, MLA_Q_RANK), 0.05)
    mla_w_qb = nrm((ne, MLA_Q_RANK, MLA_HEADS * (MLA_NOPE + MLA_ROPE)), MLA_Q_RANK ** -0.5)
    mla_kv_norm_g = 1.0 + nrm((ne, MLA_KV_RANK), 0.05)
    mla_w_kvb = nrm((ne, MLA_KV_RANK, MLA_HEADS * (MLA_NOPE + MLA_V)), MLA_KV_RANK ** -0.5)
    ab_w_out = nrm((ne, AB_OUT, D), AB_OUT ** -0.5)
    ffn_w_gate = nrm((ne, D, D_FF), D ** -0.5)
    ffn_w_up = nrm((ne, D, D_FF), D ** -0.5)
    ffn_w_down = nrm((ne, D_FF, D), D_FF ** -0.5)
    gqa_w_qkv = nrm((no, D, GQA_IN), D ** -0.5)
    gqa_b_qkv = nrm((no, GQA_IN), 0.01)
    gqa_sink = nrm((no, GQA_HEADS), 1.0)
    gqa_w_out = nrm((no, GQA_Q, D), GQA_Q ** -0.5)
    gqa_b_out = nrm((no, D), 0.01)
    moe_w_router = nrm((no, D, N_EXPERTS), D ** -0.5)
    moe_w_gate = nrm((no, N_EXPERTS, D, D_FF), D ** -0.5)
    moe_w_up = nrm((no, N_EXPERTS, D, D_FF), D ** -0.5)
    moe_w_down = nrm((no, N_EXPERTS, D_FF, D), D_FF ** -0.5)
    return {'x': x, 'c': c, 'ctx': ctx, 'c_ctx': c_ctx, 'ada_w': ada_w, 'ada_b': ada_b, 'norm_g': norm_g,
            'ab_w_in': ab_w_in, 'dn_conv_w': dn_conv_w, 'dn_a_log': dn_a_log, 'dn_dt_bias': dn_dt_bias,
            'dn_norm_g': dn_norm_g, 'mla_q_norm_g': mla_q_norm_g, 'mla_w_qb': mla_w_qb,
            'mla_kv_norm_g': mla_kv_norm_g, 'mla_w_kvb': mla_w_kvb, 'ab_w_out': ab_w_out,
            'ffn_w_gate': ffn_w_gate, 'ffn_w_up': ffn_w_up, 'ffn_w_down': ffn_w_down,
            'gqa_w_qkv': gqa_w_qkv, 'gqa_b_qkv': gqa_b_qkv, 'gqa_sink': gqa_sink, 'gqa_w_out': gqa_w_out,
            'gqa_b_out': gqa_b_out, 'moe_w_router': moe_w_router, 'moe_w_gate': moe_w_gate,
            'moe_w_up': moe_w_up, 'moe_w_down': moe_w_down}


def reference(x, c, ctx, c_ctx, ada_w, ada_b, norm_g, ab_w_in, dn_conv_w, dn_a_log, dn_dt_bias, dn_norm_g,
              mla_q_norm_g, mla_w_qb, mla_kv_norm_g, mla_w_kvb, ab_w_out, ffn_w_gate, ffn_w_up, ffn_w_down,
              gqa_w_qkv, gqa_b_qkv, gqa_sink, gqa_w_out, gqa_b_out, moe_w_router, moe_w_gate, moe_w_up,
              moe_w_down):
    B, T, D = x.shape
    rows = T // GRID_W
    cos_b, sin_b = axial_rope(rows, MLA_ROPE)
    cos_c, sin_c = axial_rope(rows, GQA_HEAD_DIM)
    silu_c = jax.nn.silu(c)[:, None, :]
    silu_cc = jax.nn.silu(c_ctx)
    h, hc = x, ctx
    for layer in range(DEPTH):
        need_ctx = layer < DEPTH - 1
        i = layer // 2
        mod = jnp.split(jnp.dot(silu_c, ada_w[layer]) + ada_b[layer], 6, axis=-1)
        mod_c = jnp.split(jnp.dot(silu_cc, ada_w[layer]) + ada_b[layer], 6, axis=-1)
        g_pre_m, g_post_m, g_pre_f, g_post_f = norm_g[layer]
        u = modulate(h, g_pre_m, mod[0], mod[1])
        uc = modulate(hc, g_pre_m, mod_c[0], mod_c[1])
        if layer % 2 == 0:
            y, yc = mixer_ab(u, uc, ab_w_in[i], dn_conv_w[i], dn_a_log[i], dn_dt_bias[i], dn_norm_g[i],
                             mla_q_norm_g[i], mla_w_qb[i], mla_kv_norm_g[i], mla_w_kvb[i], ab_w_out[i],
                             cos_b, sin_b, need_ctx)
        else:
            y, yc = mixer_c(u, uc, gqa_w_qkv[i], gqa_b_qkv[i], gqa_sink[i], gqa_w_out[i], gqa_b_out[i],
                            cos_c, sin_c, need_ctx)
        h = h + mod[2] * rms_norm(y, g_post_m)
        u = modulate(h, g_pre_f, mod[3], mod[4])
        if need_ctx:
            hc = hc + mod_c[2] * rms_norm(yc, g_post_m)
            uc = modulate(hc, g_pre_f, mod_c[3], mod_c[4])
        if layer % 2 == 0:
            y = swiglu(u, ffn_w_gate[i], ffn_w_up[i], ffn_w_down[i])
            if need_ctx:
                yc = swiglu(uc, ffn_w_gate[i], ffn_w_up[i], ffn_w_down[i])
        else:
            tok = u.reshape(-1, D)
            if need_ctx:
                tok = jnp.concatenate([tok, uc.reshape(-1, D)], axis=0)
            y_all = moe_swiglu(tok, moe_w_router[i], moe_w_gate[i], moe_w_up[i], moe_w_down[i])
            y = y_all[:B * T].reshape(B, T, D)
            if need_ctx:
                yc = y_all[B * T:].reshape(hc.shape)
        h = h + mod[5] * rms_norm(y, g_post_f)
        if need_ctx:
            hc = hc + mod_c[5] * rms_norm(yc, g_post_f)
    return h
```

```python
import functools
import math

import jax
import jax.numpy as jnp
from jax import lax
from jax.experimental import pallas as pl
from jax.experimental.pallas import tpu as pltpu

F32 = jnp.float32
BF16 = jnp.bfloat16
I32 = jnp.int32

D_MODEL = 2048
BATCH = 2
SEQ = 4096
CTX_LEN = 256
GRID_W = 64
EPS = 1e-6
ROPE_THETA = 10000.0

DN_HEADS = 8
DN_HEAD_DIM = 128
DN_CONV = 5
DN_CHUNK = 64
MLA_HEADS = 8
MLA_Q_RANK = 768
MLA_KV_RANK = 512
MLA_NOPE = 128
MLA_ROPE = 64
MLA_V = 128
GQA_HEADS = 32
GQA_KV_HEADS = 4
GQA_HEAD_DIM = 64
WINDOW = 128
Q_BLOCK = 128
D_FF = 7168
N_EXPERTS = 8
TOP_K = 2

DN_W = DN_HEADS * DN_HEAD_DIM
DN_QKV = 3 * DN_W
N_LAT = BATCH * SEQ
N_CTX = BATCH * CTX_LEN
N_TOK = N_LAT + N_CTX
GQA_Q = GQA_HEADS * GQA_HEAD_DIM
GQA_KV = GQA_KV_HEADS * GQA_HEAD_DIM

P_QKV = 0
P_Z = DN_QKV
P_CKV = P_Z + DN_W
P_CQ = P_CKV + MLA_KV_RANK
P_MISC = P_CQ + MLA_Q_RANK
P_WIDTH = 5632

ROW_TILE = 512
MM_ROW_TILE = 1088
MOE_BLOCK = 512
MOE_SLOTS = N_LAT * TOP_K + N_EXPERTS * MOE_BLOCK
VMEM_LIMIT = 56 * 1024 * 1024

_SDS = jax.ShapeDtypeStruct
_HIGH = lax.Precision.HIGHEST


def _cparams(sem, vmem=None):
    return pltpu.CompilerParams(dimension_semantics=sem, vmem_limit_bytes=vmem)


def _dot(a, b):
    return jnp.dot(a, b, preferred_element_type=F32)


def _dot_nt(a, b):
    return lax.dot_general(a, b, (((1,), (1,)), ((), ())), preferred_element_type=F32)


def _dot_tn(a, b):
    return lax.dot_general(a, b, (((0,), (0,)), ((), ())), preferred_element_type=F32)


def _dot_hi(a, b):
    return jnp.dot(a, b, preferred_element_type=F32, precision=_HIGH)


def _silu(x):
    return x * jax.nn.sigmoid(x)


def _rms(x, g):
    return x * lax.rsqrt(jnp.mean(x * x, axis=-1, keepdims=True) + EPS) * g


def _group_of_row(r0):
    return (r0 >= SEQ).astype(I32) + (r0 >= 2 * SEQ).astype(I32)


def _mod_spec(layer, k):
    return pl.BlockSpec((1, 8, D_MODEL), lambda *_: (layer, 0, k))


def _mod_row(ref, grp):
    return ref[0, pl.ds(grp, 1), :]


def _ada_kernel(c_ref, w_ref, b_ref, o_ref):
    s = _silu(c_ref[...]).astype(BF16)
    o_ref[0] = _dot(s, w_ref[0].astype(BF16)) + b_ref[0]


def _ada(cvec, ada_w, ada_b):
    n_layers, _, n = ada_w.shape
    tn = 1024
    return pl.pallas_call(
        _ada_kernel,
        out_shape=_SDS((n_layers, 8, n), F32),
        grid=(n_layers, n // tn),
        in_specs=[pl.BlockSpec((8, D_MODEL), lambda l, j: (0, 0)),
                  pl.BlockSpec((1, D_MODEL, tn), lambda l, j: (l, 0, j)),
                  pl.BlockSpec((1, 1, tn), lambda l, j: (l, 0, j))],
        out_specs=pl.BlockSpec((1, 8, tn), lambda l, j: (l, 0, j)),
        compiler_params=_cparams(("parallel", "parallel")),
        name="ada_mod",
    )(cvec, ada_w, ada_b.reshape(n_layers, 1, n))


def _modin_kernel(h_ref, g_ref, sh_ref, sc_ref, u_ref):
    grp = _group_of_row(pl.program_id(0) * ROW_TILE)
    u = _rms(h_ref[...], g_ref[...]) * (1.0 + _mod_row(sc_ref, grp)) + _mod_row(sh_ref, grp)
    u_ref[...] = u.astype(u_ref.dtype)


def _modulate_in(h, g, mod, layer, k_shift, k_scale):
    n = h.shape[0]
    row = pl.BlockSpec((ROW_TILE, D_MODEL), lambda i: (i, 0))
    vec = pl.BlockSpec((1, D_MODEL), lambda i: (0, 0))
    return pl.pallas_call(
        _modin_kernel,
        out_shape=_SDS((n, D_MODEL), BF16),
        grid=(n // ROW_TILE,),
        in_specs=[row, vec, _mod_spec(layer, k_shift), _mod_spec(layer, k_scale)],
        out_specs=row,
        compiler_params=_cparams(("parallel",)),
        name="modulate_in",
    )(h, g.reshape(1, D_MODEL), mod, mod)


def _post_kernel(h_ref, y_ref, gp_ref, gate_ref, gn_ref, sh_ref, sc_ref, hn_ref, u_ref):
    grp = _group_of_row(pl.program_id(0) * ROW_TILE)
    hn = h_ref[...] + _mod_row(gate_ref, grp) * _rms(y_ref[...], gp_ref[...])
    hn_ref[...] = hn
    u = _rms(hn, gn_ref[...]) * (1.0 + _mod_row(sc_ref, grp)) + _mod_row(sh_ref, grp)
    u_ref[...] = u.astype(u_ref.dtype)


def _post(h, y, g_post, g_next, mod, gate_lk, shift_lk, scale_lk, u_dtype):
    n = y.shape[0]
    row = pl.BlockSpec((ROW_TILE, D_MODEL), lambda i: (i, 0))
    vec = pl.BlockSpec((1, D_MODEL), lambda i: (0, 0))
    return pl.pallas_call(
        _post_kernel,
        out_shape=(_SDS((n, D_MODEL), F32), _SDS((n, D_MODEL), u_dtype)),
        grid=(n // ROW_TILE,),
        in_specs=[row, row, vec, _mod_spec(*gate_lk), vec, _mod_spec(*shift_lk), _mod_spec(*scale_lk)],
        out_specs=(row, row),
        compiler_params=_cparams(("parallel",)),
        name="residual_post",
    )(h, y, g_post.reshape(1, D_MODEL), mod, g_next.reshape(1, D_MODEL), mod, mod)


def _mm_kernel(a_ref, w_ref, b_ref, o_ref):
    o_ref[...] = (_dot(a_ref[...], w_ref[...].astype(BF16)) + b_ref[...]).astype(o_ref.dtype)


def _matmul(a, w, bias, out_dtype, tm, tn):
    m, k = a.shape
    n = w.shape[1]
    if bias is None:
        bias = jnp.zeros((n,), F32)
    return pl.pallas_call(
        _mm_kernel,
        out_shape=_SDS((m, n), out_dtype),
        grid=(m // tm, n // tn),
        in_specs=[pl.BlockSpec((tm, k), lambda i, j: (i, 0)),
                  pl.BlockSpec((k, tn), lambda i, j: (0, j)),
                  pl.BlockSpec((1, tn), lambda i, j: (0, j))],
        out_specs=pl.BlockSpec((tm, tn), lambda i, j: (i, j)),
        compiler_params=_cparams(("parallel", "arbitrary"), VMEM_LIMIT),
        name="matmul",
    )(a, w, bias.reshape(1, n))


def _ffn_kernel(u_ref, wg_ref, wu_ref, wd_ref, o_ref):
    f = pl.program_id(1)
    u = u_ref[...]
    a = _dot(u, wg_ref[...].astype(BF16))
    b = _dot(u, wu_ref[...].astype(BF16))
    y = _dot((_silu(a) * b).astype(BF16), wd_ref[...].astype(BF16))

    @pl.when(f == 0)
    def _():
        o_ref[...] = y

    @pl.when(f > 0)
    def _():
        o_ref[...] += y


def _ffn(u, w_gate, w_up, w_down):
    m = u.shape[0]
    tm, tf = MM_ROW_TILE, 256
    return pl.pallas_call(
        _ffn_kernel,
        out_shape=_SDS((m, D_MODEL), F32),
        grid=(m // tm, D_FF // tf),
        in_specs=[pl.BlockSpec((tm, D_MODEL), lambda i, f: (i, 0)),
                  pl.BlockSpec((D_MODEL, tf), lambda i, f: (0, f)),
                  pl.BlockSpec((D_MODEL, tf), lambda i, f: (0, f)),
                  pl.BlockSpec((tf, D_MODEL), lambda i, f: (f, 0))],
        out_specs=pl.BlockSpec((tm, D_MODEL), lambda i, f: (i, 0)),
        compiler_params=_cparams(("parallel", "arbitrary"), VMEM_LIMIT),
        name="swiglu_ffn",
    )(u, w_gate, w_up, w_down)


CONV_TILE = 256
_SEQ_STARTS = (0, SEQ // CONV_TILE, 2 * SEQ // CONV_TILE, 2 * SEQ // CONV_TILE + 1)
_SEQ_LASTS = (SEQ // CONV_TILE - 1, 2 * SEQ // CONV_TILE - 1, 2 * SEQ // CONV_TILE, 2 * SEQ // CONV_TILE + 1)


def _conv_kernel(prev_ref, cur_ref, next_ref, w_ref, o_ref, buf):
    i = pl.program_id(0)
    j = pl.program_id(1)
    tm = CONV_TILE
    first = functools.reduce(jnp.logical_or, [i == s for s in _SEQ_STARTS])
    last = functools.reduce(jnp.logical_or, [i == s for s in _SEQ_LASTS])
    buf[0:8, :] = jnp.where(first, 0.0, prev_ref[...])
    buf[8:8 + tm, :] = cur_ref[...]
    buf[8 + tm:16 + tm, :] = jnp.where(last, 0.0, next_ref[...])
    pad = DN_CONV // 2
    acc = buf[8 - pad:8 - pad + tm, :] * w_ref[0:1, :]
    for d in range(1, DN_CONV):
        acc = acc + buf[8 - pad + d:8 - pad + d + tm, :] * w_ref[d:d + 1, :]
    x = _silu(acc)
    q_scale = jnp.where(j == 0, DN_HEAD_DIM ** -0.5, 1.0)
    for hh in range(DN_HEADS):
        xh = x[:, hh * DN_HEAD_DIM:(hh + 1) * DN_HEAD_DIM]
        inv = lax.rsqrt(jnp.sum(xh * xh, axis=-1, keepdims=True) + EPS) * q_scale
        o_ref[:, hh * DN_HEAD_DIM:(hh + 1) * DN_HEAD_DIM] = xh * jnp.where(j == 2, 1.0, inv)


def _dn_conv(p, conv_w):
    n = p.shape[0]
    tm = CONV_TILE
    r8 = tm // 8
    nb8 = n // 8
    return pl.pallas_call(
        _conv_kernel,
        out_shape=_SDS((n, DN_QKV), F32),
        grid=(n // tm, 3),
        in_specs=[pl.BlockSpec((8, DN_W), lambda i, j: (jnp.maximum(i * r8 - 1, 0), j)),
                  pl.BlockSpec((tm, DN_W), lambda i, j: (i, j)),
                  pl.BlockSpec((8, DN_W), lambda i, j: (jnp.minimum((i + 1) * r8, nb8 - 1), j)),
                  pl.BlockSpec((DN_CONV, DN_W), lambda i, j: (0, j))],
        out_specs=pl.BlockSpec((tm, DN_W), lambda i, j: (i, j)),
        scratch_shapes=[pltpu.VMEM((tm + 16, DN_W), F32)],
        compiler_params=_cparams(("parallel", "parallel")),
        name="dn_conv",
    )(p, p, p, conv_w)


def _gates_kernel(m_ref, a_ref, dt_ref, o_ref):
    x = pltpu.roll(m_ref[...], 64, 1)
    lane = lax.broadcasted_iota(I32, x.shape, 1)
    z = x + dt_ref[...]
    softplus = jnp.maximum(z, 0.0) + jnp.log1p(jnp.exp(-jnp.abs(z)))
    g = a_ref[...] * softplus
    o_ref[...] = jnp.where(lane < 2 * DN_HEADS, jax.nn.sigmoid(x), jnp.where(lane < 4 * DN_HEADS, g, 0.0))


def _dn_gates(p, a_log, dt_bias):
    n = p.shape[0]
    neg_a = jnp.zeros((1, 128), F32).at[0, 2 * DN_HEADS:4 * DN_HEADS].set(-jnp.exp(a_log.astype(F32)).reshape(-1))
    dtb = jnp.zeros((1, 128), F32).at[0, 2 * DN_HEADS:4 * DN_HEADS].set(dt_bias.astype(F32).reshape(-1))
    tm = ROW_TILE
    vec = pl.BlockSpec((1, 128), lambda i: (0, 0))
    return pl.pallas_call(
        _gates_kernel,
        out_shape=_SDS((n, 128), F32),
        grid=(n // tm,),
        in_specs=[pl.BlockSpec((tm, 128), lambda i: (i, P_MISC // 128)), vec, vec],
        out_specs=pl.BlockSpec((tm, 128), lambda i: (i, 0)),
        compiler_params=_cparams(("parallel",)),
        name="dn_gates",
    )(p, neg_a, dtb)


_CTX_CHUNKS = CTX_LEN // DN_CHUNK
_LAT_CHUNKS = SEQ // DN_CHUNK
_DN_STEPS = _CTX_CHUNKS + _LAT_CHUNKS


def _dn_fwd_block(b, t):
    return jnp.where(t < _CTX_CHUNKS, (2 * SEQ + b * CTX_LEN) // DN_CHUNK + t, b * _LAT_CHUNKS + (t - _CTX_CHUNKS))


def _dn_bwd_block(b, t):
    return jnp.where(t < _CTX_CHUNKS, (2 * SEQ + b * CTX_LEN) // DN_CHUNK + (_CTX_CHUNKS - 1 - t),
                     b * _LAT_CHUNKS + (_DN_STEPS - 1 - t))


def _dn_chunk(q, k, v, beta_c, gc_c, gc_r, g_last, incl, strict, eye, s_ref):
    gamma = jnp.where(incl, jnp.exp(jnp.where(incl, gc_c - gc_r, 0.0)), 0.0)
    kb = k.astype(BF16)
    a = jnp.where(strict, beta_c * _dot_nt(kb, kb) * gamma, 0.0)
    tinv = eye - a
    pw = a
    for _ in range(int(math.log2(DN_CHUNK)) - 1):
        pw = _dot_hi(pw, pw)
        tinv = tinv + _dot_hi(tinv, pw)
    egc = jnp.exp(gc_c)
    u = _dot_hi(tinv, v * beta_c)
    w = _dot_hi(tinv, k * (beta_c * egc))
    qk = _dot_nt(q.astype(BF16), kb) * gamma
    s = s_ref[...]
    sb = s.astype(BF16)
    v_new = u - _dot(w.astype(BF16), sb)
    vb = v_new.astype(BF16)
    o = _dot((q * egc).astype(BF16), sb) + _dot(qk.astype(BF16), vb)
    k_dec = (k * jnp.exp(g_last - gc_c)).astype(BF16)
    s_ref[...] = s * jnp.exp(g_last) + _dot_tn(k_dec, vb)
    return o


def _dn_kernel(qf_ref, kf_ref, vf_ref, gf_ref, qb_ref, kb_ref, vb_ref, gb_ref, of_ref, ob_ref, s_ref):
    t = pl.program_id(1)

    @pl.when(t == 0)
    def _():
        s_ref[...] = jnp.zeros_like(s_ref)

    c = DN_CHUNK
    ri = lax.broadcasted_iota(I32, (c, c), 0)
    ci = lax.broadcasted_iota(I32, (c, c), 1)
    eye = (ri == ci).astype(F32)
    for d, (q_ref, k_ref, v_ref, g_ref, o_ref) in enumerate(
            ((qf_ref, kf_ref, vf_ref, gf_ref, of_ref), (qb_ref, kb_ref, vb_ref, gb_ref, ob_ref))):
        incl = (ri >= ci) if d == 0 else (ri <= ci)
        strict = (ri > ci) if d == 0 else (ri < ci)
        gates = g_ref[...]
        gc = _dot_hi(incl.astype(F32), gates)
        gct = gc.T
        last_row = c - 1 if d == 0 else 0
        for hh in range(DN_HEADS):
            lane_b = d * DN_HEADS + hh
            lane_g = 2 * DN_HEADS + lane_b
            sl = slice(hh * DN_HEAD_DIM, (hh + 1) * DN_HEAD_DIM)
            o = _dn_chunk(q_ref[:, sl], k_ref[:, sl], v_ref[:, sl],
                          gates[:, lane_b:lane_b + 1], gc[:, lane_g:lane_g + 1], gct[lane_g:lane_g + 1, :],
                          gc[last_row:last_row + 1, lane_g:lane_g + 1], incl, strict, eye, s_ref.at[d, hh])
            o_ref[:, sl] = o


def _deltanet(qkv, gates):
    n = qkv.shape[0]
    c = DN_CHUNK

    def spec(block_fn, width, col):
        return pl.BlockSpec((c, width), lambda b, t: (block_fn(b, t), col))

    ins = []
    for fn in (_dn_fwd_block, _dn_bwd_block):
        ins += [spec(fn, DN_W, 0), spec(fn, DN_W, 1), spec(fn, DN_W, 2), spec(fn, 128, 0)]
    return pl.pallas_call(
        _dn_kernel,
        out_shape=(_SDS((n, DN_W), F32), _SDS((n, DN_W), F32)),
        grid=(BATCH, _DN_STEPS),
        in_specs=ins,
        out_specs=(spec(_dn_fwd_block, DN_W, 0), spec(_dn_bwd_block, DN_W, 0)),
        scratch_shapes=[pltpu.VMEM((2, DN_HEADS, DN_HEAD_DIM, DN_HEAD_DIM), F32)],
        compiler_params=_cparams(("parallel", "arbitrary")),
        name="deltanet",
    )(qkv, qkv, qkv, gates, qkv, qkv, qkv, gates)


def _rope_tables():
    t = jnp.arange(SEQ)
    row = (t // GRID_W).astype(F32)
    col = (t % GRID_W).astype(F32)
    n_freq = MLA_ROPE // 4
    inv_freq = ROPE_THETA ** (-jnp.arange(n_freq, dtype=F32) / n_freq)
    ang = jnp.concatenate([row[:, None] * inv_freq, col[:, None] * inv_freq], axis=-1)
    cos, sin = jnp.cos(ang), jnp.sin(ang)
    cos64 = jnp.concatenate([cos, cos], axis=-1)
    sin64 = jnp.concatenate([-sin, sin], axis=-1)

    def assemble(lat_cos, lat_sin):
        c = jnp.concatenate([jnp.tile(lat_cos, (BATCH, 1)), jnp.ones((N_CTX, 128), F32)], axis=0)
        s = jnp.concatenate([jnp.tile(lat_sin, (BATCH, 1)), jnp.zeros((N_CTX, 128), F32)], axis=0)
        return c, s

    cos_l = jnp.concatenate([cos64, jnp.ones((SEQ, 64), F32)], axis=-1)
    sin_l = jnp.concatenate([sin64, jnp.zeros((SEQ, 64), F32)], axis=-1)
    one_head = assemble(cos_l, sin_l)
    two_heads = assemble(jnp.tile(cos64, (1, 2)), jnp.tile(sin64, (1, 2)))
    return one_head, two_heads


def _rope128(x, cosv, sinv):
    lane = lax.broadcasted_iota(I32, x.shape, 1)
    swapped = jnp.where((lane % 64) < 32, pltpu.roll(x, 96, 1), pltpu.roll(x, 32, 1))
    return x * cosv + swapped * sinv


MLA_QK = 256


def _mla_proj_kernel(ckv_ref, cq_ref, misc_ref, gq_ref, gkv_ref, wq_ref, wkv_ref, cos_ref, sin_ref,
                     q_ref, k_ref, v_ref):
    cq = _rms(cq_ref[...], gq_ref[...]).astype(BF16)
    ckv = _rms(ckv_ref[...], gkv_ref[...]).astype(BF16)
    qf = _dot(cq, wq_ref[...])
    kvf = _dot(ckv, wkv_ref[...])
    cosv, sinv = cos_ref[...], sin_ref[...]
    lane = lax.broadcasted_iota(I32, cosv.shape, 1)
    k_rope = _rope128(jnp.where(lane < MLA_ROPE, misc_ref[...], 0.0), cosv, sinv).astype(BF16)
    scale = (MLA_NOPE + MLA_ROPE) ** -0.5
    for hh in range(MLA_HEADS):
        lo, mid, hi = hh * MLA_QK, hh * MLA_QK + 128, (hh + 1) * MLA_QK
        q_ref[:, lo:mid] = (qf[:, lo:mid] * scale).astype(BF16)
        q_ref[:, mid:hi] = (_rope128(qf[:, mid:hi], cosv, sinv) * scale).astype(BF16)
        k_ref[:, lo:mid] = kvf[:, lo:mid].astype(BF16)
        k_ref[:, mid:hi] = k_rope
        v_ref[:, hh * MLA_V:(hh + 1) * MLA_V] = kvf[:, mid:hi].astype(BF16)


def _mla_proj(p, q_norm_g, kv_norm_g, w_q, w_kv, cosv, sinv):
    n = p.shape[0]
    tm = ROW_TILE
    wide = MLA_HEADS * MLA_QK

    def rows(width, col):
        return pl.BlockSpec((tm, width), lambda i: (i, col))

    def whole(shape):
        return pl.BlockSpec(shape, lambda i: (0, 0))

    return pl.pallas_call(
        _mla_proj_kernel,
        out_shape=(_SDS((n, wide), BF16), _SDS((n, wide), BF16), _SDS((n, MLA_HEADS * MLA_V), BF16)),
        grid=(n // tm,),
        in_specs=[rows(MLA_KV_RANK, P_CKV // MLA_KV_RANK), rows(MLA_Q_RANK, P_CQ // MLA_Q_RANK),
                  rows(128, P_MISC // 128), whole((1, MLA_Q_RANK)), whole((1, MLA_KV_RANK)),
                  whole((MLA_Q_RANK, wide)), whole((MLA_KV_RANK, wide)), rows(128, 0), rows(128, 0)],
        out_specs=(rows(wide, 0), rows(wide, 0), rows(MLA_HEADS * MLA_V, 0)),
        compiler_params=_cparams(("parallel",), VMEM_LIMIT),
        name="mla_proj",
    )(p, p, p, q_norm_g.reshape(1, -1), kv_norm_g.reshape(1, -1), w_q, w_kv, cosv, sinv)


def _attn_kernel(*refs, n_seg):
    q_ref, o_ref = refs[0], refs[-1]
    k_refs, v_refs = refs[1:1 + n_seg], refs[1 + n_seg:1 + 2 * n_seg]
    q = q_ref[...]
    s = [_dot_nt(q, k[...]) for k in k_refs]
    m = functools.reduce(jnp.maximum, [jnp.max(si, axis=-1, keepdims=True) for si in s])
    p = [jnp.exp(si - m) for si in s]
    den = functools.reduce(jnp.add, [jnp.sum(pi, axis=-1, keepdims=True) for pi in p])
    o = functools.reduce(jnp.add, [_dot(pi.astype(BF16), v[...]) for pi, v in zip(p, v_refs)])
    o_ref[...] = (o / den).astype(o_ref.dtype)


def _mla_attention(q, k, v):
    tq = 512
    nq = SEQ // tq
    ctx_blk = 2 * SEQ // CTX_LEN
    lat = pl.pallas_call(
        functools.partial(_attn_kernel, n_seg=2),
        out_shape=_SDS((N_LAT, MLA_HEADS * MLA_V), BF16),
        grid=(BATCH, MLA_HEADS, nq),
        in_specs=[pl.BlockSpec((tq, MLA_QK), lambda b, h, i: (b * nq + i, h)),
                  pl.BlockSpec((CTX_LEN, MLA_QK), lambda b, h, i: (ctx_blk + b, h)),
                  pl.BlockSpec((SEQ, MLA_QK), lambda b, h, i: (b, h)),
                  pl.BlockSpec((CTX_LEN, MLA_V), lambda b, h, i: (ctx_blk + b, h)),
                  pl.BlockSpec((SEQ, MLA_V), lambda b, h, i: (b, h))],
        out_specs=pl.BlockSpec((tq, MLA_V), lambda b, h, i: (b * nq + i, h)),
        compiler_params=_cparams(("parallel", "parallel", "arbitrary"), VMEM_LIMIT),
        name="mla_attn_latent",
    )(q, k, k, v, v)
    ctx = pl.pallas_call(
        functools.partial(_attn_kernel, n_seg=1),
        out_shape=_SDS((N_CTX, MLA_HEADS * MLA_V), BF16),
        grid=(BATCH, MLA_HEADS),
        in_specs=[pl.BlockSpec((CTX_LEN, MLA_QK), lambda b, h: (ctx_blk + b, h)),
                  pl.BlockSpec((CTX_LEN, MLA_QK), lambda b, h: (ctx_blk + b, h)),
                  pl.BlockSpec((CTX_LEN, MLA_V), lambda b, h: (ctx_blk + b, h))],
        out_specs=pl.BlockSpec((CTX_LEN, MLA_V), lambda b, h: (b, h)),
        compiler_params=_cparams(("parallel", "parallel")),
        name="mla_attn_context",
    )(q, k, v)
    return jnp.concatenate([lat, ctx], axis=0)


def _ab_out_kernel(of_ref, ob_ref, z_ref, g_ref, mla_ref, w1_ref, w2_ref, o_ref, a_ref):
    @pl.when(pl.program_id(1) == 0)
    def _():
        for hh in range(DN_HEADS):
            sl = slice(hh * DN_HEAD_DIM, (hh + 1) * DN_HEAD_DIM)
            a_ref[:, sl] = (_rms(of_ref[:, sl] + ob_ref[:, sl], g_ref[...]) * _silu(z_ref[:, sl])).astype(BF16)

    o_ref[...] = _dot(a_ref[...], w1_ref[...].astype(BF16)) + _dot(mla_ref[...], w2_ref[...].astype(BF16))


def _ab_out(o_f, o_b, p, dn_norm_g, mla_o, w_out):
    n = o_f.shape[0]
    tm, tn = 544, 512
    half = DN_W
    return pl.pallas_call(
        _ab_out_kernel,
        out_shape=_SDS((n, D_MODEL), F32),
        grid=(n // tm, D_MODEL // tn),
        in_specs=[pl.BlockSpec((tm, half), lambda i, j: (i, 0)),
                  pl.BlockSpec((tm, half), lambda i, j: (i, 0)),
                  pl.BlockSpec((tm, half), lambda i, j: (i, P_Z // half)),
                  pl.BlockSpec((1, DN_HEAD_DIM), lambda i, j: (0, 0)),
                  pl.BlockSpec((tm, half), lambda i, j: (i, 0)),
                  pl.BlockSpec((half, tn), lambda i, j: (0, j)),
                  pl.BlockSpec((half, tn), lambda i, j: (1, j))],
        out_specs=pl.BlockSpec((tm, tn), lambda i, j: (i, j)),
        scratch_shapes=[pltpu.VMEM((tm, half), BF16)],
        compiler_params=_cparams(("parallel", "arbitrary"), VMEM_LIMIT),
        name="ab_out_proj",
    )(o_f, o_b, p, dn_norm_g.reshape(1, -1), mla_o, w_out, w_out)


def _mm_rope_kernel(a_ref, w_ref, b_ref, cos_ref, sin_ref, o_ref, *, tn, rope_cols):
    y = _dot(a_ref[...], w_ref[...].astype(BF16)) + b_ref[...]
    col0 = pl.program_id(1) * tn
    cosv, sinv = cos_ref[...], sin_ref[...]
    for s in range(tn // 128):
        ys = y[:, s * 128:(s + 1) * 128]
        roped = _rope128(ys, cosv, sinv)
        o_ref[:, s * 128:(s + 1) * 128] = jnp.where(col0 + s * 128 < rope_cols, roped, ys).astype(o_ref.dtype)


def _gqa_qkv(u, w, bias, cosv, sinv):
    m, k = u.shape
    n = w.shape[1]
    tm, tn = MM_ROW_TILE, 512
    return pl.pallas_call(
        functools.partial(_mm_rope_kernel, tn=tn, rope_cols=GQA_Q + GQA_KV),
        out_shape=_SDS((m, n), BF16),
        grid=(m // tm, n // tn),
        in_specs=[pl.BlockSpec((tm, k), lambda i, j: (i, 0)),
                  pl.BlockSpec((k, tn), lambda i, j: (0, j)),
                  pl.BlockSpec((1, tn), lambda i, j: (0, j)),
                  pl.BlockSpec((tm, 128), lambda i, j: (i, 0)),
                  pl.BlockSpec((tm, 128), lambda i, j: (i, 0))],
        out_specs=pl.BlockSpec((tm, tn), lambda i, j: (i, j)),
        compiler_params=_cparams(("parallel", "arbitrary"), VMEM_LIMIT),
        name="gqa_qkv_rope",
    )(u, w, bias.reshape(1, n), cosv, sinv)


def _gqa_kernel(sink_ref, q_ref, kp_ref, kc_ref, kn_ref, vp_ref, vc_ref, vn_ref, kx_ref, vx_ref, o_ref):
    i = pl.program_id(1)
    nb = SEQ // Q_BLOCK
    span = 3 * Q_BLOCK
    qi = lax.broadcasted_iota(I32, (Q_BLOCK, span), 0)
    kj = lax.broadcasted_iota(I32, (Q_BLOCK, span), 1)
    rel = kj - Q_BLOCK - qi
    valid = (jnp.abs(rel) <= WINDOW) & ((kj >= Q_BLOCK) | (i > 0)) & ((kj < 2 * Q_BLOCK) | (i < nb - 1))
    groups = GQA_HEADS // GQA_KV_HEADS
    dh = GQA_HEAD_DIM
    scale = dh ** -0.5
    for kvh in range(GQA_KV_HEADS):
        ksl = slice(kvh * dh, (kvh + 1) * dh)
        ks = jnp.concatenate([kp_ref[:, ksl], kc_ref[:, ksl], kn_ref[:, ksl]], axis=0)
        vs = jnp.concatenate([vp_ref[:, ksl], vc_ref[:, ksl], vn_ref[:, ksl]], axis=0)
        kx, vx = kx_ref[:, ksl], vx_ref[:, ksl]
        for g in range(groups):
            hidx = kvh * groups + g
            qsl = slice(hidx * dh, (hidx + 1) * dh)
            qh = q_ref[:, qsl]
            s_x = _dot_nt(qh, kx) * scale
            s_l = jnp.where(valid, _dot_nt(qh, ks) * scale, -jnp.inf)
            sink = sink_ref[hidx]
            m = jnp.maximum(jnp.maximum(jnp.max(s_x, axis=-1, keepdims=True), jnp.max(s_l, axis=-1, keepdims=True)), sink)
            p_x, p_l = jnp.exp(s_x - m), jnp.exp(s_l - m)
            den = jnp.sum(p_x, axis=-1, keepdims=True) + jnp.sum(p_l, axis=-1, keepdims=True) + jnp.exp(sink - m)
            o = _dot(p_x.astype(BF16), vx) + _dot(p_l.astype(BF16), vs)
            o_ref[:, qsl] = (o / den).astype(o_ref.dtype)


def _gqa_attention(qkv, sink):
    nb = SEQ // Q_BLOCK
    kcol, vcol = GQA_Q // GQA_KV, GQA_Q // GQA_KV + 1
    ctx_blk = 2 * SEQ // CTX_LEN

    def win(col, shift):
        return pl.BlockSpec((Q_BLOCK, GQA_KV), lambda b, i: (b * nb + jnp.clip(i + shift, 0, nb - 1), col))

    return pl.pallas_call(
        _gqa_kernel,
        out_shape=_SDS((N_LAT, GQA_Q), BF16),
        grid=(BATCH, nb),
        in_specs=[pl.BlockSpec(memory_space=pltpu.SMEM),
                  pl.BlockSpec((Q_BLOCK, GQA_Q), lambda b, i: (b * nb + i, 0)),
                  win(kcol, -1), win(kcol, 0), win(kcol, 1), win(vcol, -1), win(vcol, 0), win(vcol, 1),
                  pl.BlockSpec((CTX_LEN, GQA_KV), lambda b, i: (ctx_blk + b, kcol)),
                  pl.BlockSpec((CTX_LEN, GQA_KV), lambda b, i: (ctx_blk + b, vcol))],
        out_specs=pl.BlockSpec((Q_BLOCK, GQA_Q), lambda b, i: (b * nb + i, 0)),
        compiler_params=_cparams(("parallel", "arbitrary")),
        name="gqa_window_attn",
    )(sink.astype(F32), qkv, qkv, qkv, qkv, qkv, qkv, qkv, qkv, qkv)


def _router_kernel(u_ref, w_ref, idx_ref, wt_ref):
    logits = _dot(u_ref[...].astype(BF16), w_ref[...])
    lane = lax.broadcasted_iota(I32, logits.shape, 1)
    neg = -jnp.inf
    l0 = jnp.where(lane < N_EXPERTS, logits, neg)
    m1 = jnp.max(l0, axis=-1, keepdims=True)
    i1 = jnp.min(jnp.where(l0 == m1, lane, 128), axis=-1, keepdims=True)
    l1 = jnp.where(lane == i1, neg, l0)
    m2 = jnp.max(l1, axis=-1, keepdims=True)
    i2 = jnp.min(jnp.where(l1 == m2, lane, 128), axis=-1, keepdims=True)
    e = jnp.exp(m2 - m1)
    w1 = 1.0 / (1.0 + e)
    idx_ref[...] = jnp.where(lane == 0, i1, jnp.where(lane == 1, i2, 0))
    wt_ref[...] = jnp.where(lane == 0, w1, jnp.where(lane == 1, e * w1, 0.0))


def _router(u, w_router):
    n = u.shape[0]
    tm = ROW_TILE
    w = jnp.zeros((D_MODEL, 128), BF16).at[:, :N_EXPERTS].set(w_router.astype(BF16))
    return pl.pallas_call(
        _router_kernel,
        out_shape=(_SDS((n, 128), I32), _SDS((n, 128), F32)),
        grid=(n // tm,),
        in_specs=[pl.BlockSpec((tm, D_MODEL), lambda i: (i, 0)), pl.BlockSpec((D_MODEL, 128), lambda i: (0, 0))],
        out_specs=(pl.BlockSpec((tm, 128), lambda i: (i, 0)), pl.BlockSpec((tm, 128), lambda i: (i, 0))),
        compiler_params=_cparams(("parallel",)),
        name="moe_router",
    )(u, w)


GATHER_TILE = 256


def _row_copy(src_hbm, row, dst, r, sem):
    return pltpu.make_async_copy(src_hbm.at[pl.ds(row, 1), :], dst.at[pl.ds(r, 1), :], sem)


def _gather_kernel(tok_ref, u_hbm, valid_ref, o_ref, buf, sem):
    base = pl.program_id(0) * GATHER_TILE

    def issue(r, carry):
        _row_copy(u_hbm, tok_ref[base + r], buf, r, sem).start()
        return carry

    def drain(r, carry):
        _row_copy(u_hbm, 0, buf, r, sem).wait()
        return carry

    lax.fori_loop(0, GATHER_TILE, issue, 0)
    lax.fori_loop(0, GATHER_TILE, drain, 0)
    o_ref[...] = jnp.where(valid_ref[...] > 0.0, buf[...], 0.0).astype(o_ref.dtype)


def _moe_gather(slot_tok, slot_valid, u):
    n_slots = slot_tok.shape[0]
    return pl.pallas_call(
        _gather_kernel,
        out_shape=_SDS((n_slots, D_MODEL), BF16),
        grid_spec=pltpu.PrefetchScalarGridSpec(
            num_scalar_prefetch=1,
            grid=(n_slots // GATHER_TILE,),
            in_specs=[pl.BlockSpec(memory_space=pl.ANY),
                      pl.BlockSpec((GATHER_TILE, 1), lambda i, tok: (i, 0))],
            out_specs=pl.BlockSpec((GATHER_TILE, D_MODEL), lambda i, tok: (i, 0)),
            scratch_shapes=[pltpu.VMEM((GATHER_TILE, D_MODEL), F32), pltpu.SemaphoreType.DMA(())]),
        compiler_params=_cparams(("arbitrary",)),
        name="moe_gather",
    )(slot_tok, u, slot_valid.reshape(n_slots, 1))


def _moe_ffn_kernel(be_ref, bv_ref, x_ref, wg_ref, wu_ref, wd_ref, o_ref):
    i, f = pl.program_id(0), pl.program_id(1)
    valid = bv_ref[i] > 0

    @pl.when(valid)
    def _():
        x = x_ref[...]
        a = _dot(x, wg_ref[0].astype(BF16))
        b = _dot(x, wu_ref[0].astype(BF16))
        y = _dot((_silu(a) * b).astype(BF16), wd_ref[0].astype(BF16))

        @pl.when(f == 0)
        def _():
            o_ref[...] = y

        @pl.when(f > 0)
        def _():
            o_ref[...] += y

    @pl.when(jnp.logical_not(valid) & (f == 0))
    def _():
        o_ref[...] = jnp.zeros_like(o_ref)


def _moe_ffn(block_expert, block_valid, xs, w_gate, w_up, w_down):
    n_slots = xs.shape[0]
    tm, tf = MOE_BLOCK, 256
    nf = D_FF // tf

    def fidx(i, f, be, bv):
        return jnp.where(bv[i] > 0, f, nf - 1)

    return pl.pallas_call(
        _moe_ffn_kernel,
        out_shape=_SDS((n_slots, D_MODEL), F32),
        grid_spec=pltpu.PrefetchScalarGridSpec(
            num_scalar_prefetch=2,
            grid=(n_slots // tm, nf),
            in_specs=[pl.BlockSpec((tm, D_MODEL), lambda i, f, be, bv: (i, 0)),
                      pl.BlockSpec((1, D_MODEL, tf), lambda i, f, be, bv: (be[i], 0, fidx(i, f, be, bv))),
                      pl.BlockSpec((1, D_MODEL, tf), lambda i, f, be, bv: (be[i], 0, fidx(i, f, be, bv))),
                      pl.BlockSpec((1, tf, D_MODEL), lambda i, f, be, bv: (be[i], fidx(i, f, be, bv), 0))],
            out_specs=pl.BlockSpec((tm, D_MODEL), lambda i, f, be, bv: (i, 0))),
        compiler_params=_cparams(("arbitrary", "arbitrary"), VMEM_LIMIT),
        name="moe_grouped_swiglu",
    )(block_expert, block_valid, xs, w_gate, w_up, w_down)


def _combine_kernel(pos_ref, y_hbm, wt_ref, h_ref, gp_ref, gate_ref, o_ref, buf_a, buf_b, sem):
    base = pl.program_id(0) * GATHER_TILE

    def issue(r, carry):
        _row_copy(y_hbm, pos_ref[2 * (base + r)], buf_a, r, sem).start()
        _row_copy(y_hbm, pos_ref[2 * (base + r) + 1], buf_b, r, sem).start()
        return carry

    def drain(r, carry):
        _row_copy(y_hbm, 0, buf_a, r, sem).wait()
        _row_copy(y_hbm, 0, buf_b, r, sem).wait()
        return carry

    lax.fori_loop(0, GATHER_TILE, issue, 0)
    lax.fori_loop(0, GATHER_TILE, drain, 0)
    wt = wt_ref[...]
    y = wt[:, 0:1] * buf_a[...] + wt[:, 1:2] * buf_b[...]
    grp = _group_of_row(base)
    o_ref[...] = h_ref[...] + _mod_row(gate_ref, grp) * _rms(y, gp_ref[...])


def _moe_combine(pos, ys, wts, h, g_post, mod, gate_lk):
    n = wts.shape[0]
    tm = GATHER_TILE
    return pl.pallas_call(
        _combine_kernel,
        out_shape=_SDS((n, D_MODEL), F32),
        grid_spec=pltpu.PrefetchScalarGridSpec(
            num_scalar_prefetch=1,
            grid=(n // tm,),
            in_specs=[pl.BlockSpec(memory_space=pl.ANY),
                      pl.BlockSpec((tm, 128), lambda i, pos: (i, 0)),
                      pl.BlockSpec((tm, D_MODEL), lambda i, pos: (i, 0)),
                      pl.BlockSpec((1, D_MODEL), lambda i, pos: (0, 0)),
                      pl.BlockSpec((1, 8, D_MODEL), lambda i, pos: (gate_lk[0], 0, gate_lk[1]))],
            out_specs=pl.BlockSpec((tm, D_MODEL), lambda i, pos: (i, 0)),
            scratch_shapes=[pltpu.VMEM((tm, D_MODEL), F32), pltpu.VMEM((tm, D_MODEL), F32),
                            pltpu.SemaphoreType.DMA(())]),
        compiler_params=_cparams(("arbitrary",)),
        name="moe_combine",
    )(pos, ys, wts, h, g_post.reshape(1, D_MODEL), mod)


def _moe_routing(idx):
    flat_e = idx[:, :TOP_K].reshape(-1)
    n_assign = flat_e.shape[0]
    onehot = (flat_e[:, None] == jnp.arange(N_EXPERTS, dtype=I32)[None, :]).astype(I32)
    csum = jnp.cumsum(onehot, axis=0)
    counts = csum[-1]
    padded = (counts + MOE_BLOCK - 1) // MOE_BLOCK * MOE_BLOCK
    pad_end = jnp.cumsum(padded)
    pad_start = pad_end - padded
    pos = jnp.sum(onehot * (pad_start[None, :] + csum - 1), axis=1).astype(I32)
    slot_tok = jnp.zeros((MOE_SLOTS,), I32).at[pos].set(jnp.arange(n_assign, dtype=I32) // TOP_K)
    slot_valid = jnp.zeros((MOE_SLOTS,), F32).at[pos].set(1.0)
    blk_start = jnp.arange(MOE_SLOTS // MOE_BLOCK, dtype=I32) * MOE_BLOCK
    blk_valid = (blk_start < pad_end[-1]).astype(I32)
    last_start = jnp.maximum(pad_end[-1] - MOE_BLOCK, 0)
    blk_e = jnp.searchsorted(pad_end, jnp.minimum(blk_start, last_start), side='right').astype(I32)
    blk_e = jnp.minimum(blk_e, N_EXPERTS - 1)
    return pos, slot_tok, slot_valid, blk_e, blk_valid


def _rearrange_w_in(w_in):
    dn_in = DN_QKV + DN_W + 4 * DN_HEADS
    qkvz = w_in[:, :DN_QKV + DN_W]
    ba = w_in[:, DN_QKV + DN_W:dn_in]
    cq = w_in[:, dn_in:dn_in + MLA_Q_RANK]
    ckv = w_in[:, dn_in + MLA_Q_RANK:dn_in + MLA_Q_RANK + MLA_KV_RANK]
    k_rope = w_in[:, dn_in + MLA_Q_RANK + MLA_KV_RANK:]
    pad = jnp.zeros((D_MODEL, P_WIDTH - P_MISC - MLA_ROPE - 4 * DN_HEADS), w_in.dtype)
    return jnp.concatenate([qkvz, ckv, cq, k_rope, ba, pad], axis=1).astype(BF16)


def _rearrange_w_qb(w_qb):
    w = w_qb.reshape(MLA_Q_RANK, MLA_HEADS, MLA_NOPE + MLA_ROPE)
    w = jnp.pad(w, ((0, 0), (0, 0), (0, MLA_QK - MLA_NOPE - MLA_ROPE)))
    return w.reshape(MLA_Q_RANK, MLA_HEADS * MLA_QK).astype(BF16)


def kernel(x, c, ctx, c_ctx, ada_w, ada_b, norm_g, ab_w_in, dn_conv_w, dn_a_log, dn_dt_bias, dn_norm_g, mla_q_norm_g, mla_w_qb, mla_kv_norm_g, mla_w_kvb, ab_w_out, ffn_w_gate, ffn_w_up, ffn_w_down, gqa_w_qkv, gqa_b_qkv, gqa_sink, gqa_w_out, gqa_b_out, moe_w_router, moe_w_gate, moe_w_up, moe_w_down):
    assert x.shape == (BATCH, SEQ, D_MODEL) and ctx.shape == (BATCH, CTX_LEN, D_MODEL) and ada_w.shape[0] == 2
    h = jnp.concatenate([x.reshape(N_LAT, D_MODEL), ctx.reshape(N_CTX, D_MODEL)], axis=0)
    cvec = jnp.zeros((8, D_MODEL), F32).at[:BATCH].set(c).at[BATCH].set(c_ctx)
    mod = _ada(cvec, ada_w, ada_b)
    (cos1, sin1), (cos2, sin2) = _rope_tables()

    u = _modulate_in(h, norm_g[0, 0], mod, 0, 0, 1)
    p = _matmul(u, _rearrange_w_in(ab_w_in[0]), None, F32, MM_ROW_TILE, 512)
    o_f, o_b = _deltanet(_dn_conv(p, dn_conv_w[0]), _dn_gates(p, dn_a_log[0], dn_dt_bias[0]))
    q, k, v = _mla_proj(p, mla_q_norm_g[0], mla_kv_norm_g[0], _rearrange_w_qb(mla_w_qb[0]),
                        mla_w_kvb[0].astype(BF16), cos1, sin1)
    y = _ab_out(o_f, o_b, p, dn_norm_g[0], _mla_attention(q, k, v), ab_w_out[0])
    h, u = _post(h, y, norm_g[0, 1], norm_g[0, 2], mod, (0, 2), (0, 3), (0, 4), BF16)
    y = _ffn(u, ffn_w_gate[0], ffn_w_up[0], ffn_w_down[0])
    h, u = _post(h, y, norm_g[0, 3], norm_g[1, 0], mod, (0, 5), (1, 0), (1, 1), BF16)

    qkv = _gqa_qkv(u, gqa_w_qkv[0], gqa_b_qkv[0], cos2, sin2)
    o = _gqa_attention(qkv, gqa_sink[0])
    y = _matmul(o, gqa_w_out[0], gqa_b_out[0], F32, 1024, 512)
    h, u = _post(h, y, norm_g[1, 1], norm_g[1, 2], mod, (1, 2), (1, 3), (1, 4), F32)
    idx, wts = _router(u, moe_w_router[0])
    pos, slot_tok, slot_valid, blk_e, blk_valid = _moe_routing(idx)
    xs = _moe_gather(slot_tok, slot_valid, u)
    ys = _moe_ffn(blk_e, blk_valid, xs, moe_w_gate[0], moe_w_up[0], moe_w_down[0])
    out = _moe_combine(pos, ys, wts, h, norm_g[1, 3], mod, (1, 5))
    return out.reshape(BATCH, SEQ, D_MODEL)
```

```python
import functools
import math

import jax
import jax.numpy as jnp
from jax import lax
from jax.experimental import pallas as pl
from jax.experimental.pallas import tpu as pltpu

F32 = jnp.float32
BF16 = jnp.bfloat16
I32 = jnp.int32

D_MODEL = 2048
BATCH = 2
SEQ = 4096
CTX_LEN = 256
GRID_W = 64
EPS = 1e-6
ROPE_THETA = 10000.0

DN_HEADS = 8
DN_HEAD_DIM = 128
DN_CONV = 5
DN_CHUNK = 64
MLA_HEADS = 8
MLA_Q_RANK = 768
MLA_KV_RANK = 512
MLA_NOPE = 128
MLA_ROPE = 64
MLA_V = 128
GQA_HEADS = 32
GQA_KV_HEADS = 4
GQA_HEAD_DIM = 64
WINDOW = 128
Q_BLOCK = 128
D_FF = 7168
N_EXPERTS = 8
TOP_K = 2

DN_W = DN_HEADS * DN_HEAD_DIM
DN_QKV = 3 * DN_W
N_LAT = BATCH * SEQ
N_CTX = BATCH * CTX_LEN
N_TOK = N_LAT + N_CTX
GQA_Q = GQA_HEADS * GQA_HEAD_DIM
GQA_KV = GQA_KV_HEADS * GQA_HEAD_DIM

P_QKV = 0
P_Z = DN_QKV
P_CKV = P_Z + DN_W
P_CQ = P_CKV + MLA_KV_RANK
P_MISC = P_CQ + MLA_Q_RANK
P_WIDTH = 5632

ROW_TILE = 512
MM_ROW_TILE = 1088
MOE_BLOCK = 1024
MOE_SLOTS = N_LAT * TOP_K + N_EXPERTS * MOE_BLOCK
VMEM_LIMIT = 56 * 1024 * 1024

_SDS = jax.ShapeDtypeStruct
_HIGH = lax.Precision.HIGHEST


def _cparams(sem, vmem=None):
    return pltpu.CompilerParams(dimension_semantics=sem, vmem_limit_bytes=vmem)


def _dot(a, b):
    return jnp.dot(a, b, preferred_element_type=F32)


def _dot_nt(a, b):
    return lax.dot_general(a, b, (((1,), (1,)), ((), ())), preferred_element_type=F32)


def _dot_tn(a, b):
    return lax.dot_general(a, b, (((0,), (0,)), ((), ())), preferred_element_type=F32)


def _dot_hi(a, b):
    return jnp.dot(a, b, preferred_element_type=F32, precision=_HIGH)


def _silu(x):
    return x * jax.nn.sigmoid(x)


def _rms(x, g):
    return x * lax.rsqrt(jnp.mean(x * x, axis=-1, keepdims=True) + EPS) * g


def _group_of_row(r0):
    return (r0 >= SEQ).astype(I32) + (r0 >= 2 * SEQ).astype(I32)


def _mod_spec(layer, k):
    return pl.BlockSpec((1, 8, D_MODEL), lambda *_: (layer, 0, k))


def _mod_row(ref, grp):
    return ref[0, pl.ds(grp, 1), :]


def _ada_kernel(c_ref, w_ref, b_ref, o_ref):
    s = _silu(c_ref[...]).astype(BF16)
    o_ref[0] = _dot(s, w_ref[0].astype(BF16)) + b_ref[0]


def _ada(cvec, ada_w, ada_b):
    n_layers, _, n = ada_w.shape
    tn = 1024
    return pl.pallas_call(
        _ada_kernel,
        out_shape=_SDS((n_layers, 8, n), F32),
        grid=(n_layers, n // tn),
        in_specs=[pl.BlockSpec((8, D_MODEL), lambda l, j: (0, 0)),
                  pl.BlockSpec((1, D_MODEL, tn), lambda l, j: (l, 0, j)),
                  pl.BlockSpec((1, 1, tn), lambda l, j: (l, 0, j))],
        out_specs=pl.BlockSpec((1, 8, tn), lambda l, j: (l, 0, j)),
        compiler_params=_cparams(("parallel", "parallel")),
        name="ada_mod",
    )(cvec, ada_w, ada_b.reshape(n_layers, 1, n))


def _modin_kernel(h_ref, g_ref, sh_ref, sc_ref, u_ref):
    grp = _group_of_row(pl.program_id(0) * ROW_TILE)
    u = _rms(h_ref[...], g_ref[...]) * (1.0 + _mod_row(sc_ref, grp)) + _mod_row(sh_ref, grp)
    u_ref[...] = u.astype(u_ref.dtype)


def _modulate_in(h, g, mod, layer, k_shift, k_scale):
    n = h.shape[0]
    row = pl.BlockSpec((ROW_TILE, D_MODEL), lambda i: (i, 0))
    vec = pl.BlockSpec((1, D_MODEL), lambda i: (0, 0))
    return pl.pallas_call(
        _modin_kernel,
        out_shape=_SDS((n, D_MODEL), BF16),
        grid=(n // ROW_TILE,),
        in_specs=[row, vec, _mod_spec(layer, k_shift), _mod_spec(layer, k_scale)],
        out_specs=row,
        compiler_params=_cparams(("parallel",)),
        name="modulate_in",
    )(h, g.reshape(1, D_MODEL), mod, mod)


def _post_kernel(h_ref, y_ref, gp_ref, gate_ref, gn_ref, sh_ref, sc_ref, hn_ref, u_ref):
    grp = _group_of_row(pl.program_id(0) * ROW_TILE)
    hn = h_ref[...] + _mod_row(gate_ref, grp) * _rms(y_ref[...], gp_ref[...])
    hn_ref[...] = hn
    u = _rms(hn, gn_ref[...]) * (1.0 + _mod_row(sc_ref, grp)) + _mod_row(sh_ref, grp)
    u_ref[...] = u.astype(u_ref.dtype)


def _post(h, y, g_post, g_next, mod, gate_lk, shift_lk, scale_lk, u_dtype):
    n = y.shape[0]
    row = pl.BlockSpec((ROW_TILE, D_MODEL), lambda i: (i, 0))
    vec = pl.BlockSpec((1, D_MODEL), lambda i: (0, 0))
    return pl.pallas_call(
        _post_kernel,
        out_shape=(_SDS((n, D_MODEL), F32), _SDS((n, D_MODEL), u_dtype)),
        grid=(n // ROW_TILE,),
        in_specs=[row, row, vec, _mod_spec(*gate_lk), vec, _mod_spec(*shift_lk), _mod_spec(*scale_lk)],
        out_specs=(row, row),
        compiler_params=_cparams(("parallel",)),
        name="residual_post",
    )(h, y, g_post.reshape(1, D_MODEL), mod, g_next.reshape(1, D_MODEL), mod, mod)


def _mm_kernel(a_ref, w_ref, b_ref, o_ref):
    o_ref[...] = (_dot(a_ref[...], w_ref[...].astype(BF16)) + b_ref[...]).astype(o_ref.dtype)


def _matmul(a, w, bias, out_dtype, tm, tn):
    m, k = a.shape
    n = w.shape[1]
    if bias is None:
        bias = jnp.zeros((n,), F32)
    return pl.pallas_call(
        _mm_kernel,
        out_shape=_SDS((m, n), out_dtype),
        grid=(m // tm, n // tn),
        in_specs=[pl.BlockSpec((tm, k), lambda i, j: (i, 0)),
                  pl.BlockSpec((k, tn), lambda i, j: (0, j)),
                  pl.BlockSpec((1, tn), lambda i, j: (0, j))],
        out_specs=pl.BlockSpec((tm, tn), lambda i, j: (i, j)),
        compiler_params=_cparams(("parallel", "arbitrary"), VMEM_LIMIT),
        name="matmul",
    )(a, w, bias.reshape(1, n))


FF_TILE = 256
OUT_TILE = 256
FF_STEPS = D_FF // FF_TILE
OUT_STEPS = D_MODEL // OUT_TILE


def _swiglu_step(s, x_ref, wg_ref, wu_ref, wd_ref, o_ref, h_ref):
    @pl.when(s < FF_STEPS)
    def _():
        x = x_ref[...]
        a = _dot(x, wg_ref[...].astype(BF16))
        b = _dot(x, wu_ref[...].astype(BF16))
        h_ref[:, pl.ds(pl.multiple_of(s * FF_TILE, FF_TILE), FF_TILE)] = (_silu(a) * b).astype(BF16)

    @pl.when(s >= FF_STEPS)
    def _():
        o_ref[...] = _dot(h_ref[...], wd_ref[...].astype(BF16))


def _ffn_kernel(u_ref, wg_ref, wu_ref, wd_ref, o_ref, h_ref):
    _swiglu_step(pl.program_id(1), u_ref, wg_ref, wu_ref, wd_ref, o_ref, h_ref)


def _ffn(u, w_gate, w_up, w_down):
    m = u.shape[0]
    tm = MM_ROW_TILE

    def ff(i, s):
        return (0, jnp.minimum(s, FF_STEPS - 1))

    def out(s):
        return jnp.maximum(s - FF_STEPS, 0)

    return pl.pallas_call(
        _ffn_kernel,
        out_shape=_SDS((m, D_MODEL), F32),
        grid=(m // tm, FF_STEPS + OUT_STEPS),
        in_specs=[pl.BlockSpec((tm, D_MODEL), lambda i, s: (i, 0), pipeline_mode=pl.Buffered(1)),
                  pl.BlockSpec((D_MODEL, FF_TILE), ff),
                  pl.BlockSpec((D_MODEL, FF_TILE), ff),
                  pl.BlockSpec((D_FF, OUT_TILE), lambda i, s: (0, out(s)))],
        out_specs=pl.BlockSpec((tm, OUT_TILE), lambda i, s: (i, out(s))),
        scratch_shapes=[pltpu.VMEM((tm, D_FF), BF16)],
        compiler_params=_cparams(("parallel", "arbitrary"), VMEM_LIMIT),
        name="swiglu_ffn",
    )(u, w_gate, w_up, w_down)


CONV_TILE = 256
_SEQ_STARTS = (0, SEQ // CONV_TILE, 2 * SEQ // CONV_TILE, 2 * SEQ // CONV_TILE + 1)
_SEQ_LASTS = (SEQ // CONV_TILE - 1, 2 * SEQ // CONV_TILE - 1, 2 * SEQ // CONV_TILE, 2 * SEQ // CONV_TILE + 1)


def _conv_kernel(prev_ref, cur_ref, next_ref, w_ref, o_ref, buf):
    i = pl.program_id(0)
    j = pl.program_id(1)
    tm = CONV_TILE
    first = functools.reduce(jnp.logical_or, [i == s for s in _SEQ_STARTS])
    last = functools.reduce(jnp.logical_or, [i == s for s in _SEQ_LASTS])
    buf[0:8, :] = jnp.where(first, 0.0, prev_ref[...])
    buf[8:8 + tm, :] = cur_ref[...]
    buf[8 + tm:16 + tm, :] = jnp.where(last, 0.0, next_ref[...])
    pad = DN_CONV // 2
    acc = buf[8 - pad:8 - pad + tm, :] * w_ref[0:1, :]
    for d in range(1, DN_CONV):
        acc = acc + buf[8 - pad + d:8 - pad + d + tm, :] * w_ref[d:d + 1, :]
    x = _silu(acc)
    q_scale = jnp.where(j == 0, DN_HEAD_DIM ** -0.5, 1.0)
    for hh in range(DN_HEADS):
        xh = x[:, hh * DN_HEAD_DIM:(hh + 1) * DN_HEAD_DIM]
        inv = lax.rsqrt(jnp.sum(xh * xh, axis=-1, keepdims=True) + EPS) * q_scale
        o_ref[:, hh * DN_HEAD_DIM:(hh + 1) * DN_HEAD_DIM] = xh * jnp.where(j == 2, 1.0, inv)


def _dn_conv(p, conv_w):
    n = p.shape[0]
    tm = CONV_TILE
    r8 = tm // 8
    nb8 = n // 8
    return pl.pallas_call(
        _conv_kernel,
        out_shape=_SDS((n, DN_QKV), F32),
        grid=(n // tm, 3),
        in_specs=[pl.BlockSpec((8, DN_W), lambda i, j: (jnp.maximum(i * r8 - 1, 0), j)),
                  pl.BlockSpec((tm, DN_W), lambda i, j: (i, j)),
                  pl.BlockSpec((8, DN_W), lambda i, j: (jnp.minimum((i + 1) * r8, nb8 - 1), j)),
                  pl.BlockSpec((DN_CONV, DN_W), lambda i, j: (0, j))],
        out_specs=pl.BlockSpec((tm, DN_W), lambda i, j: (i, j)),
        scratch_shapes=[pltpu.VMEM((tm + 16, DN_W), F32)],
        compiler_params=_cparams(("parallel", "parallel")),
        name="dn_conv",
    )(p, p, p, conv_w)


def _gates_kernel(m_ref, a_ref, dt_ref, o_ref):
    x = pltpu.roll(m_ref[...], 64, 1)
    lane = lax.broadcasted_iota(I32, x.shape, 1)
    z = x + dt_ref[...]
    softplus = jnp.maximum(z, 0.0) + jnp.log1p(jnp.exp(-jnp.abs(z)))
    g = a_ref[...] * softplus
    o_ref[...] = jnp.where(lane < 2 * DN_HEADS, jax.nn.sigmoid(x), jnp.where(lane < 4 * DN_HEADS, g, 0.0))


def _dn_gates(p, a_log, dt_bias):
    n = p.shape[0]
    neg_a = jnp.zeros((1, 128), F32).at[0, 2 * DN_HEADS:4 * DN_HEADS].set(-jnp.exp(a_log.astype(F32)).reshape(-1))
    dtb = jnp.zeros((1, 128), F32).at[0, 2 * DN_HEADS:4 * DN_HEADS].set(dt_bias.astype(F32).reshape(-1))
    tm = ROW_TILE
    vec = pl.BlockSpec((1, 128), lambda i: (0, 0))
    return pl.pallas_call(
        _gates_kernel,
        out_shape=_SDS((n, 128), F32),
        grid=(n // tm,),
        in_specs=[pl.BlockSpec((tm, 128), lambda i: (i, P_MISC // 128)), vec, vec],
        out_specs=pl.BlockSpec((tm, 128), lambda i: (i, 0)),
        compiler_params=_cparams(("parallel",)),
        name="dn_gates",
    )(p, neg_a, dtb)


_CTX_CHUNKS = CTX_LEN // DN_CHUNK
_LAT_CHUNKS = SEQ // DN_CHUNK
_DN_STEPS = _CTX_CHUNKS + _LAT_CHUNKS


def _dn_fwd_block(b, t):
    return jnp.where(t < _CTX_CHUNKS, (2 * SEQ + b * CTX_LEN) // DN_CHUNK + t, b * _LAT_CHUNKS + (t - _CTX_CHUNKS))


def _dn_bwd_block(b, t):
    return jnp.where(t < _CTX_CHUNKS, (2 * SEQ + b * CTX_LEN) // DN_CHUNK + (_CTX_CHUNKS - 1 - t),
                     b * _LAT_CHUNKS + (_DN_STEPS - 1 - t))


DN_GROUP = 4
_GROUP_ROWS = DN_GROUP * DN_CHUNK
_GROUP_STATE = DN_GROUP * DN_HEAD_DIM


_DN_BASE = 8


def _block_diag(x):
    zero = jnp.zeros((DN_CHUNK, DN_HEAD_DIM), x.dtype)
    rows = []
    for hh in range(DN_GROUP):
        piece = x[hh * DN_CHUNK:(hh + 1) * DN_CHUNK]
        rows.append(jnp.concatenate([piece if j == hh else zero for j in range(DN_GROUP)], axis=1))
    return jnp.concatenate(rows, axis=0)


def _dn_groups(items, eye, base_mask):
    n = range(len(items))
    it = items
    gx = [jnp.broadcast_to(it[g]["gc"], (_GROUP_ROWS, _GROUP_ROWS)) for g in n]
    gamma = [jnp.where(it[g]["incl"], jnp.exp(jnp.where(it[g]["incl"], gx[g] - gx[g].T, 0.0)), 0.0) for g in n]
    kb = [it[g]["k"].astype(BF16) for g in n]
    kk = [_dot_nt(kb[g], kb[g]) for g in n]
    a = [jnp.where(it[g]["strict"], it[g]["beta"] * kk[g] * gamma[g], 0.0) for g in n]
    d = [jnp.where(base_mask, a[g], 0.0) for g in n]
    tinv = [eye - d[g] for g in n]
    pw = [d[g].astype(BF16) for g in n]
    for _ in range(int(math.log2(_DN_BASE)) - 1):
        pw = [_dot(pw[g], pw[g]).astype(BF16) for g in n]
        tinv = [tinv[g] + _dot(tinv[g].astype(BF16), pw[g]) for g in n]
    for lvl in range(len(it[0]["levels"])):
        tb = [tinv[g].astype(BF16) for g in n]
        mt = [_dot(jnp.where(it[g]["levels"][lvl], a[g], 0.0).astype(BF16), tb[g]).astype(BF16) for g in n]
        tinv = [tinv[g] - _dot(tb[g], mt[g]) for g in n]
    tb = [tinv[g].astype(BF16) for g in n]
    egc = [jnp.exp(it[g]["gc"]) for g in n]
    rhs = [jnp.concatenate([it[g]["v"] * it[g]["beta"], it[g]["k"] * (it[g]["beta"] * egc[g])], axis=1) for g in n]
    rhs_hi = [rhs[g].astype(BF16) for g in n]
    rhs_lo = [(rhs[g] - rhs_hi[g].astype(F32)).astype(BF16) for g in n]
    sol = [_dot(jnp.concatenate([tb[g], tb[g]], axis=1), jnp.concatenate([rhs_hi[g], rhs_lo[g]], axis=0)) for g in n]
    qk = [(_dot_nt(it[g]["q"].astype(BF16), kb[g]) * gamma[g]).astype(BF16) for g in n]
    s = [it[g]["s_ref"][...] for g in n]
    sb = [s[g].astype(BF16) for g in n]
    v_new = [sol[g][:, :DN_HEAD_DIM] - _dot(_block_diag(sol[g][:, DN_HEAD_DIM:].astype(BF16)), sb[g]) for g in n]
    vb = [v_new[g].astype(BF16) for g in n]
    o = [_dot(_block_diag((it[g]["q"] * egc[g]).astype(BF16)), sb[g]) + _dot(qk[g], vb[g]) for g in n]
    for g in n:
        k_dec = _block_diag((it[g]["k"] * jnp.exp(it[g]["g_last"] - it[g]["gc"])).astype(BF16))
        it[g]["s_ref"][...] = s[g] * jnp.exp(it[g]["g_last_state"]) + _dot_tn(k_dec, vb[g])
    return o


def _stack_cols(x, lanes, rows_each):
    pieces = [x[:, l:l + 1] for l in lanes]
    if rows_each is not None:
        pieces = [jnp.broadcast_to(pc, (rows_each, 1)) for pc in pieces]
    return jnp.concatenate(pieces, axis=0)


def _dn_kernel(*refs):
    n_in = BATCH * 2 * 4
    in_refs, (ofl_ref, ofc_ref, obl_ref, obc_ref), (s_ref, o_scr) = refs[:n_in], refs[n_in:n_in + 4], refs[n_in + 4:]
    t = pl.program_id(0)

    @pl.when(t == 0)
    def _():
        s_ref[...] = jnp.zeros_like(s_ref)

    c = DN_CHUNK
    ri = lax.broadcasted_iota(I32, (_GROUP_ROWS, _GROUP_ROWS), 0)
    ci = lax.broadcasted_iota(I32, (_GROUP_ROWS, _GROUP_ROWS), 1)
    same_head = (ri // c) == (ci // c)
    eye = (ri == ci).astype(F32)
    r1 = lax.broadcasted_iota(I32, (c, c), 0)
    c1 = lax.broadcasted_iota(I32, (c, c), 1)
    base_mask = (ri // _DN_BASE) == (ci // _DN_BASE)
    masks = []
    for d in range(2):
        incl = same_head & ((ri >= ci) if d == 0 else (ri <= ci))
        strict = same_head & ((ri > ci) if d == 0 else (ri < ci))
        levels = []
        size = _DN_BASE
        while size < c:
            lo_blk, hi_blk = ((ci, ri) if d == 0 else (ri, ci))
            levels.append(((ri // (2 * size)) == (ci // (2 * size)))
                          & ((lo_blk // size) % 2 == 0) & ((hi_blk // size) % 2 == 1))
            size *= 2
        masks.append((incl, strict, levels))
    items, where = [], []
    for b, d in [(b, d) for b in range(BATCH) for d in range(2)]:
        q_ref, k_ref, v_ref, g_ref = in_refs[(b * 2 + d) * 4:(b * 2 + d) * 4 + 4]
        incl, strict, levels = masks[d]
        gates = g_ref[...]
        tri = ((r1 >= c1) if d == 0 else (r1 <= c1)).astype(BF16)
        g_hi = gates.astype(BF16)
        g_r1 = gates - g_hi.astype(F32)
        g_mid = g_r1.astype(BF16)
        g_lo = (g_r1 - g_mid.astype(F32)).astype(BF16)
        gc = _dot(tri, g_hi) + _dot(tri, g_mid) + _dot(tri, g_lo)
        last_row = c - 1 if d == 0 else 0
        tot = gc[last_row:last_row + 1, :]
        for grp in range(DN_HEADS // DN_GROUP):
            heads = range(grp * DN_GROUP, (grp + 1) * DN_GROUP)
            lanes_b = [d * DN_HEADS + hh for hh in heads]
            lanes_g = [2 * DN_HEADS + lb for lb in lanes_b]

            def stack(ref):
                return jnp.concatenate([ref[:, hh * DN_HEAD_DIM:(hh + 1) * DN_HEAD_DIM] for hh in heads], axis=0)

            items.append(dict(q=stack(q_ref), k=stack(k_ref), v=stack(v_ref),
                              beta=_stack_cols(gates, lanes_b, None), gc=_stack_cols(gc, lanes_g, None),
                              g_last=_stack_cols(tot, lanes_g, c), g_last_state=_stack_cols(tot, lanes_g, DN_HEAD_DIM),
                              incl=incl, strict=strict, levels=levels, s_ref=s_ref.at[b, d, grp]))
            where.append((d, b, heads))
    outs = _dn_groups(items, eye, base_mask)
    for o, (d, b, heads) in zip(outs, where):
        for j, hh in enumerate(heads):
            o_scr[d, b, :, hh * DN_HEAD_DIM:(hh + 1) * DN_HEAD_DIM] = o[j * c:(j + 1) * c]

    @pl.when(t < _CTX_CHUNKS)
    def _():
        ofc_ref[...] = o_scr[0]
        obc_ref[...] = o_scr[1]

    @pl.when(t >= _CTX_CHUNKS)
    def _():
        ofl_ref[...] = o_scr[0]
        obl_ref[...] = o_scr[1]


def _deltanet(qkv, gates):
    c = DN_CHUNK
    ins, args = [], []
    for b in range(BATCH):
        for fn in (_dn_fwd_block, _dn_bwd_block):
            for width, col, arr in ((DN_W, 0, qkv), (DN_W, 1, qkv), (DN_W, 2, qkv), (128, 0, gates)):
                ins.append(pl.BlockSpec((c, width), functools.partial(lambda t, b, fn, col: (fn(b, t), col),
                                                                      b=b, fn=fn, col=col)))
                args.append(arr)
    last = _DN_STEPS - 1
    lat = _SDS((BATCH, SEQ, DN_W), F32)
    ctx = _SDS((BATCH, CTX_LEN, DN_W), F32)
    blk = (BATCH, c, DN_W)
    outs = pl.pallas_call(
        _dn_kernel,
        out_shape=(lat, ctx, lat, ctx),
        grid=(_DN_STEPS,),
        in_specs=ins,
        out_specs=(pl.BlockSpec(blk, lambda t: (0, jnp.maximum(t - _CTX_CHUNKS, 0), 0)),
                   pl.BlockSpec(blk, lambda t: (0, jnp.minimum(t, _CTX_CHUNKS - 1), 0)),
                   pl.BlockSpec(blk, lambda t: (0, jnp.minimum(last - t, _LAT_CHUNKS - 1), 0)),
                   pl.BlockSpec(blk, lambda t: (0, jnp.maximum(_CTX_CHUNKS - 1 - t, 0), 0))),
        scratch_shapes=[pltpu.VMEM((BATCH, 2, DN_HEADS // DN_GROUP, _GROUP_STATE, DN_HEAD_DIM), F32),
                        pltpu.VMEM((2, BATCH, c, DN_W), F32)],
        compiler_params=_cparams(("arbitrary",)),
        name="deltanet",
    )(*args)
    ofl, ofc, obl, obc = outs
    o_f = jnp.concatenate([ofl.reshape(N_LAT, DN_W), ofc.reshape(N_CTX, DN_W)], axis=0)
    o_b = jnp.concatenate([obl.reshape(N_LAT, DN_W), obc.reshape(N_CTX, DN_W)], axis=0)
    return o_f, o_b


def _rope_tables():
    t = jnp.arange(SEQ)
    row = (t // GRID_W).astype(F32)
    col = (t % GRID_W).astype(F32)
    n_freq = MLA_ROPE // 4
    inv_freq = ROPE_THETA ** (-jnp.arange(n_freq, dtype=F32) / n_freq)
    ang = jnp.concatenate([row[:, None] * inv_freq, col[:, None] * inv_freq], axis=-1)
    cos, sin = jnp.cos(ang), jnp.sin(ang)
    cos64 = jnp.concatenate([cos, cos], axis=-1)
    sin64 = jnp.concatenate([-sin, sin], axis=-1)

    def assemble(lat_cos, lat_sin):
        c = jnp.concatenate([jnp.tile(lat_cos, (BATCH, 1)), jnp.ones((N_CTX, 128), F32)], axis=0)
        s = jnp.concatenate([jnp.tile(lat_sin, (BATCH, 1)), jnp.zeros((N_CTX, 128), F32)], axis=0)
        return c, s

    cos_l = jnp.concatenate([cos64, jnp.ones((SEQ, 64), F32)], axis=-1)
    sin_l = jnp.concatenate([sin64, jnp.zeros((SEQ, 64), F32)], axis=-1)
    one_head = assemble(cos_l, sin_l)
    two_heads = assemble(jnp.tile(cos64, (1, 2)), jnp.tile(sin64, (1, 2)))
    return one_head, two_heads


def _rope128(x, cosv, sinv):
    lane = lax.broadcasted_iota(I32, x.shape, 1)
    swapped = jnp.where((lane % 64) < 32, pltpu.roll(x, 96, 1), pltpu.roll(x, 32, 1))
    return x * cosv + swapped * sinv


MLA_QK = 256


def _mla_proj_kernel(ckv_ref, cq_ref, misc_ref, gq_ref, gkv_ref, wq_ref, wkv_ref, cos_ref, sin_ref,
                     q_ref, k_ref, v_ref):
    cq = _rms(cq_ref[...], gq_ref[...]).astype(BF16)
    ckv = _rms(ckv_ref[...], gkv_ref[...]).astype(BF16)
    qf = _dot(cq, wq_ref[...])
    kvf = _dot(ckv, wkv_ref[...])
    cosv, sinv = cos_ref[...], sin_ref[...]
    lane = lax.broadcasted_iota(I32, cosv.shape, 1)
    k_rope = _rope128(jnp.where(lane < MLA_ROPE, misc_ref[...], 0.0), cosv, sinv).astype(BF16)
    scale = (MLA_NOPE + MLA_ROPE) ** -0.5
    for hh in range(MLA_HEADS):
        lo, mid, hi = hh * MLA_QK, hh * MLA_QK + 128, (hh + 1) * MLA_QK
        q_ref[:, lo:mid] = (qf[:, lo:mid] * scale).astype(BF16)
        q_ref[:, mid:hi] = (_rope128(qf[:, mid:hi], cosv, sinv) * scale).astype(BF16)
        k_ref[:, lo:mid] = kvf[:, lo:mid].astype(BF16)
        k_ref[:, mid:hi] = k_rope
        v_ref[:, hh * MLA_V:(hh + 1) * MLA_V] = kvf[:, mid:hi].astype(BF16)


def _mla_proj(p, q_norm_g, kv_norm_g, w_q, w_kv, cosv, sinv):
    n = p.shape[0]
    tm = ROW_TILE
    wide = MLA_HEADS * MLA_QK

    def rows(width, col):
        return pl.BlockSpec((tm, width), lambda i: (i, col))

    def whole(shape):
        return pl.BlockSpec(shape, lambda i: (0, 0))

    return pl.pallas_call(
        _mla_proj_kernel,
        out_shape=(_SDS((n, wide), BF16), _SDS((n, wide), BF16), _SDS((n, MLA_HEADS * MLA_V), BF16)),
        grid=(n // tm,),
        in_specs=[rows(MLA_KV_RANK, P_CKV // MLA_KV_RANK), rows(MLA_Q_RANK, P_CQ // MLA_Q_RANK),
                  rows(128, P_MISC // 128), whole((1, MLA_Q_RANK)), whole((1, MLA_KV_RANK)),
                  whole((MLA_Q_RANK, wide)), whole((MLA_KV_RANK, wide)), rows(128, 0), rows(128, 0)],
        out_specs=(rows(wide, 0), rows(wide, 0), rows(MLA_HEADS * MLA_V, 0)),
        compiler_params=_cparams(("parallel",), VMEM_LIMIT),
        name="mla_proj",
    )(p, p, p, q_norm_g.reshape(1, -1), kv_norm_g.reshape(1, -1), w_q, w_kv, cosv, sinv)


def _attn_kernel(*refs, n_seg):
    q_ref, o_ref = refs[0], refs[-1]
    k_refs, v_refs = refs[1:1 + n_seg], refs[1 + n_seg:1 + 2 * n_seg]
    q = q_ref[...]
    s = [_dot_nt(q, k[...]) for k in k_refs]
    m = functools.reduce(jnp.maximum, [jnp.max(si, axis=-1, keepdims=True) for si in s])
    p = [jnp.exp(si - m) for si in s]
    den = functools.reduce(jnp.add, [jnp.sum(pi, axis=-1, keepdims=True) for pi in p])
    o = functools.reduce(jnp.add, [_dot(pi.astype(BF16), v[...]) for pi, v in zip(p, v_refs)])
    o_ref[...] = (o / den).astype(o_ref.dtype)


def _mla_attention(q, k, v):
    tq = 512
    nq = SEQ // tq
    ctx_blk = 2 * SEQ // CTX_LEN
    lat = pl.pallas_call(
        functools.partial(_attn_kernel, n_seg=2),
        out_shape=_SDS((N_LAT, MLA_HEADS * MLA_V), BF16),
        grid=(BATCH, MLA_HEADS, nq),
        in_specs=[pl.BlockSpec((tq, MLA_QK), lambda b, h, i: (b * nq + i, h)),
                  pl.BlockSpec((CTX_LEN, MLA_QK), lambda b, h, i: (ctx_blk + b, h)),
                  pl.BlockSpec((SEQ, MLA_QK), lambda b, h, i: (b, h)),
                  pl.BlockSpec((CTX_LEN, MLA_V), lambda b, h, i: (ctx_blk + b, h)),
                  pl.BlockSpec((SEQ, MLA_V), lambda b, h, i: (b, h))],
        out_specs=pl.BlockSpec((tq, MLA_V), lambda b, h, i: (b * nq + i, h)),
        compiler_params=_cparams(("parallel", "parallel", "arbitrary"), VMEM_LIMIT),
        name="mla_attn_latent",
    )(q, k, k, v, v)
    ctx = pl.pallas_call(
        functools.partial(_attn_kernel, n_seg=1),
        out_shape=_SDS((N_CTX, MLA_HEADS * MLA_V), BF16),
        grid=(BATCH, MLA_HEADS),
        in_specs=[pl.BlockSpec((CTX_LEN, MLA_QK), lambda b, h: (ctx_blk + b, h)),
                  pl.BlockSpec((CTX_LEN, MLA_QK), lambda b, h: (ctx_blk + b, h)),
                  pl.BlockSpec((CTX_LEN, MLA_V), lambda b, h: (ctx_blk + b, h))],
        out_specs=pl.BlockSpec((CTX_LEN, MLA_V), lambda b, h: (b, h)),
        compiler_params=_cparams(("parallel", "parallel")),
        name="mla_attn_context",
    )(q, k, v)
    return jnp.concatenate([lat, ctx], axis=0)


def _ab_out_kernel(of_ref, ob_ref, z_ref, g_ref, mla_ref, w1_ref, w2_ref, o_ref, a_ref):
    @pl.when(pl.program_id(1) == 0)
    def _():
        for hh in range(DN_HEADS):
            sl = slice(hh * DN_HEAD_DIM, (hh + 1) * DN_HEAD_DIM)
            a_ref[:, sl] = (_rms(of_ref[:, sl] + ob_ref[:, sl], g_ref[...]) * _silu(z_ref[:, sl])).astype(BF16)

    o_ref[...] = _dot(a_ref[...], w1_ref[...].astype(BF16)) + _dot(mla_ref[...], w2_ref[...].astype(BF16))


def _ab_out(o_f, o_b, p, dn_norm_g, mla_o, w_out):
    n = o_f.shape[0]
    tm, tn = 544, 512
    half = DN_W
    return pl.pallas_call(
        _ab_out_kernel,
        out_shape=_SDS((n, D_MODEL), F32),
        grid=(n // tm, D_MODEL // tn),
        in_specs=[pl.BlockSpec((tm, half), lambda i, j: (i, 0)),
                  pl.BlockSpec((tm, half), lambda i, j: (i, 0)),
                  pl.BlockSpec((tm, half), lambda i, j: (i, P_Z // half)),
                  pl.BlockSpec((1, DN_HEAD_DIM), lambda i, j: (0, 0)),
                  pl.BlockSpec((tm, half), lambda i, j: (i, 0)),
                  pl.BlockSpec((half, tn), lambda i, j: (0, j)),
                  pl.BlockSpec((half, tn), lambda i, j: (1, j))],
        out_specs=pl.BlockSpec((tm, tn), lambda i, j: (i, j)),
        scratch_shapes=[pltpu.VMEM((tm, half), BF16)],
        compiler_params=_cparams(("parallel", "arbitrary"), VMEM_LIMIT),
        name="ab_out_proj",
    )(o_f, o_b, p, dn_norm_g.reshape(1, -1), mla_o, w_out, w_out)


def _mm_rope_kernel(a_ref, w_ref, b_ref, cos_ref, sin_ref, o_ref, *, tn, rope_cols):
    y = _dot(a_ref[...], w_ref[...].astype(BF16)) + b_ref[...]
    col0 = pl.program_id(1) * tn
    cosv, sinv = cos_ref[...], sin_ref[...]
    for s in range(tn // 128):
        ys = y[:, s * 128:(s + 1) * 128]
        roped = _rope128(ys, cosv, sinv)
        o_ref[:, s * 128:(s + 1) * 128] = jnp.where(col0 + s * 128 < rope_cols, roped, ys).astype(o_ref.dtype)


def _gqa_qkv(u, w, bias, cosv, sinv):
    m, k = u.shape
    n = w.shape[1]
    tm, tn = MM_ROW_TILE, 512
    return pl.pallas_call(
        functools.partial(_mm_rope_kernel, tn=tn, rope_cols=GQA_Q + GQA_KV),
        out_shape=_SDS((m, n), BF16),
        grid=(m // tm, n // tn),
        in_specs=[pl.BlockSpec((tm, k), lambda i, j: (i, 0)),
                  pl.BlockSpec((k, tn), lambda i, j: (0, j)),
                  pl.BlockSpec((1, tn), lambda i, j: (0, j)),
                  pl.BlockSpec((tm, 128), lambda i, j: (i, 0)),
                  pl.BlockSpec((tm, 128), lambda i, j: (i, 0))],
        out_specs=pl.BlockSpec((tm, tn), lambda i, j: (i, j)),
        compiler_params=_cparams(("parallel", "arbitrary"), VMEM_LIMIT),
        name="gqa_qkv_rope",
    )(u, w, bias.reshape(1, n), cosv, sinv)


def _gqa_kernel(sink_ref, q_ref, kp_ref, kc_ref, kn_ref, vp_ref, vc_ref, vn_ref, kx_ref, vx_ref, o_ref):
    i = pl.program_id(1)
    nb = SEQ // Q_BLOCK
    span = 3 * Q_BLOCK
    qi = lax.broadcasted_iota(I32, (Q_BLOCK, span), 0)
    kj = lax.broadcasted_iota(I32, (Q_BLOCK, span), 1)
    rel = kj - Q_BLOCK - qi
    valid = (jnp.abs(rel) <= WINDOW) & ((kj >= Q_BLOCK) | (i > 0)) & ((kj < 2 * Q_BLOCK) | (i < nb - 1))
    groups = GQA_HEADS // GQA_KV_HEADS
    dh = GQA_HEAD_DIM
    scale = dh ** -0.5
    for kvh in range(GQA_KV_HEADS):
        ksl = slice(kvh * dh, (kvh + 1) * dh)
        ks = jnp.concatenate([kp_ref[:, ksl], kc_ref[:, ksl], kn_ref[:, ksl]], axis=0)
        vs = jnp.concatenate([vp_ref[:, ksl], vc_ref[:, ksl], vn_ref[:, ksl]], axis=0)
        kx, vx = kx_ref[:, ksl], vx_ref[:, ksl]
        for g in range(groups):
            hidx = kvh * groups + g
            qsl = slice(hidx * dh, (hidx + 1) * dh)
            qh = q_ref[:, qsl]
            s_x = _dot_nt(qh, kx) * scale
            s_l = jnp.where(valid, _dot_nt(qh, ks) * scale, -jnp.inf)
            sink = sink_ref[hidx]
            m = jnp.maximum(jnp.maximum(jnp.max(s_x, axis=-1, keepdims=True), jnp.max(s_l, axis=-1, keepdims=True)), sink)
            p_x, p_l = jnp.exp(s_x - m), jnp.exp(s_l - m)
            den = jnp.sum(p_x, axis=-1, keepdims=True) + jnp.sum(p_l, axis=-1, keepdims=True) + jnp.exp(sink - m)
            o = _dot(p_x.astype(BF16), vx) + _dot(p_l.astype(BF16), vs)
            o_ref[:, qsl] = (o / den).astype(o_ref.dtype)


def _gqa_attention(qkv, sink):
    nb = SEQ // Q_BLOCK
    kcol, vcol = GQA_Q // GQA_KV, GQA_Q // GQA_KV + 1
    ctx_blk = 2 * SEQ // CTX_LEN

    def win(col, shift):
        return pl.BlockSpec((Q_BLOCK, GQA_KV), lambda b, i: (b * nb + jnp.clip(i + shift, 0, nb - 1), col))

    return pl.pallas_call(
        _gqa_kernel,
        out_shape=_SDS((N_LAT, GQA_Q), BF16),
        grid=(BATCH, nb),
        in_specs=[pl.BlockSpec(memory_space=pltpu.SMEM),
                  pl.BlockSpec((Q_BLOCK, GQA_Q), lambda b, i: (b * nb + i, 0)),
                  win(kcol, -1), win(kcol, 0), win(kcol, 1), win(vcol, -1), win(vcol, 0), win(vcol, 1),
                  pl.BlockSpec((CTX_LEN, GQA_KV), lambda b, i: (ctx_blk + b, kcol)),
                  pl.BlockSpec((CTX_LEN, GQA_KV), lambda b, i: (ctx_blk + b, vcol))],
        out_specs=pl.BlockSpec((Q_BLOCK, GQA_Q), lambda b, i: (b * nb + i, 0)),
        compiler_params=_cparams(("parallel", "arbitrary")),
        name="gqa_window_attn",
    )(sink.astype(F32), qkv, qkv, qkv, qkv, qkv, qkv, qkv, qkv, qkv)


def _router_kernel(u_ref, w_ref, idx_ref, wt_ref):
    logits = _dot(u_ref[...].astype(BF16), w_ref[...])
    lane = lax.broadcasted_iota(I32, logits.shape, 1)
    neg = -jnp.inf
    l0 = jnp.where(lane < N_EXPERTS, logits, neg)
    m1 = jnp.max(l0, axis=-1, keepdims=True)
    i1 = jnp.min(jnp.where(l0 == m1, lane, 128), axis=-1, keepdims=True)
    l1 = jnp.where(lane == i1, neg, l0)
    m2 = jnp.max(l1, axis=-1, keepdims=True)
    i2 = jnp.min(jnp.where(l1 == m2, lane, 128), axis=-1, keepdims=True)
    e = jnp.exp(m2 - m1)
    w1 = 1.0 / (1.0 + e)
    idx_ref[...] = jnp.where(lane == 0, i1, jnp.where(lane == 1, i2, 0))
    wt_ref[...] = jnp.where(lane == 0, w1, jnp.where(lane == 1, e * w1, 0.0))


def _router(u, w_router):
    n = u.shape[0]
    tm = ROW_TILE
    w = jnp.zeros((D_MODEL, 128), BF16).at[:, :N_EXPERTS].set(w_router.astype(BF16))
    return pl.pallas_call(
        _router_kernel,
        out_shape=(_SDS((n, 128), I32), _SDS((n, 128), F32)),
        grid=(n // tm,),
        in_specs=[pl.BlockSpec((tm, D_MODEL), lambda i: (i, 0)), pl.BlockSpec((D_MODEL, 128), lambda i: (0, 0))],
        out_specs=(pl.BlockSpec((tm, 128), lambda i: (i, 0)), pl.BlockSpec((tm, 128), lambda i: (i, 0))),
        compiler_params=_cparams(("parallel",)),
        name="moe_router",
    )(u, w)


GATHER_TILE = 256


def _row_copy(src_hbm, row, dst, r, sem):
    return pltpu.make_async_copy(src_hbm.at[pl.ds(row, 1), :], dst.at[pl.ds(r, 1), :], sem)


def _gather_kernel(tok_ref, tv_ref, u_hbm, valid_ref, o_ref, buf, sem):
    i = pl.program_id(0)
    base = i * GATHER_TILE

    @pl.when(tv_ref[i] > 0)
    def _():
        def issue(r, carry):
            _row_copy(u_hbm, tok_ref[base + r], buf, r, sem).start()
            return carry

        def drain(r, carry):
            _row_copy(u_hbm, 0, buf, r, sem).wait()
            return carry

        lax.fori_loop(0, GATHER_TILE, issue, 0, unroll=8)
        lax.fori_loop(0, GATHER_TILE, drain, 0, unroll=8)
        o_ref[...] = jnp.where(valid_ref[...] > 0.0, buf[...], 0.0).astype(o_ref.dtype)

    @pl.when(tv_ref[i] == 0)
    def _():
        o_ref[...] = jnp.zeros_like(o_ref)


def _moe_gather(slot_tok, slot_valid, u):
    n_slots = slot_tok.shape[0]
    n_tiles = n_slots // GATHER_TILE
    tile_valid = (jnp.max(slot_valid.reshape(n_tiles, GATHER_TILE), axis=1) > 0.0).astype(I32)
    return pl.pallas_call(
        _gather_kernel,
        out_shape=_SDS((n_slots, D_MODEL), BF16),
        grid_spec=pltpu.PrefetchScalarGridSpec(
            num_scalar_prefetch=2,
            grid=(n_tiles,),
            in_specs=[pl.BlockSpec(memory_space=pl.ANY),
                      pl.BlockSpec((GATHER_TILE, 1), lambda i, tok, tv: (i, 0))],
            out_specs=pl.BlockSpec((GATHER_TILE, D_MODEL), lambda i, tok, tv: (i, 0)),
            scratch_shapes=[pltpu.VMEM((GATHER_TILE, D_MODEL), F32), pltpu.SemaphoreType.DMA(())]),
        compiler_params=_cparams(("arbitrary",)),
        name="moe_gather",
    )(slot_tok, tile_valid, u, slot_valid.reshape(n_slots, 1))


def _moe_ffn_kernel(be_ref, bv_ref, x_ref, wg_ref, wu_ref, wd_ref, o_ref, h_ref):
    i, s = pl.program_id(0), pl.program_id(1)
    valid = bv_ref[i] > 0

    @pl.when(valid)
    def _():
        _swiglu_step(s, x_ref, wg_ref.at[0], wu_ref.at[0], wd_ref.at[0], o_ref, h_ref)

    @pl.when(jnp.logical_not(valid) & (s >= FF_STEPS))
    def _():
        o_ref[...] = jnp.zeros_like(o_ref)


def _moe_ffn(block_expert, block_valid, xs, w_gate, w_up, w_down):
    n_slots = xs.shape[0]
    tm = MOE_BLOCK

    def ff(i, s, be, bv):
        return (be[i], 0, jnp.where(bv[i] > 0, jnp.minimum(s, FF_STEPS - 1), FF_STEPS - 1))

    def down(i, s, be, bv):
        return (be[i], 0, jnp.where(bv[i] > 0, jnp.maximum(s - FF_STEPS, 0), OUT_STEPS - 1))

    return pl.pallas_call(
        _moe_ffn_kernel,
        out_shape=_SDS((n_slots, D_MODEL), F32),
        grid_spec=pltpu.PrefetchScalarGridSpec(
            num_scalar_prefetch=2,
            grid=(n_slots // tm, FF_STEPS + OUT_STEPS),
            in_specs=[pl.BlockSpec((tm, D_MODEL), lambda i, s, be, bv: (i, 0), pipeline_mode=pl.Buffered(1)),
                      pl.BlockSpec((1, D_MODEL, FF_TILE), ff),
                      pl.BlockSpec((1, D_MODEL, FF_TILE), ff),
                      pl.BlockSpec((1, D_FF, OUT_TILE), down)],
            out_specs=pl.BlockSpec((tm, OUT_TILE), lambda i, s, be, bv: (i, jnp.maximum(s - FF_STEPS, 0))),
            scratch_shapes=[pltpu.VMEM((tm, D_FF), BF16)]),
        compiler_params=_cparams(("arbitrary", "arbitrary"), VMEM_LIMIT),
        name="moe_grouped_swiglu",
    )(block_expert, block_valid, xs, w_gate, w_up, w_down)


def _combine_kernel(pos_ref, y_hbm, wt_ref, h_ref, gp_ref, gate_ref, o_ref, buf_a, buf_b, sem):
    base = pl.program_id(0) * GATHER_TILE

    def issue(r, carry):
        _row_copy(y_hbm, pos_ref[2 * (base + r)], buf_a, r, sem).start()
        _row_copy(y_hbm, pos_ref[2 * (base + r) + 1], buf_b, r, sem).start()
        return carry

    def drain(r, carry):
        _row_copy(y_hbm, 0, buf_a, r, sem).wait()
        _row_copy(y_hbm, 0, buf_b, r, sem).wait()
        return carry

    lax.fori_loop(0, GATHER_TILE, issue, 0, unroll=8)
    lax.fori_loop(0, GATHER_TILE, drain, 0, unroll=8)
    wt = wt_ref[...]
    y = wt[:, 0:1] * buf_a[...] + wt[:, 1:2] * buf_b[...]
    grp = _group_of_row(base)
    o_ref[...] = h_ref[...] + _mod_row(gate_ref, grp) * _rms(y, gp_ref[...])


def _moe_combine(pos, ys, wts, h, g_post, mod, gate_lk):
    n = wts.shape[0]
    tm = GATHER_TILE
    return pl.pallas_call(
        _combine_kernel,
        out_shape=_SDS((n, D_MODEL), F32),
        grid_spec=pltpu.PrefetchScalarGridSpec(
            num_scalar_prefetch=1,
            grid=(n // tm,),
            in_specs=[pl.BlockSpec(memory_space=pl.ANY),
                      pl.BlockSpec((tm, 128), lambda i, pos: (i, 0)),
                      pl.BlockSpec((tm, D_MODEL), lambda i, pos: (i, 0)),
                      pl.BlockSpec((1, D_MODEL), lambda i, pos: (0, 0)),
                      pl.BlockSpec((1, 8, D_MODEL), lambda i, pos: (gate_lk[0], 0, gate_lk[1]))],
            out_specs=pl.BlockSpec((tm, D_MODEL), lambda i, pos: (i, 0)),
            scratch_shapes=[pltpu.VMEM((tm, D_MODEL), F32), pltpu.VMEM((tm, D_MODEL), F32),
                            pltpu.SemaphoreType.DMA(())]),
        compiler_params=_cparams(("arbitrary",)),
        name="moe_combine",
    )(pos, ys, wts, h, g_post.reshape(1, D_MODEL), mod)


def _moe_routing(idx):
    flat_e = idx[:, :TOP_K].reshape(-1)
    n_assign = flat_e.shape[0]
    onehot = (flat_e[:, None] == jnp.arange(N_EXPERTS, dtype=I32)[None, :]).astype(I32)
    csum = jnp.cumsum(onehot, axis=0)
    counts = csum[-1]
    padded = (counts + MOE_BLOCK - 1) // MOE_BLOCK * MOE_BLOCK
    pad_end = jnp.cumsum(padded)
    pad_start = pad_end - padded
    pos = jnp.sum(onehot * (pad_start[None, :] + csum - 1), axis=1).astype(I32)
    slot_tok = jnp.zeros((MOE_SLOTS,), I32).at[pos].set(jnp.arange(n_assign, dtype=I32) // TOP_K)
    slot_valid = jnp.zeros((MOE_SLOTS,), F32).at[pos].set(1.0)
    blk_start = jnp.arange(MOE_SLOTS // MOE_BLOCK, dtype=I32) * MOE_BLOCK
    blk_valid = (blk_start < pad_end[-1]).astype(I32)
    last_start = jnp.maximum(pad_end[-1] - MOE_BLOCK, 0)
    blk_e = jnp.searchsorted(pad_end, jnp.minimum(blk_start, last_start), side='right').astype(I32)
    blk_e = jnp.minimum(blk_e, N_EXPERTS - 1)
    return pos, slot_tok, slot_valid, blk_e, blk_valid


def _rearrange_w_in(w_in):
    dn_in = DN_QKV + DN_W + 4 * DN_HEADS
    qkvz = w_in[:, :DN_QKV + DN_W]
    ba = w_in[:, DN_QKV + DN_W:dn_in]
    cq = w_in[:, dn_in:dn_in + MLA_Q_RANK]
    ckv = w_in[:, dn_in + MLA_Q_RANK:dn_in + MLA_Q_RANK + MLA_KV_RANK]
    k_rope = w_in[:, dn_in + MLA_Q_RANK + MLA_KV_RANK:]
    pad = jnp.zeros((D_MODEL, P_WIDTH - P_MISC - MLA_ROPE - 4 * DN_HEADS), w_in.dtype)
    return jnp.concatenate([qkvz, ckv, cq, k_rope, ba, pad], axis=1).astype(BF16)


def _rearrange_w_qb(w_qb):
    w = w_qb.reshape(MLA_Q_RANK, MLA_HEADS, MLA_NOPE + MLA_ROPE)
    w = jnp.pad(w, ((0, 0), (0, 0), (0, MLA_QK - MLA_NOPE - MLA_ROPE)))
    return w.reshape(MLA_Q_RANK, MLA_HEADS * MLA_QK).astype(BF16)


def kernel(x, c, ctx, c_ctx, ada_w, ada_b, norm_g, ab_w_in, dn_conv_w, dn_a_log, dn_dt_bias, dn_norm_g, mla_q_norm_g, mla_w_qb, mla_kv_norm_g, mla_w_kvb, ab_w_out, ffn_w_gate, ffn_w_up, ffn_w_down, gqa_w_qkv, gqa_b_qkv, gqa_sink, gqa_w_out, gqa_b_out, moe_w_router, moe_w_gate, moe_w_up, moe_w_down):
    assert x.shape == (BATCH, SEQ, D_MODEL) and ctx.shape == (BATCH, CTX_LEN, D_MODEL) and ada_w.shape[0] == 2
    h = jnp.concatenate([x.reshape(N_LAT, D_MODEL), ctx.reshape(N_CTX, D_MODEL)], axis=0)
    cvec = jnp.zeros((8, D_MODEL), F32).at[:BATCH].set(c).at[BATCH].set(c_ctx)
    mod = _ada(cvec, ada_w, ada_b)
    (cos1, sin1), (cos2, sin2) = _rope_tables()

    u = _modulate_in(h, norm_g[0, 0], mod, 0, 0, 1)
    p = _matmul(u, _rearrange_w_in(ab_w_in[0]), None, F32, MM_ROW_TILE, 512)
    o_f, o_b = _deltanet(_dn_conv(p, dn_conv_w[0]), _dn_gates(p, dn_a_log[0], dn_dt_bias[0]))
    q, k, v = _mla_proj(p, mla_q_norm_g[0], mla_kv_norm_g[0], _rearrange_w_qb(mla_w_qb[0]),
                        mla_w_kvb[0].astype(BF16), cos1, sin1)
    y = _ab_out(o_f, o_b, p, dn_norm_g[0], _mla_attention(q, k, v), ab_w_out[0])
    h, u = _post(h, y, norm_g[0, 1], norm_g[0, 2], mod, (0, 2), (0, 3), (0, 4), BF16)
    y = _ffn(u, ffn_w_gate[0], ffn_w_up[0], ffn_w_down[0])
    h, u = _post(h, y, norm_g[0, 3], norm_g[1, 0], mod, (0, 5), (1, 0), (1, 1), BF16)

    qkv = _gqa_qkv(u, gqa_w_qkv[0], gqa_b_qkv[0], cos2, sin2)
    o = _gqa_attention(qkv, gqa_sink[0])
    y = _matmul(o, gqa_w_out[0], gqa_b_out[0], F32, 1024, 512)
    h, u = _post(h, y, norm_g[1, 1], norm_g[1, 2], mod, (1, 2), (1, 3), (1, 4), F32)
    idx, wts = _router(u, moe_w_router[0])
    pos, slot_tok, slot_valid, blk_e, blk_valid = _moe_routing(idx)
    xs = _moe_gather(slot_tok, slot_valid, u)
    ys = _moe_ffn(blk_e, blk_valid, xs, moe_w_gate[0], moe_w_up[0], moe_w_down[0])
    out = _moe_combine(pos, ys, wts, h, norm_g[1, 3], mod, (1, 5))
    return out.reshape(BATCH, SEQ, D_MODEL)
```

```python
import functools
import math

import jax
import jax.numpy as jnp
from jax import lax
from jax.experimental import pallas as pl
from jax.experimental.pallas import tpu as pltpu

F32 = jnp.float32
BF16 = jnp.bfloat16
I32 = jnp.int32

D_MODEL = 2048
BATCH = 2
SEQ = 4096
CTX_LEN = 256
GRID_W = 64
EPS = 1e-6
ROPE_THETA = 10000.0

DN_HEADS = 8
DN_HEAD_DIM = 128
DN_CONV = 5
DN_CHUNK = 64
MLA_HEADS = 8
MLA_Q_RANK = 768
MLA_KV_RANK = 512
MLA_NOPE = 128
MLA_ROPE = 64
MLA_V = 128
GQA_HEADS = 32
GQA_KV_HEADS = 4
GQA_HEAD_DIM = 64
WINDOW = 128
Q_BLOCK = 128
D_FF = 7168
N_EXPERTS = 8
TOP_K = 2

DN_W = DN_HEADS * DN_HEAD_DIM
DN_QKV = 3 * DN_W
N_LAT = BATCH * SEQ
N_CTX = BATCH * CTX_LEN
N_TOK = N_LAT + N_CTX
GQA_Q = GQA_HEADS * GQA_HEAD_DIM
GQA_KV = GQA_KV_HEADS * GQA_HEAD_DIM

P_QKV = 0
P_Z = DN_QKV
P_CKV = P_Z + DN_W
P_CQ = P_CKV + MLA_KV_RANK
P_MISC = P_CQ + MLA_Q_RANK
P_WIDTH = 5632

ROW_TILE = 512
MM_ROW_TILE = 1088
MOE_BLOCK = 1024
MOE_SUB = 256
MOE_SLOTS = N_LAT * TOP_K + N_EXPERTS * MOE_BLOCK
VMEM_LIMIT = 56 * 1024 * 1024

_SDS = jax.ShapeDtypeStruct
_HIGH = lax.Precision.HIGHEST


def _cparams(sem, vmem=None):
    return pltpu.CompilerParams(dimension_semantics=sem, vmem_limit_bytes=vmem)


def _dot(a, b):
    return jnp.dot(a, b, preferred_element_type=F32)


def _dot_nt(a, b):
    return lax.dot_general(a, b, (((1,), (1,)), ((), ())), preferred_element_type=F32)


def _dot_tn(a, b):
    return lax.dot_general(a, b, (((0,), (0,)), ((), ())), preferred_element_type=F32)


def _dot_hi(a, b):
    return jnp.dot(a, b, preferred_element_type=F32, precision=_HIGH)


def _silu(x):
    return x * jax.nn.sigmoid(x)


def _rms(x, g):
    return x * lax.rsqrt(jnp.mean(x * x, axis=-1, keepdims=True) + EPS) * g


def _group_of_row(r0):
    return (r0 >= SEQ).astype(I32) + (r0 >= 2 * SEQ).astype(I32)


def _mod_spec(layer, k):
    return pl.BlockSpec((1, 8, D_MODEL), lambda *_: (layer, 0, k))


def _mod_row(ref, grp):
    return ref[0, pl.ds(grp, 1), :]


def _ada_kernel(c_ref, w_ref, b_ref, o_ref):
    s = _silu(c_ref[...]).astype(BF16)
    o_ref[0] = _dot(s, w_ref[0].astype(BF16)) + b_ref[0]


def _ada(cvec, ada_w, ada_b):
    n_layers, _, n = ada_w.shape
    tn = 1024
    return pl.pallas_call(
        _ada_kernel,
        out_shape=_SDS((n_layers, 8, n), F32),
        grid=(n_layers, n // tn),
        in_specs=[pl.BlockSpec((8, D_MODEL), lambda l, j: (0, 0)),
                  pl.BlockSpec((1, D_MODEL, tn), lambda l, j: (l, 0, j)),
                  pl.BlockSpec((1, 1, tn), lambda l, j: (l, 0, j))],
        out_specs=pl.BlockSpec((1, 8, tn), lambda l, j: (l, 0, j)),
        compiler_params=_cparams(("parallel", "parallel")),
        name="ada_mod",
    )(cvec, ada_w, ada_b.reshape(n_layers, 1, n))


_LAT_TILES = N_LAT // ROW_TILE
assert N_CTX == ROW_TILE


def _split_specs(width, col=0):
    lat = pl.BlockSpec((ROW_TILE, width), lambda i, *_: (jnp.minimum(i, _LAT_TILES - 1), col))
    ctx = pl.BlockSpec((ROW_TILE, width), lambda i, *_: (0, col))
    return [lat, ctx]


def _split_read(lat_ref, ctx_ref, sl=slice(None)):
    return jnp.where(pl.program_id(0) == _LAT_TILES, ctx_ref[:, sl], lat_ref[:, sl])


def _modin_kernel(x_ref, c_ref, g_ref, sh_ref, sc_ref, u_ref):
    grp = _group_of_row(pl.program_id(0) * ROW_TILE)
    h = _split_read(x_ref, c_ref)
    u = _rms(h, g_ref[...]) * (1.0 + _mod_row(sc_ref, grp)) + _mod_row(sh_ref, grp)
    u_ref[...] = u.astype(u_ref.dtype)


def _modulate_in(x, ctx, g, mod, layer, k_shift, k_scale):
    row = pl.BlockSpec((ROW_TILE, D_MODEL), lambda i: (i, 0))
    vec = pl.BlockSpec((1, D_MODEL), lambda i: (0, 0))
    return pl.pallas_call(
        _modin_kernel,
        out_shape=_SDS((N_TOK, D_MODEL), BF16),
        grid=(N_TOK // ROW_TILE,),
        in_specs=_split_specs(D_MODEL) + [vec, _mod_spec(layer, k_shift), _mod_spec(layer, k_scale)],
        out_specs=row,
        compiler_params=_cparams(("parallel",)),
        name="modulate_in",
    )(x, ctx, g.reshape(1, D_MODEL), mod, mod)


def _post_kernel(*refs, split):
    if split:
        x_ref, c_ref, y_ref, gp_ref, gate_ref, gn_ref, sh_ref, sc_ref, hn_ref, u_ref = refs
        h = _split_read(x_ref, c_ref)
    else:
        h_ref, y_ref, gp_ref, gate_ref, gn_ref, sh_ref, sc_ref, hn_ref, u_ref = refs
        h = h_ref[...]
    grp = _group_of_row(pl.program_id(0) * ROW_TILE)
    hn = h + _mod_row(gate_ref, grp) * _rms(y_ref[...], gp_ref[...])
    hn_ref[...] = hn
    u = _rms(hn, gn_ref[...]) * (1.0 + _mod_row(sc_ref, grp)) + _mod_row(sh_ref, grp)
    u_ref[...] = u.astype(u_ref.dtype)


def _post(h, y, g_post, g_next, mod, gate_lk, shift_lk, scale_lk, u_dtype):
    n = y.shape[0]
    split = isinstance(h, tuple)
    row = pl.BlockSpec((ROW_TILE, D_MODEL), lambda i: (i, 0))
    vec = pl.BlockSpec((1, D_MODEL), lambda i: (0, 0))
    h_specs, h_args = (_split_specs(D_MODEL), list(h)) if split else ([row], [h])
    return pl.pallas_call(
        functools.partial(_post_kernel, split=split),
        out_shape=(_SDS((n, D_MODEL), F32), _SDS((n, D_MODEL), u_dtype)),
        grid=(n // ROW_TILE,),
        in_specs=h_specs + [row, vec, _mod_spec(*gate_lk), vec, _mod_spec(*shift_lk), _mod_spec(*scale_lk)],
        out_specs=(row, row),
        compiler_params=_cparams(("parallel",)),
        name="residual_post",
    )(*h_args, y, g_post.reshape(1, D_MODEL), mod, g_next.reshape(1, D_MODEL), mod, mod)


def _mm_kernel(a_ref, w_ref, b_ref, o_ref):
    o_ref[...] = (_dot(a_ref[...], w_ref[...].astype(BF16)) + b_ref[...]).astype(o_ref.dtype)


def _matmul(a, w, bias, out_dtype, tm, tn):
    m, k = a.shape
    n = w.shape[1]
    if bias is None:
        bias = jnp.zeros((n,), F32)
    return pl.pallas_call(
        _mm_kernel,
        out_shape=_SDS((m, n), out_dtype),
        grid=(m // tm, n // tn),
        in_specs=[pl.BlockSpec((tm, k), lambda i, j: (i, 0)),
                  pl.BlockSpec((k, tn), lambda i, j: (0, j)),
                  pl.BlockSpec((1, tn), lambda i, j: (0, j))],
        out_specs=pl.BlockSpec((tm, tn), lambda i, j: (i, j)),
        compiler_params=_cparams(("parallel", "arbitrary"), VMEM_LIMIT),
        name="matmul",
    )(a, w, bias.reshape(1, n))


FF_TILE = 256
OUT_TILE = 256
FF_STEPS = D_FF // FF_TILE
OUT_STEPS = D_MODEL // OUT_TILE


def _swiglu_step(s, x_ref, wg_ref, wu_ref, wd_ref, o_ref, h_ref, rows=None, guard=True):
    total = x_ref.shape[0]
    rows = total if rows is None else rows

    @pl.when(guard & (s < FF_STEPS))
    def _():
        x = x_ref[0:rows, :]
        a = _dot(x, wg_ref[...].astype(BF16))
        b = _dot(x, wu_ref[...].astype(BF16))
        h_ref[0:rows, pl.ds(pl.multiple_of(s * FF_TILE, FF_TILE), FF_TILE)] = (_silu(a) * b).astype(BF16)

    @pl.when(guard & (s >= FF_STEPS))
    def _():
        o_ref[0:rows, :] = _dot(h_ref[0:rows, :], wd_ref[...].astype(BF16))
        if rows < total:
            o_ref[rows:total, :] = jnp.zeros((total - rows, o_ref.shape[1]), o_ref.dtype)


def _ffn_kernel(u_ref, wg_ref, wu_ref, wd_ref, o_ref, h_ref):
    _swiglu_step(pl.program_id(1), u_ref, wg_ref, wu_ref, wd_ref, o_ref, h_ref)


def _ffn(u, w_gate, w_up, w_down):
    m = u.shape[0]
    tm = MM_ROW_TILE

    def ff(i, s):
        return (0, jnp.minimum(s, FF_STEPS - 1))

    def out(s):
        return jnp.maximum(s - FF_STEPS, 0)

    return pl.pallas_call(
        _ffn_kernel,
        out_shape=_SDS((m, D_MODEL), F32),
        grid=(m // tm, FF_STEPS + OUT_STEPS),
        in_specs=[pl.BlockSpec((tm, D_MODEL), lambda i, s: (i, 0), pipeline_mode=pl.Buffered(1)),
                  pl.BlockSpec((D_MODEL, FF_TILE), ff),
                  pl.BlockSpec((D_MODEL, FF_TILE), ff),
                  pl.BlockSpec((D_FF, OUT_TILE), lambda i, s: (0, out(s)))],
        out_specs=pl.BlockSpec((tm, OUT_TILE), lambda i, s: (i, out(s))),
        scratch_shapes=[pltpu.VMEM((tm, D_FF), BF16)],
        compiler_params=_cparams(("parallel", "arbitrary"), VMEM_LIMIT),
        name="swiglu_ffn",
    )(u, w_gate, w_up, w_down)


CONV_TILE = 256
_SEQ_STARTS = (0, SEQ // CONV_TILE, 2 * SEQ // CONV_TILE, 2 * SEQ // CONV_TILE + 1)
_SEQ_LASTS = (SEQ // CONV_TILE - 1, 2 * SEQ // CONV_TILE - 1, 2 * SEQ // CONV_TILE, 2 * SEQ // CONV_TILE + 1)


def _conv_kernel(prev_ref, cur_ref, next_ref, w_ref, o_ref, buf):
    i = pl.program_id(0)
    j = pl.program_id(1)
    tm = CONV_TILE
    first = functools.reduce(jnp.logical_or, [i == s for s in _SEQ_STARTS])
    last = functools.reduce(jnp.logical_or, [i == s for s in _SEQ_LASTS])
    buf[0:8, :] = jnp.where(first, 0.0, prev_ref[...])
    buf[8:8 + tm, :] = cur_ref[...]
    buf[8 + tm:16 + tm, :] = jnp.where(last, 0.0, next_ref[...])
    pad = DN_CONV // 2
    acc = buf[8 - pad:8 - pad + tm, :] * w_ref[0:1, :]
    for d in range(1, DN_CONV):
        acc = acc + buf[8 - pad + d:8 - pad + d + tm, :] * w_ref[d:d + 1, :]
    x = _silu(acc)
    q_scale = jnp.where(j == 0, DN_HEAD_DIM ** -0.5, 1.0)
    for hh in range(DN_HEADS):
        xh = x[:, hh * DN_HEAD_DIM:(hh + 1) * DN_HEAD_DIM]
        inv = lax.rsqrt(jnp.sum(xh * xh, axis=-1, keepdims=True) + EPS) * q_scale
        o_ref[:, hh * DN_HEAD_DIM:(hh + 1) * DN_HEAD_DIM] = xh * jnp.where(j == 2, 1.0, inv)


def _dn_conv(p, conv_w):
    n = p.shape[0]
    tm = CONV_TILE
    r8 = tm // 8
    nb8 = n // 8
    return pl.pallas_call(
        _conv_kernel,
        out_shape=_SDS((n, DN_QKV), F32),
        grid=(n // tm, 3),
        in_specs=[pl.BlockSpec((8, DN_W), lambda i, j: (jnp.maximum(i * r8 - 1, 0), j)),
                  pl.BlockSpec((tm, DN_W), lambda i, j: (i, j)),
                  pl.BlockSpec((8, DN_W), lambda i, j: (jnp.minimum((i + 1) * r8, nb8 - 1), j)),
                  pl.BlockSpec((DN_CONV, DN_W), lambda i, j: (0, j))],
        out_specs=pl.BlockSpec((tm, DN_W), lambda i, j: (i, j)),
        scratch_shapes=[pltpu.VMEM((tm + 16, DN_W), F32)],
        compiler_params=_cparams(("parallel", "parallel")),
        name="dn_conv",
    )(p, p, p, conv_w)


def _gates_kernel(m_ref, a_ref, dt_ref, o_ref):
    x = pltpu.roll(m_ref[...], 64, 1)
    lane = lax.broadcasted_iota(I32, x.shape, 1)
    z = x + dt_ref[...]
    softplus = jnp.maximum(z, 0.0) + jnp.log1p(jnp.exp(-jnp.abs(z)))
    g = a_ref[...] * softplus
    o_ref[...] = jnp.where(lane < 2 * DN_HEADS, jax.nn.sigmoid(x), jnp.where(lane < 4 * DN_HEADS, g, 0.0))


def _dn_gates(p, a_log, dt_bias):
    n = p.shape[0]
    neg_a = jnp.zeros((1, 128), F32).at[0, 2 * DN_HEADS:4 * DN_HEADS].set(-jnp.exp(a_log.astype(F32)).reshape(-1))
    dtb = jnp.zeros((1, 128), F32).at[0, 2 * DN_HEADS:4 * DN_HEADS].set(dt_bias.astype(F32).reshape(-1))
    tm = ROW_TILE
    vec = pl.BlockSpec((1, 128), lambda i: (0, 0))
    return pl.pallas_call(
        _gates_kernel,
        out_shape=_SDS((n, 128), F32),
        grid=(n // tm,),
        in_specs=[pl.BlockSpec((tm, 128), lambda i: (i, P_MISC // 128)), vec, vec],
        out_specs=pl.BlockSpec((tm, 128), lambda i: (i, 0)),
        compiler_params=_cparams(("parallel",)),
        name="dn_gates",
    )(p, neg_a, dtb)


_CTX_CHUNKS = CTX_LEN // DN_CHUNK
_LAT_CHUNKS = SEQ // DN_CHUNK
_DN_STEPS = _CTX_CHUNKS + _LAT_CHUNKS


def _dn_fwd_block(b, t):
    return jnp.where(t < _CTX_CHUNKS, (2 * SEQ + b * CTX_LEN) // DN_CHUNK + t, b * _LAT_CHUNKS + (t - _CTX_CHUNKS))


def _dn_bwd_block(b, t):
    return jnp.where(t < _CTX_CHUNKS, (2 * SEQ + b * CTX_LEN) // DN_CHUNK + (_CTX_CHUNKS - 1 - t),
                     b * _LAT_CHUNKS + (_DN_STEPS - 1 - t))


DN_GROUP = 4
_GROUP_ROWS = DN_GROUP * DN_CHUNK
_GROUP_STATE = DN_GROUP * DN_HEAD_DIM


_DN_BASE = 8


def _block_diag(x):
    zero = jnp.zeros((DN_CHUNK, DN_HEAD_DIM), x.dtype)
    rows = []
    for hh in range(DN_GROUP):
        piece = x[hh * DN_CHUNK:(hh + 1) * DN_CHUNK]
        rows.append(jnp.concatenate([piece if j == hh else zero for j in range(DN_GROUP)], axis=1))
    return jnp.concatenate(rows, axis=0)


def _dn_groups(items, eye, base_mask):
    n = range(len(items))
    it = items
    gx = [jnp.broadcast_to(it[g]["gc"], (_GROUP_ROWS, _GROUP_ROWS)) for g in n]
    gamma = [jnp.where(it[g]["incl"], jnp.exp(jnp.where(it[g]["incl"], gx[g] - gx[g].T, 0.0)), 0.0) for g in n]
    kb = [it[g]["k"].astype(BF16) for g in n]
    kk = [_dot_nt(kb[g], kb[g]) for g in n]
    a = [jnp.where(it[g]["strict"], it[g]["beta"] * kk[g] * gamma[g], 0.0) for g in n]
    d = [jnp.where(base_mask, a[g], 0.0) for g in n]
    tinv = [eye - d[g] for g in n]
    pw = [d[g].astype(BF16) for g in n]
    for _ in range(int(math.log2(_DN_BASE)) - 1):
        pw = [_dot(pw[g], pw[g]).astype(BF16) for g in n]
        tinv = [tinv[g] + _dot(tinv[g].astype(BF16), pw[g]) for g in n]
    for lvl in range(len(it[0]["levels"])):
        tb = [tinv[g].astype(BF16) for g in n]
        mt = [_dot(jnp.where(it[g]["levels"][lvl], a[g], 0.0).astype(BF16), tb[g]).astype(BF16) for g in n]
        tinv = [tinv[g] - _dot(tb[g], mt[g]) for g in n]
    tb = [tinv[g].astype(BF16) for g in n]
    egc = [jnp.exp(it[g]["gc"]) for g in n]
    rhs = [jnp.concatenate([it[g]["v"] * it[g]["beta"], it[g]["k"] * (it[g]["beta"] * egc[g])], axis=1) for g in n]
    rhs_hi = [rhs[g].astype(BF16) for g in n]
    rhs_lo = [(rhs[g] - rhs_hi[g].astype(F32)).astype(BF16) for g in n]
    sol = [_dot(jnp.concatenate([tb[g], tb[g]], axis=1), jnp.concatenate([rhs_hi[g], rhs_lo[g]], axis=0)) for g in n]
    qk = [(_dot_nt(it[g]["q"].astype(BF16), kb[g]) * gamma[g]).astype(BF16) for g in n]
    s = [it[g]["s_ref"][...] for g in n]
    sb = [s[g].astype(BF16) for g in n]
    v_new = [sol[g][:, :DN_HEAD_DIM] - _dot(_block_diag(sol[g][:, DN_HEAD_DIM:].astype(BF16)), sb[g]) for g in n]
    vb = [v_new[g].astype(BF16) for g in n]
    o = [_dot(_block_diag((it[g]["q"] * egc[g]).astype(BF16)), sb[g]) + _dot(qk[g], vb[g]) for g in n]
    for g in n:
        k_dec = _block_diag((it[g]["k"] * jnp.exp(it[g]["g_last"] - it[g]["gc"])).astype(BF16))
        it[g]["s_ref"][...] = s[g] * jnp.exp(it[g]["g_last_state"]) + _dot_tn(k_dec, vb[g])
    return o


def _stack_cols(x, lanes, rows_each):
    pieces = [x[:, l:l + 1] for l in lanes]
    if rows_each is not None:
        pieces = [jnp.broadcast_to(pc, (rows_each, 1)) for pc in pieces]
    return jnp.concatenate(pieces, axis=0)


def _dn_kernel(*refs):
    n_in = BATCH * 2 * 4
    in_refs, (ofl_ref, ofc_ref, obl_ref, obc_ref), (s_ref, o_scr) = refs[:n_in], refs[n_in:n_in + 4], refs[n_in + 4:]
    t = pl.program_id(0)

    @pl.when(t == 0)
    def _():
        s_ref[...] = jnp.zeros_like(s_ref)

    c = DN_CHUNK
    ri = lax.broadcasted_iota(I32, (_GROUP_ROWS, _GROUP_ROWS), 0)
    ci = lax.broadcasted_iota(I32, (_GROUP_ROWS, _GROUP_ROWS), 1)
    same_head = (ri // c) == (ci // c)
    eye = (ri == ci).astype(F32)
    r1 = lax.broadcasted_iota(I32, (c, c), 0)
    c1 = lax.broadcasted_iota(I32, (c, c), 1)
    base_mask = (ri // _DN_BASE) == (ci // _DN_BASE)
    masks = []
    for d in range(2):
        incl = same_head & ((ri >= ci) if d == 0 else (ri <= ci))
        strict = same_head & ((ri > ci) if d == 0 else (ri < ci))
        levels = []
        size = _DN_BASE
        while size < c:
            lo_blk, hi_blk = ((ci, ri) if d == 0 else (ri, ci))
            levels.append(((ri // (2 * size)) == (ci // (2 * size)))
                          & ((lo_blk // size) % 2 == 0) & ((hi_blk // size) % 2 == 1))
            size *= 2
        masks.append((incl, strict, levels))
    items, where = [], []
    for b, d in [(b, d) for b in range(BATCH) for d in range(2)]:
        q_ref, k_ref, v_ref, g_ref = in_refs[(b * 2 + d) * 4:(b * 2 + d) * 4 + 4]
        incl, strict, levels = masks[d]
        gates = g_ref[...]
        tri = ((r1 >= c1) if d == 0 else (r1 <= c1)).astype(BF16)
        g_hi = gates.astype(BF16)
        g_r1 = gates - g_hi.astype(F32)
        g_mid = g_r1.astype(BF16)
        g_lo = (g_r1 - g_mid.astype(F32)).astype(BF16)
        gc = _dot(tri, g_hi) + _dot(tri, g_mid) + _dot(tri, g_lo)
        last_row = c - 1 if d == 0 else 0
        tot = gc[last_row:last_row + 1, :]
        for grp in range(DN_HEADS // DN_GROUP):
            heads = range(grp * DN_GROUP, (grp + 1) * DN_GROUP)
            lanes_b = [d * DN_HEADS + hh for hh in heads]
            lanes_g = [2 * DN_HEADS + lb for lb in lanes_b]

            def stack(ref):
                return jnp.concatenate([ref[:, hh * DN_HEAD_DIM:(hh + 1) * DN_HEAD_DIM] for hh in heads], axis=0)

            items.append(dict(q=stack(q_ref), k=stack(k_ref), v=stack(v_ref),
                              beta=_stack_cols(gates, lanes_b, None), gc=_stack_cols(gc, lanes_g, None),
                              g_last=_stack_cols(tot, lanes_g, c), g_last_state=_stack_cols(tot, lanes_g, DN_HEAD_DIM),
                              incl=incl, strict=strict, levels=levels, s_ref=s_ref.at[b, d, grp]))
            where.append((d, b, heads))
    outs = _dn_groups(items, eye, base_mask)
    for o, (d, b, heads) in zip(outs, where):
        for j, hh in enumerate(heads):
            o_scr[d, b, :, hh * DN_HEAD_DIM:(hh + 1) * DN_HEAD_DIM] = o[j * c:(j + 1) * c]

    @pl.when(t < _CTX_CHUNKS)
    def _():
        ofc_ref[...] = o_scr[0]
        obc_ref[...] = o_scr[1]

    @pl.when(t >= _CTX_CHUNKS)
    def _():
        ofl_ref[...] = o_scr[0]
        obl_ref[...] = o_scr[1]


def _deltanet(qkv, gates):
    c = DN_CHUNK
    ins, args = [], []
    for b in range(BATCH):
        for fn in (_dn_fwd_block, _dn_bwd_block):
            for width, col, arr in ((DN_W, 0, qkv), (DN_W, 1, qkv), (DN_W, 2, qkv), (128, 0, gates)):
                ins.append(pl.BlockSpec((c, width), functools.partial(lambda t, b, fn, col: (fn(b, t), col),
                                                                      b=b, fn=fn, col=col)))
                args.append(arr)
    last = _DN_STEPS - 1
    lat = _SDS((BATCH, SEQ, DN_W), F32)
    ctx = _SDS((BATCH, CTX_LEN, DN_W), F32)
    blk = (BATCH, c, DN_W)
    outs = pl.pallas_call(
        _dn_kernel,
        out_shape=(lat, ctx, lat, ctx),
        grid=(_DN_STEPS,),
        in_specs=ins,
        out_specs=(pl.BlockSpec(blk, lambda t: (0, jnp.maximum(t - _CTX_CHUNKS, 0), 0)),
                   pl.BlockSpec(blk, lambda t: (0, jnp.minimum(t, _CTX_CHUNKS - 1), 0)),
                   pl.BlockSpec(blk, lambda t: (0, jnp.minimum(last - t, _LAT_CHUNKS - 1), 0)),
                   pl.BlockSpec(blk, lambda t: (0, jnp.maximum(_CTX_CHUNKS - 1 - t, 0), 0))),
        scratch_shapes=[pltpu.VMEM((BATCH, 2, DN_HEADS // DN_GROUP, _GROUP_STATE, DN_HEAD_DIM), F32),
                        pltpu.VMEM((2, BATCH, c, DN_W), F32)],
        compiler_params=_cparams(("arbitrary",)),
        name="deltanet",
    )(*args)
    ofl, ofc, obl, obc = outs
    return ((ofl.reshape(N_LAT, DN_W), ofc.reshape(N_CTX, DN_W)),
            (obl.reshape(N_LAT, DN_W), obc.reshape(N_CTX, DN_W)))


def _rope_tables():
    t = jnp.arange(SEQ)
    row = (t // GRID_W).astype(F32)
    col = (t % GRID_W).astype(F32)
    n_freq = MLA_ROPE // 4
    inv_freq = ROPE_THETA ** (-jnp.arange(n_freq, dtype=F32) / n_freq)
    ang = jnp.concatenate([row[:, None] * inv_freq, col[:, None] * inv_freq], axis=-1)
    cos, sin = jnp.cos(ang), jnp.sin(ang)
    cos64 = jnp.concatenate([cos, cos], axis=-1)
    sin64 = jnp.concatenate([-sin, sin], axis=-1)

    def assemble(lat_cos, lat_sin):
        c = jnp.concatenate([jnp.tile(lat_cos, (BATCH, 1)), jnp.ones((N_CTX, 128), F32)], axis=0)
        s = jnp.concatenate([jnp.tile(lat_sin, (BATCH, 1)), jnp.zeros((N_CTX, 128), F32)], axis=0)
        return c, s

    cos_l = jnp.concatenate([cos64, jnp.ones((SEQ, 64), F32)], axis=-1)
    sin_l = jnp.concatenate([sin64, jnp.zeros((SEQ, 64), F32)], axis=-1)
    one_head = assemble(cos_l, sin_l)
    two_heads = assemble(jnp.tile(cos64, (1, 2)), jnp.tile(sin64, (1, 2)))
    return one_head, two_heads


def _rope128(x, cosv, sinv):
    lane = lax.broadcasted_iota(I32, x.shape, 1)
    swapped = jnp.where((lane % 64) < 32, pltpu.roll(x, 96, 1), pltpu.roll(x, 32, 1))
    return x * cosv + swapped * sinv


MLA_QK = 256
MLA_VX = 256


def _mla_proj_kernel(ckv_ref, cq_ref, misc_ref, gq_ref, gkv_ref, wq_ref, wkv_ref, cos_ref, sin_ref,
                     q_ref, k_ref, v_ref):
    cq = _rms(cq_ref[...], gq_ref[...]).astype(BF16)
    ckv = _rms(ckv_ref[...], gkv_ref[...]).astype(BF16)
    qf = _dot(cq, wq_ref[...])
    kvf = _dot(ckv, wkv_ref[...])
    cosv, sinv = cos_ref[...], sin_ref[...]
    lane = lax.broadcasted_iota(I32, cosv.shape, 1)
    k_rope = _rope128(jnp.where(lane < MLA_ROPE, misc_ref[...], 0.0), cosv, sinv).astype(BF16)
    scale = (MLA_NOPE + MLA_ROPE) ** -0.5
    for hh in range(MLA_HEADS):
        lo, mid, hi = hh * MLA_QK, hh * MLA_QK + 128, (hh + 1) * MLA_QK
        q_ref[:, lo:mid] = (qf[:, lo:mid] * scale).astype(BF16)
        q_ref[:, mid:hi] = (_rope128(qf[:, mid:hi], cosv, sinv) * scale).astype(BF16)
        k_ref[:, lo:mid] = kvf[:, lo:mid].astype(BF16)
        k_ref[:, mid:hi] = k_rope
        v_ref[:, lo:mid] = kvf[:, mid:hi].astype(BF16)
        v_ref[:, mid:hi] = jnp.ones((kvf.shape[0], MLA_VX - MLA_V), BF16)


def _mla_proj(p, q_norm_g, kv_norm_g, w_q, w_kv, cosv, sinv):
    n = p.shape[0]
    tm = ROW_TILE
    wide = MLA_HEADS * MLA_QK

    def rows(width, col):
        return pl.BlockSpec((tm, width), lambda i: (i, col))

    def whole(shape):
        return pl.BlockSpec(shape, lambda i: (0, 0))

    return pl.pallas_call(
        _mla_proj_kernel,
        out_shape=(_SDS((n, wide), BF16), _SDS((n, wide), BF16), _SDS((n, MLA_HEADS * MLA_VX), BF16)),
        grid=(n // tm,),
        in_specs=[rows(MLA_KV_RANK, P_CKV // MLA_KV_RANK), rows(MLA_Q_RANK, P_CQ // MLA_Q_RANK),
                  rows(128, P_MISC // 128), whole((1, MLA_Q_RANK)), whole((1, MLA_KV_RANK)),
                  whole((MLA_Q_RANK, wide)), whole((MLA_KV_RANK, wide)), rows(128, 0), rows(128, 0)],
        out_specs=(rows(wide, 0), rows(wide, 0), rows(MLA_HEADS * MLA_VX, 0)),
        compiler_params=_cparams(("parallel",), VMEM_LIMIT),
        name="mla_proj",
    )(p, p, p, q_norm_g.reshape(1, -1), kv_norm_g.reshape(1, -1), w_q, w_kv, cosv, sinv)


MLA_KEY_CHUNK = 1024


def _flash_kernel(*refs, chunks):
    n_seg = (len(refs) - 2) // 2
    q_ref, o_ref = refs[0], refs[-1]
    k_refs, v_refs = refs[1:1 + n_seg], refs[1 + n_seg:1 + 2 * n_seg]
    q = q_ref[...]

    def scores(c):
        seg, off, size = c
        return _dot_nt(q, k_refs[seg][off:off + size, :])

    m = jnp.full((q.shape[0], 1), -jnp.inf, F32)
    acc = jnp.zeros((q.shape[0], MLA_VX), F32)
    s_next = scores(chunks[0])
    for j, (seg, off, size) in enumerate(chunks):
        s = s_next
        if j + 1 < len(chunks):
            s_next = scores(chunks[j + 1])
        m_new = jnp.maximum(m, jnp.max(s, axis=-1, keepdims=True))
        p = jnp.exp(s - m_new).astype(BF16)
        acc = jnp.exp(m - m_new) * acc + _dot(p, v_refs[seg][off:off + size, :])
        m = m_new
    o_ref[...] = (acc[:, :MLA_V] / acc[:, MLA_V:]).astype(o_ref.dtype)


def _mla_attention(q, k, v):
    tq = 512
    nq = SEQ // tq
    ctx_blk = 2 * SEQ // CTX_LEN
    lat_chunks = ((0, 0, CTX_LEN),) + tuple((1, off, MLA_KEY_CHUNK) for off in range(0, SEQ, MLA_KEY_CHUNK))
    lat = pl.pallas_call(
        functools.partial(_flash_kernel, chunks=lat_chunks),
        out_shape=_SDS((N_LAT, MLA_HEADS * MLA_V), BF16),
        grid=(BATCH, MLA_HEADS, nq),
        in_specs=[pl.BlockSpec((tq, MLA_QK), lambda b, h, i: (b * nq + i, h)),
                  pl.BlockSpec((CTX_LEN, MLA_QK), lambda b, h, i: (ctx_blk + b, h)),
                  pl.BlockSpec((SEQ, MLA_QK), lambda b, h, i: (b, h)),
                  pl.BlockSpec((CTX_LEN, MLA_VX), lambda b, h, i: (ctx_blk + b, h)),
                  pl.BlockSpec((SEQ, MLA_VX), lambda b, h, i: (b, h))],
        out_specs=pl.BlockSpec((tq, MLA_V), lambda b, h, i: (b * nq + i, h)),
        compiler_params=_cparams(("parallel", "parallel", "arbitrary"), VMEM_LIMIT),
        name="mla_attn_latent",
    )(q, k, k, v, v)
    ctx = pl.pallas_call(
        functools.partial(_flash_kernel, chunks=((0, 0, CTX_LEN),)),
        out_shape=_SDS((N_CTX, MLA_HEADS * MLA_V), BF16),
        grid=(BATCH, MLA_HEADS),
        in_specs=[pl.BlockSpec((CTX_LEN, MLA_QK), lambda b, h: (ctx_blk + b, h)),
                  pl.BlockSpec((CTX_LEN, MLA_QK), lambda b, h: (ctx_blk + b, h)),
                  pl.BlockSpec((CTX_LEN, MLA_VX), lambda b, h: (ctx_blk + b, h))],
        out_specs=pl.BlockSpec((CTX_LEN, MLA_V), lambda b, h: (b, h)),
        compiler_params=_cparams(("parallel", "parallel")),
        name="mla_attn_context",
    )(q, k, v)
    return lat, ctx


def _ab_out_kernel(ofl_ref, ofc_ref, obl_ref, obc_ref, ml_ref, mc_ref, z_ref, g_ref, w_ref, o_ref, a_ref):
    @pl.when(pl.program_id(1) == 0)
    def _():
        for hh in range(DN_HEADS):
            sl = slice(hh * DN_HEAD_DIM, (hh + 1) * DN_HEAD_DIM)
            o = _split_read(ofl_ref, ofc_ref, sl) + _split_read(obl_ref, obc_ref, sl)
            a_ref[:, sl] = (_rms(o, g_ref[...]) * _silu(z_ref[:, sl])).astype(BF16)
        a_ref[:, DN_W:] = _split_read(ml_ref, mc_ref)

    o_ref[...] = _dot(a_ref[...], w_ref[...].astype(BF16))


def _ab_out(o_f, o_b, p, dn_norm_g, mla_o, w_out):
    tm, tn = ROW_TILE, 512
    half = DN_W
    return pl.pallas_call(
        _ab_out_kernel,
        out_shape=_SDS((N_TOK, D_MODEL), F32),
        grid=(N_TOK // tm, D_MODEL // tn),
        in_specs=_split_specs(half) + _split_specs(half) + _split_specs(half) + [
            pl.BlockSpec((tm, half), lambda i, j: (i, P_Z // half)),
            pl.BlockSpec((1, DN_HEAD_DIM), lambda i, j: (0, 0)),
            pl.BlockSpec((2 * half, tn), lambda i, j: (0, j))],
        out_specs=pl.BlockSpec((tm, tn), lambda i, j: (i, j)),
        scratch_shapes=[pltpu.VMEM((tm, 2 * half), BF16)],
        compiler_params=_cparams(("parallel", "arbitrary"), VMEM_LIMIT),
        name="ab_out_proj",
    )(*o_f, *o_b, *mla_o, p, dn_norm_g.reshape(1, -1), w_out)


def _mm_rope_kernel(a_ref, w_ref, b_ref, cos_ref, sin_ref, o_ref, *, tn, rope_cols):
    y = _dot(a_ref[...], w_ref[...].astype(BF16)) + b_ref[...]
    col0 = pl.program_id(1) * tn
    cosv, sinv = cos_ref[...], sin_ref[...]
    for s in range(tn // 128):
        ys = y[:, s * 128:(s + 1) * 128]
        roped = _rope128(ys, cosv, sinv)
        o_ref[:, s * 128:(s + 1) * 128] = jnp.where(col0 + s * 128 < rope_cols, roped, ys).astype(o_ref.dtype)


def _gqa_qkv(u, w, bias, cosv, sinv):
    m, k = u.shape
    n = w.shape[1]
    tm, tn = MM_ROW_TILE, 512
    return pl.pallas_call(
        functools.partial(_mm_rope_kernel, tn=tn, rope_cols=GQA_Q + GQA_KV),
        out_shape=_SDS((m, n), BF16),
        grid=(m // tm, n // tn),
        in_specs=[pl.BlockSpec((tm, k), lambda i, j: (i, 0)),
                  pl.BlockSpec((k, tn), lambda i, j: (0, j)),
                  pl.BlockSpec((1, tn), lambda i, j: (0, j)),
                  pl.BlockSpec((tm, 128), lambda i, j: (i, 0)),
                  pl.BlockSpec((tm, 128), lambda i, j: (i, 0))],
        out_specs=pl.BlockSpec((tm, tn), lambda i, j: (i, j)),
        compiler_params=_cparams(("parallel", "arbitrary"), VMEM_LIMIT),
        name="gqa_qkv_rope",
    )(u, w, bias.reshape(1, n), cosv, sinv)


def _gqa_kernel(sink_ref, q_ref, kp_ref, kc_ref, kn_ref, vp_ref, vc_ref, vn_ref, kx_ref, vx_ref, o_ref):
    i = pl.program_id(1)
    nb = SEQ // Q_BLOCK
    n_keys = CTX_LEN + 3 * Q_BLOCK
    groups = GQA_HEADS // GQA_KV_HEADS
    slabs = groups // 2
    dh = GQA_HEAD_DIM
    qi = lax.broadcasted_iota(I32, (Q_BLOCK, n_keys), 0)
    kj = lax.broadcasted_iota(I32, (Q_BLOCK, n_keys), 1) - CTX_LEN
    rel = kj - Q_BLOCK - qi
    valid = (kj < 0) | ((jnp.abs(rel) <= WINDOW) & ((kj >= Q_BLOCK) | (i > 0)) & ((kj < 2 * Q_BLOCK) | (i < nb - 1)))
    bias = jnp.where(valid, 0.0, -jnp.inf).astype(F32)
    bias = jnp.concatenate([bias] * slabs, axis=0)
    k_all = jnp.concatenate([kx_ref[...], kp_ref[...], kc_ref[...], kn_ref[...]], axis=0)
    v_all = jnp.concatenate([vx_ref[...], vp_ref[...], vc_ref[...], vn_ref[...]], axis=0)
    low = lax.broadcasted_iota(I32, (n_keys, 128), 1) < dh

    def block_diag(x_all, kvh):
        pair = x_all[:, (kvh // 2) * 128:(kvh // 2 + 1) * 128].astype(F32)
        other = pltpu.roll(pair, dh, 1)
        first, second = (pair, other) if kvh % 2 == 0 else (other, pair)
        return jnp.concatenate([jnp.where(low, first, 0.0), jnp.where(low, 0.0, second)], axis=0).astype(BF16)

    def scores(kvh):
        q = jnp.concatenate([q_ref[:, (kvh * slabs + r) * 128:(kvh * slabs + r + 1) * 128] for r in range(slabs)], axis=0)
        return _dot_nt(q * (dh ** -0.5), block_diag(k_all, kvh))

    def softmax(kvh, s):
        ps, dens = [], []
        for half in range(2):
            sh = s[:, half * n_keys:(half + 1) * n_keys] + bias
            sink = jnp.concatenate([jnp.full((Q_BLOCK, 1), sink_ref[(kvh * slabs + r) * 2 + half], F32)
                                    for r in range(slabs)], axis=0)
            m = jnp.maximum(jnp.max(sh, axis=-1, keepdims=True), sink)
            p = jnp.exp(sh - m)
            dens.append(jnp.sum(p, axis=-1, keepdims=True) + jnp.exp(sink - m))
            ps.append(p.astype(BF16))
        return jnp.concatenate(ps, axis=1), dens

    def finish(kvh, p, dens):
        o = _dot(p, block_diag(v_all, kvh))
        lane_low = lax.broadcasted_iota(I32, o.shape, 1) < dh
        o = o * jnp.where(lane_low, 1.0 / dens[0], 1.0 / dens[1])
        for r in range(slabs):
            o_ref[:, (kvh * slabs + r) * 128:(kvh * slabs + r + 1) * 128] = o[r * Q_BLOCK:(r + 1) * Q_BLOCK].astype(o_ref.dtype)

    s_next = scores(0)
    for kvh in range(GQA_KV_HEADS):
        s_cur = s_next
        if kvh + 1 < GQA_KV_HEADS:
            s_next = scores(kvh + 1)
        p, dens = softmax(kvh, s_cur)
        finish(kvh, p, dens)


def _gqa_attention(qkv, sink):
    nb = SEQ // Q_BLOCK
    kcol, vcol = GQA_Q // GQA_KV, GQA_Q // GQA_KV + 1
    ctx_blk = 2 * SEQ // CTX_LEN

    def win(col, shift):
        return pl.BlockSpec((Q_BLOCK, GQA_KV), lambda b, i: (b * nb + jnp.clip(i + shift, 0, nb - 1), col))

    return pl.pallas_call(
        _gqa_kernel,
        out_shape=_SDS((N_LAT, GQA_Q), BF16),
        grid=(BATCH, nb),
        in_specs=[pl.BlockSpec(memory_space=pltpu.SMEM),
                  pl.BlockSpec((Q_BLOCK, GQA_Q), lambda b, i: (b * nb + i, 0)),
                  win(kcol, -1), win(kcol, 0), win(kcol, 1), win(vcol, -1), win(vcol, 0), win(vcol, 1),
                  pl.BlockSpec((CTX_LEN, GQA_KV), lambda b, i: (ctx_blk + b, kcol)),
                  pl.BlockSpec((CTX_LEN, GQA_KV), lambda b, i: (ctx_blk + b, vcol))],
        out_specs=pl.BlockSpec((Q_BLOCK, GQA_Q), lambda b, i: (b * nb + i, 0)),
        compiler_params=_cparams(("parallel", "arbitrary")),
        name="gqa_window_attn",
    )(sink.astype(F32), qkv, qkv, qkv, qkv, qkv, qkv, qkv, qkv, qkv)


def _router_kernel(u_ref, w_ref, idx_ref, wt_ref):
    logits = _dot(u_ref[...].astype(BF16), w_ref[...])
    lane = lax.broadcasted_iota(I32, logits.shape, 1)
    neg = -jnp.inf
    l0 = jnp.where(lane < N_EXPERTS, logits, neg)
    m1 = jnp.max(l0, axis=-1, keepdims=True)
    i1 = jnp.min(jnp.where(l0 == m1, lane, 128), axis=-1, keepdims=True)
    l1 = jnp.where(lane == i1, neg, l0)
    m2 = jnp.max(l1, axis=-1, keepdims=True)
    i2 = jnp.min(jnp.where(l1 == m2, lane, 128), axis=-1, keepdims=True)
    e = jnp.exp(m2 - m1)
    w1 = 1.0 / (1.0 + e)
    idx_ref[...] = jnp.where(lane == 0, i1, jnp.where(lane == 1, i2, 0))
    wt_ref[...] = jnp.where(lane == 0, w1, jnp.where(lane == 1, e * w1, 0.0))


def _router(u, w_router):
    n = u.shape[0]
    tm = ROW_TILE
    w = jnp.zeros((D_MODEL, 128), BF16).at[:, :N_EXPERTS].set(w_router.astype(BF16))
    return pl.pallas_call(
        _router_kernel,
        out_shape=(_SDS((n, 128), I32), _SDS((n, 128), F32)),
        grid=(n // tm,),
        in_specs=[pl.BlockSpec((tm, D_MODEL), lambda i: (i, 0)), pl.BlockSpec((D_MODEL, 128), lambda i: (0, 0))],
        out_specs=(pl.BlockSpec((tm, 128), lambda i: (i, 0)), pl.BlockSpec((tm, 128), lambda i: (i, 0))),
        compiler_params=_cparams(("parallel",)),
        name="moe_router",
    )(u, w)


GATHER_TILE = 256


def _row_copy(src_hbm, row, dst, r, sem):
    return pltpu.make_async_copy(src_hbm.at[pl.ds(row, 1), :], dst.at[pl.ds(r, 1), :], sem)


def _gather_kernel(tok_ref, tv_ref, u_hbm, valid_ref, o_ref, buf, sem):
    i = pl.program_id(0)
    base = i * GATHER_TILE

    @pl.when(tv_ref[i] > 0)
    def _():
        def issue(r, carry):
            _row_copy(u_hbm, tok_ref[base + r], buf, r, sem).start()
            return carry

        def drain(r, carry):
            _row_copy(u_hbm, 0, buf, r, sem).wait()
            return carry

        lax.fori_loop(0, GATHER_TILE, issue, 0, unroll=8)
        lax.fori_loop(0, GATHER_TILE, drain, 0, unroll=8)
        o_ref[...] = jnp.where(valid_ref[...] > 0.0, buf[...], 0.0).astype(o_ref.dtype)

    @pl.when(tv_ref[i] == 0)
    def _():
        o_ref[...] = jnp.zeros_like(o_ref)


def _moe_gather(slot_tok, slot_valid, u):
    n_slots = slot_tok.shape[0]
    n_tiles = n_slots // GATHER_TILE
    tile_valid = (jnp.max(slot_valid.reshape(n_tiles, GATHER_TILE), axis=1) > 0.0).astype(I32)
    return pl.pallas_call(
        _gather_kernel,
        out_shape=_SDS((n_slots, D_MODEL), BF16),
        grid_spec=pltpu.PrefetchScalarGridSpec(
            num_scalar_prefetch=2,
            grid=(n_tiles,),
            in_specs=[pl.BlockSpec(memory_space=pl.ANY),
                      pl.BlockSpec((GATHER_TILE, 1), lambda i, tok, tv: (i, 0))],
            out_specs=pl.BlockSpec((GATHER_TILE, D_MODEL), lambda i, tok, tv: (i, 0)),
            scratch_shapes=[pltpu.VMEM((GATHER_TILE, D_MODEL), F32), pltpu.SemaphoreType.DMA(())]),
        compiler_params=_cparams(("arbitrary",)),
        name="moe_gather",
    )(slot_tok, tile_valid, u, slot_valid.reshape(n_slots, 1))


def _moe_ffn_kernel(be_ref, bv_ref, x_ref, wg_ref, wu_ref, wd_ref, o_ref, h_ref):
    i, s = pl.program_id(0), pl.program_id(1)
    n_sub = bv_ref[i]
    for k in range(1, MOE_BLOCK // MOE_SUB + 1):
        _swiglu_step(s, x_ref, wg_ref.at[0], wu_ref.at[0], wd_ref.at[0], o_ref, h_ref,
                     rows=k * MOE_SUB, guard=n_sub == k)

    @pl.when((n_sub == 0) & (s >= FF_STEPS))
    def _():
        o_ref[...] = jnp.zeros_like(o_ref)


def _moe_ffn(block_expert, block_valid, xs, w_gate, w_up, w_down):
    n_slots = xs.shape[0]
    tm = MOE_BLOCK

    def ff(i, s, be, bv):
        return (be[i], 0, jnp.where(bv[i] > 0, jnp.minimum(s, FF_STEPS - 1), FF_STEPS - 1))

    def down(i, s, be, bv):
        return (be[i], 0, jnp.where(bv[i] > 0, jnp.maximum(s - FF_STEPS, 0), OUT_STEPS - 1))

    return pl.pallas_call(
        _moe_ffn_kernel,
        out_shape=_SDS((n_slots, D_MODEL), F32),
        grid_spec=pltpu.PrefetchScalarGridSpec(
            num_scalar_prefetch=2,
            grid=(n_slots // tm, FF_STEPS + OUT_STEPS),
            in_specs=[pl.BlockSpec((tm, D_MODEL), lambda i, s, be, bv: (i, 0), pipeline_mode=pl.Buffered(1)),
                      pl.BlockSpec((1, D_MODEL, FF_TILE), ff),
                      pl.BlockSpec((1, D_MODEL, FF_TILE), ff),
                      pl.BlockSpec((1, D_FF, OUT_TILE), down)],
            out_specs=pl.BlockSpec((tm, OUT_TILE), lambda i, s, be, bv: (i, jnp.maximum(s - FF_STEPS, 0))),
            scratch_shapes=[pltpu.VMEM((tm, D_FF), BF16)]),
        compiler_params=_cparams(("arbitrary", "arbitrary"), VMEM_LIMIT),
        name="moe_grouped_swiglu",
    )(block_expert, block_valid, xs, w_gate, w_up, w_down)


def _combine_kernel(pos_ref, y_hbm, wt_ref, h_ref, gp_ref, gate_ref, o_ref, buf_a, buf_b, sem):
    base = pl.program_id(0) * GATHER_TILE

    def issue(r, carry):
        _row_copy(y_hbm, pos_ref[2 * (base + r)], buf_a, r, sem).start()
        _row_copy(y_hbm, pos_ref[2 * (base + r) + 1], buf_b, r, sem).start()
        return carry

    def drain(r, carry):
        _row_copy(y_hbm, 0, buf_a, r, sem).wait()
        _row_copy(y_hbm, 0, buf_b, r, sem).wait()
        return carry

    lax.fori_loop(0, GATHER_TILE, issue, 0, unroll=8)
    lax.fori_loop(0, GATHER_TILE, drain, 0, unroll=8)
    wt = wt_ref[...]
    y = wt[:, 0:1] * buf_a[...] + wt[:, 1:2] * buf_b[...]
    grp = _group_of_row(base)
    o_ref[...] = h_ref[...] + _mod_row(gate_ref, grp) * _rms(y, gp_ref[...])


def _moe_combine(pos, ys, wts, h, g_post, mod, gate_lk):
    n = wts.shape[0]
    tm = GATHER_TILE
    return pl.pallas_call(
        _combine_kernel,
        out_shape=_SDS((n, D_MODEL), F32),
        grid_spec=pltpu.PrefetchScalarGridSpec(
            num_scalar_prefetch=1,
            grid=(n // tm,),
            in_specs=[pl.BlockSpec(memory_space=pl.ANY),
                      pl.BlockSpec((tm, 128), lambda i, pos: (i, 0)),
                      pl.BlockSpec((tm, D_MODEL), lambda i, pos: (i, 0)),
                      pl.BlockSpec((1, D_MODEL), lambda i, pos: (0, 0)),
                      pl.BlockSpec((1, 8, D_MODEL), lambda i, pos: (gate_lk[0], 0, gate_lk[1]))],
            out_specs=pl.BlockSpec((tm, D_MODEL), lambda i, pos: (i, 0)),
            scratch_shapes=[pltpu.VMEM((tm, D_MODEL), F32), pltpu.VMEM((tm, D_MODEL), F32),
                            pltpu.SemaphoreType.DMA(())]),
        compiler_params=_cparams(("arbitrary",)),
        name="moe_combine",
    )(pos, ys, wts, h, g_post.reshape(1, D_MODEL), mod)


def _moe_routing(idx):
    flat_e = idx[:, :TOP_K].reshape(-1)
    n_assign = flat_e.shape[0]
    onehot = (flat_e[:, None] == jnp.arange(N_EXPERTS, dtype=I32)[None, :]).astype(I32)
    csum = jnp.cumsum(onehot, axis=0)
    counts = csum[-1]
    padded = (counts + MOE_BLOCK - 1) // MOE_BLOCK * MOE_BLOCK
    pad_end = jnp.cumsum(padded)
    pad_start = pad_end - padded
    pos = jnp.sum(onehot * (pad_start[None, :] + csum - 1), axis=1).astype(I32)
    slot_tok = jnp.zeros((MOE_SLOTS,), I32).at[pos].set(jnp.arange(n_assign, dtype=I32) // TOP_K)
    slot = jnp.arange(MOE_SLOTS, dtype=I32)
    used_end = pad_start + counts
    slot_valid = jnp.any((slot[:, None] >= pad_start[None, :]) & (slot[:, None] < used_end[None, :]), axis=1)
    blk_start = jnp.arange(MOE_SLOTS // MOE_BLOCK, dtype=I32) * MOE_BLOCK
    last_start = jnp.maximum(pad_end[-1] - MOE_BLOCK, 0)
    blk_e = jnp.searchsorted(pad_end, jnp.minimum(blk_start, last_start), side='right').astype(I32)
    blk_e = jnp.minimum(blk_e, N_EXPERTS - 1)
    rows_used = jnp.clip(used_end[blk_e] - blk_start, 0, MOE_BLOCK)
    blk_sub = ((rows_used + MOE_SUB - 1) // MOE_SUB).astype(I32)
    return pos, slot_tok, slot_valid.astype(F32), blk_e, blk_sub


def _rearrange_w_in(w_in):
    dn_in = DN_QKV + DN_W + 4 * DN_HEADS
    qkvz = w_in[:, :DN_QKV + DN_W]
    ba = w_in[:, DN_QKV + DN_W:dn_in]
    cq = w_in[:, dn_in:dn_in + MLA_Q_RANK]
    ckv = w_in[:, dn_in + MLA_Q_RANK:dn_in + MLA_Q_RANK + MLA_KV_RANK]
    k_rope = w_in[:, dn_in + MLA_Q_RANK + MLA_KV_RANK:]
    pad = jnp.zeros((D_MODEL, P_WIDTH - P_MISC - MLA_ROPE - 4 * DN_HEADS), w_in.dtype)
    return jnp.concatenate([qkvz, ckv, cq, k_rope, ba, pad], axis=1).astype(BF16)


def _rearrange_w_qb(w_qb):
    w = w_qb.reshape(MLA_Q_RANK, MLA_HEADS, MLA_NOPE + MLA_ROPE)
    w = jnp.pad(w, ((0, 0), (0, 0), (0, MLA_QK - MLA_NOPE - MLA_ROPE)))
    return w.reshape(MLA_Q_RANK, MLA_HEADS * MLA_QK).astype(BF16)


def kernel(x, c, ctx, c_ctx, ada_w, ada_b, norm_g, ab_w_in, dn_conv_w, dn_a_log, dn_dt_bias, dn_norm_g, mla_q_norm_g, mla_w_qb, mla_kv_norm_g, mla_w_kvb, ab_w_out, ffn_w_gate, ffn_w_up, ffn_w_down, gqa_w_qkv, gqa_b_qkv, gqa_sink, gqa_w_out, gqa_b_out, moe_w_router, moe_w_gate, moe_w_up, moe_w_down):
    assert x.shape == (BATCH, SEQ, D_MODEL) and ctx.shape == (BATCH, CTX_LEN, D_MODEL) and ada_w.shape[0] == 2
    h = (x.reshape(N_LAT, D_MODEL), ctx.reshape(N_CTX, D_MODEL))
    cvec = jnp.zeros((8, D_MODEL), F32).at[:BATCH].set(c).at[BATCH].set(c_ctx)
    mod = _ada(cvec, ada_w, ada_b)
    (cos1, sin1), (cos2, sin2) = _rope_tables()

    u = _modulate_in(*h, norm_g[0, 0], mod, 0, 0, 1)
    p = _matmul(u, _rearrange_w_in(ab_w_in[0]), None, F32, MM_ROW_TILE, 512)
    o_f, o_b = _deltanet(_dn_conv(p, dn_conv_w[0]), _dn_gates(p, dn_a_log[0], dn_dt_bias[0]))
    q, k, v = _mla_proj(p, mla_q_norm_g[0], mla_kv_norm_g[0], _rearrange_w_qb(mla_w_qb[0]),
                        mla_w_kvb[0].astype(BF16), cos1, sin1)
    y = _ab_out(o_f, o_b, p, dn_norm_g[0], _mla_attention(q, k, v), ab_w_out[0].astype(BF16))
    h, u = _post(h, y, norm_g[0, 1], norm_g[0, 2], mod, (0, 2), (0, 3), (0, 4), BF16)
    y = _ffn(u, ffn_w_gate[0], ffn_w_up[0], ffn_w_down[0])
    h, u = _post(h, y, norm_g[0, 3], norm_g[1, 0], mod, (0, 5), (1, 0), (1, 1), BF16)

    qkv = _gqa_qkv(u, gqa_w_qkv[0], gqa_b_qkv[0], cos2, sin2)
    o = _gqa_attention(qkv, gqa_sink[0])
    y = _matmul(o, gqa_w_out[0], gqa_b_out[0], F32, 1024, 512)
    h, u = _post(h, y, norm_g[1, 1], norm_g[1, 2], mod, (1, 2), (1, 3), (1, 4), F32)
    idx, wts = _router(u, moe_w_router[0])
    pos, slot_tok, slot_valid, blk_e, blk_sub = _moe_routing(idx)
    xs = _moe_gather(slot_tok, slot_valid, u)
    ys = _moe_ffn(blk_e, blk_sub, xs, moe_w_gate[0], moe_w_up[0], moe_w_down[0])
    out = _moe_combine(pos, ys, wts, h, norm_g[1, 3], mod, (1, 5))
    return out.reshape(BATCH, SEQ, D_MODEL)
```

```python
import functools
import math

import jax
import jax.numpy as jnp
from jax import lax
from jax.experimental import pallas as pl
from jax.experimental.pallas import tpu as pltpu

F32 = jnp.float32
BF16 = jnp.bfloat16
I32 = jnp.int32

D_MODEL = 2048
BATCH = 2
SEQ = 4096
CTX_LEN = 256
GRID_W = 64
EPS = 1e-6
ROPE_THETA = 10000.0

DN_HEADS = 8
DN_HEAD_DIM = 128
DN_CONV = 5
DN_CHUNK = 64
MLA_HEADS = 8
MLA_Q_RANK = 768
MLA_KV_RANK = 512
MLA_NOPE = 128
MLA_ROPE = 64
MLA_V = 128
GQA_HEADS = 32
GQA_KV_HEADS = 4
GQA_HEAD_DIM = 64
WINDOW = 128
Q_BLOCK = 128
D_FF = 7168
N_EXPERTS = 8
TOP_K = 2

DN_W = DN_HEADS * DN_HEAD_DIM
DN_QKV = 3 * DN_W
N_LAT = BATCH * SEQ
N_CTX = BATCH * CTX_LEN
N_TOK = N_LAT + N_CTX
GQA_Q = GQA_HEADS * GQA_HEAD_DIM
GQA_KV = GQA_KV_HEADS * GQA_HEAD_DIM

P_QKV = 0
P_Z = DN_QKV
P_CKV = P_Z + DN_W
P_CQ = P_CKV + MLA_KV_RANK
P_MISC = P_CQ + MLA_Q_RANK
P_WIDTH = 5632

ROW_TILE = 512
MM_ROW_TILE = 1088
MOE_BLOCK = 1024
MOE_SUB = 256
MOE_SLOTS = N_LAT * TOP_K + N_EXPERTS * MOE_BLOCK
VMEM_LIMIT = 56 * 1024 * 1024

_SDS = jax.ShapeDtypeStruct
_HIGH = lax.Precision.HIGHEST


def _cparams(sem, vmem=None):
    return pltpu.CompilerParams(dimension_semantics=sem, vmem_limit_bytes=vmem)


def _dot(a, b):
    return jnp.dot(a, b, preferred_element_type=F32)


def _dot_nt(a, b):
    return lax.dot_general(a, b, (((1,), (1,)), ((), ())), preferred_element_type=F32)


def _dot_tn(a, b):
    return lax.dot_general(a, b, (((0,), (0,)), ((), ())), preferred_element_type=F32)


def _dot_hi(a, b):
    return jnp.dot(a, b, preferred_element_type=F32, precision=_HIGH)


def _silu(x):
    return x * jax.nn.sigmoid(x)


def _rms(x, g):
    return x * lax.rsqrt(jnp.mean(x * x, axis=-1, keepdims=True) + EPS) * g


def _group_of_row(r0):
    return (r0 >= SEQ).astype(I32) + (r0 >= 2 * SEQ).astype(I32)


def _mod_spec(layer, k):
    return pl.BlockSpec((1, 8, D_MODEL), lambda *_: (layer, 0, k))


def _mod_row(ref, grp):
    return ref[0, pl.ds(grp, 1), :]


def _ada_kernel(c_ref, w_ref, b_ref, o_ref):
    s = _silu(c_ref[...]).astype(BF16)
    o_ref[0] = _dot(s, w_ref[0].astype(BF16)) + b_ref[0]


def _ada(cvec, ada_w, ada_b):
    n_layers, _, n = ada_w.shape
    tn = 1024
    return pl.pallas_call(
        _ada_kernel,
        out_shape=_SDS((n_layers, 8, n), F32),
        grid=(n_layers, n // tn),
        in_specs=[pl.BlockSpec((8, D_MODEL), lambda l, j: (0, 0)),
                  pl.BlockSpec((1, D_MODEL, tn), lambda l, j: (l, 0, j)),
                  pl.BlockSpec((1, 1, tn), lambda l, j: (l, 0, j))],
        out_specs=pl.BlockSpec((1, 8, tn), lambda l, j: (l, 0, j)),
        compiler_params=_cparams(("parallel", "parallel")),
        name="ada_mod",
    )(cvec, ada_w, ada_b.reshape(n_layers, 1, n))


_LAT_TILES = N_LAT // ROW_TILE
assert N_CTX == ROW_TILE


def _split_specs(width, col=0):
    lat = pl.BlockSpec((ROW_TILE, width), lambda i, *_: (jnp.minimum(i, _LAT_TILES - 1), col))
    ctx = pl.BlockSpec((ROW_TILE, width), lambda i, *_: (0, col))
    return [lat, ctx]


def _split_read(lat_ref, ctx_ref, sl=slice(None)):
    return jnp.where(pl.program_id(0) == _LAT_TILES, ctx_ref[:, sl], lat_ref[:, sl])


def _modin_kernel(x_ref, c_ref, g_ref, sh_ref, sc_ref, u_ref):
    grp = _group_of_row(pl.program_id(0) * ROW_TILE)
    h = _split_read(x_ref, c_ref)
    u = _rms(h, g_ref[...]) * (1.0 + _mod_row(sc_ref, grp)) + _mod_row(sh_ref, grp)
    u_ref[...] = u.astype(u_ref.dtype)


def _modulate_in(x, ctx, g, mod, layer, k_shift, k_scale):
    row = pl.BlockSpec((ROW_TILE, D_MODEL), lambda i: (i, 0))
    vec = pl.BlockSpec((1, D_MODEL), lambda i: (0, 0))
    return pl.pallas_call(
        _modin_kernel,
        out_shape=_SDS((N_TOK, D_MODEL), BF16),
        grid=(N_TOK // ROW_TILE,),
        in_specs=_split_specs(D_MODEL) + [vec, _mod_spec(layer, k_shift), _mod_spec(layer, k_scale)],
        out_specs=row,
        compiler_params=_cparams(("parallel",)),
        name="modulate_in",
    )(x, ctx, g.reshape(1, D_MODEL), mod, mod)


def _post_kernel(*refs, split):
    if split:
        x_ref, c_ref, y_ref, gp_ref, gate_ref, gn_ref, sh_ref, sc_ref, hn_ref, u_ref = refs
        h = _split_read(x_ref, c_ref)
    else:
        h_ref, y_ref, gp_ref, gate_ref, gn_ref, sh_ref, sc_ref, hn_ref, u_ref = refs
        h = h_ref[...]
    grp = _group_of_row(pl.program_id(0) * ROW_TILE)
    hn = h + _mod_row(gate_ref, grp) * _rms(y_ref[...], gp_ref[...])
    hn_ref[...] = hn
    u = _rms(hn, gn_ref[...]) * (1.0 + _mod_row(sc_ref, grp)) + _mod_row(sh_ref, grp)
    u_ref[...] = u.astype(u_ref.dtype)


def _post(h, y, g_post, g_next, mod, gate_lk, shift_lk, scale_lk, u_dtype):
    n = y.shape[0]
    split = isinstance(h, tuple)
    row = pl.BlockSpec((ROW_TILE, D_MODEL), lambda i: (i, 0))
    vec = pl.BlockSpec((1, D_MODEL), lambda i: (0, 0))
    h_specs, h_args = (_split_specs(D_MODEL), list(h)) if split else ([row], [h])
    return pl.pallas_call(
        functools.partial(_post_kernel, split=split),
        out_shape=(_SDS((n, D_MODEL), F32), _SDS((n, D_MODEL), u_dtype)),
        grid=(n // ROW_TILE,),
        in_specs=h_specs + [row, vec, _mod_spec(*gate_lk), vec, _mod_spec(*shift_lk), _mod_spec(*scale_lk)],
        out_specs=(row, row),
        compiler_params=_cparams(("parallel",)),
        name="residual_post",
    )(*h_args, y, g_post.reshape(1, D_MODEL), mod, g_next.reshape(1, D_MODEL), mod, mod)


def _mm_kernel(a_ref, w_ref, b_ref, o_ref):
    o_ref[...] = (_dot(a_ref[...], w_ref[...].astype(BF16)) + b_ref[...]).astype(o_ref.dtype)


def _matmul(a, w, bias, out_dtype, tm, tn):
    m, k = a.shape
    n = w.shape[1]
    if bias is None:
        bias = jnp.zeros((n,), F32)
    return pl.pallas_call(
        _mm_kernel,
        out_shape=_SDS((m, n), out_dtype),
        grid=(m // tm, n // tn),
        in_specs=[pl.BlockSpec((tm, k), lambda i, j: (i, 0)),
                  pl.BlockSpec((k, tn), lambda i, j: (0, j)),
                  pl.BlockSpec((1, tn), lambda i, j: (0, j))],
        out_specs=pl.BlockSpec((tm, tn), lambda i, j: (i, j)),
        compiler_params=_cparams(("parallel", "arbitrary"), VMEM_LIMIT),
        name="matmul",
    )(a, w, bias.reshape(1, n))


FF_TILE = 512
SWIGLU_VMEM_LIMIT = 60 * 1024 * 1024
OUT_TILE = 256
FF_STEPS = D_FF // FF_TILE
OUT_STEPS = D_MODEL // OUT_TILE


def _swiglu_step(s, x_ref, wg_ref, wu_ref, wd_ref, o_ref, h_ref, rows=None, guard=True):
    total = x_ref.shape[0]
    rows = total if rows is None else rows

    @pl.when(guard & (s < FF_STEPS))
    def _():
        x = x_ref[0:rows, :]
        a = _dot(x, wg_ref[...].astype(BF16))
        b = _dot(x, wu_ref[...].astype(BF16))
        h_ref[0:rows, pl.ds(pl.multiple_of(s * FF_TILE, FF_TILE), FF_TILE)] = (_silu(a) * b).astype(BF16)

    @pl.when(guard & (s >= FF_STEPS))
    def _():
        o_ref[0:rows, :] = _dot(h_ref[0:rows, :], wd_ref[...].astype(BF16))
        if rows < total:
            o_ref[rows:total, :] = jnp.zeros((total - rows, o_ref.shape[1]), o_ref.dtype)


def _ffn_kernel(u_ref, wg_ref, wu_ref, wd_ref, o_ref, h_ref):
    _swiglu_step(pl.program_id(1), u_ref, wg_ref, wu_ref, wd_ref, o_ref, h_ref)


def _ffn(u, w_gate, w_up, w_down):
    m = u.shape[0]
    tm = MM_ROW_TILE

    def ff(i, s):
        return (0, jnp.minimum(s, FF_STEPS - 1))

    def out(s):
        return jnp.maximum(s - FF_STEPS, 0)

    return pl.pallas_call(
        _ffn_kernel,
        out_shape=_SDS((m, D_MODEL), F32),
        grid=(m // tm, FF_STEPS + OUT_STEPS),
        in_specs=[pl.BlockSpec((tm, D_MODEL), lambda i, s: (i, 0), pipeline_mode=pl.Buffered(1)),
                  pl.BlockSpec((D_MODEL, FF_TILE), ff),
                  pl.BlockSpec((D_MODEL, FF_TILE), ff),
                  pl.BlockSpec((D_FF, OUT_TILE), lambda i, s: (0, out(s)))],
        out_specs=pl.BlockSpec((tm, OUT_TILE), lambda i, s: (i, out(s))),
        scratch_shapes=[pltpu.VMEM((tm, D_FF), BF16)],
        compiler_params=_cparams(("parallel", "arbitrary"), SWIGLU_VMEM_LIMIT),
        name="swiglu_ffn",
    )(u, w_gate, w_up, w_down)


CONV_TILE = 256
_SEQ_STARTS = (0, SEQ // CONV_TILE, 2 * SEQ // CONV_TILE, 2 * SEQ // CONV_TILE + 1)
_SEQ_LASTS = (SEQ // CONV_TILE - 1, 2 * SEQ // CONV_TILE - 1, 2 * SEQ // CONV_TILE, 2 * SEQ // CONV_TILE + 1)


def _conv_kernel(prev_ref, cur_ref, next_ref, w_ref, o_ref, buf):
    i = pl.program_id(0)
    j = pl.program_id(1)
    tm = CONV_TILE
    first = functools.reduce(jnp.logical_or, [i == s for s in _SEQ_STARTS])
    last = functools.reduce(jnp.logical_or, [i == s for s in _SEQ_LASTS])
    buf[0:8, :] = jnp.where(first, 0.0, prev_ref[...])
    buf[8:8 + tm, :] = cur_ref[...]
    buf[8 + tm:16 + tm, :] = jnp.where(last, 0.0, next_ref[...])
    pad = DN_CONV // 2
    acc = buf[8 - pad:8 - pad + tm, :] * w_ref[0:1, :]
    for d in range(1, DN_CONV):
        acc = acc + buf[8 - pad + d:8 - pad + d + tm, :] * w_ref[d:d + 1, :]
    x = _silu(acc)
    q_scale = jnp.where(j == 0, DN_HEAD_DIM ** -0.5, 1.0)
    for hh in range(DN_HEADS):
        xh = x[:, hh * DN_HEAD_DIM:(hh + 1) * DN_HEAD_DIM]
        inv = lax.rsqrt(jnp.sum(xh * xh, axis=-1, keepdims=True) + EPS) * q_scale
        o_ref[:, hh * DN_HEAD_DIM:(hh + 1) * DN_HEAD_DIM] = xh * jnp.where(j == 2, 1.0, inv)


def _dn_conv(p, conv_w):
    n = p.shape[0]
    tm = CONV_TILE
    r8 = tm // 8
    nb8 = n // 8
    return pl.pallas_call(
        _conv_kernel,
        out_shape=_SDS((n, DN_QKV), F32),
        grid=(n // tm, 3),
        in_specs=[pl.BlockSpec((8, DN_W), lambda i, j: (jnp.maximum(i * r8 - 1, 0), j)),
                  pl.BlockSpec((tm, DN_W), lambda i, j: (i, j)),
                  pl.BlockSpec((8, DN_W), lambda i, j: (jnp.minimum((i + 1) * r8, nb8 - 1), j)),
                  pl.BlockSpec((DN_CONV, DN_W), lambda i, j: (0, j))],
        out_specs=pl.BlockSpec((tm, DN_W), lambda i, j: (i, j)),
        scratch_shapes=[pltpu.VMEM((tm + 16, DN_W), F32)],
        compiler_params=_cparams(("parallel", "parallel")),
        name="dn_conv",
    )(p, p, p, conv_w)


def _gates_kernel(m_ref, a_ref, dt_ref, o_ref):
    x = pltpu.roll(m_ref[...], 64, 1)
    lane = lax.broadcasted_iota(I32, x.shape, 1)
    z = x + dt_ref[...]
    softplus = jnp.maximum(z, 0.0) + jnp.log1p(jnp.exp(-jnp.abs(z)))
    g = a_ref[...] * softplus
    o_ref[...] = jnp.where(lane < 2 * DN_HEADS, jax.nn.sigmoid(x), jnp.where(lane < 4 * DN_HEADS, g, 0.0))


def _dn_gates(p, a_log, dt_bias):
    n = p.shape[0]
    neg_a = jnp.zeros((1, 128), F32).at[0, 2 * DN_HEADS:4 * DN_HEADS].set(-jnp.exp(a_log.astype(F32)).reshape(-1))
    dtb = jnp.zeros((1, 128), F32).at[0, 2 * DN_HEADS:4 * DN_HEADS].set(dt_bias.astype(F32).reshape(-1))
    tm = ROW_TILE
    vec = pl.BlockSpec((1, 128), lambda i: (0, 0))
    return pl.pallas_call(
        _gates_kernel,
        out_shape=_SDS((n, 128), F32),
        grid=(n // tm,),
        in_specs=[pl.BlockSpec((tm, 128), lambda i: (i, P_MISC // 128)), vec, vec],
        out_specs=pl.BlockSpec((tm, 128), lambda i: (i, 0)),
        compiler_params=_cparams(("parallel",)),
        name="dn_gates",
    )(p, neg_a, dtb)


_CTX_CHUNKS = CTX_LEN // DN_CHUNK
_LAT_CHUNKS = SEQ // DN_CHUNK
_DN_STEPS = _CTX_CHUNKS + _LAT_CHUNKS


def _dn_fwd_block(b, t):
    return jnp.where(t < _CTX_CHUNKS, (2 * SEQ + b * CTX_LEN) // DN_CHUNK + t, b * _LAT_CHUNKS + (t - _CTX_CHUNKS))


def _dn_bwd_block(b, t):
    return jnp.where(t < _CTX_CHUNKS, (2 * SEQ + b * CTX_LEN) // DN_CHUNK + (_CTX_CHUNKS - 1 - t),
                     b * _LAT_CHUNKS + (_DN_STEPS - 1 - t))


DN_GROUP = 4
_GROUP_ROWS = DN_GROUP * DN_CHUNK
_GROUP_STATE = DN_GROUP * DN_HEAD_DIM


_DN_BASE = 8


def _block_diag(x):
    zero = jnp.zeros((DN_CHUNK, DN_HEAD_DIM), x.dtype)
    rows = []
    for hh in range(DN_GROUP):
        piece = x[hh * DN_CHUNK:(hh + 1) * DN_CHUNK]
        rows.append(jnp.concatenate([piece if j == hh else zero for j in range(DN_GROUP)], axis=1))
    return jnp.concatenate(rows, axis=0)


def _dn_groups(items, eye, base_mask):
    n = range(len(items))
    it = items
    gx = [jnp.broadcast_to(it[g]["gc"], (_GROUP_ROWS, _GROUP_ROWS)) for g in n]
    gamma = [jnp.where(it[g]["incl"], jnp.exp(jnp.where(it[g]["incl"], gx[g] - gx[g].T, 0.0)), 0.0) for g in n]
    kb = [it[g]["k"].astype(BF16) for g in n]
    kk = [_dot_nt(kb[g], kb[g]) for g in n]
    a = [jnp.where(it[g]["strict"], it[g]["beta"] * kk[g] * gamma[g], 0.0) for g in n]
    d = [jnp.where(base_mask, a[g], 0.0) for g in n]
    tinv = [eye - d[g] for g in n]
    pw = [d[g].astype(BF16) for g in n]
    for _ in range(int(math.log2(_DN_BASE)) - 1):
        pw = [_dot(pw[g], pw[g]).astype(BF16) for g in n]
        tinv = [tinv[g] + _dot(tinv[g].astype(BF16), pw[g]) for g in n]
    for lvl in range(len(it[0]["levels"])):
        tb = [tinv[g].astype(BF16) for g in n]
        mt = [_dot(jnp.where(it[g]["levels"][lvl], a[g], 0.0).astype(BF16), tb[g]).astype(BF16) for g in n]
        tinv = [tinv[g] - _dot(tb[g], mt[g]) for g in n]
    tb = [tinv[g].astype(BF16) for g in n]
    egc = [jnp.exp(it[g]["gc"]) for g in n]
    rhs = [jnp.concatenate([it[g]["v"] * it[g]["beta"], it[g]["k"] * (it[g]["beta"] * egc[g])], axis=1) for g in n]
    rhs_hi = [rhs[g].astype(BF16) for g in n]
    rhs_lo = [(rhs[g] - rhs_hi[g].astype(F32)).astype(BF16) for g in n]
    sol = [_dot(jnp.concatenate([tb[g], tb[g]], axis=1), jnp.concatenate([rhs_hi[g], rhs_lo[g]], axis=0)) for g in n]
    qk = [(_dot_nt(it[g]["q"].astype(BF16), kb[g]) * gamma[g]).astype(BF16) for g in n]
    s = [it[g]["s_ref"][...] for g in n]
    sb = [s[g].astype(BF16) for g in n]
    v_new = [sol[g][:, :DN_HEAD_DIM] - _dot(_block_diag(sol[g][:, DN_HEAD_DIM:].astype(BF16)), sb[g]) for g in n]
    vb = [v_new[g].astype(BF16) for g in n]
    o = [_dot(_block_diag((it[g]["q"] * egc[g]).astype(BF16)), sb[g]) + _dot(qk[g], vb[g]) for g in n]
    for g in n:
        k_dec = _block_diag((it[g]["k"] * jnp.exp(it[g]["g_last"] - it[g]["gc"])).astype(BF16))
        it[g]["s_ref"][...] = s[g] * jnp.exp(it[g]["g_last_state"]) + _dot_tn(k_dec, vb[g])
    return o


def _stack_cols(x, lanes, rows_each):
    pieces = [x[:, l:l + 1] for l in lanes]
    if rows_each is not None:
        pieces = [jnp.broadcast_to(pc, (rows_each, 1)) for pc in pieces]
    return jnp.concatenate(pieces, axis=0)


def _dn_kernel(*refs):
    n_in = BATCH * 2 * 4
    in_refs, (ofl_ref, ofc_ref, obl_ref, obc_ref), (s_ref, o_scr) = refs[:n_in], refs[n_in:n_in + 4], refs[n_in + 4:]
    t = pl.program_id(0)

    @pl.when(t == 0)
    def _():
        s_ref[...] = jnp.zeros_like(s_ref)

    c = DN_CHUNK
    ri = lax.broadcasted_iota(I32, (_GROUP_ROWS, _GROUP_ROWS), 0)
    ci = lax.broadcasted_iota(I32, (_GROUP_ROWS, _GROUP_ROWS), 1)
    same_head = (ri // c) == (ci // c)
    eye = (ri == ci).astype(F32)
    r1 = lax.broadcasted_iota(I32, (c, c), 0)
    c1 = lax.broadcasted_iota(I32, (c, c), 1)
    base_mask = (ri // _DN_BASE) == (ci // _DN_BASE)
    masks = []
    for d in range(2):
        incl = same_head & ((ri >= ci) if d == 0 else (ri <= ci))
        strict = same_head & ((ri > ci) if d == 0 else (ri < ci))
        levels = []
        size = _DN_BASE
        while size < c:
            lo_blk, hi_blk = ((ci, ri) if d == 0 else (ri, ci))
            levels.append(((ri // (2 * size)) == (ci // (2 * size)))
                          & ((lo_blk // size) % 2 == 0) & ((hi_blk // size) % 2 == 1))
            size *= 2
        masks.append((incl, strict, levels))
    items, where = [], []
    for b, d in [(b, d) for b in range(BATCH) for d in range(2)]:
        q_ref, k_ref, v_ref, g_ref = in_refs[(b * 2 + d) * 4:(b * 2 + d) * 4 + 4]
        incl, strict, levels = masks[d]
        gates = g_ref[...]
        tri = ((r1 >= c1) if d == 0 else (r1 <= c1)).astype(BF16)
        g_hi = gates.astype(BF16)
        g_r1 = gates - g_hi.astype(F32)
        g_mid = g_r1.astype(BF16)
        g_lo = (g_r1 - g_mid.astype(F32)).astype(BF16)
        gc = _dot(tri, g_hi) + _dot(tri, g_mid) + _dot(tri, g_lo)
        last_row = c - 1 if d == 0 else 0
        tot = gc[last_row:last_row + 1, :]
        for grp in range(DN_HEADS // DN_GROUP):
            heads = range(grp * DN_GROUP, (grp + 1) * DN_GROUP)
            lanes_b = [d * DN_HEADS + hh for hh in heads]
            lanes_g = [2 * DN_HEADS + lb for lb in lanes_b]

            def stack(ref):
                return jnp.concatenate([ref[:, hh * DN_HEAD_DIM:(hh + 1) * DN_HEAD_DIM] for hh in heads], axis=0)

            items.append(dict(q=stack(q_ref), k=stack(k_ref), v=stack(v_ref),
                              beta=_stack_cols(gates, lanes_b, None), gc=_stack_cols(gc, lanes_g, None),
                              g_last=_stack_cols(tot, lanes_g, c), g_last_state=_stack_cols(tot, lanes_g, DN_HEAD_DIM),
                              incl=incl, strict=strict, levels=levels, s_ref=s_ref.at[b, d, grp]))
            where.append((d, b, heads))
    outs = _dn_groups(items, eye, base_mask)
    for o, (d, b, heads) in zip(outs, where):
        for j, hh in enumerate(heads):
            o_scr[d, b, :, hh * DN_HEAD_DIM:(hh + 1) * DN_HEAD_DIM] = o[j * c:(j + 1) * c]

    @pl.when(t < _CTX_CHUNKS)
    def _():
        ofc_ref[...] = o_scr[0]
        obc_ref[...] = o_scr[1]

    @pl.when(t >= _CTX_CHUNKS)
    def _():
        ofl_ref[...] = o_scr[0]
        obl_ref[...] = o_scr[1]


def _deltanet(qkv, gates):
    c = DN_CHUNK
    ins, args = [], []
    for b in range(BATCH):
        for fn in (_dn_fwd_block, _dn_bwd_block):
            for width, col, arr in ((DN_W, 0, qkv), (DN_W, 1, qkv), (DN_W, 2, qkv), (128, 0, gates)):
                ins.append(pl.BlockSpec((c, width), functools.partial(lambda t, b, fn, col: (fn(b, t), col),
                                                                      b=b, fn=fn, col=col)))
                args.append(arr)
    last = _DN_STEPS - 1
    lat = _SDS((BATCH, SEQ, DN_W), F32)
    ctx = _SDS((BATCH, CTX_LEN, DN_W), F32)
    blk = (BATCH, c, DN_W)
    outs = pl.pallas_call(
        _dn_kernel,
        out_shape=(lat, ctx, lat, ctx),
        grid=(_DN_STEPS,),
        in_specs=ins,
        out_specs=(pl.BlockSpec(blk, lambda t: (0, jnp.maximum(t - _CTX_CHUNKS, 0), 0)),
                   pl.BlockSpec(blk, lambda t: (0, jnp.minimum(t, _CTX_CHUNKS - 1), 0)),
                   pl.BlockSpec(blk, lambda t: (0, jnp.minimum(last - t, _LAT_CHUNKS - 1), 0)),
                   pl.BlockSpec(blk, lambda t: (0, jnp.maximum(_CTX_CHUNKS - 1 - t, 0), 0))),
        scratch_shapes=[pltpu.VMEM((BATCH, 2, DN_HEADS // DN_GROUP, _GROUP_STATE, DN_HEAD_DIM), F32),
                        pltpu.VMEM((2, BATCH, c, DN_W), F32)],
        compiler_params=_cparams(("arbitrary",)),
        name="deltanet",
    )(*args)
    ofl, ofc, obl, obc = outs
    return ((ofl.reshape(N_LAT, DN_W), ofc.reshape(N_CTX, DN_W)),
            (obl.reshape(N_LAT, DN_W), obc.reshape(N_CTX, DN_W)))


def _rope_tables():
    t = jnp.arange(SEQ)
    row = (t // GRID_W).astype(F32)
    col = (t % GRID_W).astype(F32)
    n_freq = MLA_ROPE // 4
    inv_freq = ROPE_THETA ** (-jnp.arange(n_freq, dtype=F32) / n_freq)
    ang = jnp.concatenate([row[:, None] * inv_freq, col[:, None] * inv_freq], axis=-1)
    cos, sin = jnp.cos(ang), jnp.sin(ang)
    cos64 = jnp.concatenate([cos, cos], axis=-1)
    sin64 = jnp.concatenate([-sin, sin], axis=-1)

    def assemble(lat_cos, lat_sin):
        c = jnp.concatenate([jnp.tile(lat_cos, (BATCH, 1)), jnp.ones((N_CTX, 128), F32)], axis=0)
        s = jnp.concatenate([jnp.tile(lat_sin, (BATCH, 1)), jnp.zeros((N_CTX, 128), F32)], axis=0)
        return c, s

    cos_l = jnp.concatenate([cos64, jnp.ones((SEQ, 64), F32)], axis=-1)
    sin_l = jnp.concatenate([sin64, jnp.zeros((SEQ, 64), F32)], axis=-1)
    one_head = assemble(cos_l, sin_l)
    two_heads = assemble(jnp.tile(cos64, (1, 2)), jnp.tile(sin64, (1, 2)))
    return one_head, two_heads


def _rope128(x, cosv, sinv):
    lane = lax.broadcasted_iota(I32, x.shape, 1)
    swapped = jnp.where((lane % 64) < 32, pltpu.roll(x, 96, 1), pltpu.roll(x, 32, 1))
    return x * cosv + swapped * sinv


MLA_QK = 256
MLA_VX = 256


def _mla_proj_kernel(ckv_ref, cq_ref, misc_ref, gq_ref, gkv_ref, wq_ref, wkv_ref, cos_ref, sin_ref,
                     q_ref, k_ref, v_ref):
    cq = _rms(cq_ref[...], gq_ref[...]).astype(BF16)
    ckv = _rms(ckv_ref[...], gkv_ref[...]).astype(BF16)
    qf = _dot(cq, wq_ref[...])
    kvf = _dot(ckv, wkv_ref[...])
    cosv, sinv = cos_ref[...], sin_ref[...]
    lane = lax.broadcasted_iota(I32, cosv.shape, 1)
    k_rope = _rope128(jnp.where(lane < MLA_ROPE, misc_ref[...], 0.0), cosv, sinv).astype(BF16)
    scale = (MLA_NOPE + MLA_ROPE) ** -0.5
    for hh in range(MLA_HEADS):
        lo, mid, hi = hh * MLA_QK, hh * MLA_QK + 128, (hh + 1) * MLA_QK
        q_ref[:, lo:mid] = (qf[:, lo:mid] * scale).astype(BF16)
        q_ref[:, mid:hi] = (_rope128(qf[:, mid:hi], cosv, sinv) * scale).astype(BF16)
        k_ref[:, lo:mid] = kvf[:, lo:mid].astype(BF16)
        k_ref[:, mid:hi] = k_rope
        v_ref[:, lo:mid] = kvf[:, mid:hi].astype(BF16)
        v_ref[:, mid:hi] = jnp.ones((kvf.shape[0], MLA_VX - MLA_V), BF16)


def _mla_proj(p, q_norm_g, kv_norm_g, w_q, w_kv, cosv, sinv):
    n = p.shape[0]
    tm = ROW_TILE
    wide = MLA_HEADS * MLA_QK

    def rows(width, col):
        return pl.BlockSpec((tm, width), lambda i: (i, col))

    def whole(shape):
        return pl.BlockSpec(shape, lambda i: (0, 0))

    return pl.pallas_call(
        _mla_proj_kernel,
        out_shape=(_SDS((n, wide), BF16), _SDS((n, wide), BF16), _SDS((n, MLA_HEADS * MLA_VX), BF16)),
        grid=(n // tm,),
        in_specs=[rows(MLA_KV_RANK, P_CKV // MLA_KV_RANK), rows(MLA_Q_RANK, P_CQ // MLA_Q_RANK),
                  rows(128, P_MISC // 128), whole((1, MLA_Q_RANK)), whole((1, MLA_KV_RANK)),
                  whole((MLA_Q_RANK, wide)), whole((MLA_KV_RANK, wide)), rows(128, 0), rows(128, 0)],
        out_specs=(rows(wide, 0), rows(wide, 0), rows(MLA_HEADS * MLA_VX, 0)),
        compiler_params=_cparams(("parallel",), VMEM_LIMIT),
        name="mla_proj",
    )(p, p, p, q_norm_g.reshape(1, -1), kv_norm_g.reshape(1, -1), w_q, w_kv, cosv, sinv)


MLA_KEY_CHUNK = 1024


def _flash_kernel(*refs, chunks):
    n_seg = (len(refs) - 2) // 2
    q_ref, o_ref = refs[0], refs[-1]
    k_refs, v_refs = refs[1:1 + n_seg], refs[1 + n_seg:1 + 2 * n_seg]
    q = q_ref[...]

    def scores(c):
        seg, off, size = c
        return _dot_nt(q, k_refs[seg][off:off + size, :])

    m = jnp.full((q.shape[0], 1), -jnp.inf, F32)
    acc = jnp.zeros((q.shape[0], MLA_VX), F32)
    s_next = scores(chunks[0])
    for j, (seg, off, size) in enumerate(chunks):
        s = s_next
        if j + 1 < len(chunks):
            s_next = scores(chunks[j + 1])
        m_new = jnp.maximum(m, jnp.max(s, axis=-1, keepdims=True))
        p = jnp.exp(s - m_new).astype(BF16)
        acc = jnp.exp(m - m_new) * acc + _dot(p, v_refs[seg][off:off + size, :])
        m = m_new
    o_ref[...] = (acc[:, :MLA_V] / acc[:, MLA_V:]).astype(o_ref.dtype)


def _mla_attention(q, k, v):
    tq = 512
    nq = SEQ // tq
    ctx_blk = 2 * SEQ // CTX_LEN
    lat_chunks = ((0, 0, CTX_LEN),) + tuple((1, off, MLA_KEY_CHUNK) for off in range(0, SEQ, MLA_KEY_CHUNK))
    lat = pl.pallas_call(
        functools.partial(_flash_kernel, chunks=lat_chunks),
        out_shape=_SDS((N_LAT, MLA_HEADS * MLA_V), BF16),
        grid=(BATCH, MLA_HEADS, nq),
        in_specs=[pl.BlockSpec((tq, MLA_QK), lambda b, h, i: (b * nq + i, h)),
                  pl.BlockSpec((CTX_LEN, MLA_QK), lambda b, h, i: (ctx_blk + b, h)),
                  pl.BlockSpec((SEQ, MLA_QK), lambda b, h, i: (b, h)),
                  pl.BlockSpec((CTX_LEN, MLA_VX), lambda b, h, i: (ctx_blk + b, h)),
                  pl.BlockSpec((SEQ, MLA_VX), lambda b, h, i: (b, h))],
        out_specs=pl.BlockSpec((tq, MLA_V), lambda b, h, i: (b * nq + i, h)),
        compiler_params=_cparams(("parallel", "parallel", "arbitrary"), VMEM_LIMIT),
        name="mla_attn_latent",
    )(q, k, k, v, v)
    ctx = pl.pallas_call(
        functools.partial(_flash_kernel, chunks=((0, 0, CTX_LEN),)),
        out_shape=_SDS((N_CTX, MLA_HEADS * MLA_V), BF16),
        grid=(BATCH, MLA_HEADS),
        in_specs=[pl.BlockSpec((CTX_LEN, MLA_QK), lambda b, h: (ctx_blk + b, h)),
                  pl.BlockSpec((CTX_LEN, MLA_QK), lambda b, h: (ctx_blk + b, h)),
                  pl.BlockSpec((CTX_LEN, MLA_VX), lambda b, h: (ctx_blk + b, h))],
        out_specs=pl.BlockSpec((CTX_LEN, MLA_V), lambda b, h: (b, h)),
        compiler_params=_cparams(("parallel", "parallel")),
        name="mla_attn_context",
    )(q, k, v)
    return lat, ctx


def _ab_out_kernel(ofl_ref, ofc_ref, obl_ref, obc_ref, ml_ref, mc_ref, z_ref, g_ref, w_ref, o_ref, a_ref):
    @pl.when(pl.program_id(1) == 0)
    def _():
        for hh in range(DN_HEADS):
            sl = slice(hh * DN_HEAD_DIM, (hh + 1) * DN_HEAD_DIM)
            o = _split_read(ofl_ref, ofc_ref, sl) + _split_read(obl_ref, obc_ref, sl)
            a_ref[:, sl] = (_rms(o, g_ref[...]) * _silu(z_ref[:, sl])).astype(BF16)
        a_ref[:, DN_W:] = _split_read(ml_ref, mc_ref)

    o_ref[...] = _dot(a_ref[...], w_ref[...].astype(BF16))


def _ab_out(o_f, o_b, p, dn_norm_g, mla_o, w_out):
    tm, tn = ROW_TILE, 512
    half = DN_W
    return pl.pallas_call(
        _ab_out_kernel,
        out_shape=_SDS((N_TOK, D_MODEL), F32),
        grid=(N_TOK // tm, D_MODEL // tn),
        in_specs=_split_specs(half) + _split_specs(half) + _split_specs(half) + [
            pl.BlockSpec((tm, half), lambda i, j: (i, P_Z // half)),
            pl.BlockSpec((1, DN_HEAD_DIM), lambda i, j: (0, 0)),
            pl.BlockSpec((2 * half, tn), lambda i, j: (0, j))],
        out_specs=pl.BlockSpec((tm, tn), lambda i, j: (i, j)),
        scratch_shapes=[pltpu.VMEM((tm, 2 * half), BF16)],
        compiler_params=_cparams(("parallel", "arbitrary"), VMEM_LIMIT),
        name="ab_out_proj",
    )(*o_f, *o_b, *mla_o, p, dn_norm_g.reshape(1, -1), w_out)


def _mm_rope_kernel(a_ref, w_ref, b_ref, cos_ref, sin_ref, o_ref, *, tn, rope_cols):
    y = _dot(a_ref[...], w_ref[...].astype(BF16)) + b_ref[...]
    col0 = pl.program_id(1) * tn
    cosv, sinv = cos_ref[...], sin_ref[...]
    for s in range(tn // 128):
        ys = y[:, s * 128:(s + 1) * 128]
        roped = _rope128(ys, cosv, sinv)
        o_ref[:, s * 128:(s + 1) * 128] = jnp.where(col0 + s * 128 < rope_cols, roped, ys).astype(o_ref.dtype)


def _gqa_qkv(u, w, bias, cosv, sinv):
    m, k = u.shape
    n = w.shape[1]
    tm, tn = MM_ROW_TILE, 512
    return pl.pallas_call(
        functools.partial(_mm_rope_kernel, tn=tn, rope_cols=GQA_Q + GQA_KV),
        out_shape=_SDS((m, n), BF16),
        grid=(m // tm, n // tn),
        in_specs=[pl.BlockSpec((tm, k), lambda i, j: (i, 0)),
                  pl.BlockSpec((k, tn), lambda i, j: (0, j)),
                  pl.BlockSpec((1, tn), lambda i, j: (0, j)),
                  pl.BlockSpec((tm, 128), lambda i, j: (i, 0)),
                  pl.BlockSpec((tm, 128), lambda i, j: (i, 0))],
        out_specs=pl.BlockSpec((tm, tn), lambda i, j: (i, j)),
        compiler_params=_cparams(("parallel", "arbitrary"), VMEM_LIMIT),
        name="gqa_qkv_rope",
    )(u, w, bias.reshape(1, n), cosv, sinv)


def _gqa_kernel(sink_ref, q_ref, kp_ref, kc_ref, kn_ref, vp_ref, vc_ref, vn_ref, kx_ref, vx_ref, o_ref):
    i = pl.program_id(1)
    nb = SEQ // Q_BLOCK
    n_keys = CTX_LEN + 3 * Q_BLOCK
    groups = GQA_HEADS // GQA_KV_HEADS
    slabs = groups // 2
    dh = GQA_HEAD_DIM
    qi = lax.broadcasted_iota(I32, (Q_BLOCK, n_keys), 0)
    kj = lax.broadcasted_iota(I32, (Q_BLOCK, n_keys), 1) - CTX_LEN
    rel = kj - Q_BLOCK - qi
    valid = (kj < 0) | ((jnp.abs(rel) <= WINDOW) & ((kj >= Q_BLOCK) | (i > 0)) & ((kj < 2 * Q_BLOCK) | (i < nb - 1)))
    bias = jnp.where(valid, 0.0, -jnp.inf).astype(F32)
    bias = jnp.concatenate([bias] * slabs, axis=0)
    k_all = jnp.concatenate([kx_ref[...], kp_ref[...], kc_ref[...], kn_ref[...]], axis=0)
    v_all = jnp.concatenate([vx_ref[...], vp_ref[...], vc_ref[...], vn_ref[...]], axis=0)
    low = lax.broadcasted_iota(I32, (n_keys, 128), 1) < dh

    def block_diag(x_all, kvh):
        pair = x_all[:, (kvh // 2) * 128:(kvh // 2 + 1) * 128].astype(F32)
        other = pltpu.roll(pair, dh, 1)
        first, second = (pair, other) if kvh % 2 == 0 else (other, pair)
        return jnp.concatenate([jnp.where(low, first, 0.0), jnp.where(low, 0.0, second)], axis=0).astype(BF16)

    def scores(kvh):
        q = jnp.concatenate([q_ref[:, (kvh * slabs + r) * 128:(kvh * slabs + r + 1) * 128] for r in range(slabs)], axis=0)
        return _dot_nt(q * (dh ** -0.5), block_diag(k_all, kvh))

    def softmax(kvh, s):
        ps, dens = [], []
        for half in range(2):
            sh = s[:, half * n_keys:(half + 1) * n_keys] + bias
            sink = jnp.concatenate([jnp.full((Q_BLOCK, 1), sink_ref[(kvh * slabs + r) * 2 + half], F32)
                                    for r in range(slabs)], axis=0)
            m = jnp.maximum(jnp.max(sh, axis=-1, keepdims=True), sink)
            p = jnp.exp(sh - m)
            dens.append(jnp.sum(p, axis=-1, keepdims=True) + jnp.exp(sink - m))
            ps.append(p.astype(BF16))
        return jnp.concatenate(ps, axis=1), dens

    def finish(kvh, p, dens):
        o = _dot(p, block_diag(v_all, kvh))
        lane_low = lax.broadcasted_iota(I32, o.shape, 1) < dh
        o = o * jnp.where(lane_low, 1.0 / dens[0], 1.0 / dens[1])
        for r in range(slabs):
            o_ref[:, (kvh * slabs + r) * 128:(kvh * slabs + r + 1) * 128] = o[r * Q_BLOCK:(r + 1) * Q_BLOCK].astype(o_ref.dtype)

    s_next = scores(0)
    for kvh in range(GQA_KV_HEADS):
        s_cur = s_next
        if kvh + 1 < GQA_KV_HEADS:
            s_next = scores(kvh + 1)
        p, dens = softmax(kvh, s_cur)
        finish(kvh, p, dens)


def _gqa_attention(qkv, sink):
    nb = SEQ // Q_BLOCK
    kcol, vcol = GQA_Q // GQA_KV, GQA_Q // GQA_KV + 1
    ctx_blk = 2 * SEQ // CTX_LEN

    def win(col, shift):
        return pl.BlockSpec((Q_BLOCK, GQA_KV), lambda b, i: (b * nb + jnp.clip(i + shift, 0, nb - 1), col))

    return pl.pallas_call(
        _gqa_kernel,
        out_shape=_SDS((N_LAT, GQA_Q), BF16),
        grid=(BATCH, nb),
        in_specs=[pl.BlockSpec(memory_space=pltpu.SMEM),
                  pl.BlockSpec((Q_BLOCK, GQA_Q), lambda b, i: (b * nb + i, 0)),
                  win(kcol, -1), win(kcol, 0), win(kcol, 1), win(vcol, -1), win(vcol, 0), win(vcol, 1),
                  pl.BlockSpec((CTX_LEN, GQA_KV), lambda b, i: (ctx_blk + b, kcol)),
                  pl.BlockSpec((CTX_LEN, GQA_KV), lambda b, i: (ctx_blk + b, vcol))],
        out_specs=pl.BlockSpec((Q_BLOCK, GQA_Q), lambda b, i: (b * nb + i, 0)),
        compiler_params=_cparams(("parallel", "arbitrary")),
        name="gqa_window_attn",
    )(sink.astype(F32), qkv, qkv, qkv, qkv, qkv, qkv, qkv, qkv, qkv)


def _router_kernel(u_ref, w_ref, idx_ref, wt_ref):
    logits = _dot(u_ref[...].astype(BF16), w_ref[...])
    lane = lax.broadcasted_iota(I32, logits.shape, 1)
    neg = -jnp.inf
    l0 = jnp.where(lane < N_EXPERTS, logits, neg)
    m1 = jnp.max(l0, axis=-1, keepdims=True)
    i1 = jnp.min(jnp.where(l0 == m1, lane, 128), axis=-1, keepdims=True)
    l1 = jnp.where(lane == i1, neg, l0)
    m2 = jnp.max(l1, axis=-1, keepdims=True)
    i2 = jnp.min(jnp.where(l1 == m2, lane, 128), axis=-1, keepdims=True)
    e = jnp.exp(m2 - m1)
    w1 = 1.0 / (1.0 + e)
    idx_ref[...] = jnp.where(lane == 0, i1, jnp.where(lane == 1, i2, 0))
    wt_ref[...] = jnp.where(lane == 0, w1, jnp.where(lane == 1, e * w1, 0.0))


def _router(u, w_router):
    n = u.shape[0]
    tm = ROW_TILE
    w = jnp.zeros((D_MODEL, 128), BF16).at[:, :N_EXPERTS].set(w_router.astype(BF16))
    return pl.pallas_call(
        _router_kernel,
        out_shape=(_SDS((n, 128), I32), _SDS((n, 128), F32)),
        grid=(n // tm,),
        in_specs=[pl.BlockSpec((tm, D_MODEL), lambda i: (i, 0)), pl.BlockSpec((D_MODEL, 128), lambda i: (0, 0))],
        out_specs=(pl.BlockSpec((tm, 128), lambda i: (i, 0)), pl.BlockSpec((tm, 128), lambda i: (i, 0))),
        compiler_params=_cparams(("parallel",)),
        name="moe_router",
    )(u, w)


GATHER_TILE = 256


def _row_copy(src_hbm, row, dst, r, sem):
    return pltpu.make_async_copy(src_hbm.at[pl.ds(row, 1), :], dst.at[pl.ds(r, 1), :], sem)


def _gather_kernel(tok_ref, tv_ref, u_hbm, valid_ref, o_ref, buf, sem):
    i = pl.program_id(0)
    base = i * GATHER_TILE

    @pl.when(tv_ref[i] > 0)
    def _():
        def issue(r8, carry):
            for j in range(8):
                r = r8 * 8 + j
                _row_copy(u_hbm, tok_ref[base + r], buf, r, sem).start(priority=j % 2)
            return carry

        def drain(r, carry):
            _row_copy(u_hbm, 0, buf, r, sem).wait()
            return carry

        lax.fori_loop(0, GATHER_TILE // 8, issue, 0)
        lax.fori_loop(0, GATHER_TILE, drain, 0, unroll=8)
        o_ref[...] = jnp.where(valid_ref[...] > 0.0, buf[...], 0.0).astype(o_ref.dtype)

    @pl.when(tv_ref[i] == 0)
    def _():
        o_ref[...] = jnp.zeros_like(o_ref)


def _moe_gather(slot_tok, slot_valid, u):
    n_slots = slot_tok.shape[0]
    n_tiles = n_slots // GATHER_TILE
    tile_valid = (jnp.max(slot_valid.reshape(n_tiles, GATHER_TILE), axis=1) > 0.0).astype(I32)
    return pl.pallas_call(
        _gather_kernel,
        out_shape=_SDS((n_slots, D_MODEL), BF16),
        grid_spec=pltpu.PrefetchScalarGridSpec(
            num_scalar_prefetch=2,
            grid=(n_tiles,),
            in_specs=[pl.BlockSpec(memory_space=pl.ANY),
                      pl.BlockSpec((GATHER_TILE, 1), lambda i, tok, tv: (i, 0))],
            out_specs=pl.BlockSpec((GATHER_TILE, D_MODEL), lambda i, tok, tv: (i, 0)),
            scratch_shapes=[pltpu.VMEM((GATHER_TILE, D_MODEL), F32), pltpu.SemaphoreType.DMA(())]),
        compiler_params=_cparams(("arbitrary",)),
        name="moe_gather",
    )(slot_tok, tile_valid, u, slot_valid.reshape(n_slots, 1))


def _moe_ffn_kernel(be_ref, bv_ref, x_ref, wg_ref, wu_ref, wd_ref, o_ref, h_ref):
    i, s = pl.program_id(0), pl.program_id(1)
    n_sub = bv_ref[i]
    for k in range(1, MOE_BLOCK // MOE_SUB + 1):
        _swiglu_step(s, x_ref, wg_ref.at[0], wu_ref.at[0], wd_ref.at[0], o_ref, h_ref,
                     rows=k * MOE_SUB, guard=n_sub == k)

    @pl.when((n_sub == 0) & (s >= FF_STEPS))
    def _():
        o_ref[...] = jnp.zeros_like(o_ref)


def _moe_ffn(block_expert, block_valid, xs, w_gate, w_up, w_down):
    n_slots = xs.shape[0]
    tm = MOE_BLOCK

    def ff(i, s, be, bv):
        return (be[i], 0, jnp.where(bv[i] > 0, jnp.minimum(s, FF_STEPS - 1), FF_STEPS - 1))

    def down(i, s, be, bv):
        return (be[i], 0, jnp.where(bv[i] > 0, jnp.maximum(s - FF_STEPS, 0), OUT_STEPS - 1))

    return pl.pallas_call(
        _moe_ffn_kernel,
        out_shape=_SDS((n_slots, D_MODEL), F32),
        grid_spec=pltpu.PrefetchScalarGridSpec(
            num_scalar_prefetch=2,
            grid=(n_slots // tm, FF_STEPS + OUT_STEPS),
            in_specs=[pl.BlockSpec((tm, D_MODEL), lambda i, s, be, bv: (i, 0), pipeline_mode=pl.Buffered(1)),
                      pl.BlockSpec((1, D_MODEL, FF_TILE), ff),
                      pl.BlockSpec((1, D_MODEL, FF_TILE), ff),
                      pl.BlockSpec((1, D_FF, OUT_TILE), down)],
            out_specs=pl.BlockSpec((tm, OUT_TILE), lambda i, s, be, bv: (i, jnp.maximum(s - FF_STEPS, 0))),
            scratch_shapes=[pltpu.VMEM((tm, D_FF), BF16)]),
        compiler_params=_cparams(("arbitrary", "arbitrary"), SWIGLU_VMEM_LIMIT),
        name="moe_grouped_swiglu",
    )(block_expert, block_valid, xs, w_gate, w_up, w_down)


def _combine_kernel(pos_ref, y_hbm, wt_ref, h_ref, gp_ref, gate_ref, o_ref, buf_a, buf_b, sem):
    base = pl.program_id(0) * GATHER_TILE

    def issue(r, carry):
        _row_copy(y_hbm, pos_ref[2 * (base + r)], buf_a, r, sem).start(priority=0)
        _row_copy(y_hbm, pos_ref[2 * (base + r) + 1], buf_b, r, sem).start(priority=1)
        return carry

    def drain(r, carry):
        _row_copy(y_hbm, 0, buf_a, r, sem).wait()
        _row_copy(y_hbm, 0, buf_b, r, sem).wait()
        return carry

    lax.fori_loop(0, GATHER_TILE, issue, 0, unroll=8)
    lax.fori_loop(0, GATHER_TILE, drain, 0, unroll=8)
    wt = wt_ref[...]
    y = wt[:, 0:1] * buf_a[...] + wt[:, 1:2] * buf_b[...]
    grp = _group_of_row(base)
    o_ref[...] = h_ref[...] + _mod_row(gate_ref, grp) * _rms(y, gp_ref[...])


def _moe_combine(pos, ys, wts, h, g_post, mod, gate_lk):
    n = wts.shape[0]
    tm = GATHER_TILE
    return pl.pallas_call(
        _combine_kernel,
        out_shape=_SDS((n, D_MODEL), F32),
        grid_spec=pltpu.PrefetchScalarGridSpec(
            num_scalar_prefetch=1,
            grid=(n // tm,),
            in_specs=[pl.BlockSpec(memory_space=pl.ANY),
                      pl.BlockSpec((tm, 128), lambda i, pos: (i, 0)),
                      pl.BlockSpec((tm, D_MODEL), lambda i, pos: (i, 0)),
                      pl.BlockSpec((1, D_MODEL), lambda i, pos: (0, 0)),
                      pl.BlockSpec((1, 8, D_MODEL), lambda i, pos: (gate_lk[0], 0, gate_lk[1]))],
            out_specs=pl.BlockSpec((tm, D_MODEL), lambda i, pos: (i, 0)),
            scratch_shapes=[pltpu.VMEM((tm, D_MODEL), F32), pltpu.VMEM((tm, D_MODEL), F32),
                            pltpu.SemaphoreType.DMA(())]),
        compiler_params=_cparams(("arbitrary",)),
        name="moe_combine",
    )(pos, ys, wts, h, g_post.reshape(1, D_MODEL), mod)


def _moe_routing(idx):
    flat_e = idx[:, :TOP_K].reshape(-1)
    n_assign = flat_e.shape[0]
    onehot = (flat_e[:, None] == jnp.arange(N_EXPERTS, dtype=I32)[None, :]).astype(I32)
    csum = jnp.cumsum(onehot, axis=0)
    counts = csum[-1]
    padded = (counts + MOE_BLOCK - 1) // MOE_BLOCK * MOE_BLOCK
    pad_end = jnp.cumsum(padded)
    pad_start = pad_end - padded
    pos = jnp.sum(onehot * (pad_start[None, :] + csum - 1), axis=1).astype(I32)
    slot_tok = jnp.zeros((MOE_SLOTS,), I32).at[pos].set(jnp.arange(n_assign, dtype=I32) // TOP_K)
    slot = jnp.arange(MOE_SLOTS, dtype=I32)
    used_end = pad_start + counts
    slot_valid = jnp.any((slot[:, None] >= pad_start[None, :]) & (slot[:, None] < used_end[None, :]), axis=1)
    blk_start = jnp.arange(MOE_SLOTS // MOE_BLOCK, dtype=I32) * MOE_BLOCK
    last_start = jnp.maximum(pad_end[-1] - MOE_BLOCK, 0)
    blk_e = jnp.searchsorted(pad_end, jnp.minimum(blk_start, last_start), side='right').astype(I32)
    blk_e = jnp.minimum(blk_e, N_EXPERTS - 1)
    rows_used = jnp.clip(used_end[blk_e] - blk_start, 0, MOE_BLOCK)
    blk_sub = ((rows_used + MOE_SUB - 1) // MOE_SUB).astype(I32)
    return pos, slot_tok, slot_valid.astype(F32), blk_e, blk_sub


def _rearrange_w_in(w_in):
    dn_in = DN_QKV + DN_W + 4 * DN_HEADS
    qkvz = w_in[:, :DN_QKV + DN_W]
    ba = w_in[:, DN_QKV + DN_W:dn_in]
    cq = w_in[:, dn_in:dn_in + MLA_Q_RANK]
    ckv = w_in[:, dn_in + MLA_Q_RANK:dn_in + MLA_Q_RANK + MLA_KV_RANK]
    k_rope = w_in[:, dn_in + MLA_Q_RANK + MLA_KV_RANK:]
    pad = jnp.zeros((D_MODEL, P_WIDTH - P_MISC - MLA_ROPE - 4 * DN_HEADS), w_in.dtype)
    return jnp.concatenate([qkvz, ckv, cq, k_rope, ba, pad], axis=1).astype(BF16)


def _rearrange_w_qb(w_qb):
    w = w_qb.reshape(MLA_Q_RANK, MLA_HEADS, MLA_NOPE + MLA_ROPE)
    w = jnp.pad(w, ((0, 0), (0, 0), (0, MLA_QK - MLA_NOPE - MLA_ROPE)))
    return w.reshape(MLA_Q_RANK, MLA_HEADS * MLA_QK).astype(BF16)


def kernel(x, c, ctx, c_ctx, ada_w, ada_b, norm_g, ab_w_in, dn_conv_w, dn_a_log, dn_dt_bias, dn_norm_g, mla_q_norm_g, mla_w_qb, mla_kv_norm_g, mla_w_kvb, ab_w_out, ffn_w_gate, ffn_w_up, ffn_w_down, gqa_w_qkv, gqa_b_qkv, gqa_sink, gqa_w_out, gqa_b_out, moe_w_router, moe_w_gate, moe_w_up, moe_w_down):
    assert x.shape == (BATCH, SEQ, D_MODEL) and ctx.shape == (BATCH, CTX_LEN, D_MODEL) and ada_w.shape[0] == 2
    h = (x.reshape(N_LAT, D_MODEL), ctx.reshape(N_CTX, D_MODEL))
    cvec = jnp.zeros((8, D_MODEL), F32).at[:BATCH].set(c).at[BATCH].set(c_ctx)
    mod = _ada(cvec, ada_w, ada_b)
    (cos1, sin1), (cos2, sin2) = _rope_tables()

    u = _modulate_in(*h, norm_g[0, 0], mod, 0, 0, 1)
    p = _matmul(u, _rearrange_w_in(ab_w_in[0]), None, F32, MM_ROW_TILE, 512)
    o_f, o_b = _deltanet(_dn_conv(p, dn_conv_w[0]), _dn_gates(p, dn_a_log[0], dn_dt_bias[0]))
    q, k, v = _mla_proj(p, mla_q_norm_g[0], mla_kv_norm_g[0], _rearrange_w_qb(mla_w_qb[0]),
                        mla_w_kvb[0].astype(BF16), cos1, sin1)
    y = _ab_out(o_f, o_b, p, dn_norm_g[0], _mla_attention(q, k, v), ab_w_out[0].astype(BF16))
    h, u = _post(h, y, norm_g[0, 1], norm_g[0, 2], mod, (0, 2), (0, 3), (0, 4), BF16)
    y = _ffn(u, ffn_w_gate[0], ffn_w_up[0], ffn_w_down[0])
    h, u = _post(h, y, norm_g[0, 3], norm_g[1, 0], mod, (0, 5), (1, 0), (1, 1), BF16)

    qkv = _gqa_qkv(u, gqa_w_qkv[0], gqa_b_qkv[0], cos2, sin2)
    o = _gqa_attention(qkv, gqa_sink[0])
    y = _matmul(o, gqa_w_out[0], gqa_b_out[0], F32, 1024, 512)
    h, u = _post(h, y, norm_g[1, 1], norm_g[1, 2], mod, (1, 2), (1, 3), (1, 4), F32)
    idx, wts = _router(u, moe_w_router[0])
    pos, slot_tok, slot_valid, blk_e, blk_sub = _moe_routing(idx)
    xs = _moe_gather(slot_tok, slot_valid, u)
    ys = _moe_ffn(blk_e, blk_sub, xs, moe_w_gate[0], moe_w_up[0], moe_w_down[0])
    out = _moe_combine(pos, ys, wts, h, norm_g[1, 3], mod, (1, 5))
    return out.reshape(BATCH, SEQ, D_MODEL)
```

```python
import functools
import math

import jax
import jax.numpy as jnp
from jax import lax
from jax.experimental import pallas as pl
from jax.experimental.pallas import tpu as pltpu

F32 = jnp.float32
BF16 = jnp.bfloat16
I32 = jnp.int32

D_MODEL = 2048
BATCH = 2
SEQ = 4096
CTX_LEN = 256
GRID_W = 64
EPS = 1e-6
ROPE_THETA = 10000.0

DN_HEADS = 8
DN_HEAD_DIM = 128
DN_CONV = 5
DN_CHUNK = 64
MLA_HEADS = 8
MLA_Q_RANK = 768
MLA_KV_RANK = 512
MLA_NOPE = 128
MLA_ROPE = 64
MLA_V = 128
GQA_HEADS = 32
GQA_KV_HEADS = 4
GQA_HEAD_DIM = 64
WINDOW = 128
Q_BLOCK = 128
D_FF = 7168
N_EXPERTS = 8
TOP_K = 2

DN_W = DN_HEADS * DN_HEAD_DIM
DN_QKV = 3 * DN_W
N_LAT = BATCH * SEQ
N_CTX = BATCH * CTX_LEN
N_TOK = N_LAT + N_CTX
GQA_Q = GQA_HEADS * GQA_HEAD_DIM
GQA_KV = GQA_KV_HEADS * GQA_HEAD_DIM

P_QKV = 0
P_Z = DN_QKV
P_CKV = P_Z + DN_W
P_CQ = P_CKV + MLA_KV_RANK
P_WIDTH = P_CQ + MLA_Q_RANK
P_MAIN_TILE = 896

ROW_TILE = 512
MM_ROW_TILE = 1088
MOE_BLOCK = 1024
MOE_SUB = 256
MOE_SLOTS = N_LAT * TOP_K + N_EXPERTS * MOE_BLOCK
VMEM_LIMIT = 56 * 1024 * 1024

_SDS = jax.ShapeDtypeStruct
_HIGH = lax.Precision.HIGHEST


def _cparams(sem, vmem=None):
    return pltpu.CompilerParams(dimension_semantics=sem, vmem_limit_bytes=vmem)


def _dot(a, b):
    return jnp.dot(a, b, preferred_element_type=F32)


def _dot_nt(a, b):
    return lax.dot_general(a, b, (((1,), (1,)), ((), ())), preferred_element_type=F32)


def _dot_tn(a, b):
    return lax.dot_general(a, b, (((0,), (0,)), ((), ())), preferred_element_type=F32)


def _dot_hi(a, b):
    return jnp.dot(a, b, preferred_element_type=F32, precision=_HIGH)


def _silu(x):
    return x * jax.nn.sigmoid(x)


def _rms(x, g):
    return x * lax.rsqrt(jnp.mean(x * x, axis=-1, keepdims=True) + EPS) * g


def _group_of_row(r0):
    return (r0 >= SEQ).astype(I32) + (r0 >= 2 * SEQ).astype(I32)


def _mod_spec(layer, k):
    return pl.BlockSpec((1, 8, D_MODEL), lambda *_: (layer, 0, k))


def _mod_row(ref, grp):
    return ref[0, pl.ds(grp, 1), :]


def _ada_kernel(c_ref, w_ref, b_ref, o_ref):
    s = _silu(c_ref[...]).astype(BF16)
    o_ref[0] = _dot(s, w_ref[0].astype(BF16)) + b_ref[0]


def _ada(cvec, ada_w, ada_b):
    n_layers, _, n = ada_w.shape
    tn = 1024
    return pl.pallas_call(
        _ada_kernel,
        out_shape=_SDS((n_layers, 8, n), F32),
        grid=(n_layers, n // tn),
        in_specs=[pl.BlockSpec((8, D_MODEL), lambda l, j: (0, 0)),
                  pl.BlockSpec((1, D_MODEL, tn), lambda l, j: (l, 0, j)),
                  pl.BlockSpec((1, 1, tn), lambda l, j: (l, 0, j))],
        out_specs=pl.BlockSpec((1, 8, tn), lambda l, j: (l, 0, j)),
        compiler_params=_cparams(("parallel", "parallel")),
        name="ada_mod",
    )(cvec, ada_w, ada_b.reshape(n_layers, 1, n))


_LAT_TILES = N_LAT // ROW_TILE
assert N_CTX == ROW_TILE


def _split_specs(width, col=0):
    lat = pl.BlockSpec((ROW_TILE, width), lambda i, *_: (jnp.minimum(i, _LAT_TILES - 1), col))
    ctx = pl.BlockSpec((ROW_TILE, width), lambda i, *_: (0, col))
    return [lat, ctx]


def _split_read(lat_ref, ctx_ref, sl=slice(None)):
    return jnp.where(pl.program_id(0) == _LAT_TILES, ctx_ref[:, sl], lat_ref[:, sl])


def _modin_kernel(x_ref, c_ref, g_ref, sh_ref, sc_ref, u_ref):
    grp = _group_of_row(pl.program_id(0) * ROW_TILE)
    h = _split_read(x_ref, c_ref)
    u = _rms(h, g_ref[...]) * (1.0 + _mod_row(sc_ref, grp)) + _mod_row(sh_ref, grp)
    u_ref[...] = u.astype(u_ref.dtype)


def _modulate_in(x, ctx, g, mod, layer, k_shift, k_scale):
    row = pl.BlockSpec((ROW_TILE, D_MODEL), lambda i: (i, 0))
    vec = pl.BlockSpec((1, D_MODEL), lambda i: (0, 0))
    return pl.pallas_call(
        _modin_kernel,
        out_shape=_SDS((N_TOK, D_MODEL), BF16),
        grid=(N_TOK // ROW_TILE,),
        in_specs=_split_specs(D_MODEL) + [vec, _mod_spec(layer, k_shift), _mod_spec(layer, k_scale)],
        out_specs=row,
        compiler_params=_cparams(("parallel",)),
        name="modulate_in",
    )(x, ctx, g.reshape(1, D_MODEL), mod, mod)


def _route_top2(logits):
    lane = lax.broadcasted_iota(I32, logits.shape, 1)
    neg = -jnp.inf
    l0 = jnp.where(lane < N_EXPERTS, logits, neg)
    m1 = jnp.max(l0, axis=-1, keepdims=True)
    i1 = jnp.min(jnp.where(l0 == m1, lane, 128), axis=-1, keepdims=True)
    l1 = jnp.where(lane == i1, neg, l0)
    m2 = jnp.max(l1, axis=-1, keepdims=True)
    i2 = jnp.min(jnp.where(l1 == m2, lane, 128), axis=-1, keepdims=True)
    e = jnp.exp(m2 - m1)
    w1 = 1.0 / (1.0 + e)
    idx = jnp.where(lane == 0, i1, jnp.where(lane == 1, i2, 0))
    wts = jnp.where(lane == 0, w1, jnp.where(lane == 1, e * w1, 0.0))
    return idx, wts


def _post_kernel(*refs, split, route):
    refs = list(refs)
    if split:
        h = _split_read(refs.pop(0), refs.pop(0))
    else:
        h = refs.pop(0)[...]
    y_ref, gp_ref, gate_ref, gn_ref, sh_ref, sc_ref = refs[:6]
    refs = refs[6:]
    wr_ref = refs.pop(0) if route else None
    hn_ref, u_ref = refs[:2]
    grp = _group_of_row(pl.program_id(0) * ROW_TILE)
    hn = h + _mod_row(gate_ref, grp) * _rms(y_ref[...], gp_ref[...])
    hn_ref[...] = hn
    u = _rms(hn, gn_ref[...]) * (1.0 + _mod_row(sc_ref, grp)) + _mod_row(sh_ref, grp)
    u_ref[...] = u.astype(u_ref.dtype)
    if route:
        idx_ref, wt_ref = refs[2:4]
        idx_ref[...], wt_ref[...] = _route_top2(_dot(u.astype(BF16), wr_ref[...]))


def _post(h, y, g_post, g_next, mod, gate_lk, shift_lk, scale_lk, u_dtype, w_router=None):
    n = y.shape[0]
    split = isinstance(h, tuple)
    route = w_router is not None
    row = pl.BlockSpec((ROW_TILE, D_MODEL), lambda i: (i, 0))
    vec = pl.BlockSpec((1, D_MODEL), lambda i: (0, 0))
    slab = pl.BlockSpec((ROW_TILE, 128), lambda i: (i, 0))
    h_specs, h_args = (_split_specs(D_MODEL), list(h)) if split else ([row], [h])
    extra_specs, extra_args, extra_out, extra_out_specs = [], [], (), ()
    if route:
        w = jnp.zeros((D_MODEL, 128), BF16).at[:, :N_EXPERTS].set(w_router.astype(BF16))
        extra_specs, extra_args = [pl.BlockSpec((D_MODEL, 128), lambda i: (0, 0))], [w]
        extra_out, extra_out_specs = (_SDS((n, 128), I32), _SDS((n, 128), F32)), (slab, slab)
    return pl.pallas_call(
        functools.partial(_post_kernel, split=split, route=route),
        out_shape=(_SDS((n, D_MODEL), F32), _SDS((n, D_MODEL), u_dtype)) + extra_out,
        grid=(n // ROW_TILE,),
        in_specs=h_specs + [row, vec, _mod_spec(*gate_lk), vec, _mod_spec(*shift_lk), _mod_spec(*scale_lk)] + extra_specs,
        out_specs=(row, row) + extra_out_specs,
        compiler_params=_cparams(("parallel",)),
        name="residual_post",
    )(*h_args, y, g_post.reshape(1, D_MODEL), mod, g_next.reshape(1, D_MODEL), mod, mod, *extra_args)


def _mm_kernel(a_ref, w_ref, b_ref, o_ref):
    o_ref[...] = (_dot(a_ref[...], w_ref[...].astype(BF16)) + b_ref[...]).astype(o_ref.dtype)


def _matmul(a, w, bias, out_dtype, tm, tn):
    m, k = a.shape
    n = w.shape[1]
    if bias is None:
        bias = jnp.zeros((n,), F32)
    return pl.pallas_call(
        _mm_kernel,
        out_shape=_SDS((m, n), out_dtype),
        grid=(m // tm, n // tn),
        in_specs=[pl.BlockSpec((tm, k), lambda i, j: (i, 0)),
                  pl.BlockSpec((k, tn), lambda i, j: (0, j)),
                  pl.BlockSpec((1, tn), lambda i, j: (0, j))],
        out_specs=pl.BlockSpec((tm, tn), lambda i, j: (i, j)),
        compiler_params=_cparams(("parallel", "arbitrary"), VMEM_LIMIT),
        name="matmul",
    )(a, w, bias.reshape(1, n))


FF_TILE = 512
SWIGLU_VMEM_LIMIT = 60 * 1024 * 1024
OUT_TILE = 256
FF_STEPS = D_FF // FF_TILE
OUT_STEPS = D_MODEL // OUT_TILE


def _swiglu_step(s, x_ref, wg_ref, wu_ref, wd_ref, o_ref, h_ref, rows=None, guard=True):
    total = x_ref.shape[0]
    rows = total if rows is None else rows

    @pl.when(guard & (s < FF_STEPS))
    def _():
        x = x_ref[0:rows, :]
        a = _dot(x, wg_ref[...].astype(BF16))
        b = _dot(x, wu_ref[...].astype(BF16))
        h_ref[0:rows, pl.ds(pl.multiple_of(s * FF_TILE, FF_TILE), FF_TILE)] = (_silu(a) * b).astype(BF16)

    @pl.when(guard & (s >= FF_STEPS))
    def _():
        o_ref[0:rows, :] = _dot(h_ref[0:rows, :], wd_ref[...].astype(BF16))
        if rows < total:
            o_ref[rows:total, :] = jnp.zeros((total - rows, o_ref.shape[1]), o_ref.dtype)


def _ffn_kernel(u_ref, wg_ref, wu_ref, wd_ref, o_ref, h_ref):
    _swiglu_step(pl.program_id(1), u_ref, wg_ref, wu_ref, wd_ref, o_ref, h_ref)


def _ffn(u, w_gate, w_up, w_down):
    m = u.shape[0]
    tm = MM_ROW_TILE

    def ff(i, s):
        return (0, jnp.minimum(s, FF_STEPS - 1))

    def out(s):
        return jnp.maximum(s - FF_STEPS, 0)

    return pl.pallas_call(
        _ffn_kernel,
        out_shape=_SDS((m, D_MODEL), F32),
        grid=(m // tm, FF_STEPS + OUT_STEPS),
        in_specs=[pl.BlockSpec((tm, D_MODEL), lambda i, s: (i, 0), pipeline_mode=pl.Buffered(1)),
                  pl.BlockSpec((D_MODEL, FF_TILE), ff),
                  pl.BlockSpec((D_MODEL, FF_TILE), ff),
                  pl.BlockSpec((D_FF, OUT_TILE), lambda i, s: (0, out(s)))],
        out_specs=pl.BlockSpec((tm, OUT_TILE), lambda i, s: (i, out(s))),
        scratch_shapes=[pltpu.VMEM((tm, D_FF), BF16)],
        compiler_params=_cparams(("parallel", "arbitrary"), SWIGLU_VMEM_LIMIT),
        name="swiglu_ffn",
    )(u, w_gate, w_up, w_down)


CONV_TILE = 256
CONV_HALO = 16
_SEQ_STARTS = (0, SEQ // CONV_TILE, 2 * SEQ // CONV_TILE, 2 * SEQ // CONV_TILE + 1)
_SEQ_LASTS = (SEQ // CONV_TILE - 1, 2 * SEQ // CONV_TILE - 1, 2 * SEQ // CONV_TILE, 2 * SEQ // CONV_TILE + 1)


def _conv_kernel(prev_ref, cur_ref, next_ref, w_ref, o_ref, buf):
    i = pl.program_id(0)
    j = pl.program_id(1)
    tm = CONV_TILE
    first = functools.reduce(jnp.logical_or, [i == s for s in _SEQ_STARTS])
    last = functools.reduce(jnp.logical_or, [i == s for s in _SEQ_LASTS])
    hb = CONV_HALO
    buf[0:hb, :] = jnp.where(first, 0.0, prev_ref[...].astype(F32))
    buf[hb:hb + tm, :] = cur_ref[...].astype(F32)
    buf[hb + tm:2 * hb + tm, :] = jnp.where(last, 0.0, next_ref[...].astype(F32))
    pad = DN_CONV // 2
    acc = buf[hb - pad:hb - pad + tm, :] * w_ref[0:1, :]
    for d in range(1, DN_CONV):
        acc = acc + buf[hb - pad + d:hb - pad + d + tm, :] * w_ref[d:d + 1, :]
    x = _silu(acc)
    q_scale = jnp.where(j == 0, DN_HEAD_DIM ** -0.5, 1.0)
    for hh in range(DN_HEADS):
        xh = x[:, hh * DN_HEAD_DIM:(hh + 1) * DN_HEAD_DIM]
        inv = lax.rsqrt(jnp.sum(xh * xh, axis=-1, keepdims=True) + EPS) * q_scale
        o_ref[:, hh * DN_HEAD_DIM:(hh + 1) * DN_HEAD_DIM] = xh * jnp.where(j == 2, 1.0, inv)


def _dn_conv(p, conv_w):
    n = p.shape[0]
    tm = CONV_TILE
    hb = CONV_HALO
    per_tile = tm // hb
    n_halo = n // hb
    return pl.pallas_call(
        _conv_kernel,
        out_shape=_SDS((n, DN_QKV), F32),
        grid=(n // tm, 3),
        in_specs=[pl.BlockSpec((hb, DN_W), lambda i, j: (jnp.maximum(i * per_tile - 1, 0), j)),
                  pl.BlockSpec((tm, DN_W), lambda i, j: (i, j)),
                  pl.BlockSpec((hb, DN_W), lambda i, j: (jnp.minimum((i + 1) * per_tile, n_halo - 1), j)),
                  pl.BlockSpec((DN_CONV, DN_W), lambda i, j: (0, j))],
        out_specs=pl.BlockSpec((tm, DN_W), lambda i, j: (i, j)),
        scratch_shapes=[pltpu.VMEM((tm + 2 * hb, DN_W), F32)],
        compiler_params=_cparams(("parallel", "parallel")),
        name="dn_conv",
    )(p, p, p, conv_w)


def _gates_kernel(m_ref, a_ref, dt_ref, o_ref):
    x = pltpu.roll(m_ref[...], 64, 1)
    lane = lax.broadcasted_iota(I32, x.shape, 1)
    z = x + dt_ref[...]
    softplus = jnp.maximum(z, 0.0) + jnp.log1p(jnp.exp(-jnp.abs(z)))
    g = a_ref[...] * softplus
    o_ref[...] = jnp.where(lane < 2 * DN_HEADS, jax.nn.sigmoid(x), jnp.where(lane < 4 * DN_HEADS, g, 0.0))


def _dn_gates(p, a_log, dt_bias):
    n = p.shape[0]
    neg_a = jnp.zeros((1, 128), F32).at[0, 2 * DN_HEADS:4 * DN_HEADS].set(-jnp.exp(a_log.astype(F32)).reshape(-1))
    dtb = jnp.zeros((1, 128), F32).at[0, 2 * DN_HEADS:4 * DN_HEADS].set(dt_bias.astype(F32).reshape(-1))
    tm = ROW_TILE
    vec = pl.BlockSpec((1, 128), lambda i: (0, 0))
    return pl.pallas_call(
        _gates_kernel,
        out_shape=_SDS((n, 128), F32),
        grid=(n // tm,),
        in_specs=[pl.BlockSpec((tm, 128), lambda i: (i, 0)), vec, vec],
        out_specs=pl.BlockSpec((tm, 128), lambda i: (i, 0)),
        compiler_params=_cparams(("parallel",)),
        name="dn_gates",
    )(p, neg_a, dtb)


_CTX_CHUNKS = CTX_LEN // DN_CHUNK
_LAT_CHUNKS = SEQ // DN_CHUNK
_DN_STEPS = _CTX_CHUNKS + _LAT_CHUNKS


def _dn_fwd_block(b, t):
    return jnp.where(t < _CTX_CHUNKS, (2 * SEQ + b * CTX_LEN) // DN_CHUNK + t, b * _LAT_CHUNKS + (t - _CTX_CHUNKS))


def _dn_bwd_block(b, t):
    return jnp.where(t < _CTX_CHUNKS, (2 * SEQ + b * CTX_LEN) // DN_CHUNK + (_CTX_CHUNKS - 1 - t),
                     b * _LAT_CHUNKS + (_DN_STEPS - 1 - t))


DN_GROUP = 4
_GROUP_ROWS = DN_GROUP * DN_CHUNK
_GROUP_STATE = DN_GROUP * DN_HEAD_DIM


_DN_BASE = 8


def _block_diag(x):
    zero = jnp.zeros((DN_CHUNK, DN_HEAD_DIM), x.dtype)
    rows = []
    for hh in range(DN_GROUP):
        piece = x[hh * DN_CHUNK:(hh + 1) * DN_CHUNK]
        rows.append(jnp.concatenate([piece if j == hh else zero for j in range(DN_GROUP)], axis=1))
    return jnp.concatenate(rows, axis=0)


def _dn_groups(items, eye, base_mask):
    n = range(len(items))
    it = items
    gx = [jnp.broadcast_to(it[g]["gc"], (_GROUP_ROWS, _GROUP_ROWS)) for g in n]
    gamma = [jnp.where(it[g]["incl"], jnp.exp(jnp.where(it[g]["incl"], gx[g] - gx[g].T, 0.0)), 0.0) for g in n]
    kb = [it[g]["k"].astype(BF16) for g in n]
    kk = [_dot_nt(kb[g], kb[g]) for g in n]
    a = [jnp.where(it[g]["strict"], it[g]["beta"] * kk[g] * gamma[g], 0.0) for g in n]
    d = [jnp.where(base_mask, a[g], 0.0) for g in n]
    tinv = [eye - d[g] for g in n]
    pw = [d[g].astype(BF16) for g in n]
    for _ in range(int(math.log2(_DN_BASE)) - 1):
        pw = [_dot(pw[g], pw[g]).astype(BF16) for g in n]
        tinv = [tinv[g] + _dot(tinv[g].astype(BF16), pw[g]) for g in n]
    for lvl in range(len(it[0]["levels"])):
        tb = [tinv[g].astype(BF16) for g in n]
        mt = [_dot(jnp.where(it[g]["levels"][lvl], a[g], 0.0).astype(BF16), tb[g]).astype(BF16) for g in n]
        tinv = [tinv[g] - _dot(tb[g], mt[g]) for g in n]
    tb = [tinv[g].astype(BF16) for g in n]
    egc = [jnp.exp(it[g]["gc"]) for g in n]
    rhs = [jnp.concatenate([it[g]["v"] * it[g]["beta"], it[g]["k"] * (it[g]["beta"] * egc[g])], axis=1) for g in n]
    rhs_hi = [rhs[g].astype(BF16) for g in n]
    rhs_lo = [(rhs[g] - rhs_hi[g].astype(F32)).astype(BF16) for g in n]
    sol = [_dot(jnp.concatenate([tb[g], tb[g]], axis=1), jnp.concatenate([rhs_hi[g], rhs_lo[g]], axis=0)) for g in n]
    qk = [(_dot_nt(it[g]["q"].astype(BF16), kb[g]) * gamma[g]).astype(BF16) for g in n]
    s = [it[g]["s_ref"][...] for g in n]
    sb = [s[g].astype(BF16) for g in n]
    v_new = [sol[g][:, :DN_HEAD_DIM] - _dot(_block_diag(sol[g][:, DN_HEAD_DIM:].astype(BF16)), sb[g]) for g in n]
    vb = [v_new[g].astype(BF16) for g in n]
    o = [_dot(_block_diag((it[g]["q"] * egc[g]).astype(BF16)), sb[g]) + _dot(qk[g], vb[g]) for g in n]
    for g in n:
        k_dec = _block_diag((it[g]["k"] * jnp.exp(it[g]["g_last"] - it[g]["gc"])).astype(BF16))
        it[g]["s_ref"][...] = s[g] * jnp.exp(it[g]["g_last_state"]) + _dot_tn(k_dec, vb[g])
    return o


def _stack_cols(x, lanes, rows_each):
    pieces = [x[:, l:l + 1] for l in lanes]
    if rows_each is not None:
        pieces = [jnp.broadcast_to(pc, (rows_each, 1)) for pc in pieces]
    return jnp.concatenate(pieces, axis=0)


def _dn_kernel(*refs):
    n_in = BATCH * 2 * 4
    in_refs, (ofl_ref, ofc_ref, obl_ref, obc_ref), (s_ref, o_scr) = refs[:n_in], refs[n_in:n_in + 4], refs[n_in + 4:]
    t = pl.program_id(0)

    @pl.when(t == 0)
    def _():
        s_ref[...] = jnp.zeros_like(s_ref)

    c = DN_CHUNK
    ri = lax.broadcasted_iota(I32, (_GROUP_ROWS, _GROUP_ROWS), 0)
    ci = lax.broadcasted_iota(I32, (_GROUP_ROWS, _GROUP_ROWS), 1)
    same_head = (ri // c) == (ci // c)
    eye = (ri == ci).astype(F32)
    r1 = lax.broadcasted_iota(I32, (c, c), 0)
    c1 = lax.broadcasted_iota(I32, (c, c), 1)
    base_mask = (ri // _DN_BASE) == (ci // _DN_BASE)
    masks = []
    for d in range(2):
        incl = same_head & ((ri >= ci) if d == 0 else (ri <= ci))
        strict = same_head & ((ri > ci) if d == 0 else (ri < ci))
        levels = []
        size = _DN_BASE
        while size < c:
            lo_blk, hi_blk = ((ci, ri) if d == 0 else (ri, ci))
            levels.append(((ri // (2 * size)) == (ci // (2 * size)))
                          & ((lo_blk // size) % 2 == 0) & ((hi_blk // size) % 2 == 1))
            size *= 2
        masks.append((incl, strict, levels))
    items, where = [], []
    for b, d in [(b, d) for b in range(BATCH) for d in range(2)]:
        q_ref, k_ref, v_ref, g_ref = in_refs[(b * 2 + d) * 4:(b * 2 + d) * 4 + 4]
        incl, strict, levels = masks[d]
        gates = g_ref[...]
        tri = ((r1 >= c1) if d == 0 else (r1 <= c1)).astype(BF16)
        g_hi = gates.astype(BF16)
        g_r1 = gates - g_hi.astype(F32)
        g_mid = g_r1.astype(BF16)
        g_lo = (g_r1 - g_mid.astype(F32)).astype(BF16)
        gc = _dot(tri, g_hi) + _dot(tri, g_mid) + _dot(tri, g_lo)
        last_row = c - 1 if d == 0 else 0
        tot = gc[last_row:last_row + 1, :]
        for grp in range(DN_HEADS // DN_GROUP):
            heads = range(grp * DN_GROUP, (grp + 1) * DN_GROUP)
            lanes_b = [d * DN_HEADS + hh for hh in heads]
            lanes_g = [2 * DN_HEADS + lb for lb in lanes_b]

            def stack(ref):
                return jnp.concatenate([ref[:, hh * DN_HEAD_DIM:(hh + 1) * DN_HEAD_DIM] for hh in heads], axis=0)

            items.append(dict(q=stack(q_ref), k=stack(k_ref), v=stack(v_ref),
                              beta=_stack_cols(gates, lanes_b, None), gc=_stack_cols(gc, lanes_g, None),
                              g_last=_stack_cols(tot, lanes_g, c), g_last_state=_stack_cols(tot, lanes_g, DN_HEAD_DIM),
                              incl=incl, strict=strict, levels=levels, s_ref=s_ref.at[b, d, grp]))
            where.append((d, b, heads))
    outs = _dn_groups(items, eye, base_mask)
    for o, (d, b, heads) in zip(outs, where):
        for j, hh in enumerate(heads):
            o_scr[d, b, :, hh * DN_HEAD_DIM:(hh + 1) * DN_HEAD_DIM] = o[j * c:(j + 1) * c]

    @pl.when(t < _CTX_CHUNKS)
    def _():
        ofc_ref[...] = o_scr[0]
        obc_ref[...] = o_scr[1]

    @pl.when(t >= _CTX_CHUNKS)
    def _():
        ofl_ref[...] = o_scr[0]
        obl_ref[...] = o_scr[1]


def _deltanet(qkv, gates):
    c = DN_CHUNK
    ins, args = [], []
    for b in range(BATCH):
        for fn in (_dn_fwd_block, _dn_bwd_block):
            for width, col, arr in ((DN_W, 0, qkv), (DN_W, 1, qkv), (DN_W, 2, qkv), (128, 0, gates)):
                ins.append(pl.BlockSpec((c, width), functools.partial(lambda t, b, fn, col: (fn(b, t), col),
                                                                      b=b, fn=fn, col=col)))
                args.append(arr)
    last = _DN_STEPS - 1
    lat = _SDS((BATCH, SEQ, DN_W), F32)
    ctx = _SDS((BATCH, CTX_LEN, DN_W), F32)
    blk = (BATCH, c, DN_W)
    outs = pl.pallas_call(
        _dn_kernel,
        out_shape=(lat, ctx, lat, ctx),
        grid=(_DN_STEPS,),
        in_specs=ins,
        out_specs=(pl.BlockSpec(blk, lambda t: (0, jnp.maximum(t - _CTX_CHUNKS, 0), 0)),
                   pl.BlockSpec(blk, lambda t: (0, jnp.minimum(t, _CTX_CHUNKS - 1), 0)),
                   pl.BlockSpec(blk, lambda t: (0, jnp.minimum(last - t, _LAT_CHUNKS - 1), 0)),
                   pl.BlockSpec(blk, lambda t: (0, jnp.maximum(_CTX_CHUNKS - 1 - t, 0), 0))),
        scratch_shapes=[pltpu.VMEM((BATCH, 2, DN_HEADS // DN_GROUP, _GROUP_STATE, DN_HEAD_DIM), F32),
                        pltpu.VMEM((2, BATCH, c, DN_W), F32)],
        compiler_params=_cparams(("arbitrary",)),
        name="deltanet",
    )(*args)
    ofl, ofc, obl, obc = outs
    return ((ofl.reshape(N_LAT, DN_W), ofc.reshape(N_CTX, DN_W)),
            (obl.reshape(N_LAT, DN_W), obc.reshape(N_CTX, DN_W)))


def _rope_tables():
    t = jnp.arange(SEQ)
    row = (t // GRID_W).astype(F32)
    col = (t % GRID_W).astype(F32)
    n_freq = MLA_ROPE // 4
    inv_freq = ROPE_THETA ** (-jnp.arange(n_freq, dtype=F32) / n_freq)
    ang = jnp.concatenate([row[:, None] * inv_freq, col[:, None] * inv_freq], axis=-1)
    cos, sin = jnp.cos(ang), jnp.sin(ang)
    cos64 = jnp.concatenate([cos, cos], axis=-1)
    sin64 = jnp.concatenate([-sin, sin], axis=-1)

    def assemble(lat_cos, lat_sin):
        c = jnp.concatenate([jnp.tile(lat_cos, (BATCH, 1)), jnp.ones((N_CTX, 128), F32)], axis=0)
        s = jnp.concatenate([jnp.tile(lat_sin, (BATCH, 1)), jnp.zeros((N_CTX, 128), F32)], axis=0)
        return c, s

    cos_l = jnp.concatenate([cos64, jnp.ones((SEQ, 64), F32)], axis=-1)
    sin_l = jnp.concatenate([sin64, jnp.zeros((SEQ, 64), F32)], axis=-1)
    one_head = assemble(cos_l, sin_l)
    two_heads = assemble(jnp.tile(cos64, (1, 2)), jnp.tile(sin64, (1, 2)))
    return one_head, two_heads


def _rope128(x, cosv, sinv):
    lane = lax.broadcasted_iota(I32, x.shape, 1)
    swapped = jnp.where((lane % 64) < 32, pltpu.roll(x, 96, 1), pltpu.roll(x, 32, 1))
    return x * cosv + swapped * sinv


MLA_QK = 256
MLA_VX = 256


def _mla_proj_kernel(ckv_ref, cq_ref, misc_ref, gq_ref, gkv_ref, wq_ref, wkv_ref, cos_ref, sin_ref,
                     q_ref, k_ref, v_ref):
    cq = _rms(cq_ref[...].astype(F32), gq_ref[...]).astype(BF16)
    ckv = _rms(ckv_ref[...].astype(F32), gkv_ref[...]).astype(BF16)
    qf = _dot(cq, wq_ref[...])
    kvf = _dot(ckv, wkv_ref[...])
    cosv, sinv = cos_ref[...], sin_ref[...]
    lane = lax.broadcasted_iota(I32, cosv.shape, 1)
    k_rope = _rope128(jnp.where(lane < MLA_ROPE, misc_ref[...], 0.0), cosv, sinv).astype(BF16)
    scale = (MLA_NOPE + MLA_ROPE) ** -0.5
    for hh in range(MLA_HEADS):
        lo, mid, hi = hh * MLA_QK, hh * MLA_QK + 128, (hh + 1) * MLA_QK
        q_ref[:, lo:mid] = (qf[:, lo:mid] * scale).astype(BF16)
        q_ref[:, mid:hi] = (_rope128(qf[:, mid:hi], cosv, sinv) * scale).astype(BF16)
        k_ref[:, lo:mid] = kvf[:, lo:mid].astype(BF16)
        k_ref[:, mid:hi] = k_rope
        v_ref[:, lo:mid] = kvf[:, mid:hi].astype(BF16)
        v_ref[:, mid:hi] = jnp.ones((kvf.shape[0], MLA_VX - MLA_V), BF16)


def _mla_proj(p, p_misc, q_norm_g, kv_norm_g, w_q, w_kv, cosv, sinv):
    n = p.shape[0]
    tm = ROW_TILE
    wide = MLA_HEADS * MLA_QK

    def rows(width, col):
        return pl.BlockSpec((tm, width), lambda i: (i, col))

    def whole(shape):
        return pl.BlockSpec(shape, lambda i: (0, 0))

    return pl.pallas_call(
        _mla_proj_kernel,
        out_shape=(_SDS((n, wide), BF16), _SDS((n, wide), BF16), _SDS((n, MLA_HEADS * MLA_VX), BF16)),
        grid=(n // tm,),
        in_specs=[rows(MLA_KV_RANK, P_CKV // MLA_KV_RANK), rows(MLA_Q_RANK, P_CQ // MLA_Q_RANK),
                  rows(128, 0), whole((1, MLA_Q_RANK)), whole((1, MLA_KV_RANK)),
                  whole((MLA_Q_RANK, wide)), whole((MLA_KV_RANK, wide)), rows(128, 0), rows(128, 0)],
        out_specs=(rows(wide, 0), rows(wide, 0), rows(MLA_HEADS * MLA_VX, 0)),
        compiler_params=_cparams(("parallel",), VMEM_LIMIT),
        name="mla_proj",
    )(p, p, p_misc, q_norm_g.reshape(1, -1), kv_norm_g.reshape(1, -1), w_q, w_kv, cosv, sinv)


MLA_KEY_CHUNK = 1024


def _flash_kernel(*refs, chunks):
    n_seg = (len(refs) - 2) // 2
    q_ref, o_ref = refs[0], refs[-1]
    k_refs, v_refs = refs[1:1 + n_seg], refs[1 + n_seg:1 + 2 * n_seg]
    q = q_ref[...]

    def scores(c):
        seg, off, size = c
        return _dot_nt(q, k_refs[seg][off:off + size, :])

    m = jnp.full((q.shape[0], 1), -jnp.inf, F32)
    acc = jnp.zeros((q.shape[0], MLA_VX), F32)
    s_next = scores(chunks[0])
    for j, (seg, off, size) in enumerate(chunks):
        s = s_next
        if j + 1 < len(chunks):
            s_next = scores(chunks[j + 1])
        m_new = jnp.maximum(m, jnp.max(s, axis=-1, keepdims=True))
        p = jnp.exp(s - m_new).astype(BF16)
        acc = jnp.exp(m - m_new) * acc + _dot(p, v_refs[seg][off:off + size, :])
        m = m_new
    o_ref[...] = (acc[:, :MLA_V] / acc[:, MLA_V:]).astype(o_ref.dtype)


def _mla_attention(q, k, v):
    tq = 512
    nq = SEQ // tq
    ctx_blk = 2 * SEQ // CTX_LEN
    lat_chunks = ((0, 0, CTX_LEN),) + tuple((1, off, MLA_KEY_CHUNK) for off in range(0, SEQ, MLA_KEY_CHUNK))
    lat = pl.pallas_call(
        functools.partial(_flash_kernel, chunks=lat_chunks),
        out_shape=_SDS((N_LAT, MLA_HEADS * MLA_V), BF16),
        grid=(BATCH, MLA_HEADS, nq),
        in_specs=[pl.BlockSpec((tq, MLA_QK), lambda b, h, i: (b * nq + i, h)),
                  pl.BlockSpec((CTX_LEN, MLA_QK), lambda b, h, i: (ctx_blk + b, h)),
                  pl.BlockSpec((SEQ, MLA_QK), lambda b, h, i: (b, h)),
                  pl.BlockSpec((CTX_LEN, MLA_VX), lambda b, h, i: (ctx_blk + b, h)),
                  pl.BlockSpec((SEQ, MLA_VX), lambda b, h, i: (b, h))],
        out_specs=pl.BlockSpec((tq, MLA_V), lambda b, h, i: (b * nq + i, h)),
        compiler_params=_cparams(("parallel", "parallel", "arbitrary"), VMEM_LIMIT),
        name="mla_attn_latent",
    )(q, k, k, v, v)
    ctx = pl.pallas_call(
        functools.partial(_flash_kernel, chunks=((0, 0, CTX_LEN),)),
        out_shape=_SDS((N_CTX, MLA_HEADS * MLA_V), BF16),
        grid=(BATCH, MLA_HEADS),
        in_specs=[pl.BlockSpec((CTX_LEN, MLA_QK), lambda b, h: (ctx_blk + b, h)),
                  pl.BlockSpec((CTX_LEN, MLA_QK), lambda b, h: (ctx_blk + b, h)),
                  pl.BlockSpec((CTX_LEN, MLA_VX), lambda b, h: (ctx_blk + b, h))],
        out_specs=pl.BlockSpec((CTX_LEN, MLA_V), lambda b, h: (b, h)),
        compiler_params=_cparams(("parallel", "parallel")),
        name="mla_attn_context",
    )(q, k, v)
    return lat, ctx


def _ab_out_kernel(ofl_ref, ofc_ref, obl_ref, obc_ref, ml_ref, mc_ref, z_ref, g_ref, w_ref, o_ref, a_ref):
    @pl.when(pl.program_id(1) == 0)
    def _():
        for hh in range(DN_HEADS):
            sl = slice(hh * DN_HEAD_DIM, (hh + 1) * DN_HEAD_DIM)
            o = _split_read(ofl_ref, ofc_ref, sl) + _split_read(obl_ref, obc_ref, sl)
            a_ref[:, sl] = (_rms(o, g_ref[...]) * _silu(z_ref[:, sl].astype(F32))).astype(BF16)
        a_ref[:, DN_W:] = _split_read(ml_ref, mc_ref)

    o_ref[...] = _dot(a_ref[...], w_ref[...].astype(BF16))


def _ab_out(o_f, o_b, p, dn_norm_g, mla_o, w_out):
    tm, tn = ROW_TILE, 1024
    half = DN_W
    return pl.pallas_call(
        _ab_out_kernel,
        out_shape=_SDS((N_TOK, D_MODEL), F32),
        grid=(N_TOK // tm, D_MODEL // tn),
        in_specs=_split_specs(half) + _split_specs(half) + _split_specs(half) + [
            pl.BlockSpec((tm, half), lambda i, j: (i, P_Z // half)),
            pl.BlockSpec((1, DN_HEAD_DIM), lambda i, j: (0, 0)),
            pl.BlockSpec((2 * half, tn), lambda i, j: (0, j))],
        out_specs=pl.BlockSpec((tm, tn), lambda i, j: (i, j)),
        scratch_shapes=[pltpu.VMEM((tm, 2 * half), BF16)],
        compiler_params=_cparams(("parallel", "arbitrary"), VMEM_LIMIT),
        name="ab_out_proj",
    )(*o_f, *o_b, *mla_o, p, dn_norm_g.reshape(1, -1), w_out)


def _mm_rope_kernel(a_ref, w_ref, b_ref, cos_ref, sin_ref, o_ref, *, tn, rope_cols):
    y = _dot(a_ref[...], w_ref[...].astype(BF16)) + b_ref[...]
    col0 = pl.program_id(1) * tn
    cosv, sinv = cos_ref[...], sin_ref[...]
    for s in range(tn // 128):
        ys = y[:, s * 128:(s + 1) * 128]
        roped = _rope128(ys, cosv, sinv)
        o_ref[:, s * 128:(s + 1) * 128] = jnp.where(col0 + s * 128 < rope_cols, roped, ys).astype(o_ref.dtype)


def _gqa_qkv(u, w, bias, cosv, sinv):
    m, k = u.shape
    n = w.shape[1]
    tm, tn = MM_ROW_TILE, 1280
    return pl.pallas_call(
        functools.partial(_mm_rope_kernel, tn=tn, rope_cols=GQA_Q + GQA_KV),
        out_shape=_SDS((m, n), BF16),
        grid=(m // tm, n // tn),
        in_specs=[pl.BlockSpec((tm, k), lambda i, j: (i, 0)),
                  pl.BlockSpec((k, tn), lambda i, j: (0, j)),
                  pl.BlockSpec((1, tn), lambda i, j: (0, j)),
                  pl.BlockSpec((tm, 128), lambda i, j: (i, 0)),
                  pl.BlockSpec((tm, 128), lambda i, j: (i, 0))],
        out_specs=pl.BlockSpec((tm, tn), lambda i, j: (i, j)),
        compiler_params=_cparams(("parallel", "arbitrary"), VMEM_LIMIT),
        name="gqa_qkv_rope",
    )(u, w, bias.reshape(1, n), cosv, sinv)


def _gqa_kernel(sink_ref, q_ref, kp_ref, kc_ref, kn_ref, vp_ref, vc_ref, vn_ref, kx_ref, vx_ref, o_ref):
    i = pl.program_id(1)
    nb = SEQ // Q_BLOCK
    n_keys = CTX_LEN + 3 * Q_BLOCK
    groups = GQA_HEADS // GQA_KV_HEADS
    slabs = groups // 2
    dh = GQA_HEAD_DIM
    qi = lax.broadcasted_iota(I32, (Q_BLOCK, n_keys), 0)
    kj = lax.broadcasted_iota(I32, (Q_BLOCK, n_keys), 1) - CTX_LEN
    rel = kj - Q_BLOCK - qi
    valid = (kj < 0) | ((jnp.abs(rel) <= WINDOW) & ((kj >= Q_BLOCK) | (i > 0)) & ((kj < 2 * Q_BLOCK) | (i < nb - 1)))
    bias = jnp.where(valid, 0.0, -jnp.inf).astype(F32)
    bias = jnp.concatenate([bias] * slabs, axis=0)
    k_all = jnp.concatenate([kx_ref[...], kp_ref[...], kc_ref[...], kn_ref[...]], axis=0)
    v_all = jnp.concatenate([vx_ref[...], vp_ref[...], vc_ref[...], vn_ref[...]], axis=0)
    low = lax.broadcasted_iota(I32, (n_keys, 128), 1) < dh

    def block_diag(x_all, kvh):
        pair = x_all[:, (kvh // 2) * 128:(kvh // 2 + 1) * 128].astype(F32)
        other = pltpu.roll(pair, dh, 1)
        first, second = (pair, other) if kvh % 2 == 0 else (other, pair)
        return jnp.concatenate([jnp.where(low, first, 0.0), jnp.where(low, 0.0, second)], axis=0).astype(BF16)

    def scores(kvh):
        q = jnp.concatenate([q_ref[:, (kvh * slabs + r) * 128:(kvh * slabs + r + 1) * 128] for r in range(slabs)], axis=0)
        return _dot_nt(q * (dh ** -0.5), block_diag(k_all, kvh))

    def softmax(kvh, s):
        ps, dens = [], []
        for half in range(2):
            sh = s[:, half * n_keys:(half + 1) * n_keys] + bias
            sink = jnp.concatenate([jnp.full((Q_BLOCK, 1), sink_ref[(kvh * slabs + r) * 2 + half], F32)
                                    for r in range(slabs)], axis=0)
            m = jnp.maximum(jnp.max(sh, axis=-1, keepdims=True), sink)
            p = jnp.exp(sh - m)
            dens.append(jnp.sum(p, axis=-1, keepdims=True) + jnp.exp(sink - m))
            ps.append(p.astype(BF16))
        return jnp.concatenate(ps, axis=1), dens

    def finish(kvh, p, dens):
        o = _dot(p, block_diag(v_all, kvh))
        lane_low = lax.broadcasted_iota(I32, o.shape, 1) < dh
        o = o * jnp.where(lane_low, 1.0 / dens[0], 1.0 / dens[1])
        for r in range(slabs):
            o_ref[:, (kvh * slabs + r) * 128:(kvh * slabs + r + 1) * 128] = o[r * Q_BLOCK:(r + 1) * Q_BLOCK].astype(o_ref.dtype)

    s_next = scores(0)
    for kvh in range(GQA_KV_HEADS):
        s_cur = s_next
        if kvh + 1 < GQA_KV_HEADS:
            s_next = scores(kvh + 1)
        p, dens = softmax(kvh, s_cur)
        finish(kvh, p, dens)


def _gqa_attention(qkv, sink):
    nb = SEQ // Q_BLOCK
    kcol, vcol = GQA_Q // GQA_KV, GQA_Q // GQA_KV + 1
    ctx_blk = 2 * SEQ // CTX_LEN

    def win(col, shift):
        return pl.BlockSpec((Q_BLOCK, GQA_KV), lambda b, i: (b * nb + jnp.clip(i + shift, 0, nb - 1), col))

    return pl.pallas_call(
        _gqa_kernel,
        out_shape=_SDS((N_LAT, GQA_Q), BF16),
        grid=(BATCH, nb),
        in_specs=[pl.BlockSpec(memory_space=pltpu.SMEM),
                  pl.BlockSpec((Q_BLOCK, GQA_Q), lambda b, i: (b * nb + i, 0)),
                  win(kcol, -1), win(kcol, 0), win(kcol, 1), win(vcol, -1), win(vcol, 0), win(vcol, 1),
                  pl.BlockSpec((CTX_LEN, GQA_KV), lambda b, i: (ctx_blk + b, kcol)),
                  pl.BlockSpec((CTX_LEN, GQA_KV), lambda b, i: (ctx_blk + b, vcol))],
        out_specs=pl.BlockSpec((Q_BLOCK, GQA_Q), lambda b, i: (b * nb + i, 0)),
        compiler_params=_cparams(("parallel", "arbitrary")),
        name="gqa_window_attn",
    )(sink.astype(F32), qkv, qkv, qkv, qkv, qkv, qkv, qkv, qkv, qkv)


GATHER_TILE = 256


def _row_copy(src_hbm, row, dst, r, sem):
    return pltpu.make_async_copy(src_hbm.at[pl.ds(row, 1), :], dst.at[pl.ds(r, 1), :], sem)


def _gather_kernel(tok_ref, tv_ref, u_hbm, valid_ref, o_ref, buf, sem):
    i = pl.program_id(0)
    n = pl.num_programs(0)

    def start_tile(t, slot):
        @pl.when(tv_ref[t] > 0)
        def _():
            def issue(r8, carry):
                for j in range(8):
                    r = r8 * 8 + j
                    _row_copy(u_hbm, tok_ref[t * GATHER_TILE + r], buf.at[slot], r, sem.at[slot]).start(priority=j % 2)
                return carry

            lax.fori_loop(0, GATHER_TILE // 8, issue, 0)

    @pl.when(i == 0)
    def _():
        start_tile(0, 0)

    @pl.when(i + 1 < n)
    def _():
        start_tile(i + 1, (i + 1) % 2)

    slot = i % 2

    @pl.when(tv_ref[i] > 0)
    def _():
        def drain(r, carry):
            _row_copy(u_hbm, 0, buf.at[slot], r, sem.at[slot]).wait()
            return carry

        lax.fori_loop(0, GATHER_TILE, drain, 0, unroll=8)
        o_ref[...] = jnp.where(valid_ref[...] > 0.0, buf[slot], 0.0).astype(o_ref.dtype)

    @pl.when(tv_ref[i] == 0)
    def _():
        o_ref[...] = jnp.zeros_like(o_ref)


def _moe_gather(slot_tok, slot_valid, u):
    n_slots = slot_tok.shape[0]
    n_tiles = n_slots // GATHER_TILE
    tile_valid = (jnp.max(slot_valid.reshape(n_tiles, GATHER_TILE), axis=1) > 0.0).astype(I32)
    return pl.pallas_call(
        _gather_kernel,
        out_shape=_SDS((n_slots, D_MODEL), BF16),
        grid_spec=pltpu.PrefetchScalarGridSpec(
            num_scalar_prefetch=2,
            grid=(n_tiles,),
            in_specs=[pl.BlockSpec(memory_space=pl.ANY),
                      pl.BlockSpec((GATHER_TILE, 1), lambda i, tok, tv: (i, 0))],
            out_specs=pl.BlockSpec((GATHER_TILE, D_MODEL), lambda i, tok, tv: (i, 0)),
            scratch_shapes=[pltpu.VMEM((2, GATHER_TILE, D_MODEL), F32), pltpu.SemaphoreType.DMA((2,))]),
        compiler_params=_cparams(("arbitrary",)),
        name="moe_gather",
    )(slot_tok, tile_valid, u, slot_valid.reshape(n_slots, 1))


def _moe_ffn_kernel(be_ref, bv_ref, x_ref, wg_ref, wu_ref, wd_ref, o_ref, h_ref):
    i, s = pl.program_id(0), pl.program_id(1)
    n_sub = bv_ref[i]
    for k in range(1, MOE_BLOCK // MOE_SUB + 1):
        _swiglu_step(s, x_ref, wg_ref.at[0], wu_ref.at[0], wd_ref.at[0], o_ref, h_ref,
                     rows=k * MOE_SUB, guard=n_sub == k)

    @pl.when((n_sub == 0) & (s >= FF_STEPS))
    def _():
        o_ref[...] = jnp.zeros_like(o_ref)


def _moe_ffn(block_expert, block_valid, xs, w_gate, w_up, w_down):
    n_slots = xs.shape[0]
    tm = MOE_BLOCK

    def ff(i, s, be, bv):
        return (be[i], 0, jnp.where(bv[i] > 0, jnp.minimum(s, FF_STEPS - 1), FF_STEPS - 1))

    def down(i, s, be, bv):
        return (be[i], 0, jnp.where(bv[i] > 0, jnp.maximum(s - FF_STEPS, 0), OUT_STEPS - 1))

    return pl.pallas_call(
        _moe_ffn_kernel,
        out_shape=_SDS((n_slots, D_MODEL), F32),
        grid_spec=pltpu.PrefetchScalarGridSpec(
            num_scalar_prefetch=2,
            grid=(n_slots // tm, FF_STEPS + OUT_STEPS),
            in_specs=[pl.BlockSpec((tm, D_MODEL), lambda i, s, be, bv: (i, 0), pipeline_mode=pl.Buffered(1)),
                      pl.BlockSpec((1, D_MODEL, FF_TILE), ff),
                      pl.BlockSpec((1, D_MODEL, FF_TILE), ff),
                      pl.BlockSpec((1, D_FF, OUT_TILE), down)],
            out_specs=pl.BlockSpec((tm, OUT_TILE), lambda i, s, be, bv: (i, jnp.maximum(s - FF_STEPS, 0))),
            scratch_shapes=[pltpu.VMEM((tm, D_FF), BF16)]),
        compiler_params=_cparams(("arbitrary", "arbitrary"), SWIGLU_VMEM_LIMIT),
        name="moe_grouped_swiglu",
    )(block_expert, block_valid, xs, w_gate, w_up, w_down)


def _combine_kernel(pos_ref, y_hbm, wt_ref, h_ref, gp_ref, gate_ref, o_ref, buf_a, buf_b, sem):
    base = pl.program_id(0) * GATHER_TILE

    def issue(r, carry):
        _row_copy(y_hbm, pos_ref[2 * (base + r)], buf_a, r, sem).start(priority=0)
        _row_copy(y_hbm, pos_ref[2 * (base + r) + 1], buf_b, r, sem).start(priority=1)
        return carry

    def drain(r, carry):
        _row_copy(y_hbm, 0, buf_a, r, sem).wait()
        _row_copy(y_hbm, 0, buf_b, r, sem).wait()
        return carry

    lax.fori_loop(0, GATHER_TILE, issue, 0, unroll=8)
    lax.fori_loop(0, GATHER_TILE, drain, 0, unroll=8)
    wt = wt_ref[...]
    y = wt[:, 0:1] * buf_a[...] + wt[:, 1:2] * buf_b[...]
    grp = _group_of_row(base)
    o_ref[...] = h_ref[...] + _mod_row(gate_ref, grp) * _rms(y, gp_ref[...])


def _moe_combine(pos, ys, wts, h, g_post, mod, gate_lk):
    n = wts.shape[0]
    tm = GATHER_TILE
    return pl.pallas_call(
        _combine_kernel,
        out_shape=_SDS((n, D_MODEL), F32),
        grid_spec=pltpu.PrefetchScalarGridSpec(
            num_scalar_prefetch=1,
            grid=(n // tm,),
            in_specs=[pl.BlockSpec(memory_space=pl.ANY),
                      pl.BlockSpec((tm, 128), lambda i, pos: (i, 0)),
                      pl.BlockSpec((tm, D_MODEL), lambda i, pos: (i, 0)),
                      pl.BlockSpec((1, D_MODEL), lambda i, pos: (0, 0)),
                      pl.BlockSpec((1, 8, D_MODEL), lambda i, pos: (gate_lk[0], 0, gate_lk[1]))],
            out_specs=pl.BlockSpec((tm, D_MODEL), lambda i, pos: (i, 0)),
            scratch_shapes=[pltpu.VMEM((tm, D_MODEL), F32), pltpu.VMEM((tm, D_MODEL), F32),
                            pltpu.SemaphoreType.DMA(())]),
        compiler_params=_cparams(("arbitrary",)),
        name="moe_combine",
    )(pos, ys, wts, h, g_post.reshape(1, D_MODEL), mod)


def _moe_routing(idx):
    flat_e = idx[:, :TOP_K].reshape(-1)
    n_assign = flat_e.shape[0]
    onehot = (flat_e[:, None] == jnp.arange(N_EXPERTS, dtype=I32)[None, :]).astype(I32)
    csum = jnp.cumsum(onehot, axis=0)
    counts = csum[-1]
    padded = (counts + MOE_BLOCK - 1) // MOE_BLOCK * MOE_BLOCK
    pad_end = jnp.cumsum(padded)
    pad_start = pad_end - padded
    pos = jnp.sum(onehot * (pad_start[None, :] + csum - 1), axis=1).astype(I32)
    slot_tok = jnp.zeros((MOE_SLOTS,), I32).at[pos].set(jnp.arange(n_assign, dtype=I32) // TOP_K)
    slot = jnp.arange(MOE_SLOTS, dtype=I32)
    used_end = pad_start + counts
    slot_valid = jnp.any((slot[:, None] >= pad_start[None, :]) & (slot[:, None] < used_end[None, :]), axis=1)
    blk_start = jnp.arange(MOE_SLOTS // MOE_BLOCK, dtype=I32) * MOE_BLOCK
    last_start = jnp.maximum(pad_end[-1] - MOE_BLOCK, 0)
    blk_e = jnp.searchsorted(pad_end, jnp.minimum(blk_start, last_start), side='right').astype(I32)
    blk_e = jnp.minimum(blk_e, N_EXPERTS - 1)
    rows_used = jnp.clip(used_end[blk_e] - blk_start, 0, MOE_BLOCK)
    blk_sub = ((rows_used + MOE_SUB - 1) // MOE_SUB).astype(I32)
    return pos, slot_tok, slot_valid.astype(F32), blk_e, blk_sub


def _rearrange_w_in(w_in):
    dn_in = DN_QKV + DN_W + 4 * DN_HEADS
    qkvz = w_in[:, :DN_QKV + DN_W]
    ba = w_in[:, DN_QKV + DN_W:dn_in]
    cq = w_in[:, dn_in:dn_in + MLA_Q_RANK]
    ckv = w_in[:, dn_in + MLA_Q_RANK:dn_in + MLA_Q_RANK + MLA_KV_RANK]
    k_rope = w_in[:, dn_in + MLA_Q_RANK + MLA_KV_RANK:]
    pad = jnp.zeros((D_MODEL, 128 - MLA_ROPE - 4 * DN_HEADS), w_in.dtype)
    main = jnp.concatenate([qkvz, ckv, cq], axis=1).astype(BF16)
    misc = jnp.concatenate([k_rope, ba, pad], axis=1).astype(BF16)
    return main, misc


def _rearrange_w_qb(w_qb):
    w = w_qb.reshape(MLA_Q_RANK, MLA_HEADS, MLA_NOPE + MLA_ROPE)
    w = jnp.pad(w, ((0, 0), (0, 0), (0, MLA_QK - MLA_NOPE - MLA_ROPE)))
    return w.reshape(MLA_Q_RANK, MLA_HEADS * MLA_QK).astype(BF16)


def kernel(x, c, ctx, c_ctx, ada_w, ada_b, norm_g, ab_w_in, dn_conv_w, dn_a_log, dn_dt_bias, dn_norm_g, mla_q_norm_g, mla_w_qb, mla_kv_norm_g, mla_w_kvb, ab_w_out, ffn_w_gate, ffn_w_up, ffn_w_down, gqa_w_qkv, gqa_b_qkv, gqa_sink, gqa_w_out, gqa_b_out, moe_w_router, moe_w_gate, moe_w_up, moe_w_down):
    assert x.shape == (BATCH, SEQ, D_MODEL) and ctx.shape == (BATCH, CTX_LEN, D_MODEL) and ada_w.shape[0] == 2
    h = (x.reshape(N_LAT, D_MODEL), ctx.reshape(N_CTX, D_MODEL))
    cvec = jnp.zeros((8, D_MODEL), F32).at[:BATCH].set(c).at[BATCH].set(c_ctx)
    mod = _ada(cvec, ada_w, ada_b)
    (cos1, sin1), (cos2, sin2) = _rope_tables()

    u = _modulate_in(*h, norm_g[0, 0], mod, 0, 0, 1)
    w_main, w_misc = _rearrange_w_in(ab_w_in[0])
    p = _matmul(u, w_main, None, BF16, MM_ROW_TILE, P_MAIN_TILE)
    p_misc = _matmul(u, w_misc, None, F32, MM_ROW_TILE, 128)
    o_f, o_b = _deltanet(_dn_conv(p, dn_conv_w[0]), _dn_gates(p_misc, dn_a_log[0], dn_dt_bias[0]))
    q, k, v = _mla_proj(p, p_misc, mla_q_norm_g[0], mla_kv_norm_g[0], _rearrange_w_qb(mla_w_qb[0]),
                        mla_w_kvb[0].astype(BF16), cos1, sin1)
    y = _ab_out(o_f, o_b, p, dn_norm_g[0], _mla_attention(q, k, v), ab_w_out[0].astype(BF16))
    h, u = _post(h, y, norm_g[0, 1], norm_g[0, 2], mod, (0, 2), (0, 3), (0, 4), BF16)
    y = _ffn(u, ffn_w_gate[0], ffn_w_up[0], ffn_w_down[0])
    h, u = _post(h, y, norm_g[0, 3], norm_g[1, 0], mod, (0, 5), (1, 0), (1, 1), BF16)

    qkv = _gqa_qkv(u, gqa_w_qkv[0].astype(BF16), gqa_b_qkv[0], cos2, sin2)
    o = _gqa_attention(qkv, gqa_sink[0])
    y = _matmul(o, gqa_w_out[0].astype(BF16), gqa_b_out[0], F32, 1024, 1024)
    h, u, idx, wts = _post(h, y, norm_g[1, 1], norm_g[1, 2], mod, (1, 2), (1, 3), (1, 4), F32, moe_w_router[0])
    pos, slot_tok, slot_valid, blk_e, blk_sub = _moe_routing(idx)
    xs = _moe_gather(slot_tok, slot_valid, u)
    ys = _moe_ffn(blk_e, blk_sub, xs, moe_w_gate[0], moe_w_up[0], moe_w_down[0])
    out = _moe_combine(pos, ys, wts, h, norm_g[1, 3], mod, (1, 5))
    return out.reshape(BATCH, SEQ, D_MODEL)
```

```python
import functools
import math

import jax
import jax.numpy as jnp
from jax import lax
from jax.experimental import pallas as pl
from jax.experimental.pallas import tpu as pltpu

F32 = jnp.float32
BF16 = jnp.bfloat16
I32 = jnp.int32

D_MODEL = 2048
BATCH = 2
SEQ = 4096
CTX_LEN = 256
GRID_W = 64
EPS = 1e-6
ROPE_THETA = 10000.0

DN_HEADS = 8
DN_HEAD_DIM = 128
DN_CONV = 5
DN_CHUNK = 64
MLA_HEADS = 8
MLA_Q_RANK = 768
MLA_KV_RANK = 512
MLA_NOPE = 128
MLA_ROPE = 64
MLA_V = 128
GQA_HEADS = 32
GQA_KV_HEADS = 4
GQA_HEAD_DIM = 64
WINDOW = 128
Q_BLOCK = 128
D_FF = 7168
N_EXPERTS = 8
TOP_K = 2

DN_W = DN_HEADS * DN_HEAD_DIM
DN_QKV = 3 * DN_W
N_LAT = BATCH * SEQ
N_CTX = BATCH * CTX_LEN
N_TOK = N_LAT + N_CTX
GQA_Q = GQA_HEADS * GQA_HEAD_DIM
GQA_KV = GQA_KV_HEADS * GQA_HEAD_DIM

P_QKV = 0
P_Z = DN_QKV
P_MAIN = P_Z + DN_W
R_CKV = 0
R_MISC = MLA_KV_RANK
R_CQ = MLA_Q_RANK
R_WIDTH = R_CQ + MLA_Q_RANK

ROW_TILE = 512
MM_ROW_TILE = 1088
MOE_BLOCK = 1024
MOE_SUB = 256
MOE_SLOTS = N_LAT * TOP_K + N_EXPERTS * MOE_BLOCK
VMEM_LIMIT = 56 * 1024 * 1024

_SDS = jax.ShapeDtypeStruct
_HIGH = lax.Precision.HIGHEST


def _cparams(sem, vmem=None):
    return pltpu.CompilerParams(dimension_semantics=sem, vmem_limit_bytes=vmem)


def _dot(a, b):
    return jnp.dot(a, b, preferred_element_type=F32)


def _dot_nt(a, b):
    return lax.dot_general(a, b, (((1,), (1,)), ((), ())), preferred_element_type=F32)


def _dot_tn(a, b):
    return lax.dot_general(a, b, (((0,), (0,)), ((), ())), preferred_element_type=F32)


def _dot_hi(a, b):
    return jnp.dot(a, b, preferred_element_type=F32, precision=_HIGH)


def _silu(x):
    return x * jax.nn.sigmoid(x)


def _rms(x, g):
    return x * lax.rsqrt(jnp.mean(x * x, axis=-1, keepdims=True) + EPS) * g


def _group_of_row(r0):
    return (r0 >= SEQ).astype(I32) + (r0 >= 2 * SEQ).astype(I32)


def _mod_spec(layer, k):
    return pl.BlockSpec((1, 8, D_MODEL), lambda *_: (layer, 0, k))


def _mod_row(ref, grp):
    return ref[0, pl.ds(grp, 1), :]


def _ada_kernel(c_ref, w_ref, b_ref, o_ref):
    s = _silu(c_ref[...]).astype(BF16)
    o_ref[0] = _dot(s, w_ref[0].astype(BF16)) + b_ref[0]


def _ada(cvec, ada_w, ada_b):
    n_layers, _, n = ada_w.shape
    tn = 1024
    return pl.pallas_call(
        _ada_kernel,
        out_shape=_SDS((n_layers, 8, n), F32),
        grid=(n_layers, n // tn),
        in_specs=[pl.BlockSpec((8, D_MODEL), lambda l, j: (0, 0)),
                  pl.BlockSpec((1, D_MODEL, tn), lambda l, j: (l, 0, j)),
                  pl.BlockSpec((1, 1, tn), lambda l, j: (l, 0, j))],
        out_specs=pl.BlockSpec((1, 8, tn), lambda l, j: (l, 0, j)),
        compiler_params=_cparams(("parallel", "parallel")),
        name="ada_mod",
    )(cvec, ada_w, ada_b.reshape(n_layers, 1, n))


_LAT_TILES = N_LAT // ROW_TILE
assert N_CTX == ROW_TILE


def _split_specs(width, col=0):
    lat = pl.BlockSpec((ROW_TILE, width), lambda i, *_: (jnp.minimum(i, _LAT_TILES - 1), col))
    ctx = pl.BlockSpec((ROW_TILE, width), lambda i, *_: (0, col))
    return [lat, ctx]


def _split_read(lat_ref, ctx_ref, sl=slice(None)):
    return jnp.where(pl.program_id(0) == _LAT_TILES, ctx_ref[:, sl], lat_ref[:, sl])


def _modin_kernel(x_ref, c_ref, g_ref, sh_ref, sc_ref, u_ref):
    grp = _group_of_row(pl.program_id(0) * ROW_TILE)
    h = _split_read(x_ref, c_ref)
    u = _rms(h, g_ref[...]) * (1.0 + _mod_row(sc_ref, grp)) + _mod_row(sh_ref, grp)
    u_ref[...] = u.astype(u_ref.dtype)


def _modulate_in(x, ctx, g, mod, layer, k_shift, k_scale):
    row = pl.BlockSpec((ROW_TILE, D_MODEL), lambda i: (i, 0))
    vec = pl.BlockSpec((1, D_MODEL), lambda i: (0, 0))
    return pl.pallas_call(
        _modin_kernel,
        out_shape=_SDS((N_TOK, D_MODEL), BF16),
        grid=(N_TOK // ROW_TILE,),
        in_specs=_split_specs(D_MODEL) + [vec, _mod_spec(layer, k_shift), _mod_spec(layer, k_scale)],
        out_specs=row,
        compiler_params=_cparams(("parallel",)),
        name="modulate_in",
    )(x, ctx, g.reshape(1, D_MODEL), mod, mod)


def _route_top2(logits):
    lane = lax.broadcasted_iota(I32, logits.shape, 1)
    neg = -jnp.inf
    l0 = jnp.where(lane < N_EXPERTS, logits, neg)
    m1 = jnp.max(l0, axis=-1, keepdims=True)
    i1 = jnp.min(jnp.where(l0 == m1, lane, 128), axis=-1, keepdims=True)
    l1 = jnp.where(lane == i1, neg, l0)
    m2 = jnp.max(l1, axis=-1, keepdims=True)
    i2 = jnp.min(jnp.where(l1 == m2, lane, 128), axis=-1, keepdims=True)
    e = jnp.exp(m2 - m1)
    w1 = 1.0 / (1.0 + e)
    idx = jnp.where(lane == 0, i1, jnp.where(lane == 1, i2, 0))
    wts = jnp.where(lane == 0, w1, jnp.where(lane == 1, e * w1, 0.0))
    return idx, wts


def _post_kernel(*refs, split, route):
    refs = list(refs)
    if split:
        h = _split_read(refs.pop(0), refs.pop(0))
    else:
        h = refs.pop(0)[...]
    y_ref, gp_ref, gate_ref, gn_ref, sh_ref, sc_ref = refs[:6]
    refs = refs[6:]
    wr_ref = refs.pop(0) if route else None
    hn_ref, u_ref = refs[:2]
    grp = _group_of_row(pl.program_id(0) * ROW_TILE)
    hn = h + _mod_row(gate_ref, grp) * _rms(y_ref[...], gp_ref[...])
    hn_ref[...] = hn
    u = _rms(hn, gn_ref[...]) * (1.0 + _mod_row(sc_ref, grp)) + _mod_row(sh_ref, grp)
    u_ref[...] = u.astype(u_ref.dtype)
    if route:
        idx_ref, wt_ref = refs[2:4]
        idx_ref[...], wt_ref[...] = _route_top2(_dot(u.astype(BF16), wr_ref[...]))


def _post(h, y, g_post, g_next, mod, gate_lk, shift_lk, scale_lk, u_dtype, w_router=None):
    n = y.shape[0]
    split = isinstance(h, tuple)
    route = w_router is not None
    row = pl.BlockSpec((ROW_TILE, D_MODEL), lambda i: (i, 0))
    vec = pl.BlockSpec((1, D_MODEL), lambda i: (0, 0))
    slab = pl.BlockSpec((ROW_TILE, 128), lambda i: (i, 0))
    h_specs, h_args = (_split_specs(D_MODEL), list(h)) if split else ([row], [h])
    extra_specs, extra_args, extra_out, extra_out_specs = [], [], (), ()
    if route:
        w = jnp.zeros((D_MODEL, 128), BF16).at[:, :N_EXPERTS].set(w_router.astype(BF16))
        extra_specs, extra_args = [pl.BlockSpec((D_MODEL, 128), lambda i: (0, 0))], [w]
        extra_out, extra_out_specs = (_SDS((n, 128), I32), _SDS((n, 128), F32)), (slab, slab)
    return pl.pallas_call(
        functools.partial(_post_kernel, split=split, route=route),
        out_shape=(_SDS((n, D_MODEL), F32), _SDS((n, D_MODEL), u_dtype)) + extra_out,
        grid=(n // ROW_TILE,),
        in_specs=h_specs + [row, vec, _mod_spec(*gate_lk), vec, _mod_spec(*shift_lk), _mod_spec(*scale_lk)] + extra_specs,
        out_specs=(row, row) + extra_out_specs,
        compiler_params=_cparams(("parallel",)),
        name="residual_post",
    )(*h_args, y, g_post.reshape(1, D_MODEL), mod, g_next.reshape(1, D_MODEL), mod, mod, *extra_args)


def _mm_kernel(a_ref, w_ref, b_ref, o_ref):
    o_ref[...] = (_dot(a_ref[...], w_ref[...].astype(BF16)) + b_ref[...]).astype(o_ref.dtype)


def _matmul(a, w, bias, out_dtype, tm, tn, n=None):
    m, k = a.shape
    n = w.shape[1] if n is None else n
    if bias is None:
        bias = jnp.zeros((n,), F32)
    return pl.pallas_call(
        _mm_kernel,
        out_shape=_SDS((m, n), out_dtype),
        grid=(m // tm, n // tn),
        in_specs=[pl.BlockSpec((tm, k), lambda i, j: (i, 0)),
                  pl.BlockSpec((k, tn), lambda i, j: (0, j)),
                  pl.BlockSpec((1, tn), lambda i, j: (0, j))],
        out_specs=pl.BlockSpec((tm, tn), lambda i, j: (i, j)),
        compiler_params=_cparams(("parallel", "arbitrary"), VMEM_LIMIT),
        name="matmul",
    )(a, w, bias.reshape(1, n))


FF_TILE = 512
SWIGLU_VMEM_LIMIT = 60 * 1024 * 1024
OUT_TILE = 256
FF_STEPS = D_FF // FF_TILE
OUT_STEPS = D_MODEL // OUT_TILE


def _swiglu_step(s, x_ref, wg_ref, wu_ref, wd_ref, o_ref, h_ref, rows=None, guard=True):
    total = x_ref.shape[0]
    rows = total if rows is None else rows

    @pl.when(guard & (s < FF_STEPS))
    def _():
        x = x_ref[0:rows, :]
        a = _dot(x, wg_ref[...].astype(BF16))
        b = _dot(x, wu_ref[...].astype(BF16))
        h_ref[0:rows, pl.ds(pl.multiple_of(s * FF_TILE, FF_TILE), FF_TILE)] = (_silu(a) * b).astype(BF16)

    @pl.when(guard & (s >= FF_STEPS))
    def _():
        o_ref[0:rows, :] = _dot(h_ref[0:rows, :], wd_ref[...].astype(BF16))
        if rows < total:
            o_ref[rows:total, :] = jnp.zeros((total - rows, o_ref.shape[1]), o_ref.dtype)


def _ffn_kernel(u_ref, wg_ref, wu_ref, wd_ref, o_ref, h_ref):
    _swiglu_step(pl.program_id(1), u_ref, wg_ref, wu_ref, wd_ref, o_ref, h_ref)


def _ffn(u, w_gate, w_up, w_down):
    m = u.shape[0]
    tm = MM_ROW_TILE

    def ff(i, s):
        return (0, jnp.minimum(s, FF_STEPS - 1))

    def out(s):
        return jnp.maximum(s - FF_STEPS, 0)

    return pl.pallas_call(
        _ffn_kernel,
        out_shape=_SDS((m, D_MODEL), F32),
        grid=(m // tm, FF_STEPS + OUT_STEPS),
        in_specs=[pl.BlockSpec((tm, D_MODEL), lambda i, s: (i, 0), pipeline_mode=pl.Buffered(1)),
                  pl.BlockSpec((D_MODEL, FF_TILE), ff),
                  pl.BlockSpec((D_MODEL, FF_TILE), ff),
                  pl.BlockSpec((D_FF, OUT_TILE), lambda i, s: (0, out(s)))],
        out_specs=pl.BlockSpec((tm, OUT_TILE), lambda i, s: (i, out(s))),
        scratch_shapes=[pltpu.VMEM((tm, D_FF), BF16)],
        compiler_params=_cparams(("parallel", "arbitrary"), SWIGLU_VMEM_LIMIT),
        name="swiglu_ffn",
    )(u, w_gate, w_up, w_down)


CONV_TILE = 256
CONV_HALO = 16
_SEQ_STARTS = (0, SEQ // CONV_TILE, 2 * SEQ // CONV_TILE, 2 * SEQ // CONV_TILE + 1)
_SEQ_LASTS = (SEQ // CONV_TILE - 1, 2 * SEQ // CONV_TILE - 1, 2 * SEQ // CONV_TILE, 2 * SEQ // CONV_TILE + 1)


def _conv_kernel(prev_ref, cur_ref, next_ref, w_ref, o_ref, buf):
    i = pl.program_id(0)
    j = pl.program_id(1)
    tm = CONV_TILE
    first = functools.reduce(jnp.logical_or, [i == s for s in _SEQ_STARTS])
    last = functools.reduce(jnp.logical_or, [i == s for s in _SEQ_LASTS])
    hb = CONV_HALO
    buf[0:hb, :] = jnp.where(first, 0.0, prev_ref[...].astype(F32))
    buf[hb:hb + tm, :] = cur_ref[...].astype(F32)
    buf[hb + tm:2 * hb + tm, :] = jnp.where(last, 0.0, next_ref[...].astype(F32))
    pad = DN_CONV // 2
    acc = buf[hb - pad:hb - pad + tm, :] * w_ref[0:1, :]
    for d in range(1, DN_CONV):
        acc = acc + buf[hb - pad + d:hb - pad + d + tm, :] * w_ref[d:d + 1, :]
    x = _silu(acc)
    q_scale = jnp.where(j == 0, DN_HEAD_DIM ** -0.5, 1.0)
    for hh in range(DN_HEADS):
        xh = x[:, hh * DN_HEAD_DIM:(hh + 1) * DN_HEAD_DIM]
        inv = lax.rsqrt(jnp.sum(xh * xh, axis=-1, keepdims=True) + EPS) * q_scale
        o_ref[:, hh * DN_HEAD_DIM:(hh + 1) * DN_HEAD_DIM] = xh * jnp.where(j == 2, 1.0, inv)


def _dn_conv(p, conv_w):
    n = p.shape[0]
    tm = CONV_TILE
    hb = CONV_HALO
    per_tile = tm // hb
    n_halo = n // hb
    return pl.pallas_call(
        _conv_kernel,
        out_shape=_SDS((n, DN_QKV), F32),
        grid=(n // tm, 3),
        in_specs=[pl.BlockSpec((hb, DN_W), lambda i, j: (jnp.maximum(i * per_tile - 1, 0), j)),
                  pl.BlockSpec((tm, DN_W), lambda i, j: (i, j)),
                  pl.BlockSpec((hb, DN_W), lambda i, j: (jnp.minimum((i + 1) * per_tile, n_halo - 1), j)),
                  pl.BlockSpec((DN_CONV, DN_W), lambda i, j: (0, j))],
        out_specs=pl.BlockSpec((tm, DN_W), lambda i, j: (i, j)),
        scratch_shapes=[pltpu.VMEM((tm + 2 * hb, DN_W), F32)],
        compiler_params=_cparams(("parallel", "parallel")),
        name="dn_conv",
    )(p, p, p, conv_w)


def _gates_kernel(m_ref, a_ref, dt_ref, o_ref):
    x = pltpu.roll(m_ref[...], 64, 1)
    lane = lax.broadcasted_iota(I32, x.shape, 1)
    z = x + dt_ref[...]
    softplus = jnp.maximum(z, 0.0) + jnp.log1p(jnp.exp(-jnp.abs(z)))
    g = a_ref[...] * softplus
    o_ref[...] = jnp.where(lane < 2 * DN_HEADS, jax.nn.sigmoid(x), jnp.where(lane < 4 * DN_HEADS, g, 0.0))


def _dn_gates(p, a_log, dt_bias):
    n = p.shape[0]
    neg_a = jnp.zeros((1, 128), F32).at[0, 2 * DN_HEADS:4 * DN_HEADS].set(-jnp.exp(a_log.astype(F32)).reshape(-1))
    dtb = jnp.zeros((1, 128), F32).at[0, 2 * DN_HEADS:4 * DN_HEADS].set(dt_bias.astype(F32).reshape(-1))
    tm = ROW_TILE
    vec = pl.BlockSpec((1, 128), lambda i: (0, 0))
    return pl.pallas_call(
        _gates_kernel,
        out_shape=_SDS((n, 128), F32),
        grid=(n // tm,),
        in_specs=[pl.BlockSpec((tm, 128), lambda i: (i, R_MISC // 128)), vec, vec],
        out_specs=pl.BlockSpec((tm, 128), lambda i: (i, 0)),
        compiler_params=_cparams(("parallel",)),
        name="dn_gates",
    )(p, neg_a, dtb)


_CTX_CHUNKS = CTX_LEN // DN_CHUNK
_LAT_CHUNKS = SEQ // DN_CHUNK
_DN_STEPS = _CTX_CHUNKS + _LAT_CHUNKS


def _dn_fwd_block(b, t):
    return jnp.where(t < _CTX_CHUNKS, (2 * SEQ + b * CTX_LEN) // DN_CHUNK + t, b * _LAT_CHUNKS + (t - _CTX_CHUNKS))


def _dn_bwd_block(b, t):
    return jnp.where(t < _CTX_CHUNKS, (2 * SEQ + b * CTX_LEN) // DN_CHUNK + (_CTX_CHUNKS - 1 - t),
                     b * _LAT_CHUNKS + (_DN_STEPS - 1 - t))


DN_GROUP = 4
_GROUP_ROWS = DN_GROUP * DN_CHUNK
_GROUP_STATE = DN_GROUP * DN_HEAD_DIM


_DN_BASE = 8


def _block_diag(x):
    zero = jnp.zeros((DN_CHUNK, DN_HEAD_DIM), x.dtype)
    rows = []
    for hh in range(DN_GROUP):
        piece = x[hh * DN_CHUNK:(hh + 1) * DN_CHUNK]
        rows.append(jnp.concatenate([piece if j == hh else zero for j in range(DN_GROUP)], axis=1))
    return jnp.concatenate(rows, axis=0)


def _dn_groups(items, eye, base_mask):
    n = range(len(items))
    it = items
    gx = [jnp.broadcast_to(it[g]["gc"], (_GROUP_ROWS, _GROUP_ROWS)) for g in n]
    gamma = [jnp.where(it[g]["incl"], jnp.exp(jnp.where(it[g]["incl"], gx[g] - gx[g].T, 0.0)), 0.0) for g in n]
    kb = [it[g]["k"].astype(BF16) for g in n]
    kk = [_dot_nt(kb[g], kb[g]) for g in n]
    a = [jnp.where(it[g]["strict"], it[g]["beta"] * kk[g] * gamma[g], 0.0) for g in n]
    d = [jnp.where(base_mask, a[g], 0.0) for g in n]
    tinv = [eye - d[g] for g in n]
    pw = [d[g].astype(BF16) for g in n]
    for _ in range(int(math.log2(_DN_BASE)) - 1):
        pw = [_dot(pw[g], pw[g]).astype(BF16) for g in n]
        tinv = [tinv[g] + _dot(tinv[g].astype(BF16), pw[g]) for g in n]
    for lvl in range(len(it[0]["levels"])):
        tb = [tinv[g].astype(BF16) for g in n]
        mt = [_dot(jnp.where(it[g]["levels"][lvl], a[g], 0.0).astype(BF16), tb[g]).astype(BF16) for g in n]
        tinv = [tinv[g] - _dot(tb[g], mt[g]) for g in n]
    tb = [tinv[g].astype(BF16) for g in n]
    egc = [jnp.exp(it[g]["gc"]) for g in n]
    rhs = [jnp.concatenate([it[g]["v"] * it[g]["beta"], it[g]["k"] * (it[g]["beta"] * egc[g])], axis=1) for g in n]
    rhs_hi = [rhs[g].astype(BF16) for g in n]
    rhs_lo = [(rhs[g] - rhs_hi[g].astype(F32)).astype(BF16) for g in n]
    sol = [_dot(jnp.concatenate([tb[g], tb[g]], axis=1), jnp.concatenate([rhs_hi[g], rhs_lo[g]], axis=0)) for g in n]
    qk = [(_dot_nt(it[g]["q"].astype(BF16), kb[g]) * gamma[g]).astype(BF16) for g in n]
    s = [it[g]["s_ref"][...] for g in n]
    sb = [s[g].astype(BF16) for g in n]
    v_new = [sol[g][:, :DN_HEAD_DIM] - _dot(_block_diag(sol[g][:, DN_HEAD_DIM:].astype(BF16)), sb[g]) for g in n]
    vb = [v_new[g].astype(BF16) for g in n]
    o = [_dot(_block_diag((it[g]["q"] * egc[g]).astype(BF16)), sb[g]) + _dot(qk[g], vb[g]) for g in n]
    for g in n:
        k_dec = _block_diag((it[g]["k"] * jnp.exp(it[g]["g_last"] - it[g]["gc"])).astype(BF16))
        it[g]["s_ref"][...] = s[g] * jnp.exp(it[g]["g_last_state"]) + _dot_tn(k_dec, vb[g])
    return o


def _stack_cols(x, lanes, rows_each):
    pieces = [x[:, l:l + 1] for l in lanes]
    if rows_each is not None:
        pieces = [jnp.broadcast_to(pc, (rows_each, 1)) for pc in pieces]
    return jnp.concatenate(pieces, axis=0)


def _dn_kernel(*refs):
    n_in = BATCH * 2 * 4
    in_refs, (ofl_ref, ofc_ref, obl_ref, obc_ref), (s_ref, o_scr) = refs[:n_in], refs[n_in:n_in + 4], refs[n_in + 4:]
    t = pl.program_id(0)

    @pl.when(t == 0)
    def _():
        s_ref[...] = jnp.zeros_like(s_ref)

    c = DN_CHUNK
    ri = lax.broadcasted_iota(I32, (_GROUP_ROWS, _GROUP_ROWS), 0)
    ci = lax.broadcasted_iota(I32, (_GROUP_ROWS, _GROUP_ROWS), 1)
    same_head = (ri // c) == (ci // c)
    eye = (ri == ci).astype(F32)
    r1 = lax.broadcasted_iota(I32, (c, c), 0)
    c1 = lax.broadcasted_iota(I32, (c, c), 1)
    base_mask = (ri // _DN_BASE) == (ci // _DN_BASE)
    masks = []
    for d in range(2):
        incl = same_head & ((ri >= ci) if d == 0 else (ri <= ci))
        strict = same_head & ((ri > ci) if d == 0 else (ri < ci))
        levels = []
        size = _DN_BASE
        while size < c:
            lo_blk, hi_blk = ((ci, ri) if d == 0 else (ri, ci))
            levels.append(((ri // (2 * size)) == (ci // (2 * size)))
                          & ((lo_blk // size) % 2 == 0) & ((hi_blk // size) % 2 == 1))
            size *= 2
        masks.append((incl, strict, levels))
    items, where = [], []
    for b, d in [(b, d) for b in range(BATCH) for d in range(2)]:
        q_ref, k_ref, v_ref, g_ref = in_refs[(b * 2 + d) * 4:(b * 2 + d) * 4 + 4]
        incl, strict, levels = masks[d]
        gates = g_ref[...]
        tri = ((r1 >= c1) if d == 0 else (r1 <= c1)).astype(BF16)
        g_hi = gates.astype(BF16)
        g_r1 = gates - g_hi.astype(F32)
        g_mid = g_r1.astype(BF16)
        g_lo = (g_r1 - g_mid.astype(F32)).astype(BF16)
        gc = _dot(tri, g_hi) + _dot(tri, g_mid) + _dot(tri, g_lo)
        last_row = c - 1 if d == 0 else 0
        tot = gc[last_row:last_row + 1, :]
        for grp in range(DN_HEADS // DN_GROUP):
            heads = range(grp * DN_GROUP, (grp + 1) * DN_GROUP)
            lanes_b = [d * DN_HEADS + hh for hh in heads]
            lanes_g = [2 * DN_HEADS + lb for lb in lanes_b]

            def stack(ref):
                return jnp.concatenate([ref[:, hh * DN_HEAD_DIM:(hh + 1) * DN_HEAD_DIM] for hh in heads], axis=0)

            items.append(dict(q=stack(q_ref), k=stack(k_ref), v=stack(v_ref),
                              beta=_stack_cols(gates, lanes_b, None), gc=_stack_cols(gc, lanes_g, None),
                              g_last=_stack_cols(tot, lanes_g, c), g_last_state=_stack_cols(tot, lanes_g, DN_HEAD_DIM),
                              incl=incl, strict=strict, levels=levels, s_ref=s_ref.at[b, d, grp]))
            where.append((d, b, heads))
    outs = _dn_groups(items, eye, base_mask)
    for o, (d, b, heads) in zip(outs, where):
        for j, hh in enumerate(heads):
            o_scr[d, b, :, hh * DN_HEAD_DIM:(hh + 1) * DN_HEAD_DIM] = o[j * c:(j + 1) * c]

    @pl.when(t < _CTX_CHUNKS)
    def _():
        ofc_ref[...] = o_scr[0]
        obc_ref[...] = o_scr[1]

    @pl.when(t >= _CTX_CHUNKS)
    def _():
        ofl_ref[...] = o_scr[0]
        obl_ref[...] = o_scr[1]


def _deltanet(qkv, gates):
    c = DN_CHUNK
    ins, args = [], []
    for b in range(BATCH):
        for fn in (_dn_fwd_block, _dn_bwd_block):
            for width, col, arr in ((DN_W, 0, qkv), (DN_W, 1, qkv), (DN_W, 2, qkv), (128, 0, gates)):
                ins.append(pl.BlockSpec((c, width), functools.partial(lambda t, b, fn, col: (fn(b, t), col),
                                                                      b=b, fn=fn, col=col)))
                args.append(arr)
    last = _DN_STEPS - 1
    lat = _SDS((BATCH, SEQ, DN_W), F32)
    ctx = _SDS((BATCH, CTX_LEN, DN_W), F32)
    blk = (BATCH, c, DN_W)
    outs = pl.pallas_call(
        _dn_kernel,
        out_shape=(lat, ctx, lat, ctx),
        grid=(_DN_STEPS,),
        in_specs=ins,
        out_specs=(pl.BlockSpec(blk, lambda t: (0, jnp.maximum(t - _CTX_CHUNKS, 0), 0)),
                   pl.BlockSpec(blk, lambda t: (0, jnp.minimum(t, _CTX_CHUNKS - 1), 0)),
                   pl.BlockSpec(blk, lambda t: (0, jnp.minimum(last - t, _LAT_CHUNKS - 1), 0)),
                   pl.BlockSpec(blk, lambda t: (0, jnp.maximum(_CTX_CHUNKS - 1 - t, 0), 0))),
        scratch_shapes=[pltpu.VMEM((BATCH, 2, DN_HEADS // DN_GROUP, _GROUP_STATE, DN_HEAD_DIM), F32),
                        pltpu.VMEM((2, BATCH, c, DN_W), F32)],
        compiler_params=_cparams(("arbitrary",)),
        name="deltanet",
    )(*args)
    ofl, ofc, obl, obc = outs
    return ((ofl.reshape(N_LAT, DN_W), ofc.reshape(N_CTX, DN_W)),
            (obl.reshape(N_LAT, DN_W), obc.reshape(N_CTX, DN_W)))


def _rope_tables():
    t = jnp.arange(SEQ)
    row = (t // GRID_W).astype(F32)
    col = (t % GRID_W).astype(F32)
    n_freq = MLA_ROPE // 4
    inv_freq = ROPE_THETA ** (-jnp.arange(n_freq, dtype=F32) / n_freq)
    ang = jnp.concatenate([row[:, None] * inv_freq, col[:, None] * inv_freq], axis=-1)
    cos, sin = jnp.cos(ang), jnp.sin(ang)
    cos64 = jnp.concatenate([cos, cos], axis=-1)
    sin64 = jnp.concatenate([-sin, sin], axis=-1)

    def assemble(lat_cos, lat_sin):
        c = jnp.concatenate([jnp.tile(lat_cos, (BATCH, 1)), jnp.ones((N_CTX, 128), F32)], axis=0)
        s = jnp.concatenate([jnp.tile(lat_sin, (BATCH, 1)), jnp.zeros((N_CTX, 128), F32)], axis=0)
        return c, s

    cos_l = jnp.concatenate([cos64, jnp.ones((SEQ, 64), F32)], axis=-1)
    sin_l = jnp.concatenate([sin64, jnp.zeros((SEQ, 64), F32)], axis=-1)
    one_head = assemble(cos_l, sin_l)
    two_heads = assemble(jnp.tile(cos64, (1, 2)), jnp.tile(sin64, (1, 2)))
    return one_head, two_heads


def _rope128(x, cosv, sinv):
    lane = lax.broadcasted_iota(I32, x.shape, 1)
    swapped = jnp.where((lane % 64) < 32, pltpu.roll(x, 96, 1), pltpu.roll(x, 32, 1))
    return x * cosv + swapped * sinv


MLA_QK = 256
MLA_VX = 256


def _mla_proj_kernel(ckv_ref, cq_ref, misc_ref, gq_ref, gkv_ref, wq_ref, wkv_ref, cos_ref, sin_ref,
                     q_ref, k_ref, v_ref):
    cq = _rms(cq_ref[...].astype(F32), gq_ref[...]).astype(BF16)
    ckv = _rms(ckv_ref[...].astype(F32), gkv_ref[...]).astype(BF16)
    qf = _dot(cq, wq_ref[...])
    kvf = _dot(ckv, wkv_ref[...])
    cosv, sinv = cos_ref[...], sin_ref[...]
    lane = lax.broadcasted_iota(I32, cosv.shape, 1)
    k_rope = _rope128(jnp.where(lane < MLA_ROPE, misc_ref[...], 0.0), cosv, sinv).astype(BF16)
    scale = (MLA_NOPE + MLA_ROPE) ** -0.5
    for hh in range(MLA_HEADS):
        lo, mid, hi = hh * MLA_QK, hh * MLA_QK + 128, (hh + 1) * MLA_QK
        q_ref[:, lo:mid] = (qf[:, lo:mid] * scale).astype(BF16)
        q_ref[:, mid:hi] = (_rope128(qf[:, mid:hi], cosv, sinv) * scale).astype(BF16)
        k_ref[:, lo:mid] = kvf[:, lo:mid].astype(BF16)
        k_ref[:, mid:hi] = k_rope
        v_ref[:, lo:mid] = kvf[:, mid:hi].astype(BF16)
        v_ref[:, mid:hi] = jnp.ones((kvf.shape[0], MLA_VX - MLA_V), BF16)


def _mla_proj(p_rest, q_norm_g, kv_norm_g, w_q, w_kv, cosv, sinv):
    n = p_rest.shape[0]
    tm = ROW_TILE
    wide = MLA_HEADS * MLA_QK

    def rows(width, col):
        return pl.BlockSpec((tm, width), lambda i: (i, col))

    def whole(shape):
        return pl.BlockSpec(shape, lambda i: (0, 0))

    return pl.pallas_call(
        _mla_proj_kernel,
        out_shape=(_SDS((n, wide), BF16), _SDS((n, wide), BF16), _SDS((n, MLA_HEADS * MLA_VX), BF16)),
        grid=(n // tm,),
        in_specs=[rows(MLA_KV_RANK, R_CKV // MLA_KV_RANK), rows(MLA_Q_RANK, R_CQ // MLA_Q_RANK),
                  rows(128, R_MISC // 128), whole((1, MLA_Q_RANK)), whole((1, MLA_KV_RANK)),
                  whole((MLA_Q_RANK, wide)), whole((MLA_KV_RANK, wide)), rows(128, 0), rows(128, 0)],
        out_specs=(rows(wide, 0), rows(wide, 0), rows(MLA_HEADS * MLA_VX, 0)),
        compiler_params=_cparams(("parallel",), VMEM_LIMIT),
        name="mla_proj",
    )(p_rest, p_rest, p_rest, q_norm_g.reshape(1, -1), kv_norm_g.reshape(1, -1), w_q, w_kv, cosv, sinv)


MLA_KEY_CHUNK = 2048


def _flash_kernel(*refs, chunks):
    n_seg = (len(refs) - 2) // 2
    q_ref, o_ref = refs[0], refs[-1]
    k_refs, v_refs = refs[1:1 + n_seg], refs[1 + n_seg:1 + 2 * n_seg]
    q = q_ref[...]

    def scores(c):
        seg, off, size = c
        return _dot_nt(q, k_refs[seg][off:off + size, :])

    m = jnp.full((q.shape[0], 1), -jnp.inf, F32)
    acc = jnp.zeros((q.shape[0], MLA_VX), F32)
    s_next = scores(chunks[0])
    for j, (seg, off, size) in enumerate(chunks):
        s = s_next
        if j + 1 < len(chunks):
            s_next = scores(chunks[j + 1])
        m_new = jnp.maximum(m, jnp.max(s, axis=-1, keepdims=True))
        p = jnp.exp(s - m_new).astype(BF16)
        acc = jnp.exp(m - m_new) * acc + _dot(p, v_refs[seg][off:off + size, :])
        m = m_new
    o_ref[...] = (acc[:, :MLA_V] / acc[:, MLA_V:]).astype(o_ref.dtype)


def _mla_attention(q, k, v):
    tq = 1024
    nq = SEQ // tq
    ctx_blk = 2 * SEQ // CTX_LEN
    lat_chunks = ((0, 0, CTX_LEN),) + tuple((1, off, MLA_KEY_CHUNK) for off in range(0, SEQ, MLA_KEY_CHUNK))
    lat = pl.pallas_call(
        functools.partial(_flash_kernel, chunks=lat_chunks),
        out_shape=_SDS((N_LAT, MLA_HEADS * MLA_V), BF16),
        grid=(BATCH, MLA_HEADS, nq),
        in_specs=[pl.BlockSpec((tq, MLA_QK), lambda b, h, i: (b * nq + i, h)),
                  pl.BlockSpec((CTX_LEN, MLA_QK), lambda b, h, i: (ctx_blk + b, h)),
                  pl.BlockSpec((SEQ, MLA_QK), lambda b, h, i: (b, h)),
                  pl.BlockSpec((CTX_LEN, MLA_VX), lambda b, h, i: (ctx_blk + b, h)),
                  pl.BlockSpec((SEQ, MLA_VX), lambda b, h, i: (b, h))],
        out_specs=pl.BlockSpec((tq, MLA_V), lambda b, h, i: (b * nq + i, h)),
        compiler_params=_cparams(("parallel", "parallel", "arbitrary"), VMEM_LIMIT),
        name="mla_attn_latent",
    )(q, k, k, v, v)
    ctx = pl.pallas_call(
        functools.partial(_flash_kernel, chunks=((0, 0, CTX_LEN),)),
        out_shape=_SDS((N_CTX, MLA_HEADS * MLA_V), BF16),
        grid=(BATCH, MLA_HEADS),
        in_specs=[pl.BlockSpec((CTX_LEN, MLA_QK), lambda b, h: (ctx_blk + b, h)),
                  pl.BlockSpec((CTX_LEN, MLA_QK), lambda b, h: (ctx_blk + b, h)),
                  pl.BlockSpec((CTX_LEN, MLA_VX), lambda b, h: (ctx_blk + b, h))],
        out_specs=pl.BlockSpec((CTX_LEN, MLA_V), lambda b, h: (b, h)),
        compiler_params=_cparams(("parallel", "parallel")),
        name="mla_attn_context",
    )(q, k, v)
    return lat, ctx


def _ab_out_kernel(ofl_ref, ofc_ref, obl_ref, obc_ref, ml_ref, mc_ref, z_ref, g_ref, w_ref, o_ref, a_ref):
    @pl.when(pl.program_id(1) == 0)
    def _():
        for hh in range(DN_HEADS):
            sl = slice(hh * DN_HEAD_DIM, (hh + 1) * DN_HEAD_DIM)
            o = _split_read(ofl_ref, ofc_ref, sl) + _split_read(obl_ref, obc_ref, sl)
            a_ref[:, sl] = (_rms(o, g_ref[...]) * _silu(z_ref[:, sl].astype(F32))).astype(BF16)
        a_ref[:, DN_W:] = _split_read(ml_ref, mc_ref)

    o_ref[...] = _dot(a_ref[...], w_ref[...].astype(BF16))


def _ab_out(o_f, o_b, p, dn_norm_g, mla_o, w_out):
    tm, tn = ROW_TILE, 1024
    half = DN_W
    return pl.pallas_call(
        _ab_out_kernel,
        out_shape=_SDS((N_TOK, D_MODEL), F32),
        grid=(N_TOK // tm, D_MODEL // tn),
        in_specs=_split_specs(half) + _split_specs(half) + _split_specs(half) + [
            pl.BlockSpec((tm, half), lambda i, j: (i, P_Z // half)),
            pl.BlockSpec((1, DN_HEAD_DIM), lambda i, j: (0, 0)),
            pl.BlockSpec((2 * half, tn), lambda i, j: (0, j))],
        out_specs=pl.BlockSpec((tm, tn), lambda i, j: (i, j)),
        scratch_shapes=[pltpu.VMEM((tm, 2 * half), BF16)],
        compiler_params=_cparams(("parallel", "arbitrary"), VMEM_LIMIT),
        name="ab_out_proj",
    )(*o_f, *o_b, *mla_o, p, dn_norm_g.reshape(1, -1), w_out)


def _mm_rope_kernel(a_ref, w_ref, b_ref, cos_ref, sin_ref, o_ref, *, tn, rope_cols):
    y = _dot(a_ref[...], w_ref[...].astype(BF16)) + b_ref[...]
    col0 = pl.program_id(1) * tn
    cosv, sinv = cos_ref[...], sin_ref[...]
    for s in range(tn // 128):
        ys = y[:, s * 128:(s + 1) * 128]
        roped = _rope128(ys, cosv, sinv)
        o_ref[:, s * 128:(s + 1) * 128] = jnp.where(col0 + s * 128 < rope_cols, roped, ys).astype(o_ref.dtype)


def _gqa_qkv(u, w, bias, cosv, sinv):
    m, k = u.shape
    n = w.shape[1]
    tm, tn = MM_ROW_TILE, 1280
    return pl.pallas_call(
        functools.partial(_mm_rope_kernel, tn=tn, rope_cols=GQA_Q + GQA_KV),
        out_shape=_SDS((m, n), BF16),
        grid=(m // tm, n // tn),
        in_specs=[pl.BlockSpec((tm, k), lambda i, j: (i, 0)),
                  pl.BlockSpec((k, tn), lambda i, j: (0, j)),
                  pl.BlockSpec((1, tn), lambda i, j: (0, j)),
                  pl.BlockSpec((tm, 128), lambda i, j: (i, 0)),
                  pl.BlockSpec((tm, 128), lambda i, j: (i, 0))],
        out_specs=pl.BlockSpec((tm, tn), lambda i, j: (i, j)),
        compiler_params=_cparams(("parallel", "arbitrary"), VMEM_LIMIT),
        name="gqa_qkv_rope",
    )(u, w, bias.reshape(1, n), cosv, sinv)


def _gqa_kernel(sink_ref, q_ref, kp_ref, kc_ref, kn_ref, vp_ref, vc_ref, vn_ref, kx_ref, vx_ref, o_ref):
    i = pl.program_id(1)
    nb = SEQ // Q_BLOCK
    n_keys = CTX_LEN + 3 * Q_BLOCK
    groups = GQA_HEADS // GQA_KV_HEADS
    slabs = groups // 2
    dh = GQA_HEAD_DIM
    qi = lax.broadcasted_iota(I32, (Q_BLOCK, n_keys), 0)
    kj = lax.broadcasted_iota(I32, (Q_BLOCK, n_keys), 1) - CTX_LEN
    rel = kj - Q_BLOCK - qi
    valid = (kj < 0) | ((jnp.abs(rel) <= WINDOW) & ((kj >= Q_BLOCK) | (i > 0)) & ((kj < 2 * Q_BLOCK) | (i < nb - 1)))
    bias = jnp.where(valid, 0.0, -jnp.inf).astype(F32)
    bias = jnp.concatenate([bias] * slabs, axis=0)
    k_all = jnp.concatenate([kx_ref[...], kp_ref[...], kc_ref[...], kn_ref[...]], axis=0)
    v_all = jnp.concatenate([vx_ref[...], vp_ref[...], vc_ref[...], vn_ref[...]], axis=0)
    low = lax.broadcasted_iota(I32, (n_keys, 128), 1) < dh

    def block_diag(x_all, kvh):
        pair = x_all[:, (kvh // 2) * 128:(kvh // 2 + 1) * 128].astype(F32)
        other = pltpu.roll(pair, dh, 1)
        first, second = (pair, other) if kvh % 2 == 0 else (other, pair)
        return jnp.concatenate([jnp.where(low, first, 0.0), jnp.where(low, 0.0, second)], axis=0).astype(BF16)

    def scores(kvh):
        q = jnp.concatenate([q_ref[:, (kvh * slabs + r) * 128:(kvh * slabs + r + 1) * 128] for r in range(slabs)], axis=0)
        return _dot_nt(q * (dh ** -0.5), block_diag(k_all, kvh))

    def softmax(kvh, s):
        ps, dens = [], []
        for half in range(2):
            sh = s[:, half * n_keys:(half + 1) * n_keys] + bias
            sink = jnp.concatenate([jnp.full((Q_BLOCK, 1), sink_ref[(kvh * slabs + r) * 2 + half], F32)
                                    for r in range(slabs)], axis=0)
            m = jnp.maximum(jnp.max(sh, axis=-1, keepdims=True), sink)
            p = jnp.exp(sh - m)
            dens.append(jnp.sum(p, axis=-1, keepdims=True) + jnp.exp(sink - m))
            ps.append(p.astype(BF16))
        return jnp.concatenate(ps, axis=1), dens

    def finish(kvh, p, dens):
        o = _dot(p, block_diag(v_all, kvh))
        lane_low = lax.broadcasted_iota(I32, o.shape, 1) < dh
        o = o * jnp.where(lane_low, 1.0 / dens[0], 1.0 / dens[1])
        for r in range(slabs):
            o_ref[:, (kvh * slabs + r) * 128:(kvh * slabs + r + 1) * 128] = o[r * Q_BLOCK:(r + 1) * Q_BLOCK].astype(o_ref.dtype)

    s_next = scores(0)
    for kvh in range(GQA_KV_HEADS):
        s_cur = s_next
        if kvh + 1 < GQA_KV_HEADS:
            s_next = scores(kvh + 1)
        p, dens = softmax(kvh, s_cur)
        finish(kvh, p, dens)


def _gqa_attention(qkv, sink):
    nb = SEQ // Q_BLOCK
    kcol, vcol = GQA_Q // GQA_KV, GQA_Q // GQA_KV + 1
    ctx_blk = 2 * SEQ // CTX_LEN

    def win(col, shift):
        return pl.BlockSpec((Q_BLOCK, GQA_KV), lambda b, i: (b * nb + jnp.clip(i + shift, 0, nb - 1), col))

    return pl.pallas_call(
        _gqa_kernel,
        out_shape=_SDS((N_LAT, GQA_Q), BF16),
        grid=(BATCH, nb),
        in_specs=[pl.BlockSpec(memory_space=pltpu.SMEM),
                  pl.BlockSpec((Q_BLOCK, GQA_Q), lambda b, i: (b * nb + i, 0)),
                  win(kcol, -1), win(kcol, 0), win(kcol, 1), win(vcol, -1), win(vcol, 0), win(vcol, 1),
                  pl.BlockSpec((CTX_LEN, GQA_KV), lambda b, i: (ctx_blk + b, kcol)),
                  pl.BlockSpec((CTX_LEN, GQA_KV), lambda b, i: (ctx_blk + b, vcol))],
        out_specs=pl.BlockSpec((Q_BLOCK, GQA_Q), lambda b, i: (b * nb + i, 0)),
        compiler_params=_cparams(("parallel", "arbitrary")),
        name="gqa_window_attn",
    )(sink.astype(F32), qkv, qkv, qkv, qkv, qkv, qkv, qkv, qkv, qkv)


GATHER_TILE = 256


def _row_copy(src_hbm, row, dst, r, sem):
    return pltpu.make_async_copy(src_hbm.at[pl.ds(row, 1), :], dst.at[pl.ds(r, 1), :], sem)


def _gather_kernel(tok_ref, tv_ref, u_hbm, valid_ref, o_ref, buf, sem):
    i = pl.program_id(0)
    n = pl.num_programs(0)

    def start_tile(t, slot):
        @pl.when(tv_ref[t] > 0)
        def _():
            def issue(r8, carry):
                for j in range(8):
                    r = r8 * 8 + j
                    _row_copy(u_hbm, tok_ref[t * GATHER_TILE + r], buf.at[slot], r, sem.at[slot]).start(priority=j % 2)
                return carry

            lax.fori_loop(0, GATHER_TILE // 8, issue, 0)

    @pl.when(i == 0)
    def _():
        start_tile(0, 0)

    @pl.when(i + 1 < n)
    def _():
        start_tile(i + 1, (i + 1) % 2)

    slot = i % 2

    @pl.when(tv_ref[i] > 0)
    def _():
        def drain(r, carry):
            _row_copy(u_hbm, 0, buf.at[slot], r, sem.at[slot]).wait()
            return carry

        lax.fori_loop(0, GATHER_TILE, drain, 0, unroll=8)
        o_ref[...] = jnp.where(valid_ref[...] > 0.0, buf[slot], 0.0).astype(o_ref.dtype)

    @pl.when(tv_ref[i] == 0)
    def _():
        o_ref[...] = jnp.zeros_like(o_ref)


def _moe_gather(slot_tok, slot_valid, u):
    n_slots = slot_tok.shape[0]
    n_tiles = n_slots // GATHER_TILE
    tile_valid = (jnp.max(slot_valid.reshape(n_tiles, GATHER_TILE), axis=1) > 0.0).astype(I32)
    return pl.pallas_call(
        _gather_kernel,
        out_shape=_SDS((n_slots, D_MODEL), BF16),
        grid_spec=pltpu.PrefetchScalarGridSpec(
            num_scalar_prefetch=2,
            grid=(n_tiles,),
            in_specs=[pl.BlockSpec(memory_space=pl.ANY),
                      pl.BlockSpec((GATHER_TILE, 1), lambda i, tok, tv: (i, 0))],
            out_specs=pl.BlockSpec((GATHER_TILE, D_MODEL), lambda i, tok, tv: (i, 0)),
            scratch_shapes=[pltpu.VMEM((2, GATHER_TILE, D_MODEL), F32), pltpu.SemaphoreType.DMA((2,))]),
        compiler_params=_cparams(("arbitrary",)),
        name="moe_gather",
    )(slot_tok, tile_valid, u, slot_valid.reshape(n_slots, 1))


def _moe_ffn_kernel(be_ref, bv_ref, x_ref, wg_ref, wu_ref, wd_ref, o_ref, h_ref):
    i, s = pl.program_id(0), pl.program_id(1)
    n_sub = bv_ref[i]
    for k in range(1, MOE_BLOCK // MOE_SUB + 1):
        _swiglu_step(s, x_ref, wg_ref.at[0], wu_ref.at[0], wd_ref.at[0], o_ref, h_ref,
                     rows=k * MOE_SUB, guard=n_sub == k)

    @pl.when((n_sub == 0) & (s >= FF_STEPS))
    def _():
        o_ref[...] = jnp.zeros_like(o_ref)


def _moe_ffn(block_expert, block_valid, xs, w_gate, w_up, w_down):
    n_slots = xs.shape[0]
    tm = MOE_BLOCK

    def ff(i, s, be, bv):
        return (be[i], 0, jnp.where(bv[i] > 0, jnp.minimum(s, FF_STEPS - 1), FF_STEPS - 1))

    def down(i, s, be, bv):
        return (be[i], 0, jnp.where(bv[i] > 0, jnp.maximum(s - FF_STEPS, 0), OUT_STEPS - 1))

    return pl.pallas_call(
        _moe_ffn_kernel,
        out_shape=_SDS((n_slots, D_MODEL), F32),
        grid_spec=pltpu.PrefetchScalarGridSpec(
            num_scalar_prefetch=2,
            grid=(n_slots // tm, FF_STEPS + OUT_STEPS),
            in_specs=[pl.BlockSpec((tm, D_MODEL), lambda i, s, be, bv: (i, 0), pipeline_mode=pl.Buffered(1)),
                      pl.BlockSpec((1, D_MODEL, FF_TILE), ff),
                      pl.BlockSpec((1, D_MODEL, FF_TILE), ff),
                      pl.BlockSpec((1, D_FF, OUT_TILE), down)],
            out_specs=pl.BlockSpec((tm, OUT_TILE), lambda i, s, be, bv: (i, jnp.maximum(s - FF_STEPS, 0))),
            scratch_shapes=[pltpu.VMEM((tm, D_FF), BF16)]),
        compiler_params=_cparams(("arbitrary", "arbitrary"), SWIGLU_VMEM_LIMIT),
        name="moe_grouped_swiglu",
    )(block_expert, block_valid, xs, w_gate, w_up, w_down)


def _combine_kernel(pos_ref, y_hbm, wt_ref, h_ref, gp_ref, gate_ref, o_ref, buf_a, buf_b, sem):
    i = pl.program_id(0)
    n = pl.num_programs(0)
    base = i * GATHER_TILE

    def start_tile(t, slot):
        def issue(r, carry):
            a = 2 * (t * GATHER_TILE + r)
            _row_copy(y_hbm, pos_ref[a], buf_a.at[slot], r, sem.at[slot]).start(priority=0)
            _row_copy(y_hbm, pos_ref[a + 1], buf_b.at[slot], r, sem.at[slot]).start(priority=1)
            return carry

        lax.fori_loop(0, GATHER_TILE, issue, 0, unroll=8)

    @pl.when(i == 0)
    def _():
        start_tile(0, 0)

    @pl.when(i + 1 < n)
    def _():
        start_tile(i + 1, (i + 1) % 2)

    slot = i % 2

    def drain(r, carry):
        _row_copy(y_hbm, 0, buf_a.at[slot], r, sem.at[slot]).wait()
        _row_copy(y_hbm, 0, buf_b.at[slot], r, sem.at[slot]).wait()
        return carry

    lax.fori_loop(0, GATHER_TILE, drain, 0, unroll=8)
    wt = wt_ref[...]
    y = wt[:, 0:1] * buf_a[slot] + wt[:, 1:2] * buf_b[slot]
    grp = _group_of_row(base)
    o_ref[...] = h_ref[...] + _mod_row(gate_ref, grp) * _rms(y, gp_ref[...])


def _moe_combine(pos, ys, wts, h, g_post, mod, gate_lk):
    n = wts.shape[0]
    tm = GATHER_TILE
    return pl.pallas_call(
        _combine_kernel,
        out_shape=_SDS((n, D_MODEL), F32),
        grid_spec=pltpu.PrefetchScalarGridSpec(
            num_scalar_prefetch=1,
            grid=(n // tm,),
            in_specs=[pl.BlockSpec(memory_space=pl.ANY),
                      pl.BlockSpec((tm, 128), lambda i, pos: (i, 0)),
                      pl.BlockSpec((tm, D_MODEL), lambda i, pos: (i, 0)),
                      pl.BlockSpec((1, D_MODEL), lambda i, pos: (0, 0)),
                      pl.BlockSpec((1, 8, D_MODEL), lambda i, pos: (gate_lk[0], 0, gate_lk[1]))],
            out_specs=pl.BlockSpec((tm, D_MODEL), lambda i, pos: (i, 0)),
            scratch_shapes=[pltpu.VMEM((2, tm, D_MODEL), F32), pltpu.VMEM((2, tm, D_MODEL), F32),
                            pltpu.SemaphoreType.DMA((2,))]),
        compiler_params=_cparams(("arbitrary",)),
        name="moe_combine",
    )(pos, ys, wts, h, g_post.reshape(1, D_MODEL), mod)


def _moe_routing(idx):
    flat_e = idx[:, :TOP_K].reshape(-1)
    n_assign = flat_e.shape[0]
    onehot = (flat_e[:, None] == jnp.arange(N_EXPERTS, dtype=I32)[None, :]).astype(I32)
    csum = jnp.cumsum(onehot, axis=0)
    counts = csum[-1]
    padded = (counts + MOE_BLOCK - 1) // MOE_BLOCK * MOE_BLOCK
    pad_end = jnp.cumsum(padded)
    pad_start = pad_end - padded
    pos = jnp.sum(onehot * (pad_start[None, :] + csum - 1), axis=1).astype(I32)
    slot_tok = jnp.zeros((MOE_SLOTS,), I32).at[pos].set(jnp.arange(n_assign, dtype=I32) // TOP_K)
    slot = jnp.arange(MOE_SLOTS, dtype=I32)
    used_end = pad_start + counts
    slot_valid = jnp.any((slot[:, None] >= pad_start[None, :]) & (slot[:, None] < used_end[None, :]), axis=1)
    blk_start = jnp.arange(MOE_SLOTS // MOE_BLOCK, dtype=I32) * MOE_BLOCK
    last_start = jnp.maximum(pad_end[-1] - MOE_BLOCK, 0)
    blk_e = jnp.searchsorted(pad_end, jnp.minimum(blk_start, last_start), side='right').astype(I32)
    blk_e = jnp.minimum(blk_e, N_EXPERTS - 1)
    rows_used = jnp.clip(used_end[blk_e] - blk_start, 0, MOE_BLOCK)
    blk_sub = ((rows_used + MOE_SUB - 1) // MOE_SUB).astype(I32)
    return pos, slot_tok, slot_valid.astype(F32), blk_e, blk_sub


def _rest_of_w_in(w_in):
    dn_in = P_MAIN + 4 * DN_HEADS
    ba = w_in[:, P_MAIN:dn_in]
    cq = w_in[:, dn_in:dn_in + MLA_Q_RANK]
    ckv = w_in[:, dn_in + MLA_Q_RANK:dn_in + MLA_Q_RANK + MLA_KV_RANK]
    k_rope = w_in[:, dn_in + MLA_Q_RANK + MLA_KV_RANK:]
    pad = jnp.zeros((D_MODEL, R_CQ - R_MISC - MLA_ROPE - 4 * DN_HEADS), w_in.dtype)
    return jnp.concatenate([ckv, k_rope, ba, pad, cq], axis=1).astype(BF16)


def _rearrange_w_qb(w_qb):
    w = w_qb.reshape(MLA_Q_RANK, MLA_HEADS, MLA_NOPE + MLA_ROPE)
    w = jnp.pad(w, ((0, 0), (0, 0), (0, MLA_QK - MLA_NOPE - MLA_ROPE)))
    return w.reshape(MLA_Q_RANK, MLA_HEADS * MLA_QK).astype(BF16)


def kernel(x, c, ctx, c_ctx, ada_w, ada_b, norm_g, ab_w_in, dn_conv_w, dn_a_log, dn_dt_bias, dn_norm_g, mla_q_norm_g, mla_w_qb, mla_kv_norm_g, mla_w_kvb, ab_w_out, ffn_w_gate, ffn_w_up, ffn_w_down, gqa_w_qkv, gqa_b_qkv, gqa_sink, gqa_w_out, gqa_b_out, moe_w_router, moe_w_gate, moe_w_up, moe_w_down):
    assert x.shape == (BATCH, SEQ, D_MODEL) and ctx.shape == (BATCH, CTX_LEN, D_MODEL) and ada_w.shape[0] == 2
    h = (x.reshape(N_LAT, D_MODEL), ctx.reshape(N_CTX, D_MODEL))
    cvec = jnp.zeros((8, D_MODEL), F32).at[:BATCH].set(c).at[BATCH].set(c_ctx)
    mod = _ada(cvec, ada_w, ada_b)
    (cos1, sin1), (cos2, sin2) = _rope_tables()

    u = _modulate_in(*h, norm_g[0, 0], mod, 0, 0, 1)
    p = _matmul(u, ab_w_in[0], None, BF16, MM_ROW_TILE, 1024, n=P_MAIN)
    p_rest = _matmul(u, _rest_of_w_in(ab_w_in[0]), None, F32, MM_ROW_TILE, R_WIDTH // 2)
    o_f, o_b = _deltanet(_dn_conv(p, dn_conv_w[0]), _dn_gates(p_rest, dn_a_log[0], dn_dt_bias[0]))
    q, k, v = _mla_proj(p_rest, mla_q_norm_g[0], mla_kv_norm_g[0], _rearrange_w_qb(mla_w_qb[0]),
                        mla_w_kvb[0].astype(BF16), cos1, sin1)
    y = _ab_out(o_f, o_b, p, dn_norm_g[0], _mla_attention(q, k, v), ab_w_out[0].astype(BF16))
    h, u = _post(h, y, norm_g[0, 1], norm_g[0, 2], mod, (0, 2), (0, 3), (0, 4), BF16)
    y = _ffn(u, ffn_w_gate[0], ffn_w_up[0], ffn_w_down[0])
    h, u = _post(h, y, norm_g[0, 3], norm_g[1, 0], mod, (0, 5), (1, 0), (1, 1), BF16)

    qkv = _gqa_qkv(u, gqa_w_qkv[0].astype(BF16), gqa_b_qkv[0], cos2, sin2)
    o = _gqa_attention(qkv, gqa_sink[0])
    y = _matmul(o, gqa_w_out[0].astype(BF16), gqa_b_out[0], F32, 1024, 1024)
    h, u, idx, wts = _post(h, y, norm_g[1, 1], norm_g[1, 2], mod, (1, 2), (1, 3), (1, 4), F32, moe_w_router[0])
    pos, slot_tok, slot_valid, blk_e, blk_sub = _moe_routing(idx)
    xs = _moe_gather(slot_tok, slot_valid, u)
    ys = _moe_ffn(blk_e, blk_sub, xs, moe_w_gate[0], moe_w_up[0], moe_w_down[0])
    out = _moe_combine(pos, ys, wts, h, norm_g[1, 3], mod, (1, 5))
    return out.reshape(BATCH, SEQ, D_MODEL)
```

```python
import functools
import math

import jax
import jax.numpy as jnp
from jax import lax
from jax.experimental import pallas as pl
from jax.experimental.pallas import tpu as pltpu

F32 = jnp.float32
BF16 = jnp.bfloat16
I32 = jnp.int32

D_MODEL = 2048
BATCH = 2
SEQ = 4096
CTX_LEN = 256
GRID_W = 64
EPS = 1e-6
ROPE_THETA = 10000.0

DN_HEADS = 8
DN_HEAD_DIM = 128
DN_CONV = 5
DN_CHUNK = 64
MLA_HEADS = 8
MLA_Q_RANK = 768
MLA_KV_RANK = 512
MLA_NOPE = 128
MLA_ROPE = 64
MLA_V = 128
GQA_HEADS = 32
GQA_KV_HEADS = 4
GQA_HEAD_DIM = 64
WINDOW = 128
Q_BLOCK = 128
D_FF = 7168
N_EXPERTS = 8
TOP_K = 2

DN_W = DN_HEADS * DN_HEAD_DIM
DN_QKV = 3 * DN_W
N_LAT = BATCH * SEQ
N_CTX = BATCH * CTX_LEN
N_TOK = N_LAT + N_CTX
GQA_Q = GQA_HEADS * GQA_HEAD_DIM
GQA_KV = GQA_KV_HEADS * GQA_HEAD_DIM

P_QKV = 0
P_Z = DN_QKV
P_MAIN = P_Z + DN_W
R_CKV = 0
R_MISC = MLA_KV_RANK
R_CQ = MLA_Q_RANK
R_WIDTH = R_CQ + MLA_Q_RANK

ROW_TILE = 512
MM_ROW_TILE = 1088
MOE_BLOCK = 1024
MOE_SUB = 256
MOE_SLOTS = N_LAT * TOP_K + N_EXPERTS * MOE_BLOCK
VMEM_LIMIT = 56 * 1024 * 1024

_SDS = jax.ShapeDtypeStruct
_HIGH = lax.Precision.HIGHEST


def _cparams(sem, vmem=None):
    return pltpu.CompilerParams(dimension_semantics=sem, vmem_limit_bytes=vmem)


def _dot(a, b):
    return jnp.dot(a, b, preferred_element_type=F32)


def _dot_nt(a, b):
    return lax.dot_general(a, b, (((1,), (1,)), ((), ())), preferred_element_type=F32)


def _dot_tn(a, b):
    return lax.dot_general(a, b, (((0,), (0,)), ((), ())), preferred_element_type=F32)


def _dot_hi(a, b):
    return jnp.dot(a, b, preferred_element_type=F32, precision=_HIGH)


def _silu(x):
    return x * jax.nn.sigmoid(x)


def _rms(x, g):
    return x * lax.rsqrt(jnp.mean(x * x, axis=-1, keepdims=True) + EPS) * g


def _group_of_row(r0):
    return (r0 >= SEQ).astype(I32) + (r0 >= 2 * SEQ).astype(I32)


def _mod_spec(layer, k):
    return pl.BlockSpec((1, 8, D_MODEL), lambda *_: (layer, 0, k))


def _mod_row(ref, grp):
    return ref[0, pl.ds(grp, 1), :]


def _ada_kernel(c_ref, w_ref, b_ref, o_ref):
    s = _silu(c_ref[...]).astype(BF16)
    o_ref[0] = _dot(s, w_ref[0].astype(BF16)) + b_ref[0]


def _ada(cvec, ada_w, ada_b):
    n_layers, _, n = ada_w.shape
    tn = 1024
    return pl.pallas_call(
        _ada_kernel,
        out_shape=_SDS((n_layers, 8, n), F32),
        grid=(n_layers, n // tn),
        in_specs=[pl.BlockSpec((8, D_MODEL), lambda l, j: (0, 0)),
                  pl.BlockSpec((1, D_MODEL, tn), lambda l, j: (l, 0, j)),
                  pl.BlockSpec((1, 1, tn), lambda l, j: (l, 0, j))],
        out_specs=pl.BlockSpec((1, 8, tn), lambda l, j: (l, 0, j)),
        compiler_params=_cparams(("parallel", "parallel")),
        name="ada_mod",
    )(cvec, ada_w, ada_b.reshape(n_layers, 1, n))


_LAT_TILES = N_LAT // ROW_TILE
assert N_CTX == ROW_TILE


def _split_specs(width, col=0):
    lat = pl.BlockSpec((ROW_TILE, width), lambda i, *_: (jnp.minimum(i, _LAT_TILES - 1), col))
    ctx = pl.BlockSpec((ROW_TILE, width), lambda i, *_: (0, col))
    return [lat, ctx]


def _split_read(lat_ref, ctx_ref, sl=slice(None)):
    return jnp.where(pl.program_id(0) == _LAT_TILES, ctx_ref[:, sl], lat_ref[:, sl])


def _modin_kernel(x_ref, c_ref, g_ref, sh_ref, sc_ref, u_ref):
    grp = _group_of_row(pl.program_id(0) * ROW_TILE)
    h = _split_read(x_ref, c_ref)
    u = _rms(h, g_ref[...]) * (1.0 + _mod_row(sc_ref, grp)) + _mod_row(sh_ref, grp)
    u_ref[...] = u.astype(u_ref.dtype)


def _modulate_in(x, ctx, g, mod, layer, k_shift, k_scale):
    row = pl.BlockSpec((ROW_TILE, D_MODEL), lambda i: (i, 0))
    vec = pl.BlockSpec((1, D_MODEL), lambda i: (0, 0))
    return pl.pallas_call(
        _modin_kernel,
        out_shape=_SDS((N_TOK, D_MODEL), BF16),
        grid=(N_TOK // ROW_TILE,),
        in_specs=_split_specs(D_MODEL) + [vec, _mod_spec(layer, k_shift), _mod_spec(layer, k_scale)],
        out_specs=row,
        compiler_params=_cparams(("parallel",)),
        name="modulate_in",
    )(x, ctx, g.reshape(1, D_MODEL), mod, mod)


def _route_top2(logits):
    lane = lax.broadcasted_iota(I32, logits.shape, 1)
    neg = -jnp.inf
    l0 = jnp.where(lane < N_EXPERTS, logits, neg)
    m1 = jnp.max(l0, axis=-1, keepdims=True)
    i1 = jnp.min(jnp.where(l0 == m1, lane, 128), axis=-1, keepdims=True)
    l1 = jnp.where(lane == i1, neg, l0)
    m2 = jnp.max(l1, axis=-1, keepdims=True)
    i2 = jnp.min(jnp.where(l1 == m2, lane, 128), axis=-1, keepdims=True)
    e = jnp.exp(m2 - m1)
    w1 = 1.0 / (1.0 + e)
    idx = jnp.where(lane == 0, i1, jnp.where(lane == 1, i2, 0))
    wts = jnp.where(lane == 0, w1, jnp.where(lane == 1, e * w1, 0.0))
    return idx, wts


def _post_kernel(*refs, split, route):
    refs = list(refs)
    if split:
        h = _split_read(refs.pop(0), refs.pop(0))
    else:
        h = refs.pop(0)[...]
    y_ref, gp_ref, gate_ref, gn_ref, sh_ref, sc_ref = refs[:6]
    refs = refs[6:]
    wr_ref = refs.pop(0) if route else None
    hn_ref, u_ref = refs[:2]
    grp = _group_of_row(pl.program_id(0) * ROW_TILE)
    hn = h + _mod_row(gate_ref, grp) * _rms(y_ref[...], gp_ref[...])
    hn_ref[...] = hn
    u = _rms(hn, gn_ref[...]) * (1.0 + _mod_row(sc_ref, grp)) + _mod_row(sh_ref, grp)
    u_ref[...] = u.astype(u_ref.dtype)
    if route:
        idx_ref, wt_ref = refs[2:4]
        idx_ref[...], wt_ref[...] = _route_top2(_dot(u.astype(BF16), wr_ref[...]))


def _post(h, y, g_post, g_next, mod, gate_lk, shift_lk, scale_lk, u_dtype, w_router=None):
    n = y.shape[0]
    split = isinstance(h, tuple)
    route = w_router is not None
    row = pl.BlockSpec((ROW_TILE, D_MODEL), lambda i: (i, 0))
    vec = pl.BlockSpec((1, D_MODEL), lambda i: (0, 0))
    slab = pl.BlockSpec((ROW_TILE, 128), lambda i: (i, 0))
    h_specs, h_args = (_split_specs(D_MODEL), list(h)) if split else ([row], [h])
    extra_specs, extra_args, extra_out, extra_out_specs = [], [], (), ()
    if route:
        w = jnp.zeros((D_MODEL, 128), BF16).at[:, :N_EXPERTS].set(w_router.astype(BF16))
        extra_specs, extra_args = [pl.BlockSpec((D_MODEL, 128), lambda i: (0, 0))], [w]
        extra_out, extra_out_specs = (_SDS((n, 128), I32), _SDS((n, 128), F32)), (slab, slab)
    return pl.pallas_call(
        functools.partial(_post_kernel, split=split, route=route),
        out_shape=(_SDS((n, D_MODEL), F32), _SDS((n, D_MODEL), u_dtype)) + extra_out,
        grid=(n // ROW_TILE,),
        in_specs=h_specs + [row, vec, _mod_spec(*gate_lk), vec, _mod_spec(*shift_lk), _mod_spec(*scale_lk)] + extra_specs,
        out_specs=(row, row) + extra_out_specs,
        compiler_params=_cparams(("parallel",)),
        name="residual_post",
    )(*h_args, y, g_post.reshape(1, D_MODEL), mod, g_next.reshape(1, D_MODEL), mod, mod, *extra_args)


def _mm_kernel(a_ref, w_ref, b_ref, o_ref):
    o_ref[...] = (_dot(a_ref[...], w_ref[...].astype(BF16)) + b_ref[...]).astype(o_ref.dtype)


def _matmul(a, w, bias, out_dtype, tm, tn, n=None):
    m, k = a.shape
    n = w.shape[1] if n is None else n
    if bias is None:
        bias = jnp.zeros((n,), F32)
    return pl.pallas_call(
        _mm_kernel,
        out_shape=_SDS((m, n), out_dtype),
        grid=(m // tm, n // tn),
        in_specs=[pl.BlockSpec((tm, k), lambda i, j: (i, 0)),
                  pl.BlockSpec((k, tn), lambda i, j: (0, j)),
                  pl.BlockSpec((1, tn), lambda i, j: (0, j))],
        out_specs=pl.BlockSpec((tm, tn), lambda i, j: (i, j)),
        compiler_params=_cparams(("parallel", "arbitrary"), VMEM_LIMIT),
        name="matmul",
    )(a, w, bias.reshape(1, n))


FF_TILE = 512
SWIGLU_VMEM_LIMIT = 60 * 1024 * 1024
OUT_TILE = 256
FF_STEPS = D_FF // FF_TILE
OUT_STEPS = D_MODEL // OUT_TILE


def _swiglu_step(s, x_ref, wg_ref, wu_ref, wd_ref, o_ref, h_ref, rows=None, guard=True):
    total = x_ref.shape[0]
    rows = total if rows is None else rows

    @pl.when(guard & (s < FF_STEPS))
    def _():
        x = x_ref[0:rows, :]
        a = _dot(x, wg_ref[...].astype(BF16))
        b = _dot(x, wu_ref[...].astype(BF16))
        h_ref[0:rows, pl.ds(pl.multiple_of(s * FF_TILE, FF_TILE), FF_TILE)] = (_silu(a) * b).astype(BF16)

    @pl.when(guard & (s >= FF_STEPS))
    def _():
        o_ref[0:rows, :] = _dot(h_ref[0:rows, :], wd_ref[...].astype(BF16))
        if rows < total:
            o_ref[rows:total, :] = jnp.zeros((total - rows, o_ref.shape[1]), o_ref.dtype)


def _ffn_kernel(u_ref, wg_ref, wu_ref, wd_ref, o_ref, h_ref):
    _swiglu_step(pl.program_id(1), u_ref, wg_ref, wu_ref, wd_ref, o_ref, h_ref)


def _ffn(u, w_gate, w_up, w_down):
    m = u.shape[0]
    tm = MM_ROW_TILE

    def ff(i, s):
        return (0, jnp.minimum(s, FF_STEPS - 1))

    def out(s):
        return jnp.maximum(s - FF_STEPS, 0)

    return pl.pallas_call(
        _ffn_kernel,
        out_shape=_SDS((m, D_MODEL), F32),
        grid=(m // tm, FF_STEPS + OUT_STEPS),
        in_specs=[pl.BlockSpec((tm, D_MODEL), lambda i, s: (i, 0), pipeline_mode=pl.Buffered(1)),
                  pl.BlockSpec((D_MODEL, FF_TILE), ff),
                  pl.BlockSpec((D_MODEL, FF_TILE), ff),
                  pl.BlockSpec((D_FF, OUT_TILE), lambda i, s: (0, out(s)))],
        out_specs=pl.BlockSpec((tm, OUT_TILE), lambda i, s: (i, out(s))),
        scratch_shapes=[pltpu.VMEM((tm, D_FF), BF16)],
        compiler_params=_cparams(("parallel", "arbitrary"), SWIGLU_VMEM_LIMIT),
        name="swiglu_ffn",
    )(u, w_gate, w_up, w_down)


CONV_TILE = 256
CONV_HALO = 16
_SEQ_STARTS = (0, SEQ // CONV_TILE, 2 * SEQ // CONV_TILE, 2 * SEQ // CONV_TILE + 1)
_SEQ_LASTS = (SEQ // CONV_TILE - 1, 2 * SEQ // CONV_TILE - 1, 2 * SEQ // CONV_TILE, 2 * SEQ // CONV_TILE + 1)


def _conv_kernel(prev_ref, cur_ref, next_ref, w_ref, o_ref, buf):
    i = pl.program_id(0)
    j = pl.program_id(1)
    tm = CONV_TILE
    first = functools.reduce(jnp.logical_or, [i == s for s in _SEQ_STARTS])
    last = functools.reduce(jnp.logical_or, [i == s for s in _SEQ_LASTS])
    hb = CONV_HALO
    buf[0:hb, :] = jnp.where(first, 0.0, prev_ref[...].astype(F32))
    buf[hb:hb + tm, :] = cur_ref[...].astype(F32)
    buf[hb + tm:2 * hb + tm, :] = jnp.where(last, 0.0, next_ref[...].astype(F32))
    pad = DN_CONV // 2
    acc = buf[hb - pad:hb - pad + tm, :] * w_ref[0:1, :]
    for d in range(1, DN_CONV):
        acc = acc + buf[hb - pad + d:hb - pad + d + tm, :] * w_ref[d:d + 1, :]
    x = _silu(acc)
    q_scale = jnp.where(j == 0, DN_HEAD_DIM ** -0.5, 1.0)
    for hh in range(DN_HEADS):
        xh = x[:, hh * DN_HEAD_DIM:(hh + 1) * DN_HEAD_DIM]
        inv = lax.rsqrt(jnp.sum(xh * xh, axis=-1, keepdims=True) + EPS) * q_scale
        o_ref[:, hh * DN_HEAD_DIM:(hh + 1) * DN_HEAD_DIM] = xh * jnp.where(j == 2, 1.0, inv)


def _dn_conv(p, conv_w):
    n = p.shape[0]
    tm = CONV_TILE
    hb = CONV_HALO
    per_tile = tm // hb
    n_halo = n // hb
    return pl.pallas_call(
        _conv_kernel,
        out_shape=_SDS((n, DN_QKV), F32),
        grid=(n // tm, 3),
        in_specs=[pl.BlockSpec((hb, DN_W), lambda i, j: (jnp.maximum(i * per_tile - 1, 0), j)),
                  pl.BlockSpec((tm, DN_W), lambda i, j: (i, j)),
                  pl.BlockSpec((hb, DN_W), lambda i, j: (jnp.minimum((i + 1) * per_tile, n_halo - 1), j)),
                  pl.BlockSpec((DN_CONV, DN_W), lambda i, j: (0, j))],
        out_specs=pl.BlockSpec((tm, DN_W), lambda i, j: (i, j)),
        scratch_shapes=[pltpu.VMEM((tm + 2 * hb, DN_W), F32)],
        compiler_params=_cparams(("parallel", "parallel")),
        name="dn_conv",
    )(p, p, p, conv_w)


def _gates_kernel(m_ref, a_ref, dt_ref, o_ref):
    x = pltpu.roll(m_ref[...], 64, 1)
    lane = lax.broadcasted_iota(I32, x.shape, 1)
    z = x + dt_ref[...]
    softplus = jnp.maximum(z, 0.0) + jnp.log1p(jnp.exp(-jnp.abs(z)))
    g = a_ref[...] * softplus
    o_ref[...] = jnp.where(lane < 2 * DN_HEADS, jax.nn.sigmoid(x), jnp.where(lane < 4 * DN_HEADS, g, 0.0))


def _dn_gates(p, a_log, dt_bias):
    n = p.shape[0]
    neg_a = jnp.zeros((1, 128), F32).at[0, 2 * DN_HEADS:4 * DN_HEADS].set(-jnp.exp(a_log.astype(F32)).reshape(-1))
    dtb = jnp.zeros((1, 128), F32).at[0, 2 * DN_HEADS:4 * DN_HEADS].set(dt_bias.astype(F32).reshape(-1))
    tm = ROW_TILE
    vec = pl.BlockSpec((1, 128), lambda i: (0, 0))
    return pl.pallas_call(
        _gates_kernel,
        out_shape=_SDS((n, 128), F32),
        grid=(n // tm,),
        in_specs=[pl.BlockSpec((tm, 128), lambda i: (i, R_MISC // 128)), vec, vec],
        out_specs=pl.BlockSpec((tm, 128), lambda i: (i, 0)),
        compiler_params=_cparams(("parallel",)),
        name="dn_gates",
    )(p, neg_a, dtb)


_CTX_CHUNKS = CTX_LEN // DN_CHUNK
_LAT_CHUNKS = SEQ // DN_CHUNK
_DN_STEPS = _CTX_CHUNKS + _LAT_CHUNKS


def _dn_fwd_block(b, t):
    return jnp.where(t < _CTX_CHUNKS, (2 * SEQ + b * CTX_LEN) // DN_CHUNK + t, b * _LAT_CHUNKS + (t - _CTX_CHUNKS))


def _dn_bwd_block(b, t):
    return jnp.where(t < _CTX_CHUNKS, (2 * SEQ + b * CTX_LEN) // DN_CHUNK + (_CTX_CHUNKS - 1 - t),
                     b * _LAT_CHUNKS + (_DN_STEPS - 1 - t))


DN_GROUP = 2
_GROUP_ROWS = DN_GROUP * DN_CHUNK
_GROUP_STATE = DN_GROUP * DN_HEAD_DIM


_DN_BASE = 8


def _block_diag(x):
    zero = jnp.zeros((DN_CHUNK, DN_HEAD_DIM), x.dtype)
    rows = []
    for hh in range(DN_GROUP):
        piece = x[hh * DN_CHUNK:(hh + 1) * DN_CHUNK]
        rows.append(jnp.concatenate([piece if j == hh else zero for j in range(DN_GROUP)], axis=1))
    return jnp.concatenate(rows, axis=0)


def _dn_groups(items, eye, base_mask):
    it = items
    n = range(len(it))
    gx = [jnp.broadcast_to(it[g]["gc"], (_GROUP_ROWS, _GROUP_ROWS)) for g in n]
    gamma = [jnp.where(it[g]["incl"], jnp.exp(jnp.where(it[g]["incl"], gx[g] - gx[g].T, 0.0)), 0.0) for g in n]
    kb = [it[g]["k"].astype(BF16) for g in n]
    kk = [_dot_nt(kb[g], kb[g]) for g in n]
    a = [jnp.where(it[g]["strict"], it[g]["beta"] * kk[g] * gamma[g], 0.0) for g in n]
    d = [jnp.where(base_mask, a[g], 0.0) for g in n]
    tinv = [eye - d[g] for g in n]
    pw = [d[g].astype(BF16) for g in n]
    for _ in range(int(math.log2(_DN_BASE)) - 1):
        pw = [_dot(pw[g], pw[g]).astype(BF16) for g in n]
        tinv = [tinv[g] + _dot(tinv[g].astype(BF16), pw[g]) for g in n]
    for lvl in range(len(it[0]["levels"])):
        tb = [tinv[g].astype(BF16) for g in n]
        mt = [_dot(jnp.where(it[g]["levels"][lvl], a[g], 0.0).astype(BF16), tb[g]).astype(BF16) for g in n]
        tinv = [tinv[g] - _dot(tb[g], mt[g]) for g in n]
    tb = [tinv[g].astype(BF16) for g in n]
    egc = [jnp.exp(it[g]["gc"]) for g in n]
    rhs = [jnp.concatenate([it[g]["v"] * it[g]["beta"], it[g]["k"] * (it[g]["beta"] * egc[g])], axis=1) for g in n]
    rhs_hi = [rhs[g].astype(BF16) for g in n]
    rhs_lo = [(rhs[g] - rhs_hi[g].astype(F32)).astype(BF16) for g in n]
    sol = [_dot(jnp.concatenate([tb[g], tb[g]], axis=1), jnp.concatenate([rhs_hi[g], rhs_lo[g]], axis=0)) for g in n]
    qk = [(_dot_nt(it[g]["q"].astype(BF16), kb[g]) * gamma[g]).astype(BF16) for g in n]
    s = [it[g]["s_ref"][...] for g in n]
    sb = [s[g].astype(BF16) for g in n]
    v_new = [sol[g][:, :DN_HEAD_DIM] - _dot(_block_diag(sol[g][:, DN_HEAD_DIM:].astype(BF16)), sb[g]) for g in n]
    vb = [v_new[g].astype(BF16) for g in n]
    o = [_dot(_block_diag((it[g]["q"] * egc[g]).astype(BF16)), sb[g]) + _dot(qk[g], vb[g]) for g in n]
    for g in n:
        k_dec = _block_diag((it[g]["k"] * jnp.exp(it[g]["g_last"] - it[g]["gc"])).astype(BF16))
        it[g]["s_ref"][...] = s[g] * jnp.exp(it[g]["g_last_state"]) + _dot_tn(k_dec, vb[g])
    return o


def _stack_cols(x, lanes, rows_each):
    pieces = [x[:, l:l + 1] for l in lanes]
    if rows_each is not None:
        pieces = [jnp.broadcast_to(pc, (rows_each, 1)) for pc in pieces]
    return jnp.concatenate(pieces, axis=0)


def _dn_kernel(*refs):
    n_in = BATCH * 2 * 4
    in_refs, (ofl_ref, ofc_ref, obl_ref, obc_ref), (s_ref, o_scr) = refs[:n_in], refs[n_in:n_in + 4], refs[n_in + 4:]
    t = pl.program_id(0)

    @pl.when(t == 0)
    def _():
        s_ref[...] = jnp.zeros_like(s_ref)

    c = DN_CHUNK
    ri = lax.broadcasted_iota(I32, (_GROUP_ROWS, _GROUP_ROWS), 0)
    ci = lax.broadcasted_iota(I32, (_GROUP_ROWS, _GROUP_ROWS), 1)
    same_head = (ri // c) == (ci // c)
    eye = (ri == ci).astype(F32)
    r1 = lax.broadcasted_iota(I32, (c, c), 0)
    c1 = lax.broadcasted_iota(I32, (c, c), 1)
    base_mask = (ri // _DN_BASE) == (ci // _DN_BASE)
    masks = []
    for d in range(2):
        incl = same_head & ((ri >= ci) if d == 0 else (ri <= ci))
        strict = same_head & ((ri > ci) if d == 0 else (ri < ci))
        levels = []
        size = _DN_BASE
        while size < c:
            lo_blk, hi_blk = ((ci, ri) if d == 0 else (ri, ci))
            levels.append(((ri // (2 * size)) == (ci // (2 * size)))
                          & ((lo_blk // size) % 2 == 0) & ((hi_blk // size) % 2 == 1))
            size *= 2
        masks.append((incl, strict, levels))
    items, where = [], []
    for b, d in [(b, d) for b in range(BATCH) for d in range(2)]:
        q_ref, k_ref, v_ref, g_ref = in_refs[(b * 2 + d) * 4:(b * 2 + d) * 4 + 4]
        incl, strict, levels = masks[d]
        gates = g_ref[...]
        tri = ((r1 >= c1) if d == 0 else (r1 <= c1)).astype(BF16)
        g_hi = gates.astype(BF16)
        g_r1 = gates - g_hi.astype(F32)
        g_mid = g_r1.astype(BF16)
        g_lo = (g_r1 - g_mid.astype(F32)).astype(BF16)
        gc = _dot(tri, g_hi) + _dot(tri, g_mid) + _dot(tri, g_lo)
        last_row = c - 1 if d == 0 else 0
        tot = gc[last_row:last_row + 1, :]
        for grp in range(DN_HEADS // DN_GROUP):
            heads = range(grp * DN_GROUP, (grp + 1) * DN_GROUP)
            lanes_b = [d * DN_HEADS + hh for hh in heads]
            lanes_g = [2 * DN_HEADS + lb for lb in lanes_b]

            def stack(ref):
                return jnp.concatenate([ref[:, hh * DN_HEAD_DIM:(hh + 1) * DN_HEAD_DIM] for hh in heads], axis=0)

            items.append(dict(q=stack(q_ref), k=stack(k_ref), v=stack(v_ref),
                              beta=_stack_cols(gates, lanes_b, None), gc=_stack_cols(gc, lanes_g, None),
                              g_last=_stack_cols(tot, lanes_g, c), g_last_state=_stack_cols(tot, lanes_g, DN_HEAD_DIM),
                              incl=incl, strict=strict, levels=levels, s_ref=s_ref.at[b, d, grp]))
            where.append((d, b, heads))
    outs = _dn_groups(items, eye, base_mask)
    for o, (d, b, heads) in zip(outs, where):
        for j, hh in enumerate(heads):
            o_scr[d, b, :, hh * DN_HEAD_DIM:(hh + 1) * DN_HEAD_DIM] = o[j * c:(j + 1) * c]

    @pl.when(t < _CTX_CHUNKS)
    def _():
        ofc_ref[...] = o_scr[0]
        obc_ref[...] = o_scr[1]

    @pl.when(t >= _CTX_CHUNKS)
    def _():
        ofl_ref[...] = o_scr[0]
        obl_ref[...] = o_scr[1]


def _deltanet(qkv, gates):
    c = DN_CHUNK
    ins, args = [], []
    for b in range(BATCH):
        for fn in (_dn_fwd_block, _dn_bwd_block):
            for width, col, arr in ((DN_W, 0, qkv), (DN_W, 1, qkv), (DN_W, 2, qkv), (128, 0, gates)):
                ins.append(pl.BlockSpec((c, width), functools.partial(lambda t, b, fn, col: (fn(b, t), col),
                                                                      b=b, fn=fn, col=col)))
                args.append(arr)
    last = _DN_STEPS - 1
    lat = _SDS((BATCH, SEQ, DN_W), F32)
    ctx = _SDS((BATCH, CTX_LEN, DN_W), F32)
    blk = (BATCH, c, DN_W)
    outs = pl.pallas_call(
        _dn_kernel,
        out_shape=(lat, ctx, lat, ctx),
        grid=(_DN_STEPS,),
        in_specs=ins,
        out_specs=(pl.BlockSpec(blk, lambda t: (0, jnp.maximum(t - _CTX_CHUNKS, 0), 0)),
                   pl.BlockSpec(blk, lambda t: (0, jnp.minimum(t, _CTX_CHUNKS - 1), 0)),
                   pl.BlockSpec(blk, lambda t: (0, jnp.minimum(last - t, _LAT_CHUNKS - 1), 0)),
                   pl.BlockSpec(blk, lambda t: (0, jnp.maximum(_CTX_CHUNKS - 1 - t, 0), 0))),
        scratch_shapes=[pltpu.VMEM((BATCH, 2, DN_HEADS // DN_GROUP, _GROUP_STATE, DN_HEAD_DIM), F32),
                        pltpu.VMEM((2, BATCH, c, DN_W), F32)],
        compiler_params=_cparams(("arbitrary",)),
        name="deltanet",
    )(*args)
    ofl, ofc, obl, obc = outs
    return ((ofl.reshape(N_LAT, DN_W), ofc.reshape(N_CTX, DN_W)),
            (obl.reshape(N_LAT, DN_W), obc.reshape(N_CTX, DN_W)))


def _rope_tables():
    t = jnp.arange(SEQ)
    row = (t // GRID_W).astype(F32)
    col = (t % GRID_W).astype(F32)
    n_freq = MLA_ROPE // 4
    inv_freq = ROPE_THETA ** (-jnp.arange(n_freq, dtype=F32) / n_freq)
    ang = jnp.concatenate([row[:, None] * inv_freq, col[:, None] * inv_freq], axis=-1)
    cos, sin = jnp.cos(ang), jnp.sin(ang)
    cos64 = jnp.concatenate([cos, cos], axis=-1)
    sin64 = jnp.concatenate([-sin, sin], axis=-1)

    def assemble(lat_cos, lat_sin):
        c = jnp.concatenate([jnp.tile(lat_cos, (BATCH, 1)), jnp.ones((N_CTX, 128), F32)], axis=0)
        s = jnp.concatenate([jnp.tile(lat_sin, (BATCH, 1)), jnp.zeros((N_CTX, 128), F32)], axis=0)
        return c, s

    cos_l = jnp.concatenate([cos64, jnp.ones((SEQ, 64), F32)], axis=-1)
    sin_l = jnp.concatenate([sin64, jnp.zeros((SEQ, 64), F32)], axis=-1)
    one_head = assemble(cos_l, sin_l)
    two_heads = assemble(jnp.tile(cos64, (1, 2)), jnp.tile(sin64, (1, 2)))
    return one_head, two_heads


def _rope128(x, cosv, sinv):
    lane = lax.broadcasted_iota(I32, x.shape, 1)
    swapped = jnp.where((lane % 64) < 32, pltpu.roll(x, 96, 1), pltpu.roll(x, 32, 1))
    return x * cosv + swapped * sinv


MLA_QK = 256
MLA_VX = 256


def _mla_proj_kernel(ckv_ref, cq_ref, misc_ref, gq_ref, gkv_ref, wq_ref, wkv_ref, cos_ref, sin_ref,
                     q_ref, k_ref, v_ref):
    cq = _rms(cq_ref[...].astype(F32), gq_ref[...]).astype(BF16)
    ckv = _rms(ckv_ref[...].astype(F32), gkv_ref[...]).astype(BF16)
    qf = _dot(cq, wq_ref[...])
    kvf = _dot(ckv, wkv_ref[...])
    cosv, sinv = cos_ref[...], sin_ref[...]
    lane = lax.broadcasted_iota(I32, cosv.shape, 1)
    k_rope = _rope128(jnp.where(lane < MLA_ROPE, misc_ref[...], 0.0), cosv, sinv).astype(BF16)
    scale = (MLA_NOPE + MLA_ROPE) ** -0.5
    for hh in range(MLA_HEADS):
        lo, mid, hi = hh * MLA_QK, hh * MLA_QK + 128, (hh + 1) * MLA_QK
        q_ref[:, lo:mid] = (qf[:, lo:mid] * scale).astype(BF16)
        q_ref[:, mid:hi] = (_rope128(qf[:, mid:hi], cosv, sinv) * scale).astype(BF16)
        k_ref[:, lo:mid] = kvf[:, lo:mid].astype(BF16)
        k_ref[:, mid:hi] = k_rope
        v_ref[:, lo:mid] = kvf[:, mid:hi].astype(BF16)
        v_ref[:, mid:hi] = jnp.ones((kvf.shape[0], MLA_VX - MLA_V), BF16)


def _mla_proj(p_rest, q_norm_g, kv_norm_g, w_q, w_kv, cosv, sinv):
    n = p_rest.shape[0]
    tm = ROW_TILE
    wide = MLA_HEADS * MLA_QK

    def rows(width, col):
        return pl.BlockSpec((tm, width), lambda i: (i, col))

    def whole(shape):
        return pl.BlockSpec(shape, lambda i: (0, 0))

    return pl.pallas_call(
        _mla_proj_kernel,
        out_shape=(_SDS((n, wide), BF16), _SDS((n, wide), BF16), _SDS((n, MLA_HEADS * MLA_VX), BF16)),
        grid=(n // tm,),
        in_specs=[rows(MLA_KV_RANK, R_CKV // MLA_KV_RANK), rows(MLA_Q_RANK, R_CQ // MLA_Q_RANK),
                  rows(128, R_MISC // 128), whole((1, MLA_Q_RANK)), whole((1, MLA_KV_RANK)),
                  whole((MLA_Q_RANK, wide)), whole((MLA_KV_RANK, wide)), rows(128, 0), rows(128, 0)],
        out_specs=(rows(wide, 0), rows(wide, 0), rows(MLA_HEADS * MLA_VX, 0)),
        compiler_params=_cparams(("parallel",), VMEM_LIMIT),
        name="mla_proj",
    )(p_rest, p_rest, p_rest, q_norm_g.reshape(1, -1), kv_norm_g.reshape(1, -1), w_q, w_kv, cosv, sinv)


MLA_KEY_CHUNK = 2048


def _flash_kernel(*refs, chunks):
    n_seg = (len(refs) - 2) // 2
    q_ref, o_ref = refs[0], refs[-1]
    k_refs, v_refs = refs[1:1 + n_seg], refs[1 + n_seg:1 + 2 * n_seg]
    q = q_ref[...]

    def scores(c):
        seg, off, size = c
        return _dot_nt(q, k_refs[seg][off:off + size, :])

    m = jnp.full((q.shape[0], 1), -jnp.inf, F32)
    acc = jnp.zeros((q.shape[0], MLA_VX), F32)
    s_next = scores(chunks[0])
    for j, (seg, off, size) in enumerate(chunks):
        s = s_next
        if j + 1 < len(chunks):
            s_next = scores(chunks[j + 1])
        m_new = jnp.maximum(m, jnp.max(s, axis=-1, keepdims=True))
        p = jnp.exp(s - m_new).astype(BF16)
        acc = jnp.exp(m - m_new) * acc + _dot(p, v_refs[seg][off:off + size, :])
        m = m_new
    o_ref[...] = (acc[:, :MLA_V] / acc[:, MLA_V:]).astype(o_ref.dtype)


def _mla_attention(q, k, v):
    tq = 1024
    nq = SEQ // tq
    ctx_blk = 2 * SEQ // CTX_LEN
    lat_chunks = ((0, 0, CTX_LEN),) + tuple((1, off, MLA_KEY_CHUNK) for off in range(0, SEQ, MLA_KEY_CHUNK))
    lat = pl.pallas_call(
        functools.partial(_flash_kernel, chunks=lat_chunks),
        out_shape=_SDS((N_LAT, MLA_HEADS * MLA_V), BF16),
        grid=(BATCH, MLA_HEADS, nq),
        in_specs=[pl.BlockSpec((tq, MLA_QK), lambda b, h, i: (b * nq + i, h)),
                  pl.BlockSpec((CTX_LEN, MLA_QK), lambda b, h, i: (ctx_blk + b, h)),
                  pl.BlockSpec((SEQ, MLA_QK), lambda b, h, i: (b, h)),
                  pl.BlockSpec((CTX_LEN, MLA_VX), lambda b, h, i: (ctx_blk + b, h)),
                  pl.BlockSpec((SEQ, MLA_VX), lambda b, h, i: (b, h))],
        out_specs=pl.BlockSpec((tq, MLA_V), lambda b, h, i: (b * nq + i, h)),
        compiler_params=_cparams(("parallel", "parallel", "arbitrary"), VMEM_LIMIT),
        name="mla_attn_latent",
    )(q, k, k, v, v)
    ctx = pl.pallas_call(
        functools.partial(_flash_kernel, chunks=((0, 0, CTX_LEN),)),
        out_shape=_SDS((N_CTX, MLA_HEADS * MLA_V), BF16),
        grid=(BATCH, MLA_HEADS),
        in_specs=[pl.BlockSpec((CTX_LEN, MLA_QK), lambda b, h: (ctx_blk + b, h)),
                  pl.BlockSpec((CTX_LEN, MLA_QK), lambda b, h: (ctx_blk + b, h)),
                  pl.BlockSpec((CTX_LEN, MLA_VX), lambda b, h: (ctx_blk + b, h))],
        out_specs=pl.BlockSpec((CTX_LEN, MLA_V), lambda b, h: (b, h)),
        compiler_params=_cparams(("parallel", "parallel")),
        name="mla_attn_context",
    )(q, k, v)
    return lat, ctx


def _ab_out_kernel(ofl_ref, ofc_ref, obl_ref, obc_ref, ml_ref, mc_ref, z_ref, g_ref, w_ref, o_ref, a_ref):
    @pl.when(pl.program_id(1) == 0)
    def _():
        for hh in range(DN_HEADS):
            sl = slice(hh * DN_HEAD_DIM, (hh + 1) * DN_HEAD_DIM)
            o = _split_read(ofl_ref, ofc_ref, sl) + _split_read(obl_ref, obc_ref, sl)
            a_ref[:, sl] = (_rms(o, g_ref[...]) * _silu(z_ref[:, sl].astype(F32))).astype(BF16)
        a_ref[:, DN_W:] = _split_read(ml_ref, mc_ref)

    o_ref[...] = _dot(a_ref[...], w_ref[...].astype(BF16))


def _ab_out(o_f, o_b, p, dn_norm_g, mla_o, w_out):
    tm, tn = ROW_TILE, 1024
    half = DN_W
    return pl.pallas_call(
        _ab_out_kernel,
        out_shape=_SDS((N_TOK, D_MODEL), F32),
        grid=(N_TOK // tm, D_MODEL // tn),
        in_specs=_split_specs(half) + _split_specs(half) + _split_specs(half) + [
            pl.BlockSpec((tm, half), lambda i, j: (i, P_Z // half)),
            pl.BlockSpec((1, DN_HEAD_DIM), lambda i, j: (0, 0)),
            pl.BlockSpec((2 * half, tn), lambda i, j: (0, j))],
        out_specs=pl.BlockSpec((tm, tn), lambda i, j: (i, j)),
        scratch_shapes=[pltpu.VMEM((tm, 2 * half), BF16)],
        compiler_params=_cparams(("parallel", "arbitrary"), VMEM_LIMIT),
        name="ab_out_proj",
    )(*o_f, *o_b, *mla_o, p, dn_norm_g.reshape(1, -1), w_out)


def _mm_rope_kernel(a_ref, w_ref, b_ref, cos_ref, sin_ref, o_ref, *, tn, rope_cols):
    y = _dot(a_ref[...], w_ref[...].astype(BF16)) + b_ref[...]
    col0 = pl.program_id(1) * tn
    cosv, sinv = cos_ref[...], sin_ref[...]
    for s in range(tn // 128):
        ys = y[:, s * 128:(s + 1) * 128]
        roped = _rope128(ys, cosv, sinv)
        o_ref[:, s * 128:(s + 1) * 128] = jnp.where(col0 + s * 128 < rope_cols, roped, ys).astype(o_ref.dtype)


def _gqa_qkv(u, w, bias, cosv, sinv):
    m, k = u.shape
    n = w.shape[1]
    tm, tn = MM_ROW_TILE, 1280
    return pl.pallas_call(
        functools.partial(_mm_rope_kernel, tn=tn, rope_cols=GQA_Q + GQA_KV),
        out_shape=_SDS((m, n), BF16),
        grid=(m // tm, n // tn),
        in_specs=[pl.BlockSpec((tm, k), lambda i, j: (i, 0)),
                  pl.BlockSpec((k, tn), lambda i, j: (0, j)),
                  pl.BlockSpec((1, tn), lambda i, j: (0, j)),
                  pl.BlockSpec((tm, 128), lambda i, j: (i, 0)),
                  pl.BlockSpec((tm, 128), lambda i, j: (i, 0))],
        out_specs=pl.BlockSpec((tm, tn), lambda i, j: (i, j)),
        compiler_params=_cparams(("parallel", "arbitrary"), VMEM_LIMIT),
        name="gqa_qkv_rope",
    )(u, w, bias.reshape(1, n), cosv, sinv)


def _gqa_kernel(sink_ref, q_ref, kp_ref, kc_ref, kn_ref, vp_ref, vc_ref, vn_ref, kx_ref, vx_ref, o_ref):
    i = pl.program_id(1)
    nb = SEQ // Q_BLOCK
    n_keys = CTX_LEN + 3 * Q_BLOCK
    groups = GQA_HEADS // GQA_KV_HEADS
    slabs = groups // 2
    dh = GQA_HEAD_DIM
    qi = lax.broadcasted_iota(I32, (Q_BLOCK, n_keys), 0)
    kj = lax.broadcasted_iota(I32, (Q_BLOCK, n_keys), 1) - CTX_LEN
    rel = kj - Q_BLOCK - qi
    valid = (kj < 0) | ((jnp.abs(rel) <= WINDOW) & ((kj >= Q_BLOCK) | (i > 0)) & ((kj < 2 * Q_BLOCK) | (i < nb - 1)))
    bias = jnp.where(valid, 0.0, -jnp.inf).astype(F32)
    bias = jnp.concatenate([bias] * slabs, axis=0)
    k_all = jnp.concatenate([kx_ref[...], kp_ref[...], kc_ref[...], kn_ref[...]], axis=0)
    v_all = jnp.concatenate([vx_ref[...], vp_ref[...], vc_ref[...], vn_ref[...]], axis=0)
    low = lax.broadcasted_iota(I32, (n_keys, 128), 1) < dh

    def block_diag(x_all, kvh):
        pair = x_all[:, (kvh // 2) * 128:(kvh // 2 + 1) * 128].astype(F32)
        other = pltpu.roll(pair, dh, 1)
        first, second = (pair, other) if kvh % 2 == 0 else (other, pair)
        return jnp.concatenate([jnp.where(low, first, 0.0), jnp.where(low, 0.0, second)], axis=0).astype(BF16)

    def scores(kvh):
        q = jnp.concatenate([q_ref[:, (kvh * slabs + r) * 128:(kvh * slabs + r + 1) * 128] for r in range(slabs)], axis=0)
        return _dot_nt(q * (dh ** -0.5), block_diag(k_all, kvh))

    row_lo = lax.broadcasted_iota(I32, (2 * n_keys, 128), 0) < n_keys
    lane_lo = lax.broadcasted_iota(I32, (2 * n_keys, 128), 1) < dh
    ones_cols = jnp.where(row_lo == lane_lo, 1.0, 0.0).astype(BF16)

    def softmax(kvh, s):
        ps, sink_terms = [], []
        for half in range(2):
            sh = s[:, half * n_keys:(half + 1) * n_keys] + bias
            sink = jnp.concatenate([jnp.full((Q_BLOCK, 1), sink_ref[(kvh * slabs + r) * 2 + half], F32)
                                    for r in range(slabs)], axis=0)
            m = jnp.maximum(jnp.max(sh, axis=-1, keepdims=True), sink)
            ps.append(jnp.exp(sh - m).astype(BF16))
            sink_terms.append(jnp.exp(sink - m))
        return jnp.concatenate(ps, axis=1), sink_terms

    def finish(kvh, p, sink_terms):
        o = _dot(p, jnp.concatenate([block_diag(v_all, kvh), ones_cols], axis=1))
        num, den = o[:, :128], o[:, 128:]
        lane_low = lax.broadcasted_iota(I32, num.shape, 1) < dh
        o = num / (den + jnp.where(lane_low, sink_terms[0], sink_terms[1]))
        for r in range(slabs):
            o_ref[:, (kvh * slabs + r) * 128:(kvh * slabs + r + 1) * 128] = o[r * Q_BLOCK:(r + 1) * Q_BLOCK].astype(o_ref.dtype)

    s_next = scores(0)
    for kvh in range(GQA_KV_HEADS):
        s_cur = s_next
        if kvh + 1 < GQA_KV_HEADS:
            s_next = scores(kvh + 1)
        p, dens = softmax(kvh, s_cur)
        finish(kvh, p, dens)


def _gqa_attention(qkv, sink):
    nb = SEQ // Q_BLOCK
    kcol, vcol = GQA_Q // GQA_KV, GQA_Q // GQA_KV + 1
    ctx_blk = 2 * SEQ // CTX_LEN

    def win(col, shift):
        return pl.BlockSpec((Q_BLOCK, GQA_KV), lambda b, i: (b * nb + jnp.clip(i + shift, 0, nb - 1), col))

    return pl.pallas_call(
        _gqa_kernel,
        out_shape=_SDS((N_LAT, GQA_Q), BF16),
        grid=(BATCH, nb),
        in_specs=[pl.BlockSpec(memory_space=pltpu.SMEM),
                  pl.BlockSpec((Q_BLOCK, GQA_Q), lambda b, i: (b * nb + i, 0)),
                  win(kcol, -1), win(kcol, 0), win(kcol, 1), win(vcol, -1), win(vcol, 0), win(vcol, 1),
                  pl.BlockSpec((CTX_LEN, GQA_KV), lambda b, i: (ctx_blk + b, kcol)),
                  pl.BlockSpec((CTX_LEN, GQA_KV), lambda b, i: (ctx_blk + b, vcol))],
        out_specs=pl.BlockSpec((Q_BLOCK, GQA_Q), lambda b, i: (b * nb + i, 0)),
        compiler_params=_cparams(("parallel", "arbitrary")),
        name="gqa_window_attn",
    )(sink.astype(F32), qkv, qkv, qkv, qkv, qkv, qkv, qkv, qkv, qkv)


GATHER_TILE = 256


def _row_copy(src_hbm, row, dst, r, sem):
    return pltpu.make_async_copy(src_hbm.at[pl.ds(row, 1), :], dst.at[pl.ds(r, 1), :], sem)


def _gather_kernel(tok_ref, tv_ref, u_hbm, valid_ref, o_ref, buf, sem):
    i = pl.program_id(0)
    n = pl.num_programs(0)

    def start_tile(t, slot):
        @pl.when(tv_ref[t] > 0)
        def _():
            def issue(r8, carry):
                for j in range(8):
                    r = r8 * 8 + j
                    _row_copy(u_hbm, tok_ref[t * GATHER_TILE + r], buf.at[slot], r, sem.at[slot]).start(priority=j % 2)
                return carry

            lax.fori_loop(0, GATHER_TILE // 8, issue, 0)

    @pl.when(i == 0)
    def _():
        start_tile(0, 0)

    @pl.when(i + 1 < n)
    def _():
        start_tile(i + 1, (i + 1) % 2)

    slot = i % 2

    @pl.when(tv_ref[i] > 0)
    def _():
        def drain(r, carry):
            _row_copy(u_hbm, 0, buf.at[slot], r, sem.at[slot]).wait()
            return carry

        lax.fori_loop(0, GATHER_TILE, drain, 0, unroll=8)
        o_ref[...] = jnp.where(valid_ref[...] > 0.0, buf[slot], 0.0).astype(o_ref.dtype)

    @pl.when(tv_ref[i] == 0)
    def _():
        o_ref[...] = jnp.zeros_like(o_ref)


def _moe_gather(slot_tok, slot_valid, u):
    n_slots = slot_tok.shape[0]
    n_tiles = n_slots // GATHER_TILE
    tile_valid = (jnp.max(slot_valid.reshape(n_tiles, GATHER_TILE), axis=1) > 0.0).astype(I32)
    return pl.pallas_call(
        _gather_kernel,
        out_shape=_SDS((n_slots, D_MODEL), BF16),
        grid_spec=pltpu.PrefetchScalarGridSpec(
            num_scalar_prefetch=2,
            grid=(n_tiles,),
            in_specs=[pl.BlockSpec(memory_space=pl.ANY),
                      pl.BlockSpec((GATHER_TILE, 1), lambda i, tok, tv: (i, 0))],
            out_specs=pl.BlockSpec((GATHER_TILE, D_MODEL), lambda i, tok, tv: (i, 0)),
            scratch_shapes=[pltpu.VMEM((2, GATHER_TILE, D_MODEL), F32), pltpu.SemaphoreType.DMA((2,))]),
        compiler_params=_cparams(("arbitrary",)),
        name="moe_gather",
    )(slot_tok, tile_valid, u, slot_valid.reshape(n_slots, 1))


def _moe_ffn_kernel(be_ref, bv_ref, x_ref, wg_ref, wu_ref, wd_ref, o_ref, h_ref):
    i, s = pl.program_id(0), pl.program_id(1)
    n_sub = bv_ref[i]
    for k in range(1, MOE_BLOCK // MOE_SUB + 1):
        _swiglu_step(s, x_ref, wg_ref.at[0], wu_ref.at[0], wd_ref.at[0], o_ref, h_ref,
                     rows=k * MOE_SUB, guard=n_sub == k)

    @pl.when((n_sub == 0) & (s >= FF_STEPS))
    def _():
        o_ref[...] = jnp.zeros_like(o_ref)


def _moe_ffn(block_expert, block_valid, xs, w_gate, w_up, w_down):
    n_slots = xs.shape[0]
    tm = MOE_BLOCK

    def ff(i, s, be, bv):
        return (be[i], 0, jnp.where(bv[i] > 0, jnp.minimum(s, FF_STEPS - 1), FF_STEPS - 1))

    def down(i, s, be, bv):
        return (be[i], 0, jnp.where(bv[i] > 0, jnp.maximum(s - FF_STEPS, 0), OUT_STEPS - 1))

    return pl.pallas_call(
        _moe_ffn_kernel,
        out_shape=_SDS((n_slots, D_MODEL), F32),
        grid_spec=pltpu.PrefetchScalarGridSpec(
            num_scalar_prefetch=2,
            grid=(n_slots // tm, FF_STEPS + OUT_STEPS),
            in_specs=[pl.BlockSpec((tm, D_MODEL), lambda i, s, be, bv: (i, 0), pipeline_mode=pl.Buffered(1)),
                      pl.BlockSpec((1, D_MODEL, FF_TILE), ff),
                      pl.BlockSpec((1, D_MODEL, FF_TILE), ff),
                      pl.BlockSpec((1, D_FF, OUT_TILE), down)],
            out_specs=pl.BlockSpec((tm, OUT_TILE), lambda i, s, be, bv: (i, jnp.maximum(s - FF_STEPS, 0))),
            scratch_shapes=[pltpu.VMEM((tm, D_FF), BF16)]),
        compiler_params=_cparams(("arbitrary", "arbitrary"), SWIGLU_VMEM_LIMIT),
        name="moe_grouped_swiglu",
    )(block_expert, block_valid, xs, w_gate, w_up, w_down)


def _combine_kernel(pos_ref, y_hbm, wt_ref, h_ref, gp_ref, gate_ref, o_ref, buf_a, buf_b, sem):
    i = pl.program_id(0)
    n = pl.num_programs(0)
    base = i * GATHER_TILE

    def start_tile(t, slot):
        def issue(r, carry):
            a = 2 * (t * GATHER_TILE + r)
            _row_copy(y_hbm, pos_ref[a], buf_a.at[slot], r, sem.at[slot]).start(priority=0)
            _row_copy(y_hbm, pos_ref[a + 1], buf_b.at[slot], r, sem.at[slot]).start(priority=1)
            return carry

        lax.fori_loop(0, GATHER_TILE, issue, 0, unroll=8)

    @pl.when(i == 0)
    def _():
        start_tile(0, 0)

    @pl.when(i + 1 < n)
    def _():
        start_tile(i + 1, (i + 1) % 2)

    slot = i % 2

    def drain(r, carry):
        _row_copy(y_hbm, 0, buf_a.at[slot], r, sem.at[slot]).wait()
        _row_copy(y_hbm, 0, buf_b.at[slot], r, sem.at[slot]).wait()
        return carry

    lax.fori_loop(0, GATHER_TILE, drain, 0, unroll=8)
    wt = wt_ref[...]
    y = wt[:, 0:1] * buf_a[slot] + wt[:, 1:2] * buf_b[slot]
    grp = _group_of_row(base)
    o_ref[...] = h_ref[...] + _mod_row(gate_ref, grp) * _rms(y, gp_ref[...])


def _moe_combine(pos, ys, wts, h, g_post, mod, gate_lk):
    n = wts.shape[0]
    tm = GATHER_TILE
    return pl.pallas_call(
        _combine_kernel,
        out_shape=_SDS((n, D_MODEL), F32),
        grid_spec=pltpu.PrefetchScalarGridSpec(
            num_scalar_prefetch=1,
            grid=(n // tm,),
            in_specs=[pl.BlockSpec(memory_space=pl.ANY),
                      pl.BlockSpec((tm, 128), lambda i, pos: (i, 0)),
                      pl.BlockSpec((tm, D_MODEL), lambda i, pos: (i, 0)),
                      pl.BlockSpec((1, D_MODEL), lambda i, pos: (0, 0)),
                      pl.BlockSpec((1, 8, D_MODEL), lambda i, pos: (gate_lk[0], 0, gate_lk[1]))],
            out_specs=pl.BlockSpec((tm, D_MODEL), lambda i, pos: (i, 0)),
            scratch_shapes=[pltpu.VMEM((2, tm, D_MODEL), F32), pltpu.VMEM((2, tm, D_MODEL), F32),
                            pltpu.SemaphoreType.DMA((2,))]),
        compiler_params=_cparams(("arbitrary",)),
        name="moe_combine",
    )(pos, ys, wts, h, g_post.reshape(1, D_MODEL), mod)


def _moe_routing(idx):
    flat_e = idx[:, :TOP_K].reshape(-1)
    n_assign = flat_e.shape[0]
    onehot = (flat_e[:, None] == jnp.arange(N_EXPERTS, dtype=I32)[None, :]).astype(I32)
    csum = jnp.cumsum(onehot, axis=0)
    counts = csum[-1]
    padded = (counts + MOE_BLOCK - 1) // MOE_BLOCK * MOE_BLOCK
    pad_end = jnp.cumsum(padded)
    pad_start = pad_end - padded
    pos = jnp.sum(onehot * (pad_start[None, :] + csum - 1), axis=1).astype(I32)
    slot_tok = jnp.zeros((MOE_SLOTS,), I32).at[pos].set(jnp.arange(n_assign, dtype=I32) // TOP_K)
    slot = jnp.arange(MOE_SLOTS, dtype=I32)
    used_end = pad_start + counts
    slot_valid = jnp.any((slot[:, None] >= pad_start[None, :]) & (slot[:, None] < used_end[None, :]), axis=1)
    blk_start = jnp.arange(MOE_SLOTS // MOE_BLOCK, dtype=I32) * MOE_BLOCK
    last_start = jnp.maximum(pad_end[-1] - MOE_BLOCK, 0)
    blk_e = jnp.searchsorted(pad_end, jnp.minimum(blk_start, last_start), side='right').astype(I32)
    blk_e = jnp.minimum(blk_e, N_EXPERTS - 1)
    rows_used = jnp.clip(used_end[blk_e] - blk_start, 0, MOE_BLOCK)
    blk_sub = ((rows_used + MOE_SUB - 1) // MOE_SUB).astype(I32)
    return pos, slot_tok, slot_valid.astype(F32), blk_e, blk_sub


def _rest_of_w_in(w_in):
    dn_in = P_MAIN + 4 * DN_HEADS
    ba = w_in[:, P_MAIN:dn_in]
    cq = w_in[:, dn_in:dn_in + MLA_Q_RANK]
    ckv = w_in[:, dn_in + MLA_Q_RANK:dn_in + MLA_Q_RANK + MLA_KV_RANK]
    k_rope = w_in[:, dn_in + MLA_Q_RANK + MLA_KV_RANK:]
    pad = jnp.zeros((D_MODEL, R_CQ - R_MISC - MLA_ROPE - 4 * DN_HEADS), w_in.dtype)
    return jnp.concatenate([ckv, k_rope, ba, pad, cq], axis=1)


def _rearrange_w_qb(w_qb):
    w = w_qb.reshape(MLA_Q_RANK, MLA_HEADS, MLA_NOPE + MLA_ROPE)
    w = jnp.pad(w, ((0, 0), (0, 0), (0, MLA_QK - MLA_NOPE - MLA_ROPE)))
    return w.reshape(MLA_Q_RANK, MLA_HEADS * MLA_QK).astype(BF16)


def kernel(x, c, ctx, c_ctx, ada_w, ada_b, norm_g, ab_w_in, dn_conv_w, dn_a_log, dn_dt_bias, dn_norm_g, mla_q_norm_g, mla_w_qb, mla_kv_norm_g, mla_w_kvb, ab_w_out, ffn_w_gate, ffn_w_up, ffn_w_down, gqa_w_qkv, gqa_b_qkv, gqa_sink, gqa_w_out, gqa_b_out, moe_w_router, moe_w_gate, moe_w_up, moe_w_down):
    assert x.shape == (BATCH, SEQ, D_MODEL) and ctx.shape == (BATCH, CTX_LEN, D_MODEL) and ada_w.shape[0] == 2
    h = (x.reshape(N_LAT, D_MODEL), ctx.reshape(N_CTX, D_MODEL))
    cvec = jnp.zeros((8, D_MODEL), F32).at[:BATCH].set(c).at[BATCH].set(c_ctx)
    mod = _ada(cvec, ada_w, ada_b)
    (cos1, sin1), (cos2, sin2) = _rope_tables()

    u = _modulate_in(*h, norm_g[0, 0], mod, 0, 0, 1)
    p = _matmul(u, ab_w_in[0], None, BF16, MM_ROW_TILE, 1024, n=P_MAIN)
    p_rest = _matmul(u, _rest_of_w_in(ab_w_in[0]), None, F32, MM_ROW_TILE, R_WIDTH // 2)
    o_f, o_b = _deltanet(_dn_conv(p, dn_conv_w[0]), _dn_gates(p_rest, dn_a_log[0], dn_dt_bias[0]))
    q, k, v = _mla_proj(p_rest, mla_q_norm_g[0], mla_kv_norm_g[0], _rearrange_w_qb(mla_w_qb[0]),
                        mla_w_kvb[0].astype(BF16), cos1, sin1)
    y = _ab_out(o_f, o_b, p, dn_norm_g[0], _mla_attention(q, k, v), ab_w_out[0].astype(BF16))
    h, u = _post(h, y, norm_g[0, 1], norm_g[0, 2], mod, (0, 2), (0, 3), (0, 4), BF16)
    y = _ffn(u, ffn_w_gate[0], ffn_w_up[0], ffn_w_down[0])
    h, u = _post(h, y, norm_g[0, 3], norm_g[1, 0], mod, (0, 5), (1, 0), (1, 1), BF16)

    qkv = _gqa_qkv(u, gqa_w_qkv[0].astype(BF16), gqa_b_qkv[0], cos2, sin2)
    o = _gqa_attention(qkv, gqa_sink[0])
    y = _matmul(o, gqa_w_out[0].astype(BF16), gqa_b_out[0], F32, 1024, 1024)
    h, u, idx, wts = _post(h, y, norm_g[1, 1], norm_g[1, 2], mod, (1, 2), (1, 3), (1, 4), F32, moe_w_router[0])
    pos, slot_tok, slot_valid, blk_e, blk_sub = _moe_routing(idx)
    xs = _moe_gather(slot_tok, slot_valid, u)
    ys = _moe_ffn(blk_e, blk_sub, xs, moe_w_gate[0], moe_w_up[0], moe_w_down[0])
    out = _moe_combine(pos, ys, wts, h, norm_g[1, 3], mod, (1, 5))
    return out.reshape(BATCH, SEQ, D_MODEL)
```

```python
import functools
import math

import jax
import jax.numpy as jnp
from jax import lax
from jax.experimental import pallas as pl
from jax.experimental.pallas import tpu as pltpu

F32 = jnp.float32
BF16 = jnp.bfloat16
I32 = jnp.int32

D_MODEL = 2048
BATCH = 2
SEQ = 4096
CTX_LEN = 256
GRID_W = 64
EPS = 1e-6
ROPE_THETA = 10000.0

DN_HEADS = 8
DN_HEAD_DIM = 128
DN_CONV = 5
DN_CHUNK = 64
MLA_HEADS = 8
MLA_Q_RANK = 768
MLA_KV_RANK = 512
MLA_NOPE = 128
MLA_ROPE = 64
MLA_V = 128
GQA_HEADS = 32
GQA_KV_HEADS = 4
GQA_HEAD_DIM = 64
WINDOW = 128
Q_BLOCK = 128
D_FF = 7168
N_EXPERTS = 8
TOP_K = 2

DN_W = DN_HEADS * DN_HEAD_DIM
DN_QKV = 3 * DN_W
N_LAT = BATCH * SEQ
N_CTX = BATCH * CTX_LEN
N_TOK = N_LAT + N_CTX
GQA_Q = GQA_HEADS * GQA_HEAD_DIM
GQA_KV = GQA_KV_HEADS * GQA_HEAD_DIM

P_QKV = 0
P_Z = DN_QKV
P_MAIN = P_Z + DN_W
R_CKV = 0
R_MISC = MLA_KV_RANK
R_CQ = MLA_Q_RANK
R_WIDTH = R_CQ + MLA_Q_RANK

ROW_TILE = 512
MM_ROW_TILE = 1088
MOE_BLOCK = 1024
MOE_SUB = 256
MOE_SLOTS = N_LAT * TOP_K + N_EXPERTS * MOE_BLOCK
VMEM_LIMIT = 56 * 1024 * 1024

_SDS = jax.ShapeDtypeStruct
_HIGH = lax.Precision.HIGHEST


def _cparams(sem, vmem=None):
    return pltpu.CompilerParams(dimension_semantics=sem, vmem_limit_bytes=vmem)


def _dot(a, b):
    return jnp.dot(a, b, preferred_element_type=F32)


def _dot_nt(a, b):
    return lax.dot_general(a, b, (((1,), (1,)), ((), ())), preferred_element_type=F32)


def _dot_tn(a, b):
    return lax.dot_general(a, b, (((0,), (0,)), ((), ())), preferred_element_type=F32)


def _dot_hi(a, b):
    return jnp.dot(a, b, preferred_element_type=F32, precision=_HIGH)


def _silu(x):
    return x * jax.nn.sigmoid(x)


def _rms(x, g):
    return x * lax.rsqrt(jnp.mean(x * x, axis=-1, keepdims=True) + EPS) * g


def _group_of_row(r0):
    return (r0 >= SEQ).astype(I32) + (r0 >= 2 * SEQ).astype(I32)


def _mod_spec(layer, k):
    return pl.BlockSpec((1, 8, D_MODEL), lambda *_: (layer, 0, k))


def _mod_row(ref, grp):
    return ref[0, pl.ds(grp, 1), :]


def _ada_kernel(c_ref, w_ref, b_ref, o_ref):
    s = _silu(c_ref[...]).astype(BF16)
    o_ref[0] = _dot(s, w_ref[0].astype(BF16)) + b_ref[0]


def _ada(cvec, ada_w, ada_b):
    n_layers, _, n = ada_w.shape
    tn = 1024
    return pl.pallas_call(
        _ada_kernel,
        out_shape=_SDS((n_layers, 8, n), F32),
        grid=(n_layers, n // tn),
        in_specs=[pl.BlockSpec((8, D_MODEL), lambda l, j: (0, 0)),
                  pl.BlockSpec((1, D_MODEL, tn), lambda l, j: (l, 0, j)),
                  pl.BlockSpec((1, 1, tn), lambda l, j: (l, 0, j))],
        out_specs=pl.BlockSpec((1, 8, tn), lambda l, j: (l, 0, j)),
        compiler_params=_cparams(("parallel", "parallel")),
        name="ada_mod",
    )(cvec, ada_w, ada_b.reshape(n_layers, 1, n))


_LAT_TILES = N_LAT // ROW_TILE
assert N_CTX == ROW_TILE


def _split_specs(width, col=0):
    lat = pl.BlockSpec((ROW_TILE, width), lambda i, *_: (jnp.minimum(i, _LAT_TILES - 1), col))
    ctx = pl.BlockSpec((ROW_TILE, width), lambda i, *_: (0, col))
    return [lat, ctx]


def _split_read(lat_ref, ctx_ref, sl=slice(None)):
    return jnp.where(pl.program_id(0) == _LAT_TILES, ctx_ref[:, sl], lat_ref[:, sl])


def _modin_kernel(x_ref, c_ref, g_ref, sh_ref, sc_ref, u_ref):
    grp = _group_of_row(pl.program_id(0) * ROW_TILE)
    h = _split_read(x_ref, c_ref)
    u = _rms(h, g_ref[...]) * (1.0 + _mod_row(sc_ref, grp)) + _mod_row(sh_ref, grp)
    u_ref[...] = u.astype(u_ref.dtype)


def _modulate_in(x, ctx, g, mod, layer, k_shift, k_scale):
    row = pl.BlockSpec((ROW_TILE, D_MODEL), lambda i: (i, 0))
    vec = pl.BlockSpec((1, D_MODEL), lambda i: (0, 0))
    return pl.pallas_call(
        _modin_kernel,
        out_shape=_SDS((N_TOK, D_MODEL), BF16),
        grid=(N_TOK // ROW_TILE,),
        in_specs=_split_specs(D_MODEL) + [vec, _mod_spec(layer, k_shift), _mod_spec(layer, k_scale)],
        out_specs=row,
        compiler_params=_cparams(("parallel",)),
        name="modulate_in",
    )(x, ctx, g.reshape(1, D_MODEL), mod, mod)


def _route_top2(logits):
    lane = lax.broadcasted_iota(I32, logits.shape, 1)
    neg = -jnp.inf
    l0 = jnp.where(lane < N_EXPERTS, logits, neg)
    m1 = jnp.max(l0, axis=-1, keepdims=True)
    i1 = jnp.min(jnp.where(l0 == m1, lane, 128), axis=-1, keepdims=True)
    l1 = jnp.where(lane == i1, neg, l0)
    m2 = jnp.max(l1, axis=-1, keepdims=True)
    i2 = jnp.min(jnp.where(l1 == m2, lane, 128), axis=-1, keepdims=True)
    e = jnp.exp(m2 - m1)
    w1 = 1.0 / (1.0 + e)
    idx = jnp.where(lane == 0, i1, jnp.where(lane == 1, i2, 0))
    wts = jnp.where(lane == 0, w1, jnp.where(lane == 1, e * w1, 0.0))
    return idx, wts


def _post_kernel(*refs, split, route):
    refs = list(refs)
    if split:
        h = _split_read(refs.pop(0), refs.pop(0))
    else:
        h = refs.pop(0)[...]
    y_ref, gp_ref, gate_ref, gn_ref, sh_ref, sc_ref = refs[:6]
    refs = refs[6:]
    wr_ref = refs.pop(0) if route else None
    hn_ref, u_ref = refs[:2]
    grp = _group_of_row(pl.program_id(0) * ROW_TILE)
    hn = h + _mod_row(gate_ref, grp) * _rms(y_ref[...].astype(F32), gp_ref[...])
    hn_ref[...] = hn
    u = _rms(hn, gn_ref[...]) * (1.0 + _mod_row(sc_ref, grp)) + _mod_row(sh_ref, grp)
    u_ref[...] = u.astype(u_ref.dtype)
    if route:
        idx_ref, wt_ref = refs[2:4]
        idx_ref[...], wt_ref[...] = _route_top2(_dot(u.astype(BF16), wr_ref[...]))


def _post(h, y, g_post, g_next, mod, gate_lk, shift_lk, scale_lk, u_dtype, w_router=None):
    n = y.shape[0]
    split = isinstance(h, tuple)
    route = w_router is not None
    row = pl.BlockSpec((ROW_TILE, D_MODEL), lambda i: (i, 0))
    vec = pl.BlockSpec((1, D_MODEL), lambda i: (0, 0))
    slab = pl.BlockSpec((ROW_TILE, 128), lambda i: (i, 0))
    h_specs, h_args = (_split_specs(D_MODEL), list(h)) if split else ([row], [h])
    extra_specs, extra_args, extra_out, extra_out_specs = [], [], (), ()
    if route:
        w = jnp.zeros((D_MODEL, 128), BF16).at[:, :N_EXPERTS].set(w_router.astype(BF16))
        extra_specs, extra_args = [pl.BlockSpec((D_MODEL, 128), lambda i: (0, 0))], [w]
        extra_out, extra_out_specs = (_SDS((n, 128), I32), _SDS((n, 128), F32)), (slab, slab)
    return pl.pallas_call(
        functools.partial(_post_kernel, split=split, route=route),
        out_shape=(_SDS((n, D_MODEL), F32), _SDS((n, D_MODEL), u_dtype)) + extra_out,
        grid=(n // ROW_TILE,),
        in_specs=h_specs + [row, vec, _mod_spec(*gate_lk), vec, _mod_spec(*shift_lk), _mod_spec(*scale_lk)] + extra_specs,
        out_specs=(row, row) + extra_out_specs,
        compiler_params=_cparams(("parallel",)),
        name="residual_post",
    )(*h_args, y, g_post.reshape(1, D_MODEL), mod, g_next.reshape(1, D_MODEL), mod, mod, *extra_args)


def _mm_kernel(a_ref, w_ref, b_ref, o_ref, *, w_is_transposed):
    dot = _dot_nt if w_is_transposed else _dot
    o_ref[...] = (dot(a_ref[...], w_ref[...].astype(BF16)) + b_ref[...]).astype(o_ref.dtype)


def _matmul(a, w, bias, out_dtype, tm, tn, n=None, w_is_transposed=False):
    m, k = a.shape
    if w_is_transposed:
        n = w.shape[-2] if n is None else n
        if w.ndim == 3:
            w_spec = pl.BlockSpec((None, tn, k), lambda i, j: (0, j, 0))
        else:
            w_spec = pl.BlockSpec((tn, k), lambda i, j: (j, 0))
    else:
        n = w.shape[-1] if n is None else n
        w_spec = pl.BlockSpec((k, tn), lambda i, j: (0, j))
    if bias is None:
        bias = jnp.zeros((n,), F32)
    return pl.pallas_call(
        functools.partial(_mm_kernel, w_is_transposed=w_is_transposed),
        out_shape=_SDS((m, n), out_dtype),
        grid=(m // tm, n // tn),
        in_specs=[pl.BlockSpec((tm, k), lambda i, j: (i, 0)),
                  w_spec,
                  pl.BlockSpec((1, tn), lambda i, j: (0, j))],
        out_specs=pl.BlockSpec((tm, tn), lambda i, j: (i, j)),
        compiler_params=_cparams(("parallel", "arbitrary"), VMEM_LIMIT),
        name="matmul",
    )(a, w, bias.reshape(1, n))


FF_TILE = 512
SWIGLU_VMEM_LIMIT = 60 * 1024 * 1024
OUT_TILE = 256
FF_STEPS = D_FF // FF_TILE
OUT_STEPS = D_MODEL // OUT_TILE


def _swiglu_step(s, x_ref, wg_ref, wu_ref, wd_ref, o_ref, h_ref, rows=None, guard=True):
    total = x_ref.shape[0]
    rows = total if rows is None else rows

    @pl.when(guard & (s < FF_STEPS))
    def _():
        x = x_ref[0:rows, :]
        a = _dot(x, wg_ref[...].astype(BF16))
        b = _dot(x, wu_ref[...].astype(BF16))
        h_ref[0:rows, pl.ds(pl.multiple_of(s * FF_TILE, FF_TILE), FF_TILE)] = (_silu(a) * b).astype(BF16)

    @pl.when(guard & (s >= FF_STEPS))
    def _():
        o_ref[0:rows, :] = _dot(h_ref[0:rows, :], wd_ref[...].astype(BF16)).astype(o_ref.dtype)
        if rows < total:
            o_ref[rows:total, :] = jnp.zeros((total - rows, o_ref.shape[1]), o_ref.dtype)


def _ffn_kernel(u_ref, wg_ref, wu_ref, wd_ref, o_ref, h_ref):
    _swiglu_step(pl.program_id(1), u_ref, wg_ref, wu_ref, wd_ref, o_ref, h_ref)


def _ffn(u, w_gate, w_up, w_down):
    m = u.shape[0]
    tm = MM_ROW_TILE

    def ff(i, s):
        return (0, jnp.minimum(s, FF_STEPS - 1))

    def out(s):
        return jnp.maximum(s - FF_STEPS, 0)

    return pl.pallas_call(
        _ffn_kernel,
        out_shape=_SDS((m, D_MODEL), BF16),
        grid=(m // tm, FF_STEPS + OUT_STEPS),
        in_specs=[pl.BlockSpec((tm, D_MODEL), lambda i, s: (i, 0), pipeline_mode=pl.Buffered(1)),
                  pl.BlockSpec((D_MODEL, FF_TILE), ff),
                  pl.BlockSpec((D_MODEL, FF_TILE), ff),
                  pl.BlockSpec((D_FF, OUT_TILE), lambda i, s: (0, out(s)))],
        out_specs=pl.BlockSpec((tm, OUT_TILE), lambda i, s: (i, out(s))),
        scratch_shapes=[pltpu.VMEM((tm, D_FF), BF16)],
        compiler_params=_cparams(("parallel", "arbitrary"), SWIGLU_VMEM_LIMIT),
        name="swiglu_ffn",
    )(u, w_gate, w_up, w_down)


CONV_TILE = 256
CONV_HALO = 16
_SEQ_STARTS = (0, SEQ // CONV_TILE, 2 * SEQ // CONV_TILE, 2 * SEQ // CONV_TILE + 1)
_SEQ_LASTS = (SEQ // CONV_TILE - 1, 2 * SEQ // CONV_TILE - 1, 2 * SEQ // CONV_TILE, 2 * SEQ // CONV_TILE + 1)


def _conv_kernel(prev_ref, cur_ref, next_ref, w_ref, o_ref, buf):
    i = pl.program_id(0)
    j = pl.program_id(1)
    tm = CONV_TILE
    first = functools.reduce(jnp.logical_or, [i == s for s in _SEQ_STARTS])
    last = functools.reduce(jnp.logical_or, [i == s for s in _SEQ_LASTS])
    hb = CONV_HALO
    buf[0:hb, :] = jnp.where(first, 0.0, prev_ref[...].astype(F32))
    buf[hb:hb + tm, :] = cur_ref[...].astype(F32)
    buf[hb + tm:2 * hb + tm, :] = jnp.where(last, 0.0, next_ref[...].astype(F32))
    pad = DN_CONV // 2
    acc = buf[hb - pad:hb - pad + tm, :] * w_ref[0:1, :]
    for d in range(1, DN_CONV):
        acc = acc + buf[hb - pad + d:hb - pad + d + tm, :] * w_ref[d:d + 1, :]
    x = _silu(acc)
    q_scale = jnp.where(j == 0, DN_HEAD_DIM ** -0.5, 1.0)
    for hh in range(DN_HEADS):
        xh = x[:, hh * DN_HEAD_DIM:(hh + 1) * DN_HEAD_DIM]
        inv = lax.rsqrt(jnp.sum(xh * xh, axis=-1, keepdims=True) + EPS) * q_scale
        o_ref[:, hh * DN_HEAD_DIM:(hh + 1) * DN_HEAD_DIM] = xh * jnp.where(j == 2, 1.0, inv)


def _dn_conv(p, conv_w):
    n = p.shape[0]
    tm = CONV_TILE
    hb = CONV_HALO
    per_tile = tm // hb
    n_halo = n // hb
    return pl.pallas_call(
        _conv_kernel,
        out_shape=_SDS((n, DN_QKV), F32),
        grid=(n // tm, 3),
        in_specs=[pl.BlockSpec((hb, DN_W), lambda i, j: (jnp.maximum(i * per_tile - 1, 0), j)),
                  pl.BlockSpec((tm, DN_W), lambda i, j: (i, j)),
                  pl.BlockSpec((hb, DN_W), lambda i, j: (jnp.minimum((i + 1) * per_tile, n_halo - 1), j)),
                  pl.BlockSpec((DN_CONV, DN_W), lambda i, j: (0, j))],
        out_specs=pl.BlockSpec((tm, DN_W), lambda i, j: (i, j)),
        scratch_shapes=[pltpu.VMEM((tm + 2 * hb, DN_W), F32)],
        compiler_params=_cparams(("parallel", "parallel")),
        name="dn_conv",
    )(p, p, p, conv_w)


def _gates_kernel(m_ref, a_ref, dt_ref, o_ref):
    x = pltpu.roll(m_ref[...], 64, 1)
    lane = lax.broadcasted_iota(I32, x.shape, 1)
    z = x + dt_ref[...]
    softplus = jnp.maximum(z, 0.0) + jnp.log1p(jnp.exp(-jnp.abs(z)))
    g = a_ref[...] * softplus
    o_ref[...] = jnp.where(lane < 2 * DN_HEADS, jax.nn.sigmoid(x), jnp.where(lane < 4 * DN_HEADS, g, 0.0))


def _dn_gates(p, a_log, dt_bias):
    n = p.shape[0]
    neg_a = jnp.zeros((1, 128), F32).at[0, 2 * DN_HEADS:4 * DN_HEADS].set(-jnp.exp(a_log.astype(F32)).reshape(-1))
    dtb = jnp.zeros((1, 128), F32).at[0, 2 * DN_HEADS:4 * DN_HEADS].set(dt_bias.astype(F32).reshape(-1))
    tm = ROW_TILE
    vec = pl.BlockSpec((1, 128), lambda i: (0, 0))
    return pl.pallas_call(
        _gates_kernel,
        out_shape=_SDS((n, 128), F32),
        grid=(n // tm,),
        in_specs=[pl.BlockSpec((tm, 128), lambda i: (i, R_MISC // 128)), vec, vec],
        out_specs=pl.BlockSpec((tm, 128), lambda i: (i, 0)),
        compiler_params=_cparams(("parallel",)),
        name="dn_gates",
    )(p, neg_a, dtb)


_CTX_CHUNKS = CTX_LEN // DN_CHUNK
_LAT_CHUNKS = SEQ // DN_CHUNK
_DN_STEPS = _CTX_CHUNKS + _LAT_CHUNKS


def _dn_fwd_block(b, t):
    return jnp.where(t < _CTX_CHUNKS, (2 * SEQ + b * CTX_LEN) // DN_CHUNK + t, b * _LAT_CHUNKS + (t - _CTX_CHUNKS))


def _dn_bwd_block(b, t):
    return jnp.where(t < _CTX_CHUNKS, (2 * SEQ + b * CTX_LEN) // DN_CHUNK + (_CTX_CHUNKS - 1 - t),
                     b * _LAT_CHUNKS + (_DN_STEPS - 1 - t))


DN_GROUP = 2
_GROUP_ROWS = DN_GROUP * DN_CHUNK
_GROUP_STATE = DN_GROUP * DN_HEAD_DIM


_DN_BASE = 8


def _block_diag(x):
    zero = jnp.zeros((DN_CHUNK, DN_HEAD_DIM), x.dtype)
    rows = []
    for hh in range(DN_GROUP):
        piece = x[hh * DN_CHUNK:(hh + 1) * DN_CHUNK]
        rows.append(jnp.concatenate([piece if j == hh else zero for j in range(DN_GROUP)], axis=1))
    return jnp.concatenate(rows, axis=0)


def _dn_groups(items, eye, base_mask):
    it = items
    n = range(len(it))
    gx = [jnp.broadcast_to(it[g]["gc"], (_GROUP_ROWS, _GROUP_ROWS)) for g in n]
    gamma = [jnp.where(it[g]["incl"], jnp.exp(jnp.where(it[g]["incl"], gx[g] - gx[g].T, 0.0)), 0.0) for g in n]
    kb = [it[g]["k"].astype(BF16) for g in n]
    kk = [_dot_nt(kb[g], kb[g]) for g in n]
    a = [jnp.where(it[g]["strict"], it[g]["beta"] * kk[g] * gamma[g], 0.0) for g in n]
    d = [jnp.where(base_mask, a[g], 0.0) for g in n]
    tinv = [eye - d[g] for g in n]
    pw = [d[g].astype(BF16) for g in n]
    for _ in range(int(math.log2(_DN_BASE)) - 1):
        pw = [_dot(pw[g], pw[g]).astype(BF16) for g in n]
        tinv = [tinv[g] + _dot(tinv[g].astype(BF16), pw[g]) for g in n]
    for lvl in range(len(it[0]["levels"])):
        tb = [tinv[g].astype(BF16) for g in n]
        mt = [_dot(jnp.where(it[g]["levels"][lvl], a[g], 0.0).astype(BF16), tb[g]).astype(BF16) for g in n]
        tinv = [tinv[g] - _dot(tb[g], mt[g]) for g in n]
    tb = [tinv[g].astype(BF16) for g in n]
    egc = [jnp.exp(it[g]["gc"]) for g in n]
    rhs = [jnp.concatenate([it[g]["v"] * it[g]["beta"], it[g]["k"] * (it[g]["beta"] * egc[g])], axis=1) for g in n]
    rhs_hi = [rhs[g].astype(BF16) for g in n]
    rhs_lo = [(rhs[g] - rhs_hi[g].astype(F32)).astype(BF16) for g in n]
    sol = [_dot(jnp.concatenate([tb[g], tb[g]], axis=1), jnp.concatenate([rhs_hi[g], rhs_lo[g]], axis=0)) for g in n]
    qk = [(_dot_nt(it[g]["q"].astype(BF16), kb[g]) * gamma[g]).astype(BF16) for g in n]
    s = [it[g]["s_ref"][...] for g in n]
    sb = [s[g].astype(BF16) for g in n]
    v_new = [sol[g][:, :DN_HEAD_DIM] - _dot(_block_diag(sol[g][:, DN_HEAD_DIM:].astype(BF16)), sb[g]) for g in n]
    vb = [v_new[g].astype(BF16) for g in n]
    o = [_dot(_block_diag((it[g]["q"] * egc[g]).astype(BF16)), sb[g]) + _dot(qk[g], vb[g]) for g in n]
    for g in n:
        k_dec = _block_diag((it[g]["k"] * jnp.exp(it[g]["g_last"] - it[g]["gc"])).astype(BF16))
        it[g]["s_ref"][...] = s[g] * jnp.exp(it[g]["g_last_state"]) + _dot_tn(k_dec, vb[g])
    return o


def _stack_cols(x, lanes, rows_each):
    pieces = [x[:, l:l + 1] for l in lanes]
    if rows_each is not None:
        pieces = [jnp.broadcast_to(pc, (rows_each, 1)) for pc in pieces]
    return jnp.concatenate(pieces, axis=0)


def _dn_kernel(*refs):
    n_in = BATCH * 2 * 4
    in_refs, (ofl_ref, ofc_ref, obl_ref, obc_ref), (s_ref, o_scr) = refs[:n_in], refs[n_in:n_in + 4], refs[n_in + 4:]
    t = pl.program_id(0)

    @pl.when(t == 0)
    def _():
        s_ref[...] = jnp.zeros_like(s_ref)

    c = DN_CHUNK
    ri = lax.broadcasted_iota(I32, (_GROUP_ROWS, _GROUP_ROWS), 0)
    ci = lax.broadcasted_iota(I32, (_GROUP_ROWS, _GROUP_ROWS), 1)
    same_head = (ri // c) == (ci // c)
    eye = (ri == ci).astype(F32)
    r1 = lax.broadcasted_iota(I32, (c, c), 0)
    c1 = lax.broadcasted_iota(I32, (c, c), 1)
    base_mask = (ri // _DN_BASE) == (ci // _DN_BASE)
    masks = []
    for d in range(2):
        incl = same_head & ((ri >= ci) if d == 0 else (ri <= ci))
        strict = same_head & ((ri > ci) if d == 0 else (ri < ci))
        levels = []
        size = _DN_BASE
        while size < c:
            lo_blk, hi_blk = ((ci, ri) if d == 0 else (ri, ci))
            levels.append(((ri // (2 * size)) == (ci // (2 * size)))
                          & ((lo_blk // size) % 2 == 0) & ((hi_blk // size) % 2 == 1))
            size *= 2
        masks.append((incl, strict, levels))
    items, where = [], []
    for b, d in [(b, d) for b in range(BATCH) for d in range(2)]:
        q_ref, k_ref, v_ref, g_ref = in_refs[(b * 2 + d) * 4:(b * 2 + d) * 4 + 4]
        incl, strict, levels = masks[d]
        gates = g_ref[...]
        tri = ((r1 >= c1) if d == 0 else (r1 <= c1)).astype(BF16)
        g_hi = gates.astype(BF16)
        g_r1 = gates - g_hi.astype(F32)
        g_mid = g_r1.astype(BF16)
        g_lo = (g_r1 - g_mid.astype(F32)).astype(BF16)
        gc = _dot(tri, g_hi) + _dot(tri, g_mid) + _dot(tri, g_lo)
        last_row = c - 1 if d == 0 else 0
        tot = gc[last_row:last_row + 1, :]
        for grp in range(DN_HEADS // DN_GROUP):
            heads = range(grp * DN_GROUP, (grp + 1) * DN_GROUP)
            lanes_b = [d * DN_HEADS + hh for hh in heads]
            lanes_g = [2 * DN_HEADS + lb for lb in lanes_b]

            def stack(ref):
                return jnp.concatenate([ref[:, hh * DN_HEAD_DIM:(hh + 1) * DN_HEAD_DIM] for hh in heads], axis=0)

            items.append(dict(q=stack(q_ref), k=stack(k_ref), v=stack(v_ref),
                              beta=_stack_cols(gates, lanes_b, None), gc=_stack_cols(gc, lanes_g, None),
                              g_last=_stack_cols(tot, lanes_g, c), g_last_state=_stack_cols(tot, lanes_g, DN_HEAD_DIM),
                              incl=incl, strict=strict, levels=levels, s_ref=s_ref.at[b, d, grp]))
            where.append((d, b, heads))
    outs = _dn_groups(items, eye, base_mask)
    for o, (d, b, heads) in zip(outs, where):
        for j, hh in enumerate(heads):
            o_scr[d, b, :, hh * DN_HEAD_DIM:(hh + 1) * DN_HEAD_DIM] = o[j * c:(j + 1) * c]

    @pl.when(t < _CTX_CHUNKS)
    def _():
        ofc_ref[...] = o_scr[0]
        obc_ref[...] = o_scr[1]

    @pl.when(t >= _CTX_CHUNKS)
    def _():
        ofl_ref[...] = o_scr[0]
        obl_ref[...] = o_scr[1]


def _deltanet(qkv, gates):
    c = DN_CHUNK
    ins, args = [], []
    for b in range(BATCH):
        for fn in (_dn_fwd_block, _dn_bwd_block):
            for width, col, arr in ((DN_W, 0, qkv), (DN_W, 1, qkv), (DN_W, 2, qkv), (128, 0, gates)):
                ins.append(pl.BlockSpec((c, width), functools.partial(lambda t, b, fn, col: (fn(b, t), col),
                                                                      b=b, fn=fn, col=col)))
                args.append(arr)
    last = _DN_STEPS - 1
    lat = _SDS((BATCH, SEQ, DN_W), F32)
    ctx = _SDS((BATCH, CTX_LEN, DN_W), F32)
    blk = (BATCH, c, DN_W)
    outs = pl.pallas_call(
        _dn_kernel,
        out_shape=(lat, ctx, lat, ctx),
        grid=(_DN_STEPS,),
        in_specs=ins,
        out_specs=(pl.BlockSpec(blk, lambda t: (0, jnp.maximum(t - _CTX_CHUNKS, 0), 0)),
                   pl.BlockSpec(blk, lambda t: (0, jnp.minimum(t, _CTX_CHUNKS - 1), 0)),
                   pl.BlockSpec(blk, lambda t: (0, jnp.minimum(last - t, _LAT_CHUNKS - 1), 0)),
                   pl.BlockSpec(blk, lambda t: (0, jnp.maximum(_CTX_CHUNKS - 1 - t, 0), 0))),
        scratch_shapes=[pltpu.VMEM((BATCH, 2, DN_HEADS // DN_GROUP, _GROUP_STATE, DN_HEAD_DIM), F32),
                        pltpu.VMEM((2, BATCH, c, DN_W), F32)],
        compiler_params=_cparams(("arbitrary",)),
        name="deltanet",
    )(*args)
    ofl, ofc, obl, obc = outs
    return ((ofl.reshape(N_LAT, DN_W), ofc.reshape(N_CTX, DN_W)),
            (obl.reshape(N_LAT, DN_W), obc.reshape(N_CTX, DN_W)))


def _rope_tables():
    t = jnp.arange(SEQ)
    row = (t // GRID_W).astype(F32)
    col = (t % GRID_W).astype(F32)
    n_freq = MLA_ROPE // 4
    inv_freq = ROPE_THETA ** (-jnp.arange(n_freq, dtype=F32) / n_freq)
    ang = jnp.concatenate([row[:, None] * inv_freq, col[:, None] * inv_freq], axis=-1)
    cos, sin = jnp.cos(ang), jnp.sin(ang)
    cos64 = jnp.concatenate([cos, cos], axis=-1)
    sin64 = jnp.concatenate([-sin, sin], axis=-1)

    def assemble(lat_cos, lat_sin):
        c = jnp.concatenate([jnp.tile(lat_cos, (BATCH, 1)), jnp.ones((N_CTX, 128), F32)], axis=0)
        s = jnp.concatenate([jnp.tile(lat_sin, (BATCH, 1)), jnp.zeros((N_CTX, 128), F32)], axis=0)
        return c, s

    cos_l = jnp.concatenate([cos64, jnp.ones((SEQ, 64), F32)], axis=-1)
    sin_l = jnp.concatenate([sin64, jnp.zeros((SEQ, 64), F32)], axis=-1)
    one_head = assemble(cos_l, sin_l)
    two_heads = assemble(jnp.tile(cos64, (1, 2)), jnp.tile(sin64, (1, 2)))
    return one_head, two_heads


def _rope128(x, cosv, sinv):
    lane = lax.broadcasted_iota(I32, x.shape, 1)
    swapped = jnp.where((lane % 64) < 32, pltpu.roll(x, 96, 1), pltpu.roll(x, 32, 1))
    return x * cosv + swapped * sinv


MLA_QK = 256
MLA_VX = 256


def _mla_proj_kernel(ckv_ref, cq_ref, misc_ref, gq_ref, gkv_ref, wq_ref, wkv_ref, cos_ref, sin_ref,
                     q_ref, k_ref, v_ref):
    cq = _rms(cq_ref[...].astype(F32), gq_ref[...]).astype(BF16)
    ckv = _rms(ckv_ref[...].astype(F32), gkv_ref[...]).astype(BF16)
    qf = _dot(cq, wq_ref[...])
    kvf = _dot(ckv, wkv_ref[...])
    cosv, sinv = cos_ref[...], sin_ref[...]
    lane = lax.broadcasted_iota(I32, cosv.shape, 1)
    k_rope = _rope128(jnp.where(lane < MLA_ROPE, misc_ref[...], 0.0), cosv, sinv).astype(BF16)
    scale = (MLA_NOPE + MLA_ROPE) ** -0.5
    for hh in range(MLA_HEADS):
        lo, mid, hi = hh * MLA_QK, hh * MLA_QK + 128, (hh + 1) * MLA_QK
        q_ref[:, lo:mid] = (qf[:, lo:mid] * scale).astype(BF16)
        q_ref[:, mid:hi] = (_rope128(qf[:, mid:hi], cosv, sinv) * scale).astype(BF16)
        k_ref[:, lo:mid] = kvf[:, lo:mid].astype(BF16)
        k_ref[:, mid:hi] = k_rope
        v_ref[:, lo:mid] = kvf[:, mid:hi].astype(BF16)
        v_ref[:, mid:hi] = jnp.ones((kvf.shape[0], MLA_VX - MLA_V), BF16)


def _mla_proj(p_rest, q_norm_g, kv_norm_g, w_q, w_kv, cosv, sinv):
    n = p_rest.shape[0]
    tm = ROW_TILE
    wide = MLA_HEADS * MLA_QK

    def rows(width, col):
        return pl.BlockSpec((tm, width), lambda i: (i, col))

    def whole(shape):
        return pl.BlockSpec(shape, lambda i: (0, 0))

    return pl.pallas_call(
        _mla_proj_kernel,
        out_shape=(_SDS((n, wide), BF16), _SDS((n, wide), BF16), _SDS((n, MLA_HEADS * MLA_VX), BF16)),
        grid=(n // tm,),
        in_specs=[rows(MLA_KV_RANK, R_CKV // MLA_KV_RANK), rows(MLA_Q_RANK, R_CQ // MLA_Q_RANK),
                  rows(128, R_MISC // 128), whole((1, MLA_Q_RANK)), whole((1, MLA_KV_RANK)),
                  whole((MLA_Q_RANK, wide)), whole((MLA_KV_RANK, wide)), rows(128, 0), rows(128, 0)],
        out_specs=(rows(wide, 0), rows(wide, 0), rows(MLA_HEADS * MLA_VX, 0)),
        compiler_params=_cparams(("parallel",), VMEM_LIMIT),
        name="mla_proj",
    )(p_rest, p_rest, p_rest, q_norm_g.reshape(1, -1), kv_norm_g.reshape(1, -1), w_q, w_kv, cosv, sinv)


MLA_KEY_CHUNK = 2048


def _flash_kernel(*refs, chunks):
    n_seg = (len(refs) - 2) // 2
    q_ref, o_ref = refs[0], refs[-1]
    k_refs, v_refs = refs[1:1 + n_seg], refs[1 + n_seg:1 + 2 * n_seg]
    q = q_ref[...]

    def scores(c):
        seg, off, size = c
        return _dot_nt(q, k_refs[seg][off:off + size, :])

    m = jnp.full((q.shape[0], 1), -jnp.inf, F32)
    acc = jnp.zeros((q.shape[0], MLA_VX), F32)
    s_next = scores(chunks[0])
    for j, (seg, off, size) in enumerate(chunks):
        s = s_next
        if j + 1 < len(chunks):
            s_next = scores(chunks[j + 1])
        m_new = jnp.maximum(m, jnp.max(s, axis=-1, keepdims=True))
        p = jnp.exp(s - m_new).astype(BF16)
        acc = jnp.exp(m - m_new) * acc + _dot(p, v_refs[seg][off:off + size, :])
        m = m_new
    o_ref[...] = (acc[:, :MLA_V] / acc[:, MLA_V:]).astype(o_ref.dtype)


def _mla_attention(q, k, v):
    tq = 1024
    nq = SEQ // tq
    ctx_blk = 2 * SEQ // CTX_LEN
    lat_chunks = ((0, 0, CTX_LEN),) + tuple((1, off, MLA_KEY_CHUNK) for off in range(0, SEQ, MLA_KEY_CHUNK))
    lat = pl.pallas_call(
        functools.partial(_flash_kernel, chunks=lat_chunks),
        out_shape=_SDS((N_LAT, MLA_HEADS * MLA_V), BF16),
        grid=(BATCH, MLA_HEADS, nq),
        in_specs=[pl.BlockSpec((tq, MLA_QK), lambda b, h, i: (b * nq + i, h)),
                  pl.BlockSpec((CTX_LEN, MLA_QK), lambda b, h, i: (ctx_blk + b, h)),
                  pl.BlockSpec((SEQ, MLA_QK), lambda b, h, i: (b, h)),
                  pl.BlockSpec((CTX_LEN, MLA_VX), lambda b, h, i: (ctx_blk + b, h)),
                  pl.BlockSpec((SEQ, MLA_VX), lambda b, h, i: (b, h))],
        out_specs=pl.BlockSpec((tq, MLA_V), lambda b, h, i: (b * nq + i, h)),
        compiler_params=_cparams(("parallel", "parallel", "arbitrary"), VMEM_LIMIT),
        name="mla_attn_latent",
    )(q, k, k, v, v)
    ctx = pl.pallas_call(
        functools.partial(_flash_kernel, chunks=((0, 0, CTX_LEN),)),
        out_shape=_SDS((N_CTX, MLA_HEADS * MLA_V), BF16),
        grid=(BATCH, MLA_HEADS),
        in_specs=[pl.BlockSpec((CTX_LEN, MLA_QK), lambda b, h: (ctx_blk + b, h)),
                  pl.BlockSpec((CTX_LEN, MLA_QK), lambda b, h: (ctx_blk + b, h)),
                  pl.BlockSpec((CTX_LEN, MLA_VX), lambda b, h: (ctx_blk + b, h))],
        out_specs=pl.BlockSpec((CTX_LEN, MLA_V), lambda b, h: (b, h)),
        compiler_params=_cparams(("parallel", "parallel")),
        name="mla_attn_context",
    )(q, k, v)
    return lat, ctx


def _ab_mix_kernel(ofl_ref, ofc_ref, obl_ref, obc_ref, ml_ref, mc_ref, z_ref, g_ref, a_ref):
    for hh in range(DN_HEADS):
        sl = slice(hh * DN_HEAD_DIM, (hh + 1) * DN_HEAD_DIM)
        o = _split_read(ofl_ref, ofc_ref, sl) + _split_read(obl_ref, obc_ref, sl)
        a_ref[:, sl] = (_rms(o, g_ref[...]) * _silu(z_ref[:, sl].astype(F32))).astype(BF16)
    a_ref[:, DN_W:] = _split_read(ml_ref, mc_ref)


def _ab_mix(o_f, o_b, p, dn_norm_g, mla_o):
    tm = ROW_TILE
    half = DN_W
    return pl.pallas_call(
        _ab_mix_kernel,
        out_shape=_SDS((N_TOK, 2 * half), BF16),
        grid=(N_TOK // tm,),
        in_specs=_split_specs(half) + _split_specs(half) + _split_specs(half) + [
            pl.BlockSpec((tm, half), lambda i: (i, P_Z // half)),
            pl.BlockSpec((1, DN_HEAD_DIM), lambda i: (0, 0))],
        out_specs=pl.BlockSpec((tm, 2 * half), lambda i: (i, 0)),
        compiler_params=_cparams(("parallel",)),
        name="ab_mix",
    )(*o_f, *o_b, *mla_o, p, dn_norm_g.reshape(1, -1))


def _mm_rope_kernel(a_ref, w_ref, b_ref, cos_ref, sin_ref, o_ref, *, tn, rope_cols):
    y = _dot(a_ref[...], w_ref[...].astype(BF16)) + b_ref[...]
    col0 = pl.program_id(1) * tn
    cosv, sinv = cos_ref[...], sin_ref[...]
    for s in range(tn // 128):
        ys = y[:, s * 128:(s + 1) * 128]
        roped = _rope128(ys, cosv, sinv)
        o_ref[:, s * 128:(s + 1) * 128] = jnp.where(col0 + s * 128 < rope_cols, roped, ys).astype(o_ref.dtype)


def _gqa_qkv(u, w, bias, cosv, sinv):
    m, k = u.shape
    n = w.shape[1]
    tm, tn = MM_ROW_TILE, 1280
    return pl.pallas_call(
        functools.partial(_mm_rope_kernel, tn=tn, rope_cols=GQA_Q + GQA_KV),
        out_shape=_SDS((m, n), BF16),
        grid=(m // tm, n // tn),
        in_specs=[pl.BlockSpec((tm, k), lambda i, j: (i, 0)),
                  pl.BlockSpec((k, tn), lambda i, j: (0, j)),
                  pl.BlockSpec((1, tn), lambda i, j: (0, j)),
                  pl.BlockSpec((tm, 128), lambda i, j: (i, 0)),
                  pl.BlockSpec((tm, 128), lambda i, j: (i, 0))],
        out_specs=pl.BlockSpec((tm, tn), lambda i, j: (i, j)),
        compiler_params=_cparams(("parallel", "arbitrary"), VMEM_LIMIT),
        name="gqa_qkv_rope",
    )(u, w, bias.reshape(1, n), cosv, sinv)


def _gqa_kernel(sink_ref, q_ref, kp_ref, kc_ref, kn_ref, vp_ref, vc_ref, vn_ref, kx_ref, vx_ref, o_ref):
    i = pl.program_id(1)
    nb = SEQ // Q_BLOCK
    n_keys = CTX_LEN + 3 * Q_BLOCK
    groups = GQA_HEADS // GQA_KV_HEADS
    slabs = groups // 2
    dh = GQA_HEAD_DIM
    qi = lax.broadcasted_iota(I32, (Q_BLOCK, n_keys), 0)
    kj = lax.broadcasted_iota(I32, (Q_BLOCK, n_keys), 1) - CTX_LEN
    rel = kj - Q_BLOCK - qi
    valid = (kj < 0) | ((jnp.abs(rel) <= WINDOW) & ((kj >= Q_BLOCK) | (i > 0)) & ((kj < 2 * Q_BLOCK) | (i < nb - 1)))
    bias = jnp.where(valid, 0.0, -jnp.inf).astype(F32)
    bias = jnp.concatenate([bias] * slabs, axis=0)
    k_all = jnp.concatenate([kx_ref[...], kp_ref[...], kc_ref[...], kn_ref[...]], axis=0)
    v_all = jnp.concatenate([vx_ref[...], vp_ref[...], vc_ref[...], vn_ref[...]], axis=0)
    low = lax.broadcasted_iota(I32, (n_keys, 128), 1) < dh

    def block_diag(x_all, kvh):
        pair = x_all[:, (kvh // 2) * 128:(kvh // 2 + 1) * 128].astype(F32)
        other = pltpu.roll(pair, dh, 1)
        first, second = (pair, other) if kvh % 2 == 0 else (other, pair)
        return jnp.concatenate([jnp.where(low, first, 0.0), jnp.where(low, 0.0, second)], axis=0).astype(BF16)

    def scores(kvh):
        q = jnp.concatenate([q_ref[:, (kvh * slabs + r) * 128:(kvh * slabs + r + 1) * 128] for r in range(slabs)], axis=0)
        return _dot_nt(q * (dh ** -0.5), block_diag(k_all, kvh))

    row_lo = lax.broadcasted_iota(I32, (2 * n_keys, 128), 0) < n_keys
    lane_lo = lax.broadcasted_iota(I32, (2 * n_keys, 128), 1) < dh
    ones_cols = jnp.where(row_lo == lane_lo, 1.0, 0.0).astype(BF16)

    def softmax(kvh, s):
        ps, sink_terms = [], []
        for half in range(2):
            sh = s[:, half * n_keys:(half + 1) * n_keys] + bias
            sink = jnp.concatenate([jnp.full((Q_BLOCK, 1), sink_ref[(kvh * slabs + r) * 2 + half], F32)
                                    for r in range(slabs)], axis=0)
            m = jnp.maximum(jnp.max(sh, axis=-1, keepdims=True), sink)
            ps.append(jnp.exp(sh - m).astype(BF16))
            sink_terms.append(jnp.exp(sink - m))
        return jnp.concatenate(ps, axis=1), sink_terms

    def finish(kvh, p, sink_terms):
        o = _dot(p, jnp.concatenate([block_diag(v_all, kvh), ones_cols], axis=1))
        num, den = o[:, :128], o[:, 128:]
        lane_low = lax.broadcasted_iota(I32, num.shape, 1) < dh
        o = num / (den + jnp.where(lane_low, sink_terms[0], sink_terms[1]))
        for r in range(slabs):
            o_ref[:, (kvh * slabs + r) * 128:(kvh * slabs + r + 1) * 128] = o[r * Q_BLOCK:(r + 1) * Q_BLOCK].astype(o_ref.dtype)

    s_next = scores(0)
    for kvh in range(GQA_KV_HEADS):
        s_cur = s_next
        if kvh + 1 < GQA_KV_HEADS:
            s_next = scores(kvh + 1)
        p, dens = softmax(kvh, s_cur)
        finish(kvh, p, dens)


def _gqa_attention(qkv, sink):
    nb = SEQ // Q_BLOCK
    kcol, vcol = GQA_Q // GQA_KV, GQA_Q // GQA_KV + 1
    ctx_blk = 2 * SEQ // CTX_LEN

    def win(col, shift):
        return pl.BlockSpec((Q_BLOCK, GQA_KV), lambda b, i: (b * nb + jnp.clip(i + shift, 0, nb - 1), col))

    return pl.pallas_call(
        _gqa_kernel,
        out_shape=_SDS((N_LAT, GQA_Q), BF16),
        grid=(BATCH, nb),
        in_specs=[pl.BlockSpec(memory_space=pltpu.SMEM),
                  pl.BlockSpec((Q_BLOCK, GQA_Q), lambda b, i: (b * nb + i, 0)),
                  win(kcol, -1), win(kcol, 0), win(kcol, 1), win(vcol, -1), win(vcol, 0), win(vcol, 1),
                  pl.BlockSpec((CTX_LEN, GQA_KV), lambda b, i: (ctx_blk + b, kcol)),
                  pl.BlockSpec((CTX_LEN, GQA_KV), lambda b, i: (ctx_blk + b, vcol))],
        out_specs=pl.BlockSpec((Q_BLOCK, GQA_Q), lambda b, i: (b * nb + i, 0)),
        compiler_params=_cparams(("parallel", "arbitrary")),
        name="gqa_window_attn",
    )(sink.astype(F32), qkv, qkv, qkv, qkv, qkv, qkv, qkv, qkv, qkv)


GATHER_TILE = 256


def _row_copy(src_hbm, row, dst, r, sem):
    return pltpu.make_async_copy(src_hbm.at[pl.ds(row, 1), :], dst.at[pl.ds(r, 1), :], sem)


def _gather_kernel(tok_ref, tv_ref, u_hbm, valid_ref, o_ref, buf, sem):
    i = pl.program_id(0)
    n = pl.num_programs(0)

    def start_tile(t, slot):
        @pl.when(tv_ref[t] > 0)
        def _():
            def issue(r8, carry):
                for j in range(8):
                    r = r8 * 8 + j
                    _row_copy(u_hbm, tok_ref[t * GATHER_TILE + r], buf.at[slot], r, sem.at[slot]).start(priority=j % 2)
                return carry

            lax.fori_loop(0, GATHER_TILE // 8, issue, 0)

    @pl.when(i == 0)
    def _():
        start_tile(0, 0)

    @pl.when(i + 1 < n)
    def _():
        start_tile(i + 1, (i + 1) % 2)

    slot = i % 2

    @pl.when(tv_ref[i] > 0)
    def _():
        def drain(r, carry):
            _row_copy(u_hbm, 0, buf.at[slot], r, sem.at[slot]).wait()
            return carry

        lax.fori_loop(0, GATHER_TILE, drain, 0, unroll=8)
        o_ref[...] = jnp.where(valid_ref[...] > 0.0, buf[slot], 0.0).astype(o_ref.dtype)

    @pl.when(tv_ref[i] == 0)
    def _():
        o_ref[...] = jnp.zeros_like(o_ref)


def _moe_gather(slot_tok, slot_valid, u):
    n_slots = slot_tok.shape[0]
    n_tiles = n_slots // GATHER_TILE
    tile_valid = (jnp.max(slot_valid.reshape(n_tiles, GATHER_TILE), axis=1) > 0.0).astype(I32)
    return pl.pallas_call(
        _gather_kernel,
        out_shape=_SDS((n_slots, D_MODEL), BF16),
        grid_spec=pltpu.PrefetchScalarGridSpec(
            num_scalar_prefetch=2,
            grid=(n_tiles,),
            in_specs=[pl.BlockSpec(memory_space=pl.ANY),
                      pl.BlockSpec((GATHER_TILE, 1), lambda i, tok, tv: (i, 0))],
            out_specs=pl.BlockSpec((GATHER_TILE, D_MODEL), lambda i, tok, tv: (i, 0)),
            scratch_shapes=[pltpu.VMEM((2, GATHER_TILE, D_MODEL), F32), pltpu.SemaphoreType.DMA((2,))]),
        compiler_params=_cparams(("arbitrary",)),
        name="moe_gather",
    )(slot_tok, tile_valid, u, slot_valid.reshape(n_slots, 1))


def _moe_ffn_kernel(be_ref, bv_ref, x_ref, wg_ref, wu_ref, wd_ref, o_ref, h_ref):
    i, s = pl.program_id(0), pl.program_id(1)
    n_sub = bv_ref[i]
    for k in range(1, MOE_BLOCK // MOE_SUB + 1):
        _swiglu_step(s, x_ref, wg_ref.at[0], wu_ref.at[0], wd_ref.at[0], o_ref, h_ref,
                     rows=k * MOE_SUB, guard=n_sub == k)

    @pl.when((n_sub == 0) & (s >= FF_STEPS))
    def _():
        o_ref[...] = jnp.zeros_like(o_ref)


def _moe_ffn(block_expert, block_valid, xs, w_gate, w_up, w_down):
    n_slots = xs.shape[0]
    tm = MOE_BLOCK

    def ff(i, s, be, bv):
        return (be[i], 0, jnp.where(bv[i] > 0, jnp.minimum(s, FF_STEPS - 1), FF_STEPS - 1))

    def down(i, s, be, bv):
        return (be[i], 0, jnp.where(bv[i] > 0, jnp.maximum(s - FF_STEPS, 0), OUT_STEPS - 1))

    return pl.pallas_call(
        _moe_ffn_kernel,
        out_shape=_SDS((n_slots, D_MODEL), F32),
        grid_spec=pltpu.PrefetchScalarGridSpec(
            num_scalar_prefetch=2,
            grid=(n_slots // tm, FF_STEPS + OUT_STEPS),
            in_specs=[pl.BlockSpec((tm, D_MODEL), lambda i, s, be, bv: (i, 0), pipeline_mode=pl.Buffered(1)),
                      pl.BlockSpec((1, D_MODEL, FF_TILE), ff),
                      pl.BlockSpec((1, D_MODEL, FF_TILE), ff),
                      pl.BlockSpec((1, D_FF, OUT_TILE), down)],
            out_specs=pl.BlockSpec((tm, OUT_TILE), lambda i, s, be, bv: (i, jnp.maximum(s - FF_STEPS, 0))),
            scratch_shapes=[pltpu.VMEM((tm, D_FF), BF16)]),
        compiler_params=_cparams(("arbitrary", "arbitrary"), SWIGLU_VMEM_LIMIT),
        name="moe_grouped_swiglu",
    )(block_expert, block_valid, xs, w_gate, w_up, w_down)


def _combine_kernel(pos_ref, y_hbm, wt_ref, h_ref, gp_ref, gate_ref, o_ref, buf_a, buf_b, sem):
    i = pl.program_id(0)
    n = pl.num_programs(0)
    base = i * GATHER_TILE

    def start_tile(t, slot):
        def issue(r, carry):
            a = 2 * (t * GATHER_TILE + r)
            _row_copy(y_hbm, pos_ref[a], buf_a.at[slot], r, sem.at[slot]).start(priority=0)
            _row_copy(y_hbm, pos_ref[a + 1], buf_b.at[slot], r, sem.at[slot]).start(priority=1)
            return carry

        lax.fori_loop(0, GATHER_TILE, issue, 0, unroll=8)

    @pl.when(i == 0)
    def _():
        start_tile(0, 0)

    @pl.when(i + 1 < n)
    def _():
        start_tile(i + 1, (i + 1) % 2)

    slot = i % 2

    def drain(r, carry):
        _row_copy(y_hbm, 0, buf_a.at[slot], r, sem.at[slot]).wait()
        _row_copy(y_hbm, 0, buf_b.at[slot], r, sem.at[slot]).wait()
        return carry

    lax.fori_loop(0, GATHER_TILE, drain, 0, unroll=8)
    wt = wt_ref[...]
    y = wt[:, 0:1] * buf_a[slot] + wt[:, 1:2] * buf_b[slot]
    grp = _group_of_row(base)
    o_ref[...] = h_ref[...] + _mod_row(gate_ref, grp) * _rms(y, gp_ref[...])


def _moe_combine(pos, ys, wts, h, g_post, mod, gate_lk):
    n = wts.shape[0]
    tm = GATHER_TILE
    return pl.pallas_call(
        _combine_kernel,
        out_shape=_SDS((n, D_MODEL), F32),
        grid_spec=pltpu.PrefetchScalarGridSpec(
            num_scalar_prefetch=1,
            grid=(n // tm,),
            in_specs=[pl.BlockSpec(memory_space=pl.ANY),
                      pl.BlockSpec((tm, 128), lambda i, pos: (i, 0)),
                      pl.BlockSpec((tm, D_MODEL), lambda i, pos: (i, 0)),
                      pl.BlockSpec((1, D_MODEL), lambda i, pos: (0, 0)),
                      pl.BlockSpec((1, 8, D_MODEL), lambda i, pos: (gate_lk[0], 0, gate_lk[1]))],
            out_specs=pl.BlockSpec((tm, D_MODEL), lambda i, pos: (i, 0)),
            scratch_shapes=[pltpu.VMEM((2, tm, D_MODEL), F32), pltpu.VMEM((2, tm, D_MODEL), F32),
                            pltpu.SemaphoreType.DMA((2,))]),
        compiler_params=_cparams(("arbitrary",)),
        name="moe_combine",
    )(pos, ys, wts, h, g_post.reshape(1, D_MODEL), mod)


def _moe_routing(idx):
    flat_e = idx[:, :TOP_K].reshape(-1)
    n_assign = flat_e.shape[0]
    onehot = (flat_e[:, None] == jnp.arange(N_EXPERTS, dtype=I32)[None, :]).astype(I32)
    csum = jnp.cumsum(onehot, axis=0)
    counts = csum[-1]
    padded = (counts + MOE_BLOCK - 1) // MOE_BLOCK * MOE_BLOCK
    pad_end = jnp.cumsum(padded)
    pad_start = pad_end - padded
    pos = jnp.sum(onehot * (pad_start[None, :] + csum - 1), axis=1).astype(I32)
    slot_tok = jnp.zeros((MOE_SLOTS,), I32).at[pos].set(jnp.arange(n_assign, dtype=I32) // TOP_K)
    slot = jnp.arange(MOE_SLOTS, dtype=I32)
    used_end = pad_start + counts
    slot_valid = jnp.any((slot[:, None] >= pad_start[None, :]) & (slot[:, None] < used_end[None, :]), axis=1)
    blk_start = jnp.arange(MOE_SLOTS // MOE_BLOCK, dtype=I32) * MOE_BLOCK
    last_start = jnp.maximum(pad_end[-1] - MOE_BLOCK, 0)
    blk_e = jnp.searchsorted(pad_end, jnp.minimum(blk_start, last_start), side='right').astype(I32)
    blk_e = jnp.minimum(blk_e, N_EXPERTS - 1)
    rows_used = jnp.clip(used_end[blk_e] - blk_start, 0, MOE_BLOCK)
    blk_sub = ((rows_used + MOE_SUB - 1) // MOE_SUB).astype(I32)
    return pos, slot_tok, slot_valid.astype(F32), blk_e, blk_sub


def _rest_of_w_in_t(w_in_t):
    dn_in = P_MAIN + 4 * DN_HEADS
    ba = w_in_t[P_MAIN:dn_in]
    cq = w_in_t[dn_in:dn_in + MLA_Q_RANK]
    ckv = w_in_t[dn_in + MLA_Q_RANK:dn_in + MLA_Q_RANK + MLA_KV_RANK]
    k_rope = w_in_t[dn_in + MLA_Q_RANK + MLA_KV_RANK:]
    pad = jnp.zeros((R_CQ - R_MISC - MLA_ROPE - 4 * DN_HEADS, D_MODEL), w_in_t.dtype)
    return jnp.concatenate([ckv, k_rope, ba, pad, cq], axis=0)


def _rearrange_w_qb(w_qb):
    w = w_qb.reshape(MLA_Q_RANK, MLA_HEADS, MLA_NOPE + MLA_ROPE)
    w = jnp.pad(w, ((0, 0), (0, 0), (0, MLA_QK - MLA_NOPE - MLA_ROPE)))
    return w.reshape(MLA_Q_RANK, MLA_HEADS * MLA_QK).astype(BF16)


def kernel(x, c, ctx, c_ctx, ada_w, ada_b, norm_g, ab_w_in, dn_conv_w, dn_a_log, dn_dt_bias, dn_norm_g, mla_q_norm_g, mla_w_qb, mla_kv_norm_g, mla_w_kvb, ab_w_out, ffn_w_gate, ffn_w_up, ffn_w_down, gqa_w_qkv, gqa_b_qkv, gqa_sink, gqa_w_out, gqa_b_out, moe_w_router, moe_w_gate, moe_w_up, moe_w_down):
    assert x.shape == (BATCH, SEQ, D_MODEL) and ctx.shape == (BATCH, CTX_LEN, D_MODEL) and ada_w.shape[0] == 2
    h = (x.reshape(N_LAT, D_MODEL), ctx.reshape(N_CTX, D_MODEL))
    cvec = jnp.zeros((8, D_MODEL), F32).at[:BATCH].set(c).at[BATCH].set(c_ctx)
    mod = _ada(cvec, ada_w, ada_b)
    (cos1, sin1), (cos2, sin2) = _rope_tables()

    u = _modulate_in(*h, norm_g[0, 0], mod, 0, 0, 1)
    w_in_t = jnp.swapaxes(ab_w_in, 1, 2)
    p = _matmul(u, w_in_t, None, BF16, MM_ROW_TILE, 1024, n=P_MAIN, w_is_transposed=True)
    p_rest = _matmul(u, _rest_of_w_in_t(w_in_t[0]), None, F32, MM_ROW_TILE, R_WIDTH // 2, w_is_transposed=True)
    o_f, o_b = _deltanet(_dn_conv(p, dn_conv_w[0]), _dn_gates(p_rest, dn_a_log[0], dn_dt_bias[0]))
    q, k, v = _mla_proj(p_rest, mla_q_norm_g[0], mla_kv_norm_g[0], _rearrange_w_qb(mla_w_qb[0]),
                        mla_w_kvb[0].astype(BF16), cos1, sin1)
    a = _ab_mix(o_f, o_b, p, dn_norm_g[0], _mla_attention(q, k, v))
    y = _matmul(a, ab_w_out[0].astype(BF16), None, BF16, MM_ROW_TILE, 1024)
    h, u = _post(h, y, norm_g[0, 1], norm_g[0, 2], mod, (0, 2), (0, 3), (0, 4), BF16)
    y = _ffn(u, ffn_w_gate[0], ffn_w_up[0], ffn_w_down[0])
    h, u = _post(h, y, norm_g[0, 3], norm_g[1, 0], mod, (0, 5), (1, 0), (1, 1), BF16)

    qkv = _gqa_qkv(u, gqa_w_qkv[0].astype(BF16), gqa_b_qkv[0], cos2, sin2)
    o = _gqa_attention(qkv, gqa_sink[0])
    y = _matmul(o, gqa_w_out[0].astype(BF16), gqa_b_out[0], BF16, 1024, 1024)
    h, u, idx, wts = _post(h, y, norm_g[1, 1], norm_g[1, 2], mod, (1, 2), (1, 3), (1, 4), F32, moe_w_router[0])
    pos, slot_tok, slot_valid, blk_e, blk_sub = _moe_routing(idx)
    xs = _moe_gather(slot_tok, slot_valid, u)
    ys = _moe_ffn(blk_e, blk_sub, xs, moe_w_gate[0], moe_w_up[0], moe_w_down[0])
    out = _moe_combine(pos, ys, wts, h, norm_g[1, 3], mod, (1, 5))
    return out.reshape(BATCH, SEQ, D_MODEL)
```

```python
import functools
import math

import jax
import jax.numpy as jnp
from jax import lax
from jax.experimental import pallas as pl
from jax.experimental.pallas import tpu as pltpu

F32 = jnp.float32
BF16 = jnp.bfloat16
I32 = jnp.int32

D_MODEL = 2048
BATCH = 2
SEQ = 4096
CTX_LEN = 256
GRID_W = 64
EPS = 1e-6
ROPE_THETA = 10000.0

DN_HEADS = 8
DN_HEAD_DIM = 128
DN_CONV = 5
DN_CHUNK = 64
MLA_HEADS = 8
MLA_Q_RANK = 768
MLA_KV_RANK = 512
MLA_NOPE = 128
MLA_ROPE = 64
MLA_V = 128
GQA_HEADS = 32
GQA_KV_HEADS = 4
GQA_HEAD_DIM = 64
WINDOW = 128
Q_BLOCK = 128
D_FF = 7168
N_EXPERTS = 8
TOP_K = 2

DN_W = DN_HEADS * DN_HEAD_DIM
DN_QKV = 3 * DN_W
N_LAT = BATCH * SEQ
N_CTX = BATCH * CTX_LEN
N_TOK = N_LAT + N_CTX
GQA_Q = GQA_HEADS * GQA_HEAD_DIM
GQA_KV = GQA_KV_HEADS * GQA_HEAD_DIM

P_QKV = 0
P_Z = DN_QKV
P_MAIN = P_Z + DN_W
R_CKV = 0
R_MISC = MLA_KV_RANK
R_CQ = MLA_Q_RANK
R_WIDTH = R_CQ + MLA_Q_RANK

ROW_TILE = 512
MM_ROW_TILE = 1088
MOE_BLOCK = 1024
MOE_SUB = 128
MOE_SLOTS = N_LAT * TOP_K + N_EXPERTS * MOE_BLOCK
VMEM_LIMIT = 56 * 1024 * 1024

_SDS = jax.ShapeDtypeStruct
_HIGH = lax.Precision.HIGHEST


def _cparams(sem, vmem=None):
    return pltpu.CompilerParams(dimension_semantics=sem, vmem_limit_bytes=vmem)


def _dot(a, b):
    return jnp.dot(a, b, preferred_element_type=F32)


def _dot_nt(a, b):
    return lax.dot_general(a, b, (((1,), (1,)), ((), ())), preferred_element_type=F32)


def _dot_tn(a, b):
    return lax.dot_general(a, b, (((0,), (0,)), ((), ())), preferred_element_type=F32)


def _dot_hi(a, b):
    return jnp.dot(a, b, preferred_element_type=F32, precision=_HIGH)


def _silu(x):
    return x * jax.nn.sigmoid(x)


def _rms(x, g):
    return x * lax.rsqrt(jnp.mean(x * x, axis=-1, keepdims=True) + EPS) * g


def _group_of_row(r0):
    return (r0 >= SEQ).astype(I32) + (r0 >= 2 * SEQ).astype(I32)


def _mod_spec(layer, k):
    return pl.BlockSpec((1, 8, D_MODEL), lambda *_: (layer, 0, k))


def _mod_row(ref, grp):
    return ref[0, pl.ds(grp, 1), :]


def _ada_kernel(c_ref, w_ref, b_ref, o_ref):
    s = _silu(c_ref[...]).astype(BF16)
    o_ref[0] = _dot(s, w_ref[0].astype(BF16)) + b_ref[0]


def _ada(cvec, ada_w, ada_b):
    n_layers, _, n = ada_w.shape
    tn = 1024
    return pl.pallas_call(
        _ada_kernel,
        out_shape=_SDS((n_layers, 8, n), F32),
        grid=(n_layers, n // tn),
        in_specs=[pl.BlockSpec((8, D_MODEL), lambda l, j: (0, 0)),
                  pl.BlockSpec((1, D_MODEL, tn), lambda l, j: (l, 0, j)),
                  pl.BlockSpec((1, 1, tn), lambda l, j: (l, 0, j))],
        out_specs=pl.BlockSpec((1, 8, tn), lambda l, j: (l, 0, j)),
        compiler_params=_cparams(("parallel", "parallel")),
        name="ada_mod",
    )(cvec, ada_w, ada_b.reshape(n_layers, 1, n))


_LAT_TILES = N_LAT // ROW_TILE
assert N_CTX == ROW_TILE


def _split_specs(width, col=0):
    lat = pl.BlockSpec((ROW_TILE, width), lambda i, *_: (jnp.minimum(i, _LAT_TILES - 1), col))
    ctx = pl.BlockSpec((ROW_TILE, width), lambda i, *_: (0, col))
    return [lat, ctx]


def _split_read(lat_ref, ctx_ref, sl=slice(None)):
    return jnp.where(pl.program_id(0) == _LAT_TILES, ctx_ref[:, sl], lat_ref[:, sl])


def _modin_kernel(x_ref, c_ref, g_ref, sh_ref, sc_ref, u_ref):
    grp = _group_of_row(pl.program_id(0) * ROW_TILE)
    h = _split_read(x_ref, c_ref)
    u = _rms(h, g_ref[...]) * (1.0 + _mod_row(sc_ref, grp)) + _mod_row(sh_ref, grp)
    u_ref[...] = u.astype(u_ref.dtype)


def _modulate_in(x, ctx, g, mod, layer, k_shift, k_scale):
    row = pl.BlockSpec((ROW_TILE, D_MODEL), lambda i: (i, 0))
    vec = pl.BlockSpec((1, D_MODEL), lambda i: (0, 0))
    return pl.pallas_call(
        _modin_kernel,
        out_shape=_SDS((N_TOK, D_MODEL), BF16),
        grid=(N_TOK // ROW_TILE,),
        in_specs=_split_specs(D_MODEL) + [vec, _mod_spec(layer, k_shift), _mod_spec(layer, k_scale)],
        out_specs=row,
        compiler_params=_cparams(("parallel",)),
        name="modulate_in",
    )(x, ctx, g.reshape(1, D_MODEL), mod, mod)


def _route_top2(logits):
    lane = lax.broadcasted_iota(I32, logits.shape, 1)
    neg = -jnp.inf
    l0 = jnp.where(lane < N_EXPERTS, logits, neg)
    m1 = jnp.max(l0, axis=-1, keepdims=True)
    i1 = jnp.min(jnp.where(l0 == m1, lane, 128), axis=-1, keepdims=True)
    l1 = jnp.where(lane == i1, neg, l0)
    m2 = jnp.max(l1, axis=-1, keepdims=True)
    i2 = jnp.min(jnp.where(l1 == m2, lane, 128), axis=-1, keepdims=True)
    e = jnp.exp(m2 - m1)
    w1 = 1.0 / (1.0 + e)
    idx = jnp.where(lane == 0, i1, jnp.where(lane == 1, i2, 0))
    wts = jnp.where(lane == 0, w1, jnp.where(lane == 1, e * w1, 0.0))
    return idx, wts


U32 = jnp.uint32


def _pack_bf16_halves(x):
    n = x.shape[1] // 2
    lo = lax.bitcast_convert_type(x[:, :n].astype(F32), U32) >> 16
    hi = (lax.bitcast_convert_type(x[:, n:].astype(F32), U32) >> 16) << 16
    return hi | lo


def _unpack_bf16_halves(w):
    lo = lax.bitcast_convert_type(w << 16, F32).astype(BF16)
    hi = lax.bitcast_convert_type((w >> 16) << 16, F32).astype(BF16)
    return lo, hi


def _post_kernel(*refs, split, route):
    refs = list(refs)
    if split:
        h = _split_read(refs.pop(0), refs.pop(0))
    else:
        h = refs.pop(0)[...]
    y_ref, gp_ref, gate_ref, gn_ref, sh_ref, sc_ref = refs[:6]
    refs = refs[6:]
    wr_ref = refs.pop(0) if route else None
    hn_ref, u_ref = refs[:2]
    grp = _group_of_row(pl.program_id(0) * ROW_TILE)
    hn = h + _mod_row(gate_ref, grp) * _rms(y_ref[...].astype(F32), gp_ref[...])
    hn_ref[...] = hn
    u = _rms(hn, gn_ref[...]) * (1.0 + _mod_row(sc_ref, grp)) + _mod_row(sh_ref, grp)
    if route:
        ub = u.astype(BF16)
        u_ref[...] = _pack_bf16_halves(ub)
        idx_ref, wt_ref = refs[2:4]
        idx_ref[...], wt_ref[...] = _route_top2(_dot(ub, wr_ref[...]))
    else:
        u_ref[...] = u.astype(u_ref.dtype)


def _post(h, y, g_post, g_next, mod, gate_lk, shift_lk, scale_lk, u_dtype, w_router=None):
    n = y.shape[0]
    split = isinstance(h, tuple)
    route = w_router is not None
    row = pl.BlockSpec((ROW_TILE, D_MODEL), lambda i: (i, 0))
    vec = pl.BlockSpec((1, D_MODEL), lambda i: (0, 0))
    slab = pl.BlockSpec((ROW_TILE, 128), lambda i: (i, 0))
    h_specs, h_args = (_split_specs(D_MODEL), list(h)) if split else ([row], [h])
    extra_specs, extra_args, extra_out, extra_out_specs = [], [], (), ()
    u_shape, u_spec = _SDS((n, D_MODEL), u_dtype), row
    if route:
        w = jnp.zeros((D_MODEL, 128), BF16).at[:, :N_EXPERTS].set(w_router.astype(BF16))
        extra_specs, extra_args = [pl.BlockSpec((D_MODEL, 128), lambda i: (0, 0))], [w]
        extra_out, extra_out_specs = (_SDS((n, 128), I32), _SDS((n, 128), F32)), (slab, slab)
        u_shape, u_spec = _SDS((n, D_MODEL // 2), U32), pl.BlockSpec((ROW_TILE, D_MODEL // 2), lambda i: (i, 0))
    return pl.pallas_call(
        functools.partial(_post_kernel, split=split, route=route),
        out_shape=(_SDS((n, D_MODEL), F32), u_shape) + extra_out,
        grid=(n // ROW_TILE,),
        in_specs=h_specs + [row, vec, _mod_spec(*gate_lk), vec, _mod_spec(*shift_lk), _mod_spec(*scale_lk)] + extra_specs,
        out_specs=(row, u_spec) + extra_out_specs,
        compiler_params=_cparams(("parallel",)),
        name="residual_post",
    )(*h_args, y, g_post.reshape(1, D_MODEL), mod, g_next.reshape(1, D_MODEL), mod, mod, *extra_args)


def _mm_kernel(a_ref, w_ref, b_ref, o_ref, *, w_is_transposed):
    dot = _dot_nt if w_is_transposed else _dot
    o_ref[...] = (dot(a_ref[...], w_ref[...].astype(BF16)) + b_ref[...]).astype(o_ref.dtype)


def _matmul(a, w, bias, out_dtype, tm, tn, n=None, w_is_transposed=False):
    m, k = a.shape
    if w_is_transposed:
        n = w.shape[-2] if n is None else n
        if w.ndim == 3:
            w_spec = pl.BlockSpec((None, tn, k), lambda i, j: (0, j, 0))
        else:
            w_spec = pl.BlockSpec((tn, k), lambda i, j: (j, 0))
    else:
        n = w.shape[-1] if n is None else n
        w_spec = pl.BlockSpec((k, tn), lambda i, j: (0, j))
    if bias is None:
        bias = jnp.zeros((n,), F32)
    return pl.pallas_call(
        functools.partial(_mm_kernel, w_is_transposed=w_is_transposed),
        out_shape=_SDS((m, n), out_dtype),
        grid=(m // tm, n // tn),
        in_specs=[pl.BlockSpec((tm, k), lambda i, j: (i, 0)),
                  w_spec,
                  pl.BlockSpec((1, tn), lambda i, j: (0, j))],
        out_specs=pl.BlockSpec((tm, tn), lambda i, j: (i, j)),
        compiler_params=_cparams(("parallel", "arbitrary"), VMEM_LIMIT),
        name="matmul",
    )(a, w, bias.reshape(1, n))


FF_TILE = 512
SWIGLU_VMEM_LIMIT = 60 * 1024 * 1024
OUT_TILE = 256
FF_STEPS = D_FF // FF_TILE
OUT_STEPS = D_MODEL // OUT_TILE


def _swiglu_step(s, x_ref, wg_ref, wu_ref, wd_ref, o_ref, h_ref, rows=None, guard=True):
    total = x_ref.shape[0]
    rows = total if rows is None else rows

    @pl.when(guard & (s < FF_STEPS))
    def _():
        x = x_ref[0:rows, :]
        a = _dot(x, wg_ref[...].astype(BF16))
        b = _dot(x, wu_ref[...].astype(BF16))
        h_ref[0:rows, pl.ds(pl.multiple_of(s * FF_TILE, FF_TILE), FF_TILE)] = (_silu(a) * b).astype(BF16)

    @pl.when(guard & (s >= FF_STEPS))
    def _():
        o_ref[0:rows, :] = _dot(h_ref[0:rows, :], wd_ref[...].astype(BF16)).astype(o_ref.dtype)
        if rows < total:
            o_ref[rows:total, :] = jnp.zeros((total - rows, o_ref.shape[1]), o_ref.dtype)


def _ffn_kernel(u_ref, wg_ref, wu_ref, wd_ref, o_ref, h_ref):
    _swiglu_step(pl.program_id(1), u_ref, wg_ref, wu_ref, wd_ref, o_ref, h_ref)


def _ffn(u, w_gate, w_up, w_down):
    m = u.shape[0]
    tm = MM_ROW_TILE

    def ff(i, s):
        return (0, jnp.minimum(s, FF_STEPS - 1))

    def out(s):
        return jnp.maximum(s - FF_STEPS, 0)

    return pl.pallas_call(
        _ffn_kernel,
        out_shape=_SDS((m, D_MODEL), BF16),
        grid=(m // tm, FF_STEPS + OUT_STEPS),
        in_specs=[pl.BlockSpec((tm, D_MODEL), lambda i, s: (i, 0), pipeline_mode=pl.Buffered(1)),
                  pl.BlockSpec((D_MODEL, FF_TILE), ff),
                  pl.BlockSpec((D_MODEL, FF_TILE), ff),
                  pl.BlockSpec((D_FF, OUT_TILE), lambda i, s: (0, out(s)))],
        out_specs=pl.BlockSpec((tm, OUT_TILE), lambda i, s: (i, out(s))),
        scratch_shapes=[pltpu.VMEM((tm, D_FF), BF16)],
        compiler_params=_cparams(("parallel", "arbitrary"), SWIGLU_VMEM_LIMIT),
        name="swiglu_ffn",
    )(u, w_gate, w_up, w_down)


CONV_TILE = 256
CONV_HALO = 16
_SEQ_STARTS = (0, SEQ // CONV_TILE, 2 * SEQ // CONV_TILE, 2 * SEQ // CONV_TILE + 1)
_SEQ_LASTS = (SEQ // CONV_TILE - 1, 2 * SEQ // CONV_TILE - 1, 2 * SEQ // CONV_TILE, 2 * SEQ // CONV_TILE + 1)


def _conv_kernel(prev_ref, cur_ref, next_ref, w_ref, o_ref, buf):
    i = pl.program_id(0)
    j = pl.program_id(1)
    tm = CONV_TILE
    first = functools.reduce(jnp.logical_or, [i == s for s in _SEQ_STARTS])
    last = functools.reduce(jnp.logical_or, [i == s for s in _SEQ_LASTS])
    hb = CONV_HALO
    buf[0:hb, :] = jnp.where(first, 0.0, prev_ref[...].astype(F32))
    buf[hb:hb + tm, :] = cur_ref[...].astype(F32)
    buf[hb + tm:2 * hb + tm, :] = jnp.where(last, 0.0, next_ref[...].astype(F32))
    pad = DN_CONV // 2
    acc = buf[hb - pad:hb - pad + tm, :] * w_ref[0:1, :]
    for d in range(1, DN_CONV):
        acc = acc + buf[hb - pad + d:hb - pad + d + tm, :] * w_ref[d:d + 1, :]
    x = _silu(acc)
    q_scale = jnp.where(j == 0, DN_HEAD_DIM ** -0.5, 1.0)
    for hh in range(DN_HEADS):
        xh = x[:, hh * DN_HEAD_DIM:(hh + 1) * DN_HEAD_DIM]
        inv = lax.rsqrt(jnp.sum(xh * xh, axis=-1, keepdims=True) + EPS) * q_scale
        o_ref[:, hh * DN_HEAD_DIM:(hh + 1) * DN_HEAD_DIM] = xh * jnp.where(j == 2, 1.0, inv)


def _dn_conv(p, conv_w):
    n = p.shape[0]
    tm = CONV_TILE
    hb = CONV_HALO
    per_tile = tm // hb
    n_halo = n // hb
    return pl.pallas_call(
        _conv_kernel,
        out_shape=_SDS((n, DN_QKV), F32),
        grid=(n // tm, 3),
        in_specs=[pl.BlockSpec((hb, DN_W), lambda i, j: (jnp.maximum(i * per_tile - 1, 0), j)),
                  pl.BlockSpec((tm, DN_W), lambda i, j: (i, j)),
                  pl.BlockSpec((hb, DN_W), lambda i, j: (jnp.minimum((i + 1) * per_tile, n_halo - 1), j)),
                  pl.BlockSpec((DN_CONV, DN_W), lambda i, j: (0, j))],
        out_specs=pl.BlockSpec((tm, DN_W), lambda i, j: (i, j)),
        scratch_shapes=[pltpu.VMEM((tm + 2 * hb, DN_W), F32)],
        compiler_params=_cparams(("parallel", "parallel")),
        name="dn_conv",
    )(p, p, p, conv_w)


def _gates_kernel(m_ref, a_ref, dt_ref, o_ref):
    x = pltpu.roll(m_ref[...], 64, 1)
    lane = lax.broadcasted_iota(I32, x.shape, 1)
    z = x + dt_ref[...]
    softplus = jnp.maximum(z, 0.0) + jnp.log1p(jnp.exp(-jnp.abs(z)))
    g = a_ref[...] * softplus
    o_ref[...] = jnp.where(lane < 2 * DN_HEADS, jax.nn.sigmoid(x), jnp.where(lane < 4 * DN_HEADS, g, 0.0))


def _dn_gates(p, a_log, dt_bias):
    n = p.shape[0]
    neg_a = jnp.zeros((1, 128), F32).at[0, 2 * DN_HEADS:4 * DN_HEADS].set(-jnp.exp(a_log.astype(F32)).reshape(-1))
    dtb = jnp.zeros((1, 128), F32).at[0, 2 * DN_HEADS:4 * DN_HEADS].set(dt_bias.astype(F32).reshape(-1))
    tm = ROW_TILE
    vec = pl.BlockSpec((1, 128), lambda i: (0, 0))
    return pl.pallas_call(
        _gates_kernel,
        out_shape=_SDS((n, 128), F32),
        grid=(n // tm,),
        in_specs=[pl.BlockSpec((tm, 128), lambda i: (i, R_MISC // 128)), vec, vec],
        out_specs=pl.BlockSpec((tm, 128), lambda i: (i, 0)),
        compiler_params=_cparams(("parallel",)),
        name="dn_gates",
    )(p, neg_a, dtb)


_CTX_CHUNKS = CTX_LEN // DN_CHUNK
_LAT_CHUNKS = SEQ // DN_CHUNK
_DN_STEPS = _CTX_CHUNKS + _LAT_CHUNKS


def _dn_fwd_block(b, t):
    return jnp.where(t < _CTX_CHUNKS, (2 * SEQ + b * CTX_LEN) // DN_CHUNK + t, b * _LAT_CHUNKS + (t - _CTX_CHUNKS))


def _dn_bwd_block(b, t):
    return jnp.where(t < _CTX_CHUNKS, (2 * SEQ + b * CTX_LEN) // DN_CHUNK + (_CTX_CHUNKS - 1 - t),
                     b * _LAT_CHUNKS + (_DN_STEPS - 1 - t))


DN_GROUP = 2
_GROUP_ROWS = DN_GROUP * DN_CHUNK
_GROUP_STATE = DN_GROUP * DN_HEAD_DIM


_DN_BASE = 8


def _block_diag(x):
    zero = jnp.zeros((DN_CHUNK, DN_HEAD_DIM), x.dtype)
    rows = []
    for hh in range(DN_GROUP):
        piece = x[hh * DN_CHUNK:(hh + 1) * DN_CHUNK]
        rows.append(jnp.concatenate([piece if j == hh else zero for j in range(DN_GROUP)], axis=1))
    return jnp.concatenate(rows, axis=0)


def _dn_groups(items, eye, base_mask):
    it = items
    n = range(len(it))
    gx = [jnp.broadcast_to(it[g]["gc"], (_GROUP_ROWS, _GROUP_ROWS)) for g in n]
    gamma = [jnp.where(it[g]["incl"], jnp.exp(jnp.where(it[g]["incl"], gx[g] - gx[g].T, 0.0)), 0.0) for g in n]
    kb = [it[g]["k"].astype(BF16) for g in n]
    kk = [_dot_nt(kb[g], kb[g]) for g in n]
    a = [jnp.where(it[g]["strict"], it[g]["beta"] * kk[g] * gamma[g], 0.0) for g in n]
    d = [jnp.where(base_mask, a[g], 0.0) for g in n]
    tinv = [eye - d[g] for g in n]
    pw = [d[g].astype(BF16) for g in n]
    for _ in range(int(math.log2(_DN_BASE)) - 1):
        pw = [_dot(pw[g], pw[g]).astype(BF16) for g in n]
        tinv = [tinv[g] + _dot(tinv[g].astype(BF16), pw[g]) for g in n]
    for lvl in range(len(it[0]["levels"])):
        tb = [tinv[g].astype(BF16) for g in n]
        mt = [_dot(jnp.where(it[g]["levels"][lvl], a[g], 0.0).astype(BF16), tb[g]).astype(BF16) for g in n]
        tinv = [tinv[g] - _dot(tb[g], mt[g]) for g in n]
    tb = [tinv[g].astype(BF16) for g in n]
    egc = [jnp.exp(it[g]["gc"]) for g in n]
    rhs = [jnp.concatenate([it[g]["v"] * it[g]["beta"], it[g]["k"] * (it[g]["beta"] * egc[g])], axis=1) for g in n]
    rhs_hi = [rhs[g].astype(BF16) for g in n]
    rhs_lo = [(rhs[g] - rhs_hi[g].astype(F32)).astype(BF16) for g in n]
    sol = [_dot(jnp.concatenate([tb[g], tb[g]], axis=1), jnp.concatenate([rhs_hi[g], rhs_lo[g]], axis=0)) for g in n]
    qk = [(_dot_nt(it[g]["q"].astype(BF16), kb[g]) * gamma[g]).astype(BF16) for g in n]
    s = [it[g]["s_ref"][...] for g in n]
    sb = [s[g].astype(BF16) for g in n]
    v_new = [sol[g][:, :DN_HEAD_DIM] - _dot(_block_diag(sol[g][:, DN_HEAD_DIM:].astype(BF16)), sb[g]) for g in n]
    vb = [v_new[g].astype(BF16) for g in n]
    o = [_dot(_block_diag((it[g]["q"] * egc[g]).astype(BF16)), sb[g]) + _dot(qk[g], vb[g]) for g in n]
    for g in n:
        k_dec = _block_diag((it[g]["k"] * jnp.exp(it[g]["g_last"] - it[g]["gc"])).astype(BF16))
        it[g]["s_ref"][...] = s[g] * jnp.exp(it[g]["g_last_state"]) + _dot_tn(k_dec, vb[g])
    return o


def _stack_cols(x, lanes, rows_each):
    pieces = [x[:, l:l + 1] for l in lanes]
    if rows_each is not None:
        pieces = [jnp.broadcast_to(pc, (rows_each, 1)) for pc in pieces]
    return jnp.concatenate(pieces, axis=0)


def _dn_kernel(*refs):
    n_in = BATCH * 2 * 4
    in_refs, (ofl_ref, ofc_ref, obl_ref, obc_ref), (s_ref, o_scr) = refs[:n_in], refs[n_in:n_in + 4], refs[n_in + 4:]
    t = pl.program_id(0)

    @pl.when(t == 0)
    def _():
        s_ref[...] = jnp.zeros_like(s_ref)

    c = DN_CHUNK
    ri = lax.broadcasted_iota(I32, (_GROUP_ROWS, _GROUP_ROWS), 0)
    ci = lax.broadcasted_iota(I32, (_GROUP_ROWS, _GROUP_ROWS), 1)
    same_head = (ri // c) == (ci // c)
    eye = (ri == ci).astype(F32)
    r1 = lax.broadcasted_iota(I32, (c, c), 0)
    c1 = lax.broadcasted_iota(I32, (c, c), 1)
    base_mask = (ri // _DN_BASE) == (ci // _DN_BASE)
    masks = []
    for d in range(2):
        incl = same_head & ((ri >= ci) if d == 0 else (ri <= ci))
        strict = same_head & ((ri > ci) if d == 0 else (ri < ci))
        levels = []
        size = _DN_BASE
        while size < c:
            lo_blk, hi_blk = ((ci, ri) if d == 0 else (ri, ci))
            levels.append(((ri // (2 * size)) == (ci // (2 * size)))
                          & ((lo_blk // size) % 2 == 0) & ((hi_blk // size) % 2 == 1))
            size *= 2
        masks.append((incl, strict, levels))
    items, where = [], []
    for b, d in [(b, d) for b in range(BATCH) for d in range(2)]:
        q_ref, k_ref, v_ref, g_ref = in_refs[(b * 2 + d) * 4:(b * 2 + d) * 4 + 4]
        incl, strict, levels = masks[d]
        gates = g_ref[...]
        tri = ((r1 >= c1) if d == 0 else (r1 <= c1)).astype(BF16)
        g_hi = gates.astype(BF16)
        g_r1 = gates - g_hi.astype(F32)
        g_mid = g_r1.astype(BF16)
        g_lo = (g_r1 - g_mid.astype(F32)).astype(BF16)
        gc = _dot(tri, g_hi) + _dot(tri, g_mid) + _dot(tri, g_lo)
        last_row = c - 1 if d == 0 else 0
        tot = gc[last_row:last_row + 1, :]
        for grp in range(DN_HEADS // DN_GROUP):
            heads = range(grp * DN_GROUP, (grp + 1) * DN_GROUP)
            lanes_b = [d * DN_HEADS + hh for hh in heads]
            lanes_g = [2 * DN_HEADS + lb for lb in lanes_b]

            def stack(ref):
                return jnp.concatenate([ref[:, hh * DN_HEAD_DIM:(hh + 1) * DN_HEAD_DIM] for hh in heads], axis=0)

            items.append(dict(q=stack(q_ref), k=stack(k_ref), v=stack(v_ref),
                              beta=_stack_cols(gates, lanes_b, None), gc=_stack_cols(gc, lanes_g, None),
                              g_last=_stack_cols(tot, lanes_g, c), g_last_state=_stack_cols(tot, lanes_g, DN_HEAD_DIM),
                              incl=incl, strict=strict, levels=levels, s_ref=s_ref.at[b, d, grp]))
            where.append((d, b, heads))
    outs = _dn_groups(items, eye, base_mask)
    for o, (d, b, heads) in zip(outs, where):
        for j, hh in enumerate(heads):
            o_scr[d, b, :, hh * DN_HEAD_DIM:(hh + 1) * DN_HEAD_DIM] = o[j * c:(j + 1) * c]

    @pl.when(t < _CTX_CHUNKS)
    def _():
        ofc_ref[...] = o_scr[0]
        obc_ref[...] = o_scr[1]

    @pl.when(t >= _CTX_CHUNKS)
    def _():
        ofl_ref[...] = o_scr[0]
        obl_ref[...] = o_scr[1]


def _deltanet(qkv, gates):
    c = DN_CHUNK
    ins, args = [], []
    for b in range(BATCH):
        for fn in (_dn_fwd_block, _dn_bwd_block):
            for width, col, arr in ((DN_W, 0, qkv), (DN_W, 1, qkv), (DN_W, 2, qkv), (128, 0, gates)):
                ins.append(pl.BlockSpec((c, width), functools.partial(lambda t, b, fn, col: (fn(b, t), col),
                                                                      b=b, fn=fn, col=col)))
                args.append(arr)
    last = _DN_STEPS - 1
    lat = _SDS((BATCH, SEQ, DN_W), F32)
    ctx = _SDS((BATCH, CTX_LEN, DN_W), F32)
    blk = (BATCH, c, DN_W)
    outs = pl.pallas_call(
        _dn_kernel,
        out_shape=(lat, ctx, lat, ctx),
        grid=(_DN_STEPS,),
        in_specs=ins,
        out_specs=(pl.BlockSpec(blk, lambda t: (0, jnp.maximum(t - _CTX_CHUNKS, 0), 0)),
                   pl.BlockSpec(blk, lambda t: (0, jnp.minimum(t, _CTX_CHUNKS - 1), 0)),
                   pl.BlockSpec(blk, lambda t: (0, jnp.minimum(last - t, _LAT_CHUNKS - 1), 0)),
                   pl.BlockSpec(blk, lambda t: (0, jnp.maximum(_CTX_CHUNKS - 1 - t, 0), 0))),
        scratch_shapes=[pltpu.VMEM((BATCH, 2, DN_HEADS // DN_GROUP, _GROUP_STATE, DN_HEAD_DIM), F32),
                        pltpu.VMEM((2, BATCH, c, DN_W), F32)],
        compiler_params=_cparams(("arbitrary",)),
        name="deltanet",
    )(*args)
    ofl, ofc, obl, obc = outs
    return ((ofl.reshape(N_LAT, DN_W), ofc.reshape(N_CTX, DN_W)),
            (obl.reshape(N_LAT, DN_W), obc.reshape(N_CTX, DN_W)))


def _rope_tables():
    t = jnp.arange(SEQ)
    row = (t // GRID_W).astype(F32)
    col = (t % GRID_W).astype(F32)
    n_freq = MLA_ROPE // 4
    inv_freq = ROPE_THETA ** (-jnp.arange(n_freq, dtype=F32) / n_freq)
    ang = jnp.concatenate([row[:, None] * inv_freq, col[:, None] * inv_freq], axis=-1)
    cos, sin = jnp.cos(ang), jnp.sin(ang)
    cos64 = jnp.concatenate([cos, cos], axis=-1)
    sin64 = jnp.concatenate([-sin, sin], axis=-1)

    def assemble(lat_cos, lat_sin):
        c = jnp.concatenate([jnp.tile(lat_cos, (BATCH, 1)), jnp.ones((N_CTX, 128), F32)], axis=0)
        s = jnp.concatenate([jnp.tile(lat_sin, (BATCH, 1)), jnp.zeros((N_CTX, 128), F32)], axis=0)
        return c, s

    cos_l = jnp.concatenate([cos64, jnp.ones((SEQ, 64), F32)], axis=-1)
    sin_l = jnp.concatenate([sin64, jnp.zeros((SEQ, 64), F32)], axis=-1)
    one_head = assemble(cos_l, sin_l)
    two_heads = assemble(jnp.tile(cos64, (1, 2)), jnp.tile(sin64, (1, 2)))
    return one_head, two_heads


def _rope128(x, cosv, sinv):
    lane = lax.broadcasted_iota(I32, x.shape, 1)
    swapped = jnp.where((lane % 64) < 32, pltpu.roll(x, 96, 1), pltpu.roll(x, 32, 1))
    return x * cosv + swapped * sinv


MLA_QK = 256
MLA_VX = 256


def _mla_proj_kernel(ckv_ref, cq_ref, misc_ref, gq_ref, gkv_ref, wq_ref, wkv_ref, cos_ref, sin_ref,
                     q_ref, k_ref, v_ref):
    cq = _rms(cq_ref[...].astype(F32), gq_ref[...]).astype(BF16)
    ckv = _rms(ckv_ref[...].astype(F32), gkv_ref[...]).astype(BF16)
    qf = _dot(cq, wq_ref[...])
    kvf = _dot(ckv, wkv_ref[...])
    cosv, sinv = cos_ref[...], sin_ref[...]
    lane = lax.broadcasted_iota(I32, cosv.shape, 1)
    k_rope = _rope128(jnp.where(lane < MLA_ROPE, misc_ref[...], 0.0), cosv, sinv).astype(BF16)
    scale = (MLA_NOPE + MLA_ROPE) ** -0.5
    for hh in range(MLA_HEADS):
        lo, mid, hi = hh * MLA_QK, hh * MLA_QK + 128, (hh + 1) * MLA_QK
        q_ref[:, lo:mid] = (qf[:, lo:mid] * scale).astype(BF16)
        q_ref[:, mid:hi] = (_rope128(qf[:, mid:hi], cosv, sinv) * scale).astype(BF16)
        k_ref[:, lo:mid] = kvf[:, lo:mid].astype(BF16)
        k_ref[:, mid:hi] = k_rope
        v_ref[:, lo:mid] = kvf[:, mid:hi].astype(BF16)
        v_ref[:, mid:hi] = jnp.ones((kvf.shape[0], MLA_VX - MLA_V), BF16)


def _mla_proj(p_rest, q_norm_g, kv_norm_g, w_q, w_kv, cosv, sinv):
    n = p_rest.shape[0]
    tm = ROW_TILE
    wide = MLA_HEADS * MLA_QK

    def rows(width, col):
        return pl.BlockSpec((tm, width), lambda i: (i, col))

    def whole(shape):
        return pl.BlockSpec(shape, lambda i: (0, 0))

    return pl.pallas_call(
        _mla_proj_kernel,
        out_shape=(_SDS((n, wide), BF16), _SDS((n, wide), BF16), _SDS((n, MLA_HEADS * MLA_VX), BF16)),
        grid=(n // tm,),
        in_specs=[rows(MLA_KV_RANK, R_CKV // MLA_KV_RANK), rows(MLA_Q_RANK, R_CQ // MLA_Q_RANK),
                  rows(128, R_MISC // 128), whole((1, MLA_Q_RANK)), whole((1, MLA_KV_RANK)),
                  whole((MLA_Q_RANK, wide)), whole((MLA_KV_RANK, wide)), rows(128, 0), rows(128, 0)],
        out_specs=(rows(wide, 0), rows(wide, 0), rows(MLA_HEADS * MLA_VX, 0)),
        compiler_params=_cparams(("parallel",), VMEM_LIMIT),
        name="mla_proj",
    )(p_rest, p_rest, p_rest, q_norm_g.reshape(1, -1), kv_norm_g.reshape(1, -1), w_q, w_kv, cosv, sinv)


MLA_KEY_CHUNK = 2048


def _flash_kernel(*refs, chunks):
    n_seg = (len(refs) - 2) // 2
    q_ref, o_ref = refs[0], refs[-1]
    k_refs, v_refs = refs[1:1 + n_seg], refs[1 + n_seg:1 + 2 * n_seg]
    q = q_ref[...]

    def scores(c):
        seg, off, size = c
        return _dot_nt(q, k_refs[seg][off:off + size, :])

    m = jnp.full((q.shape[0], 1), -jnp.inf, F32)
    acc = jnp.zeros((q.shape[0], MLA_VX), F32)
    s_next = scores(chunks[0])
    for j, (seg, off, size) in enumerate(chunks):
        s = s_next
        if j + 1 < len(chunks):
            s_next = scores(chunks[j + 1])
        m_new = jnp.maximum(m, jnp.max(s, axis=-1, keepdims=True))
        p = jnp.exp(s - m_new).astype(BF16)
        acc = jnp.exp(m - m_new) * acc + _dot(p, v_refs[seg][off:off + size, :])
        m = m_new
    o_ref[...] = (acc[:, :MLA_V] / acc[:, MLA_V:]).astype(o_ref.dtype)


def _mla_attention(q, k, v):
    tq = 1024
    nq = SEQ // tq
    ctx_blk = 2 * SEQ // CTX_LEN
    lat_chunks = ((0, 0, CTX_LEN),) + tuple((1, off, MLA_KEY_CHUNK) for off in range(0, SEQ, MLA_KEY_CHUNK))
    lat = pl.pallas_call(
        functools.partial(_flash_kernel, chunks=lat_chunks),
        out_shape=_SDS((N_LAT, MLA_HEADS * MLA_V), BF16),
        grid=(BATCH, MLA_HEADS, nq),
        in_specs=[pl.BlockSpec((tq, MLA_QK), lambda b, h, i: (b * nq + i, h)),
                  pl.BlockSpec((CTX_LEN, MLA_QK), lambda b, h, i: (ctx_blk + b, h)),
                  pl.BlockSpec((SEQ, MLA_QK), lambda b, h, i: (b, h)),
                  pl.BlockSpec((CTX_LEN, MLA_VX), lambda b, h, i: (ctx_blk + b, h)),
                  pl.BlockSpec((SEQ, MLA_VX), lambda b, h, i: (b, h))],
        out_specs=pl.BlockSpec((tq, MLA_V), lambda b, h, i: (b * nq + i, h)),
        compiler_params=_cparams(("parallel", "parallel", "arbitrary"), VMEM_LIMIT),
        name="mla_attn_latent",
    )(q, k, k, v, v)
    ctx = pl.pallas_call(
        functools.partial(_flash_kernel, chunks=((0, 0, CTX_LEN),)),
        out_shape=_SDS((N_CTX, MLA_HEADS * MLA_V), BF16),
        grid=(BATCH, MLA_HEADS),
        in_specs=[pl.BlockSpec((CTX_LEN, MLA_QK), lambda b, h: (ctx_blk + b, h)),
                  pl.BlockSpec((CTX_LEN, MLA_QK), lambda b, h: (ctx_blk + b, h)),
                  pl.BlockSpec((CTX_LEN, MLA_VX), lambda b, h: (ctx_blk + b, h))],
        out_specs=pl.BlockSpec((CTX_LEN, MLA_V), lambda b, h: (b, h)),
        compiler_params=_cparams(("parallel", "parallel")),
        name="mla_attn_context",
    )(q, k, v)
    return lat, ctx


def _ab_mix_kernel(ofl_ref, ofc_ref, obl_ref, obc_ref, ml_ref, mc_ref, z_ref, g_ref, a_ref):
    for hh in range(DN_HEADS):
        sl = slice(hh * DN_HEAD_DIM, (hh + 1) * DN_HEAD_DIM)
        o = _split_read(ofl_ref, ofc_ref, sl) + _split_read(obl_ref, obc_ref, sl)
        a_ref[:, sl] = (_rms(o, g_ref[...]) * _silu(z_ref[:, sl].astype(F32))).astype(BF16)
    a_ref[:, DN_W:] = _split_read(ml_ref, mc_ref)


def _ab_mix(o_f, o_b, p, dn_norm_g, mla_o):
    tm = ROW_TILE
    half = DN_W
    return pl.pallas_call(
        _ab_mix_kernel,
        out_shape=_SDS((N_TOK, 2 * half), BF16),
        grid=(N_TOK // tm,),
        in_specs=_split_specs(half) + _split_specs(half) + _split_specs(half) + [
            pl.BlockSpec((tm, half), lambda i: (i, P_Z // half)),
            pl.BlockSpec((1, DN_HEAD_DIM), lambda i: (0, 0))],
        out_specs=pl.BlockSpec((tm, 2 * half), lambda i: (i, 0)),
        compiler_params=_cparams(("parallel",)),
        name="ab_mix",
    )(*o_f, *o_b, *mla_o, p, dn_norm_g.reshape(1, -1))


def _mm_rope_kernel(a_ref, w_ref, b_ref, cos_ref, sin_ref, o_ref, *, tn, rope_cols):
    y = _dot(a_ref[...], w_ref[...].astype(BF16)) + b_ref[...]
    col0 = pl.program_id(1) * tn
    cosv, sinv = cos_ref[...], sin_ref[...]
    for s in range(tn // 128):
        ys = y[:, s * 128:(s + 1) * 128]
        roped = _rope128(ys, cosv, sinv)
        o_ref[:, s * 128:(s + 1) * 128] = jnp.where(col0 + s * 128 < rope_cols, roped, ys).astype(o_ref.dtype)


def _gqa_qkv(u, w, bias, cosv, sinv):
    m, k = u.shape
    n = w.shape[1]
    tm, tn = MM_ROW_TILE, 1280
    return pl.pallas_call(
        functools.partial(_mm_rope_kernel, tn=tn, rope_cols=GQA_Q + GQA_KV),
        out_shape=_SDS((m, n), BF16),
        grid=(m // tm, n // tn),
        in_specs=[pl.BlockSpec((tm, k), lambda i, j: (i, 0)),
                  pl.BlockSpec((k, tn), lambda i, j: (0, j)),
                  pl.BlockSpec((1, tn), lambda i, j: (0, j)),
                  pl.BlockSpec((tm, 128), lambda i, j: (i, 0)),
                  pl.BlockSpec((tm, 128), lambda i, j: (i, 0))],
        out_specs=pl.BlockSpec((tm, tn), lambda i, j: (i, j)),
        compiler_params=_cparams(("parallel", "arbitrary"), VMEM_LIMIT),
        name="gqa_qkv_rope",
    )(u, w, bias.reshape(1, n), cosv, sinv)


def _gqa_kernel(sink_ref, q_ref, kp_ref, kc_ref, kn_ref, vp_ref, vc_ref, vn_ref, kx_ref, vx_ref, o_ref):
    i = pl.program_id(1)
    nb = SEQ // Q_BLOCK
    n_keys = CTX_LEN + 3 * Q_BLOCK
    groups = GQA_HEADS // GQA_KV_HEADS
    slabs = groups // 2
    dh = GQA_HEAD_DIM
    qi = lax.broadcasted_iota(I32, (Q_BLOCK, n_keys), 0)
    kj = lax.broadcasted_iota(I32, (Q_BLOCK, n_keys), 1) - CTX_LEN
    rel = kj - Q_BLOCK - qi
    valid = (kj < 0) | ((jnp.abs(rel) <= WINDOW) & ((kj >= Q_BLOCK) | (i > 0)) & ((kj < 2 * Q_BLOCK) | (i < nb - 1)))
    bias = jnp.where(valid, 0.0, -jnp.inf).astype(F32)
    bias = jnp.concatenate([bias] * slabs, axis=0)
    k_all = jnp.concatenate([kx_ref[...], kp_ref[...], kc_ref[...], kn_ref[...]], axis=0)
    v_all = jnp.concatenate([vx_ref[...], vp_ref[...], vc_ref[...], vn_ref[...]], axis=0)
    low = lax.broadcasted_iota(I32, (n_keys, 128), 1) < dh

    def block_diag(x_all, kvh):
        pair = x_all[:, (kvh // 2) * 128:(kvh // 2 + 1) * 128].astype(F32)
        other = pltpu.roll(pair, dh, 1)
        first, second = (pair, other) if kvh % 2 == 0 else (other, pair)
        return jnp.concatenate([jnp.where(low, first, 0.0), jnp.where(low, 0.0, second)], axis=0).astype(BF16)

    def scores(kvh):
        q = jnp.concatenate([q_ref[:, (kvh * slabs + r) * 128:(kvh * slabs + r + 1) * 128] for r in range(slabs)], axis=0)
        return _dot_nt(q * (dh ** -0.5), block_diag(k_all, kvh))

    row_lo = lax.broadcasted_iota(I32, (2 * n_keys, 128), 0) < n_keys
    lane_lo = lax.broadcasted_iota(I32, (2 * n_keys, 128), 1) < dh
    ones_cols = jnp.where(row_lo == lane_lo, 1.0, 0.0).astype(BF16)

    def softmax(kvh, s):
        ps, sink_terms = [], []
        for half in range(2):
            sh = s[:, half * n_keys:(half + 1) * n_keys] + bias
            sink = jnp.concatenate([jnp.full((Q_BLOCK, 1), sink_ref[(kvh * slabs + r) * 2 + half], F32)
                                    for r in range(slabs)], axis=0)
            m = jnp.maximum(jnp.max(sh, axis=-1, keepdims=True), sink)
            ps.append(jnp.exp(sh - m).astype(BF16))
            sink_terms.append(jnp.exp(sink - m))
        return jnp.concatenate(ps, axis=1), sink_terms

    def finish(kvh, p, sink_terms):
        o = _dot(p, jnp.concatenate([block_diag(v_all, kvh), ones_cols], axis=1))
        num, den = o[:, :128], o[:, 128:]
        lane_low = lax.broadcasted_iota(I32, num.shape, 1) < dh
        o = num / (den + jnp.where(lane_low, sink_terms[0], sink_terms[1]))
        for r in range(slabs):
            o_ref[:, (kvh * slabs + r) * 128:(kvh * slabs + r + 1) * 128] = o[r * Q_BLOCK:(r + 1) * Q_BLOCK].astype(o_ref.dtype)

    s_next = scores(0)
    for kvh in range(GQA_KV_HEADS):
        s_cur = s_next
        if kvh + 1 < GQA_KV_HEADS:
            s_next = scores(kvh + 1)
        p, dens = softmax(kvh, s_cur)
        finish(kvh, p, dens)


def _gqa_attention(qkv, sink):
    nb = SEQ // Q_BLOCK
    kcol, vcol = GQA_Q // GQA_KV, GQA_Q // GQA_KV + 1
    ctx_blk = 2 * SEQ // CTX_LEN

    def win(col, shift):
        return pl.BlockSpec((Q_BLOCK, GQA_KV), lambda b, i: (b * nb + jnp.clip(i + shift, 0, nb - 1), col))

    return pl.pallas_call(
        _gqa_kernel,
        out_shape=_SDS((N_LAT, GQA_Q), BF16),
        grid=(BATCH, nb),
        in_specs=[pl.BlockSpec(memory_space=pltpu.SMEM),
                  pl.BlockSpec((Q_BLOCK, GQA_Q), lambda b, i: (b * nb + i, 0)),
                  win(kcol, -1), win(kcol, 0), win(kcol, 1), win(vcol, -1), win(vcol, 0), win(vcol, 1),
                  pl.BlockSpec((CTX_LEN, GQA_KV), lambda b, i: (ctx_blk + b, kcol)),
                  pl.BlockSpec((CTX_LEN, GQA_KV), lambda b, i: (ctx_blk + b, vcol))],
        out_specs=pl.BlockSpec((Q_BLOCK, GQA_Q), lambda b, i: (b * nb + i, 0)),
        compiler_params=_cparams(("parallel", "arbitrary")),
        name="gqa_window_attn",
    )(sink.astype(F32), qkv, qkv, qkv, qkv, qkv, qkv, qkv, qkv, qkv)


GATHER_TILE = 256


def _row_copy(src_hbm, row, dst, r, sem):
    return pltpu.make_async_copy(src_hbm.at[pl.ds(row, 1), :], dst.at[pl.ds(r, 1), :], sem)


def _gather_kernel(tok_ref, tv_ref, u_hbm, valid_ref, o_ref, buf, sem):
    i = pl.program_id(0)
    n = pl.num_programs(0)

    def start_tile(t, slot):
        @pl.when(tv_ref[t] > 0)
        def _():
            def issue(r8, carry):
                for j in range(8):
                    r = r8 * 8 + j
                    _row_copy(u_hbm, tok_ref[t * GATHER_TILE + r], buf.at[slot], r, sem.at[slot]).start(priority=j % 2)
                return carry

            lax.fori_loop(0, GATHER_TILE // 8, issue, 0)

    @pl.when(i == 0)
    def _():
        start_tile(0, 0)

    @pl.when(i + 1 < n)
    def _():
        start_tile(i + 1, (i + 1) % 2)

    slot = i % 2

    @pl.when(tv_ref[i] > 0)
    def _():
        def drain(r, carry):
            _row_copy(u_hbm, 0, buf.at[slot], r, sem.at[slot]).wait()
            return carry

        lax.fori_loop(0, GATHER_TILE, drain, 0, unroll=8)
        lo, hi = _unpack_bf16_halves(buf[slot])
        keep = valid_ref[...] > 0.0
        half = D_MODEL // 2
        o_ref[:, :half] = jnp.where(keep, lo, jnp.zeros_like(lo))
        o_ref[:, half:] = jnp.where(keep, hi, jnp.zeros_like(hi))

    @pl.when(tv_ref[i] == 0)
    def _():
        o_ref[...] = jnp.zeros_like(o_ref)


def _moe_gather(slot_tok, slot_valid, u):
    n_slots = slot_tok.shape[0]
    n_tiles = n_slots // GATHER_TILE
    tile_valid = (jnp.max(slot_valid.reshape(n_tiles, GATHER_TILE), axis=1) > 0.0).astype(I32)
    return pl.pallas_call(
        _gather_kernel,
        out_shape=_SDS((n_slots, D_MODEL), BF16),
        grid_spec=pltpu.PrefetchScalarGridSpec(
            num_scalar_prefetch=2,
            grid=(n_tiles,),
            in_specs=[pl.BlockSpec(memory_space=pl.ANY),
                      pl.BlockSpec((GATHER_TILE, 1), lambda i, tok, tv: (i, 0))],
            out_specs=pl.BlockSpec((GATHER_TILE, D_MODEL), lambda i, tok, tv: (i, 0)),
            scratch_shapes=[pltpu.VMEM((2, GATHER_TILE, D_MODEL // 2), U32), pltpu.SemaphoreType.DMA((2,))]),
        compiler_params=_cparams(("arbitrary",)),
        name="moe_gather",
    )(slot_tok, tile_valid, u, slot_valid.reshape(n_slots, 1))


def _moe_ffn_kernel(be_ref, bv_ref, x_ref, wg_ref, wu_ref, wd_ref, o_ref, h_ref):
    i, s = pl.program_id(0), pl.program_id(1)
    n_sub = bv_ref[i]
    for k in range(1, MOE_BLOCK // MOE_SUB + 1):
        _swiglu_step(s, x_ref, wg_ref.at[0], wu_ref.at[0], wd_ref.at[0], o_ref, h_ref,
                     rows=k * MOE_SUB, guard=n_sub == k)

    @pl.when((n_sub == 0) & (s >= FF_STEPS))
    def _():
        o_ref[...] = jnp.zeros_like(o_ref)


def _moe_ffn(block_expert, block_valid, xs, w_gate, w_up, w_down):
    n_slots = xs.shape[0]
    tm = MOE_BLOCK

    def ff(i, s, be, bv):
        return (be[i], 0, jnp.where(bv[i] > 0, jnp.minimum(s, FF_STEPS - 1), FF_STEPS - 1))

    def down(i, s, be, bv):
        return (be[i], 0, jnp.where(bv[i] > 0, jnp.maximum(s - FF_STEPS, 0), OUT_STEPS - 1))

    return pl.pallas_call(
        _moe_ffn_kernel,
        out_shape=_SDS((n_slots, D_MODEL), F32),
        grid_spec=pltpu.PrefetchScalarGridSpec(
            num_scalar_prefetch=2,
            grid=(n_slots // tm, FF_STEPS + OUT_STEPS),
            in_specs=[pl.BlockSpec((tm, D_MODEL), lambda i, s, be, bv: (i, 0), pipeline_mode=pl.Buffered(1)),
                      pl.BlockSpec((1, D_MODEL, FF_TILE), ff),
                      pl.BlockSpec((1, D_MODEL, FF_TILE), ff),
                      pl.BlockSpec((1, D_FF, OUT_TILE), down)],
            out_specs=pl.BlockSpec((tm, OUT_TILE), lambda i, s, be, bv: (i, jnp.maximum(s - FF_STEPS, 0))),
            scratch_shapes=[pltpu.VMEM((tm, D_FF), BF16)]),
        compiler_params=_cparams(("arbitrary", "arbitrary"), SWIGLU_VMEM_LIMIT),
        name="moe_grouped_swiglu",
    )(block_expert, block_valid, xs, w_gate, w_up, w_down)


def _combine_kernel(pos_ref, y_hbm, wt_ref, h_ref, gp_ref, gate_ref, o_ref, buf_a, buf_b, sem):
    i = pl.program_id(0)
    n = pl.num_programs(0)
    base = i * GATHER_TILE

    def start_tile(t, slot):
        def issue(r, carry):
            a = 2 * (t * GATHER_TILE + r)
            _row_copy(y_hbm, pos_ref[a], buf_a.at[slot], r, sem.at[slot]).start(priority=0)
            _row_copy(y_hbm, pos_ref[a + 1], buf_b.at[slot], r, sem.at[slot]).start(priority=1)
            return carry

        lax.fori_loop(0, GATHER_TILE, issue, 0, unroll=8)

    @pl.when(i == 0)
    def _():
        start_tile(0, 0)

    @pl.when(i + 1 < n)
    def _():
        start_tile(i + 1, (i + 1) % 2)

    slot = i % 2

    def drain(r, carry):
        _row_copy(y_hbm, 0, buf_a.at[slot], r, sem.at[slot]).wait()
        _row_copy(y_hbm, 0, buf_b.at[slot], r, sem.at[slot]).wait()
        return carry

    lax.fori_loop(0, GATHER_TILE, drain, 0, unroll=8)
    wt = wt_ref[...]
    y = wt[:, 0:1] * buf_a[slot] + wt[:, 1:2] * buf_b[slot]
    grp = _group_of_row(base)
    o_ref[...] = h_ref[...] + _mod_row(gate_ref, grp) * _rms(y, gp_ref[...])


def _moe_combine(pos, ys, wts, h, g_post, mod, gate_lk):
    n = wts.shape[0]
    tm = GATHER_TILE
    return pl.pallas_call(
        _combine_kernel,
        out_shape=_SDS((n, D_MODEL), F32),
        grid_spec=pltpu.PrefetchScalarGridSpec(
            num_scalar_prefetch=1,
            grid=(n // tm,),
            in_specs=[pl.BlockSpec(memory_space=pl.ANY),
                      pl.BlockSpec((tm, 128), lambda i, pos: (i, 0)),
                      pl.BlockSpec((tm, D_MODEL), lambda i, pos: (i, 0)),
                      pl.BlockSpec((1, D_MODEL), lambda i, pos: (0, 0)),
                      pl.BlockSpec((1, 8, D_MODEL), lambda i, pos: (gate_lk[0], 0, gate_lk[1]))],
            out_specs=pl.BlockSpec((tm, D_MODEL), lambda i, pos: (i, 0)),
            scratch_shapes=[pltpu.VMEM((2, tm, D_MODEL), F32), pltpu.VMEM((2, tm, D_MODEL), F32),
                            pltpu.SemaphoreType.DMA((2,))]),
        compiler_params=_cparams(("arbitrary",)),
        name="moe_combine",
    )(pos, ys, wts, h, g_post.reshape(1, D_MODEL), mod)


def _moe_routing(idx):
    flat_e = idx[:, :TOP_K].reshape(-1)
    n_assign = flat_e.shape[0]
    onehot = (flat_e[:, None] == jnp.arange(N_EXPERTS, dtype=I32)[None, :]).astype(I32)
    csum = jnp.cumsum(onehot, axis=0)
    counts = csum[-1]
    padded = (counts + MOE_BLOCK - 1) // MOE_BLOCK * MOE_BLOCK
    pad_end = jnp.cumsum(padded)
    pad_start = pad_end - padded
    pos = jnp.sum(onehot * (pad_start[None, :] + csum - 1), axis=1).astype(I32)
    slot_tok = jnp.zeros((MOE_SLOTS,), I32).at[pos].set(jnp.arange(n_assign, dtype=I32) // TOP_K)
    slot = jnp.arange(MOE_SLOTS, dtype=I32)
    used_end = pad_start + counts
    slot_valid = jnp.any((slot[:, None] >= pad_start[None, :]) & (slot[:, None] < used_end[None, :]), axis=1)
    blk_start = jnp.arange(MOE_SLOTS // MOE_BLOCK, dtype=I32) * MOE_BLOCK
    last_start = jnp.maximum(pad_end[-1] - MOE_BLOCK, 0)
    blk_e = jnp.searchsorted(pad_end, jnp.minimum(blk_start, last_start), side='right').astype(I32)
    blk_e = jnp.minimum(blk_e, N_EXPERTS - 1)
    rows_used = jnp.clip(used_end[blk_e] - blk_start, 0, MOE_BLOCK)
    blk_sub = ((rows_used + MOE_SUB - 1) // MOE_SUB).astype(I32)
    return pos, slot_tok, slot_valid.astype(F32), blk_e, blk_sub


def _rest_of_w_in_t(w_in_t):
    dn_in = P_MAIN + 4 * DN_HEADS
    ba = w_in_t[P_MAIN:dn_in]
    cq = w_in_t[dn_in:dn_in + MLA_Q_RANK]
    ckv = w_in_t[dn_in + MLA_Q_RANK:dn_in + MLA_Q_RANK + MLA_KV_RANK]
    k_rope = w_in_t[dn_in + MLA_Q_RANK + MLA_KV_RANK:]
    pad = jnp.zeros((R_CQ - R_MISC - MLA_ROPE - 4 * DN_HEADS, D_MODEL), w_in_t.dtype)
    return jnp.concatenate([ckv, k_rope, ba, pad, cq], axis=0)


def _rearrange_w_qb(w_qb):
    w = w_qb.reshape(MLA_Q_RANK, MLA_HEADS, MLA_NOPE + MLA_ROPE)
    w = jnp.pad(w, ((0, 0), (0, 0), (0, MLA_QK - MLA_NOPE - MLA_ROPE)))
    return w.reshape(MLA_Q_RANK, MLA_HEADS * MLA_QK).astype(BF16)


def kernel(x, c, ctx, c_ctx, ada_w, ada_b, norm_g, ab_w_in, dn_conv_w, dn_a_log, dn_dt_bias, dn_norm_g, mla_q_norm_g, mla_w_qb, mla_kv_norm_g, mla_w_kvb, ab_w_out, ffn_w_gate, ffn_w_up, ffn_w_down, gqa_w_qkv, gqa_b_qkv, gqa_sink, gqa_w_out, gqa_b_out, moe_w_router, moe_w_gate, moe_w_up, moe_w_down):
    assert x.shape == (BATCH, SEQ, D_MODEL) and ctx.shape == (BATCH, CTX_LEN, D_MODEL) and ada_w.shape[0] == 2
    h = (x.reshape(N_LAT, D_MODEL), ctx.reshape(N_CTX, D_MODEL))
    cvec = jnp.zeros((8, D_MODEL), F32).at[:BATCH].set(c).at[BATCH].set(c_ctx)
    mod = _ada(cvec, ada_w, ada_b)
    (cos1, sin1), (cos2, sin2) = _rope_tables()

    u = _modulate_in(*h, norm_g[0, 0], mod, 0, 0, 1)
    w_in_t = jnp.swapaxes(ab_w_in, 1, 2)
    p = _matmul(u, w_in_t, None, BF16, MM_ROW_TILE, 1024, n=P_MAIN, w_is_transposed=True)
    p_rest = _matmul(u, _rest_of_w_in_t(w_in_t[0]), None, F32, MM_ROW_TILE, R_WIDTH // 2, w_is_transposed=True)
    o_f, o_b = _deltanet(_dn_conv(p, dn_conv_w[0]), _dn_gates(p_rest, dn_a_log[0], dn_dt_bias[0]))
    q, k, v = _mla_proj(p_rest, mla_q_norm_g[0], mla_kv_norm_g[0], _rearrange_w_qb(mla_w_qb[0]),
                        mla_w_kvb[0].astype(BF16), cos1, sin1)
    a = _ab_mix(o_f, o_b, p, dn_norm_g[0], _mla_attention(q, k, v))
    y = _matmul(a, ab_w_out[0].astype(BF16), None, BF16, MM_ROW_TILE, 1024)
    h, u = _post(h, y, norm_g[0, 1], norm_g[0, 2], mod, (0, 2), (0, 3), (0, 4), BF16)
    y = _ffn(u, ffn_w_gate[0], ffn_w_up[0], ffn_w_down[0])
    h, u = _post(h, y, norm_g[0, 3], norm_g[1, 0], mod, (0, 5), (1, 0), (1, 1), BF16)

    qkv = _gqa_qkv(u, gqa_w_qkv[0].astype(BF16), gqa_b_qkv[0], cos2, sin2)
    o = _gqa_attention(qkv, gqa_sink[0])
    y = _matmul(o, gqa_w_out[0].astype(BF16), gqa_b_out[0], BF16, 1024, 1024)
    h, u, idx, wts = _post(h, y, norm_g[1, 1], norm_g[1, 2], mod, (1, 2), (1, 3), (1, 4), F32, moe_w_router[0])
    pos, slot_tok, slot_valid, blk_e, blk_sub = _moe_routing(idx)
    xs = _moe_gather(slot_tok, slot_valid, u)
    ys = _moe_ffn(blk_e, blk_sub, xs, moe_w_gate[0], moe_w_up[0], moe_w_down[0])
    out = _moe_combine(pos, ys, wts, h, norm_g[1, 3], mod, (1, 5))
    return out.reshape(BATCH, SEQ, D_MODEL)
```

```python
import functools
import math

import jax
import jax.numpy as jnp
from jax import lax
from jax.experimental import pallas as pl
from jax.experimental.pallas import tpu as pltpu

F32 = jnp.float32
BF16 = jnp.bfloat16
I32 = jnp.int32

D_MODEL = 2048
BATCH = 2
SEQ = 4096
CTX_LEN = 256
GRID_W = 64
EPS = 1e-6
ROPE_THETA = 10000.0

DN_HEADS = 8
DN_HEAD_DIM = 128
DN_CONV = 5
DN_CHUNK = 64
MLA_HEADS = 8
MLA_Q_RANK = 768
MLA_KV_RANK = 512
MLA_NOPE = 128
MLA_ROPE = 64
MLA_V = 128
GQA_HEADS = 32
GQA_KV_HEADS = 4
GQA_HEAD_DIM = 64
WINDOW = 128
Q_BLOCK = 128
D_FF = 7168
N_EXPERTS = 8
TOP_K = 2

DN_W = DN_HEADS * DN_HEAD_DIM
DN_QKV = 3 * DN_W
N_LAT = BATCH * SEQ
N_CTX = BATCH * CTX_LEN
N_TOK = N_LAT + N_CTX
GQA_Q = GQA_HEADS * GQA_HEAD_DIM
GQA_KV = GQA_KV_HEADS * GQA_HEAD_DIM

P_QKV = 0
P_Z = DN_QKV
P_MAIN = P_Z + DN_W
R_CKV = 0
R_MISC = MLA_KV_RANK
R_CQ = MLA_Q_RANK
R_WIDTH = R_CQ + MLA_Q_RANK

ROW_TILE = 512
MM_ROW_TILE = 1088
MOE_BLOCK = 1024
MOE_SUB = 128
MOE_SLOTS = N_LAT * TOP_K + N_EXPERTS * MOE_BLOCK
VMEM_LIMIT = 56 * 1024 * 1024

_SDS = jax.ShapeDtypeStruct
_HIGH = lax.Precision.HIGHEST


def _cparams(sem, vmem=None):
    return pltpu.CompilerParams(dimension_semantics=sem, vmem_limit_bytes=vmem)


def _dot(a, b):
    return jnp.dot(a, b, preferred_element_type=F32)


def _dot_nt(a, b):
    return lax.dot_general(a, b, (((1,), (1,)), ((), ())), preferred_element_type=F32)


def _dot_tn(a, b):
    return lax.dot_general(a, b, (((0,), (0,)), ((), ())), preferred_element_type=F32)


def _dot_hi(a, b):
    return jnp.dot(a, b, preferred_element_type=F32, precision=_HIGH)


def _silu(x):
    return x * jax.nn.sigmoid(x)


def _rms(x, g):
    return x * lax.rsqrt(jnp.mean(x * x, axis=-1, keepdims=True) + EPS) * g


def _group_of_row(r0):
    return (r0 >= SEQ).astype(I32) + (r0 >= 2 * SEQ).astype(I32)


def _mod_spec(layer, k):
    return pl.BlockSpec((1, 8, D_MODEL), lambda *_: (layer, 0, k))


def _mod_row(ref, grp):
    return ref[0, pl.ds(grp, 1), :]


def _ada_kernel(c_ref, w_ref, b_ref, o_ref):
    s = _silu(c_ref[...]).astype(BF16)
    o_ref[0] = _dot(s, w_ref[0].astype(BF16)) + b_ref[0]


def _ada(cvec, ada_w, ada_b):
    n_layers, _, n = ada_w.shape
    tn = 1024
    return pl.pallas_call(
        _ada_kernel,
        out_shape=_SDS((n_layers, 8, n), F32),
        grid=(n_layers, n // tn),
        in_specs=[pl.BlockSpec((8, D_MODEL), lambda l, j: (0, 0)),
                  pl.BlockSpec((1, D_MODEL, tn), lambda l, j: (l, 0, j)),
                  pl.BlockSpec((1, 1, tn), lambda l, j: (l, 0, j))],
        out_specs=pl.BlockSpec((1, 8, tn), lambda l, j: (l, 0, j)),
        compiler_params=_cparams(("parallel", "parallel")),
        name="ada_mod",
    )(cvec, ada_w, ada_b.reshape(n_layers, 1, n))


_LAT_TILES = N_LAT // ROW_TILE
assert N_CTX == ROW_TILE


def _split_specs(width, col=0):
    lat = pl.BlockSpec((ROW_TILE, width), lambda i, *_: (jnp.minimum(i, _LAT_TILES - 1), col))
    ctx = pl.BlockSpec((ROW_TILE, width), lambda i, *_: (0, col))
    return [lat, ctx]


def _split_read(lat_ref, ctx_ref, sl=slice(None)):
    return jnp.where(pl.program_id(0) == _LAT_TILES, ctx_ref[:, sl], lat_ref[:, sl])


def _modin_kernel(x_ref, c_ref, g_ref, sh_ref, sc_ref, u_ref):
    grp = _group_of_row(pl.program_id(0) * ROW_TILE)
    h = _split_read(x_ref, c_ref)
    u = _rms(h, g_ref[...]) * (1.0 + _mod_row(sc_ref, grp)) + _mod_row(sh_ref, grp)
    u_ref[...] = u.astype(u_ref.dtype)


def _modulate_in(x, ctx, g, mod, layer, k_shift, k_scale):
    row = pl.BlockSpec((ROW_TILE, D_MODEL), lambda i: (i, 0))
    vec = pl.BlockSpec((1, D_MODEL), lambda i: (0, 0))
    return pl.pallas_call(
        _modin_kernel,
        out_shape=_SDS((N_TOK, D_MODEL), BF16),
        grid=(N_TOK // ROW_TILE,),
        in_specs=_split_specs(D_MODEL) + [vec, _mod_spec(layer, k_shift), _mod_spec(layer, k_scale)],
        out_specs=row,
        compiler_params=_cparams(("parallel",)),
        name="modulate_in",
    )(x, ctx, g.reshape(1, D_MODEL), mod, mod)


def _route_top2(logits):
    lane = lax.broadcasted_iota(I32, logits.shape, 1)
    neg = -jnp.inf
    l0 = jnp.where(lane < N_EXPERTS, logits, neg)
    m1 = jnp.max(l0, axis=-1, keepdims=True)
    i1 = jnp.min(jnp.where(l0 == m1, lane, 128), axis=-1, keepdims=True)
    l1 = jnp.where(lane == i1, neg, l0)
    m2 = jnp.max(l1, axis=-1, keepdims=True)
    i2 = jnp.min(jnp.where(l1 == m2, lane, 128), axis=-1, keepdims=True)
    e = jnp.exp(m2 - m1)
    w1 = 1.0 / (1.0 + e)
    idx = jnp.where(lane == 0, i1, jnp.where(lane == 1, i2, 0))
    wts = jnp.where(lane == 0, w1, jnp.where(lane == 1, e * w1, 0.0))
    return idx, wts


U32 = jnp.uint32


def _pack_bf16_halves(x):
    n = x.shape[1] // 2
    lo = lax.bitcast_convert_type(x[:, :n].astype(F32), U32) >> 16
    hi = (lax.bitcast_convert_type(x[:, n:].astype(F32), U32) >> 16) << 16
    return hi | lo


def _unpack_bf16_halves(w):
    lo = lax.bitcast_convert_type(w << 16, F32).astype(BF16)
    hi = lax.bitcast_convert_type((w >> 16) << 16, F32).astype(BF16)
    return lo, hi


def _post_kernel(*refs, split, route):
    refs = list(refs)
    if split:
        h = _split_read(refs.pop(0), refs.pop(0))
    else:
        h = refs.pop(0)[...]
    y_ref, gp_ref, gate_ref, gn_ref, sh_ref, sc_ref = refs[:6]
    refs = refs[6:]
    wr_ref = refs.pop(0) if route else None
    hn_ref, u_ref = refs[:2]
    grp = _group_of_row(pl.program_id(0) * ROW_TILE)
    hn = h + _mod_row(gate_ref, grp) * _rms(y_ref[...].astype(F32), gp_ref[...])
    hn_ref[...] = hn
    u = _rms(hn, gn_ref[...]) * (1.0 + _mod_row(sc_ref, grp)) + _mod_row(sh_ref, grp)
    if route:
        ub = u.astype(BF16)
        u_ref[...] = _pack_bf16_halves(ub)
        idx_ref, wt_ref = refs[2:4]
        idx_ref[...], wt_ref[...] = _route_top2(_dot(ub, wr_ref[...]))
    else:
        u_ref[...] = u.astype(u_ref.dtype)


def _post(h, y, g_post, g_next, mod, gate_lk, shift_lk, scale_lk, u_dtype, w_router=None):
    n = y.shape[0]
    split = isinstance(h, tuple)
    route = w_router is not None
    row = pl.BlockSpec((ROW_TILE, D_MODEL), lambda i: (i, 0))
    vec = pl.BlockSpec((1, D_MODEL), lambda i: (0, 0))
    slab = pl.BlockSpec((ROW_TILE, 128), lambda i: (i, 0))
    h_specs, h_args = (_split_specs(D_MODEL), list(h)) if split else ([row], [h])
    extra_specs, extra_args, extra_out, extra_out_specs = [], [], (), ()
    u_shape, u_spec = _SDS((n, D_MODEL), u_dtype), row
    if route:
        w = jnp.zeros((D_MODEL, 128), BF16).at[:, :N_EXPERTS].set(w_router.astype(BF16))
        extra_specs, extra_args = [pl.BlockSpec((D_MODEL, 128), lambda i: (0, 0))], [w]
        extra_out, extra_out_specs = (_SDS((n, 128), I32), _SDS((n, 128), F32)), (slab, slab)
        u_shape, u_spec = _SDS((n, D_MODEL // 2), U32), pl.BlockSpec((ROW_TILE, D_MODEL // 2), lambda i: (i, 0))
    return pl.pallas_call(
        functools.partial(_post_kernel, split=split, route=route),
        out_shape=(_SDS((n, D_MODEL), F32), u_shape) + extra_out,
        grid=(n // ROW_TILE,),
        in_specs=h_specs + [row, vec, _mod_spec(*gate_lk), vec, _mod_spec(*shift_lk), _mod_spec(*scale_lk)] + extra_specs,
        out_specs=(row, u_spec) + extra_out_specs,
        compiler_params=_cparams(("parallel",)),
        name="residual_post",
    )(*h_args, y, g_post.reshape(1, D_MODEL), mod, g_next.reshape(1, D_MODEL), mod, mod, *extra_args)


def _mm_kernel(a_ref, w_ref, b_ref, o_ref, *, w_is_transposed):
    dot = _dot_nt if w_is_transposed else _dot
    o_ref[...] = (dot(a_ref[...], w_ref[...].astype(BF16)) + b_ref[...]).astype(o_ref.dtype)


def _matmul(a, w, bias, out_dtype, tm, tn, n=None, w_is_transposed=False):
    m, k = a.shape
    if w_is_transposed:
        n = w.shape[-2] if n is None else n
        if w.ndim == 3:
            w_spec = pl.BlockSpec((None, tn, k), lambda i, j: (0, j, 0))
        else:
            w_spec = pl.BlockSpec((tn, k), lambda i, j: (j, 0))
    else:
        n = w.shape[-1] if n is None else n
        w_spec = pl.BlockSpec((k, tn), lambda i, j: (0, j))
    if bias is None:
        bias = jnp.zeros((n,), F32)
    return pl.pallas_call(
        functools.partial(_mm_kernel, w_is_transposed=w_is_transposed),
        out_shape=_SDS((m, n), out_dtype),
        grid=(m // tm, n // tn),
        in_specs=[pl.BlockSpec((tm, k), lambda i, j: (i, 0)),
                  w_spec,
                  pl.BlockSpec((1, tn), lambda i, j: (0, j))],
        out_specs=pl.BlockSpec((tm, tn), lambda i, j: (i, j)),
        compiler_params=_cparams(("parallel", "arbitrary"), VMEM_LIMIT),
        name="matmul",
    )(a, w, bias.reshape(1, n))


FF_TILE = 512
SWIGLU_VMEM_LIMIT = 60 * 1024 * 1024
OUT_TILE = 256
FF_STEPS = D_FF // FF_TILE
OUT_STEPS = D_MODEL // OUT_TILE


def _swiglu_step(s, x_ref, wg_ref, wu_ref, wd_ref, o_ref, h_ref, rows=None, guard=True):
    total = x_ref.shape[0]
    rows = total if rows is None else rows

    @pl.when(guard & (s < FF_STEPS))
    def _():
        x = x_ref[0:rows, :]
        a = _dot(x, wg_ref[...].astype(BF16))
        b = _dot(x, wu_ref[...].astype(BF16))
        h_ref[0:rows, pl.ds(pl.multiple_of(s * FF_TILE, FF_TILE), FF_TILE)] = (_silu(a) * b).astype(BF16)

    @pl.when(guard & (s >= FF_STEPS))
    def _():
        o_ref[0:rows, :] = _dot(h_ref[0:rows, :], wd_ref[...].astype(BF16)).astype(o_ref.dtype)
        if rows < total:
            o_ref[rows:total, :] = jnp.zeros((total - rows, o_ref.shape[1]), o_ref.dtype)


def _ffn_kernel(u_ref, wg_ref, wu_ref, wd_ref, o_ref, h_ref):
    _swiglu_step(pl.program_id(1), u_ref, wg_ref, wu_ref, wd_ref, o_ref, h_ref)


def _ffn(u, w_gate, w_up, w_down):
    m = u.shape[0]
    tm = MM_ROW_TILE

    def ff(i, s):
        return (0, jnp.minimum(s, FF_STEPS - 1))

    def out(s):
        return jnp.maximum(s - FF_STEPS, 0)

    return pl.pallas_call(
        _ffn_kernel,
        out_shape=_SDS((m, D_MODEL), BF16),
        grid=(m // tm, FF_STEPS + OUT_STEPS),
        in_specs=[pl.BlockSpec((tm, D_MODEL), lambda i, s: (i, 0), pipeline_mode=pl.Buffered(1)),
                  pl.BlockSpec((D_MODEL, FF_TILE), ff),
                  pl.BlockSpec((D_MODEL, FF_TILE), ff),
                  pl.BlockSpec((D_FF, OUT_TILE), lambda i, s: (0, out(s)))],
        out_specs=pl.BlockSpec((tm, OUT_TILE), lambda i, s: (i, out(s))),
        scratch_shapes=[pltpu.VMEM((tm, D_FF), BF16)],
        compiler_params=_cparams(("parallel", "arbitrary"), SWIGLU_VMEM_LIMIT),
        name="swiglu_ffn",
    )(u, w_gate, w_up, w_down)


CONV_TILE = 256
CONV_HALO = 16
_SEQ_STARTS = (0, SEQ // CONV_TILE, 2 * SEQ // CONV_TILE, 2 * SEQ // CONV_TILE + 1)
_SEQ_LASTS = (SEQ // CONV_TILE - 1, 2 * SEQ // CONV_TILE - 1, 2 * SEQ // CONV_TILE, 2 * SEQ // CONV_TILE + 1)


def _conv_kernel(prev_ref, cur_ref, next_ref, w_ref, o_ref, buf):
    i = pl.program_id(0)
    j = pl.program_id(1)
    tm = CONV_TILE
    first = functools.reduce(jnp.logical_or, [i == s for s in _SEQ_STARTS])
    last = functools.reduce(jnp.logical_or, [i == s for s in _SEQ_LASTS])
    hb = CONV_HALO
    buf[0:hb, :] = jnp.where(first, 0.0, prev_ref[...].astype(F32))
    buf[hb:hb + tm, :] = cur_ref[...].astype(F32)
    buf[hb + tm:2 * hb + tm, :] = jnp.where(last, 0.0, next_ref[...].astype(F32))
    pad = DN_CONV // 2
    acc = buf[hb - pad:hb - pad + tm, :] * w_ref[0:1, :]
    for d in range(1, DN_CONV):
        acc = acc + buf[hb - pad + d:hb - pad + d + tm, :] * w_ref[d:d + 1, :]
    x = _silu(acc)
    q_scale = jnp.where(j == 0, DN_HEAD_DIM ** -0.5, 1.0)
    for hh in range(DN_HEADS):
        xh = x[:, hh * DN_HEAD_DIM:(hh + 1) * DN_HEAD_DIM]
        inv = lax.rsqrt(jnp.sum(xh * xh, axis=-1, keepdims=True) + EPS) * q_scale
        o_ref[:, hh * DN_HEAD_DIM:(hh + 1) * DN_HEAD_DIM] = xh * jnp.where(j == 2, 1.0, inv)


def _dn_conv(p, conv_w):
    n = p.shape[0]
    tm = CONV_TILE
    hb = CONV_HALO
    per_tile = tm // hb
    n_halo = n // hb
    return pl.pallas_call(
        _conv_kernel,
        out_shape=_SDS((n, DN_QKV), F32),
        grid=(n // tm, 3),
        in_specs=[pl.BlockSpec((hb, DN_W), lambda i, j: (jnp.maximum(i * per_tile - 1, 0), j)),
                  pl.BlockSpec((tm, DN_W), lambda i, j: (i, j)),
                  pl.BlockSpec((hb, DN_W), lambda i, j: (jnp.minimum((i + 1) * per_tile, n_halo - 1), j)),
                  pl.BlockSpec((DN_CONV, DN_W), lambda i, j: (0, j))],
        out_specs=pl.BlockSpec((tm, DN_W), lambda i, j: (i, j)),
        scratch_shapes=[pltpu.VMEM((tm + 2 * hb, DN_W), F32)],
        compiler_params=_cparams(("parallel", "parallel")),
        name="dn_conv",
    )(p, p, p, conv_w)


def _gates_kernel(m_ref, a_ref, dt_ref, o_ref):
    x = pltpu.roll(m_ref[...], 64, 1)
    lane = lax.broadcasted_iota(I32, x.shape, 1)
    z = x + dt_ref[...]
    softplus = jnp.maximum(z, 0.0) + jnp.log1p(jnp.exp(-jnp.abs(z)))
    g = a_ref[...] * softplus
    o_ref[...] = jnp.where(lane < 2 * DN_HEADS, jax.nn.sigmoid(x), jnp.where(lane < 4 * DN_HEADS, g, 0.0))


def _dn_gates(p, a_log, dt_bias):
    n = p.shape[0]
    neg_a = jnp.zeros((1, 128), F32).at[0, 2 * DN_HEADS:4 * DN_HEADS].set(-jnp.exp(a_log.astype(F32)).reshape(-1))
    dtb = jnp.zeros((1, 128), F32).at[0, 2 * DN_HEADS:4 * DN_HEADS].set(dt_bias.astype(F32).reshape(-1))
    tm = ROW_TILE
    vec = pl.BlockSpec((1, 128), lambda i: (0, 0))
    return pl.pallas_call(
        _gates_kernel,
        out_shape=_SDS((n, 128), F32),
        grid=(n // tm,),
        in_specs=[pl.BlockSpec((tm, 128), lambda i: (i, R_MISC // 128)), vec, vec],
        out_specs=pl.BlockSpec((tm, 128), lambda i: (i, 0)),
        compiler_params=_cparams(("parallel",)),
        name="dn_gates",
    )(p, neg_a, dtb)


_CTX_CHUNKS = CTX_LEN // DN_CHUNK
_LAT_CHUNKS = SEQ // DN_CHUNK
_DN_STEPS = _CTX_CHUNKS + _LAT_CHUNKS


def _dn_fwd_block(b, t):
    return jnp.where(t < _CTX_CHUNKS, (2 * SEQ + b * CTX_LEN) // DN_CHUNK + t, b * _LAT_CHUNKS + (t - _CTX_CHUNKS))


def _dn_bwd_block(b, t):
    return jnp.where(t < _CTX_CHUNKS, (2 * SEQ + b * CTX_LEN) // DN_CHUNK + (_CTX_CHUNKS - 1 - t),
                     b * _LAT_CHUNKS + (_DN_STEPS - 1 - t))


DN_GROUP = 2
_GROUP_ROWS = DN_GROUP * DN_CHUNK
_GROUP_STATE = DN_GROUP * DN_HEAD_DIM


_DN_BASE = 8


def _block_diag(x):
    zero = jnp.zeros((DN_CHUNK, DN_HEAD_DIM), x.dtype)
    rows = []
    for hh in range(DN_GROUP):
        piece = x[hh * DN_CHUNK:(hh + 1) * DN_CHUNK]
        rows.append(jnp.concatenate([piece if j == hh else zero for j in range(DN_GROUP)], axis=1))
    return jnp.concatenate(rows, axis=0)


def _dn_groups(items, eye, base_mask):
    it = items
    n = range(len(it))
    gx = [jnp.broadcast_to(it[g]["gc"], (_GROUP_ROWS, _GROUP_ROWS)) for g in n]
    gamma = [jnp.where(it[g]["incl"], jnp.exp(jnp.where(it[g]["incl"], gx[g] - gx[g].T, 0.0)), 0.0) for g in n]
    kb = [it[g]["k"].astype(BF16) for g in n]
    kk = [_dot_nt(kb[g], kb[g]) for g in n]
    a = [jnp.where(it[g]["strict"], it[g]["beta"] * kk[g] * gamma[g], 0.0) for g in n]
    d = [jnp.where(base_mask, a[g], 0.0) for g in n]
    tinv = [eye - d[g] for g in n]
    pw = [d[g].astype(BF16) for g in n]
    for _ in range(int(math.log2(_DN_BASE)) - 1):
        pw = [_dot(pw[g], pw[g]).astype(BF16) for g in n]
        tinv = [tinv[g] + _dot(tinv[g].astype(BF16), pw[g]) for g in n]
    for lvl in range(len(it[0]["levels"])):
        tb = [tinv[g].astype(BF16) for g in n]
        mt = [_dot(jnp.where(it[g]["levels"][lvl], a[g], 0.0).astype(BF16), tb[g]).astype(BF16) for g in n]
        tinv = [tinv[g] - _dot(tb[g], mt[g]) for g in n]
    tb = [tinv[g].astype(BF16) for g in n]
    egc = [jnp.exp(it[g]["gc"]) for g in n]
    rhs = [jnp.concatenate([it[g]["v"] * it[g]["beta"], it[g]["k"] * (it[g]["beta"] * egc[g])], axis=1) for g in n]
    rhs_hi = [rhs[g].astype(BF16) for g in n]
    rhs_lo = [(rhs[g] - rhs_hi[g].astype(F32)).astype(BF16) for g in n]
    sol = [_dot(jnp.concatenate([tb[g], tb[g]], axis=1), jnp.concatenate([rhs_hi[g], rhs_lo[g]], axis=0)) for g in n]
    qk = [(_dot_nt(it[g]["q"].astype(BF16), kb[g]) * gamma[g]).astype(BF16) for g in n]
    s = [it[g]["s_ref"][...] for g in n]
    sb = [s[g].astype(BF16) for g in n]
    v_new = [sol[g][:, :DN_HEAD_DIM] - _dot(_block_diag(sol[g][:, DN_HEAD_DIM:].astype(BF16)), sb[g]) for g in n]
    vb = [v_new[g].astype(BF16) for g in n]
    o = [_dot(_block_diag((it[g]["q"] * egc[g]).astype(BF16)), sb[g]) + _dot(qk[g], vb[g]) for g in n]
    for g in n:
        k_dec = _block_diag((it[g]["k"] * jnp.exp(it[g]["g_last"] - it[g]["gc"])).astype(BF16))
        it[g]["s_ref"][...] = s[g] * jnp.exp(it[g]["g_last_state"]) + _dot_tn(k_dec, vb[g])
    return o


def _stack_cols(x, lanes, rows_each):
    pieces = [x[:, l:l + 1] for l in lanes]
    if rows_each is not None:
        pieces = [jnp.broadcast_to(pc, (rows_each, 1)) for pc in pieces]
    return jnp.concatenate(pieces, axis=0)


def _dn_kernel(*refs):
    n_in = BATCH * 2 * 4
    in_refs, (ofl_ref, ofc_ref, obl_ref, obc_ref), (s_ref, o_scr) = refs[:n_in], refs[n_in:n_in + 4], refs[n_in + 4:]
    t = pl.program_id(0)

    @pl.when(t == 0)
    def _():
        s_ref[...] = jnp.zeros_like(s_ref)

    c = DN_CHUNK
    ri = lax.broadcasted_iota(I32, (_GROUP_ROWS, _GROUP_ROWS), 0)
    ci = lax.broadcasted_iota(I32, (_GROUP_ROWS, _GROUP_ROWS), 1)
    same_head = (ri // c) == (ci // c)
    eye = (ri == ci).astype(F32)
    r1 = lax.broadcasted_iota(I32, (c, c), 0)
    c1 = lax.broadcasted_iota(I32, (c, c), 1)
    base_mask = (ri // _DN_BASE) == (ci // _DN_BASE)
    masks = []
    for d in range(2):
        incl = same_head & ((ri >= ci) if d == 0 else (ri <= ci))
        strict = same_head & ((ri > ci) if d == 0 else (ri < ci))
        levels = []
        size = _DN_BASE
        while size < c:
            lo_blk, hi_blk = ((ci, ri) if d == 0 else (ri, ci))
            levels.append(((ri // (2 * size)) == (ci // (2 * size)))
                          & ((lo_blk // size) % 2 == 0) & ((hi_blk // size) % 2 == 1))
            size *= 2
        masks.append((incl, strict, levels))
    items, where = [], []
    for b, d in [(b, d) for b in range(BATCH) for d in range(2)]:
        q_ref, k_ref, v_ref, g_ref = in_refs[(b * 2 + d) * 4:(b * 2 + d) * 4 + 4]
        incl, strict, levels = masks[d]
        gates = g_ref[...]
        tri = ((r1 >= c1) if d == 0 else (r1 <= c1)).astype(BF16)
        g_hi = gates.astype(BF16)
        g_r1 = gates - g_hi.astype(F32)
        g_mid = g_r1.astype(BF16)
        g_lo = (g_r1 - g_mid.astype(F32)).astype(BF16)
        gc = _dot(tri, g_hi) + _dot(tri, g_mid) + _dot(tri, g_lo)
        last_row = c - 1 if d == 0 else 0
        tot = gc[last_row:last_row + 1, :]
        for grp in range(DN_HEADS // DN_GROUP):
            heads = range(grp * DN_GROUP, (grp + 1) * DN_GROUP)
            lanes_b = [d * DN_HEADS + hh for hh in heads]
            lanes_g = [2 * DN_HEADS + lb for lb in lanes_b]

            def stack(ref):
                return jnp.concatenate([ref[:, hh * DN_HEAD_DIM:(hh + 1) * DN_HEAD_DIM] for hh in heads], axis=0)

            items.append(dict(q=stack(q_ref), k=stack(k_ref), v=stack(v_ref),
                              beta=_stack_cols(gates, lanes_b, None), gc=_stack_cols(gc, lanes_g, None),
                              g_last=_stack_cols(tot, lanes_g, c), g_last_state=_stack_cols(tot, lanes_g, DN_HEAD_DIM),
                              incl=incl, strict=strict, levels=levels, s_ref=s_ref.at[b, d, grp]))
            where.append((d, b, heads))
    outs = _dn_groups(items, eye, base_mask)
    for o, (d, b, heads) in zip(outs, where):
        for j, hh in enumerate(heads):
            o_scr[d, b, :, hh * DN_HEAD_DIM:(hh + 1) * DN_HEAD_DIM] = o[j * c:(j + 1) * c]

    @pl.when(t < _CTX_CHUNKS)
    def _():
        ofc_ref[...] = o_scr[0]
        obc_ref[...] = o_scr[1]

    @pl.when(t >= _CTX_CHUNKS)
    def _():
        ofl_ref[...] = o_scr[0]
        obl_ref[...] = o_scr[1]


def _deltanet(qkv, gates):
    c = DN_CHUNK
    ins, args = [], []
    for b in range(BATCH):
        for fn in (_dn_fwd_block, _dn_bwd_block):
            for width, col, arr in ((DN_W, 0, qkv), (DN_W, 1, qkv), (DN_W, 2, qkv), (128, 0, gates)):
                ins.append(pl.BlockSpec((c, width), functools.partial(lambda t, b, fn, col: (fn(b, t), col),
                                                                      b=b, fn=fn, col=col)))
                args.append(arr)
    last = _DN_STEPS - 1
    lat = _SDS((BATCH, SEQ, DN_W), F32)
    ctx = _SDS((BATCH, CTX_LEN, DN_W), F32)
    blk = (BATCH, c, DN_W)
    outs = pl.pallas_call(
        _dn_kernel,
        out_shape=(lat, ctx, lat, ctx),
        grid=(_DN_STEPS,),
        in_specs=ins,
        out_specs=(pl.BlockSpec(blk, lambda t: (0, jnp.maximum(t - _CTX_CHUNKS, 0), 0)),
                   pl.BlockSpec(blk, lambda t: (0, jnp.minimum(t, _CTX_CHUNKS - 1), 0)),
                   pl.BlockSpec(blk, lambda t: (0, jnp.minimum(last - t, _LAT_CHUNKS - 1), 0)),
                   pl.BlockSpec(blk, lambda t: (0, jnp.maximum(_CTX_CHUNKS - 1 - t, 0), 0))),
        scratch_shapes=[pltpu.VMEM((BATCH, 2, DN_HEADS // DN_GROUP, _GROUP_STATE, DN_HEAD_DIM), F32),
                        pltpu.VMEM((2, BATCH, c, DN_W), F32)],
        compiler_params=_cparams(("arbitrary",)),
        name="deltanet",
    )(*args)
    ofl, ofc, obl, obc = outs
    return ((ofl.reshape(N_LAT, DN_W), ofc.reshape(N_CTX, DN_W)),
            (obl.reshape(N_LAT, DN_W), obc.reshape(N_CTX, DN_W)))


def _rope_tables():
    t = jnp.arange(SEQ)
    row = (t // GRID_W).astype(F32)
    col = (t % GRID_W).astype(F32)
    n_freq = MLA_ROPE // 4
    inv_freq = ROPE_THETA ** (-jnp.arange(n_freq, dtype=F32) / n_freq)
    ang = jnp.concatenate([row[:, None] * inv_freq, col[:, None] * inv_freq], axis=-1)
    cos, sin = jnp.cos(ang), jnp.sin(ang)
    cos64 = jnp.concatenate([cos, cos], axis=-1)
    sin64 = jnp.concatenate([-sin, sin], axis=-1)

    def assemble(lat_cos, lat_sin):
        c = jnp.concatenate([jnp.tile(lat_cos, (BATCH, 1)), jnp.ones((N_CTX, 128), F32)], axis=0)
        s = jnp.concatenate([jnp.tile(lat_sin, (BATCH, 1)), jnp.zeros((N_CTX, 128), F32)], axis=0)
        return c, s

    cos_l = jnp.concatenate([cos64, jnp.ones((SEQ, 64), F32)], axis=-1)
    sin_l = jnp.concatenate([sin64, jnp.zeros((SEQ, 64), F32)], axis=-1)
    one_head = assemble(cos_l, sin_l)
    two_heads = assemble(jnp.tile(cos64, (1, 2)), jnp.tile(sin64, (1, 2)))
    return one_head, two_heads


def _rope128(x, cosv, sinv):
    lane = lax.broadcasted_iota(I32, x.shape, 1)
    swapped = jnp.where((lane % 64) < 32, pltpu.roll(x, 96, 1), pltpu.roll(x, 32, 1))
    return x * cosv + swapped * sinv


MLA_QK = 256
MLA_VX = 256


def _mla_proj_kernel(ckv_ref, cq_ref, misc_ref, gq_ref, gkv_ref, wq_ref, wkv_ref, cos_ref, sin_ref,
                     q_ref, k_ref, v_ref):
    cq = _rms(cq_ref[...].astype(F32), gq_ref[...]).astype(BF16)
    ckv = _rms(ckv_ref[...].astype(F32), gkv_ref[...]).astype(BF16)
    qf = _dot(cq, wq_ref[...])
    kvf = _dot(ckv, wkv_ref[...])
    cosv, sinv = cos_ref[...], sin_ref[...]
    lane = lax.broadcasted_iota(I32, cosv.shape, 1)
    k_rope = _rope128(jnp.where(lane < MLA_ROPE, misc_ref[...], 0.0), cosv, sinv).astype(BF16)
    scale = (MLA_NOPE + MLA_ROPE) ** -0.5
    for hh in range(MLA_HEADS):
        lo, mid, hi = hh * MLA_QK, hh * MLA_QK + 128, (hh + 1) * MLA_QK
        q_ref[:, lo:mid] = (qf[:, lo:mid] * scale).astype(BF16)
        q_ref[:, mid:hi] = (_rope128(qf[:, mid:hi], cosv, sinv) * scale).astype(BF16)
        k_ref[:, lo:mid] = kvf[:, lo:mid].astype(BF16)
        k_ref[:, mid:hi] = k_rope
        v_ref[:, lo:mid] = kvf[:, mid:hi].astype(BF16)
        v_ref[:, mid:hi] = jnp.ones((kvf.shape[0], MLA_VX - MLA_V), BF16)


def _mla_proj(p_rest, q_norm_g, kv_norm_g, w_q, w_kv, cosv, sinv):
    n = p_rest.shape[0]
    tm = ROW_TILE
    wide = MLA_HEADS * MLA_QK

    def rows(width, col):
        return pl.BlockSpec((tm, width), lambda i: (i, col))

    def whole(shape):
        return pl.BlockSpec(shape, lambda i: (0, 0))

    return pl.pallas_call(
        _mla_proj_kernel,
        out_shape=(_SDS((n, wide), BF16), _SDS((n, wide), BF16), _SDS((n, MLA_HEADS * MLA_VX), BF16)),
        grid=(n // tm,),
        in_specs=[rows(MLA_KV_RANK, R_CKV // MLA_KV_RANK), rows(MLA_Q_RANK, R_CQ // MLA_Q_RANK),
                  rows(128, R_MISC // 128), whole((1, MLA_Q_RANK)), whole((1, MLA_KV_RANK)),
                  whole((MLA_Q_RANK, wide)), whole((MLA_KV_RANK, wide)), rows(128, 0), rows(128, 0)],
        out_specs=(rows(wide, 0), rows(wide, 0), rows(MLA_HEADS * MLA_VX, 0)),
        compiler_params=_cparams(("parallel",), VMEM_LIMIT),
        name="mla_proj",
    )(p_rest, p_rest, p_rest, q_norm_g.reshape(1, -1), kv_norm_g.reshape(1, -1), w_q, w_kv, cosv, sinv)


MLA_KEY_CHUNK = 2048


def _flash_kernel(*refs, chunks):
    n_seg = (len(refs) - 2) // 2
    q_ref, o_ref = refs[0], refs[-1]
    k_refs, v_refs = refs[1:1 + n_seg], refs[1 + n_seg:1 + 2 * n_seg]
    q = q_ref[...]

    def scores(c):
        seg, off, size = c
        return _dot_nt(q, k_refs[seg][off:off + size, :])

    m = jnp.full((q.shape[0], 1), -jnp.inf, F32)
    acc = jnp.zeros((q.shape[0], MLA_VX), F32)
    s_next = scores(chunks[0])
    for j, (seg, off, size) in enumerate(chunks):
        s = s_next
        if j + 1 < len(chunks):
            s_next = scores(chunks[j + 1])
        m_new = jnp.maximum(m, jnp.max(s, axis=-1, keepdims=True))
        p = jnp.exp(s - m_new).astype(BF16)
        acc = jnp.exp(m - m_new) * acc + _dot(p, v_refs[seg][off:off + size, :])
        m = m_new
    o_ref[...] = (acc[:, :MLA_V] / acc[:, MLA_V:]).astype(o_ref.dtype)


def _mla_attention(q, k, v):
    tq = 1024
    nq = SEQ // tq
    ctx_blk = 2 * SEQ // CTX_LEN
    lat_chunks = ((0, 0, CTX_LEN),) + tuple((1, off, MLA_KEY_CHUNK) for off in range(0, SEQ, MLA_KEY_CHUNK))
    lat = pl.pallas_call(
        functools.partial(_flash_kernel, chunks=lat_chunks),
        out_shape=_SDS((N_LAT, MLA_HEADS * MLA_V), BF16),
        grid=(BATCH, MLA_HEADS, nq),
        in_specs=[pl.BlockSpec((tq, MLA_QK), lambda b, h, i: (b * nq + i, h)),
                  pl.BlockSpec((CTX_LEN, MLA_QK), lambda b, h, i: (ctx_blk + b, h)),
                  pl.BlockSpec((SEQ, MLA_QK), lambda b, h, i: (b, h)),
                  pl.BlockSpec((CTX_LEN, MLA_VX), lambda b, h, i: (ctx_blk + b, h)),
                  pl.BlockSpec((SEQ, MLA_VX), lambda b, h, i: (b, h))],
        out_specs=pl.BlockSpec((tq, MLA_V), lambda b, h, i: (b * nq + i, h)),
        compiler_params=_cparams(("parallel", "parallel", "arbitrary"), VMEM_LIMIT),
        name="mla_attn_latent",
    )(q, k, k, v, v)
    ctx = pl.pallas_call(
        functools.partial(_flash_kernel, chunks=((0, 0, CTX_LEN),)),
        out_shape=_SDS((N_CTX, MLA_HEADS * MLA_V), BF16),
        grid=(BATCH, MLA_HEADS),
        in_specs=[pl.BlockSpec((CTX_LEN, MLA_QK), lambda b, h: (ctx_blk + b, h)),
                  pl.BlockSpec((CTX_LEN, MLA_QK), lambda b, h: (ctx_blk + b, h)),
                  pl.BlockSpec((CTX_LEN, MLA_VX), lambda b, h: (ctx_blk + b, h))],
        out_specs=pl.BlockSpec((CTX_LEN, MLA_V), lambda b, h: (b, h)),
        compiler_params=_cparams(("parallel", "parallel")),
        name="mla_attn_context",
    )(q, k, v)
    return lat, ctx


def _ab_mix_kernel(ofl_ref, ofc_ref, obl_ref, obc_ref, ml_ref, mc_ref, z_ref, g_ref, a_ref):
    for hh in range(DN_HEADS):
        sl = slice(hh * DN_HEAD_DIM, (hh + 1) * DN_HEAD_DIM)
        o = _split_read(ofl_ref, ofc_ref, sl) + _split_read(obl_ref, obc_ref, sl)
        a_ref[:, sl] = (_rms(o, g_ref[...]) * _silu(z_ref[:, sl].astype(F32))).astype(BF16)
    a_ref[:, DN_W:] = _split_read(ml_ref, mc_ref)


def _ab_mix(o_f, o_b, p, dn_norm_g, mla_o):
    tm = ROW_TILE
    half = DN_W
    return pl.pallas_call(
        _ab_mix_kernel,
        out_shape=_SDS((N_TOK, 2 * half), BF16),
        grid=(N_TOK // tm,),
        in_specs=_split_specs(half) + _split_specs(half) + _split_specs(half) + [
            pl.BlockSpec((tm, half), lambda i: (i, P_Z // half)),
            pl.BlockSpec((1, DN_HEAD_DIM), lambda i: (0, 0))],
        out_specs=pl.BlockSpec((tm, 2 * half), lambda i: (i, 0)),
        compiler_params=_cparams(("parallel",)),
        name="ab_mix",
    )(*o_f, *o_b, *mla_o, p, dn_norm_g.reshape(1, -1))


def _mm_rope_kernel(a_ref, w_ref, b_ref, cos_ref, sin_ref, o_ref, *, tn, rope_cols):
    y = _dot(a_ref[...], w_ref[...].astype(BF16)) + b_ref[...]
    col0 = pl.program_id(1) * tn
    cosv, sinv = cos_ref[...], sin_ref[...]
    for s in range(tn // 128):
        ys = y[:, s * 128:(s + 1) * 128]
        roped = _rope128(ys, cosv, sinv)
        o_ref[:, s * 128:(s + 1) * 128] = jnp.where(col0 + s * 128 < rope_cols, roped, ys).astype(o_ref.dtype)


def _gqa_qkv(u, w, bias, cosv, sinv):
    m, k = u.shape
    n = w.shape[1]
    tm, tn = MM_ROW_TILE, 1280
    return pl.pallas_call(
        functools.partial(_mm_rope_kernel, tn=tn, rope_cols=GQA_Q + GQA_KV),
        out_shape=_SDS((m, n), BF16),
        grid=(m // tm, n // tn),
        in_specs=[pl.BlockSpec((tm, k), lambda i, j: (i, 0)),
                  pl.BlockSpec((k, tn), lambda i, j: (0, j)),
                  pl.BlockSpec((1, tn), lambda i, j: (0, j)),
                  pl.BlockSpec((tm, 128), lambda i, j: (i, 0)),
                  pl.BlockSpec((tm, 128), lambda i, j: (i, 0))],
        out_specs=pl.BlockSpec((tm, tn), lambda i, j: (i, j)),
        compiler_params=_cparams(("parallel", "arbitrary"), VMEM_LIMIT),
        name="gqa_qkv_rope",
    )(u, w, bias.reshape(1, n), cosv, sinv)


def _gqa_kernel(sink_ref, q_ref, kp_ref, kc_ref, kn_ref, vp_ref, vc_ref, vn_ref, kx_ref, vx_ref, o_ref):
    i = pl.program_id(1)
    nb = SEQ // Q_BLOCK
    n_keys = CTX_LEN + 3 * Q_BLOCK
    groups = GQA_HEADS // GQA_KV_HEADS
    slabs = groups // 2
    dh = GQA_HEAD_DIM
    qi = lax.broadcasted_iota(I32, (Q_BLOCK, n_keys), 0)
    kj = lax.broadcasted_iota(I32, (Q_BLOCK, n_keys), 1) - CTX_LEN
    rel = kj - Q_BLOCK - qi
    valid = (kj < 0) | ((jnp.abs(rel) <= WINDOW) & ((kj >= Q_BLOCK) | (i > 0)) & ((kj < 2 * Q_BLOCK) | (i < nb - 1)))
    bias = jnp.where(valid, 0.0, -jnp.inf).astype(F32)
    bias = jnp.concatenate([bias] * slabs, axis=0)
    k_all = jnp.concatenate([kx_ref[...], kp_ref[...], kc_ref[...], kn_ref[...]], axis=0)
    v_all = jnp.concatenate([vx_ref[...], vp_ref[...], vc_ref[...], vn_ref[...]], axis=0)
    low = lax.broadcasted_iota(I32, (n_keys, 128), 1) < dh

    def block_diag(x_all, kvh):
        pair = x_all[:, (kvh // 2) * 128:(kvh // 2 + 1) * 128].astype(F32)
        other = pltpu.roll(pair, dh, 1)
        first, second = (pair, other) if kvh % 2 == 0 else (other, pair)
        return jnp.concatenate([jnp.where(low, first, 0.0), jnp.where(low, 0.0, second)], axis=0).astype(BF16)

    def scores(kvh):
        q = jnp.concatenate([q_ref[:, (kvh * slabs + r) * 128:(kvh * slabs + r + 1) * 128] for r in range(slabs)], axis=0)
        return _dot_nt(q * (dh ** -0.5), block_diag(k_all, kvh))

    row_lo = lax.broadcasted_iota(I32, (2 * n_keys, 128), 0) < n_keys
    lane_lo = lax.broadcasted_iota(I32, (2 * n_keys, 128), 1) < dh
    ones_cols = jnp.where(row_lo == lane_lo, 1.0, 0.0).astype(BF16)

    def softmax(kvh, s):
        ps, sink_terms = [], []
        for half in range(2):
            sh = s[:, half * n_keys:(half + 1) * n_keys] + bias
            sink = jnp.concatenate([jnp.full((Q_BLOCK, 1), sink_ref[(kvh * slabs + r) * 2 + half], F32)
                                    for r in range(slabs)], axis=0)
            m = jnp.maximum(jnp.max(sh, axis=-1, keepdims=True), sink)
            ps.append(jnp.exp(sh - m).astype(BF16))
            sink_terms.append(jnp.exp(sink - m))
        return jnp.concatenate(ps, axis=1), sink_terms

    def finish(kvh, p, sink_terms):
        o = _dot(p, jnp.concatenate([block_diag(v_all, kvh), ones_cols], axis=1))
        num, den = o[:, :128], o[:, 128:]
        lane_low = lax.broadcasted_iota(I32, num.shape, 1) < dh
        o = num / (den + jnp.where(lane_low, sink_terms[0], sink_terms[1]))
        for r in range(slabs):
            o_ref[:, (kvh * slabs + r) * 128:(kvh * slabs + r + 1) * 128] = o[r * Q_BLOCK:(r + 1) * Q_BLOCK].astype(o_ref.dtype)

    s_next = scores(0)
    for kvh in range(GQA_KV_HEADS):
        s_cur = s_next
        if kvh + 1 < GQA_KV_HEADS:
            s_next = scores(kvh + 1)
        p, dens = softmax(kvh, s_cur)
        finish(kvh, p, dens)


def _gqa_attention(qkv, sink):
    nb = SEQ // Q_BLOCK
    kcol, vcol = GQA_Q // GQA_KV, GQA_Q // GQA_KV + 1
    ctx_blk = 2 * SEQ // CTX_LEN

    def win(col, shift):
        return pl.BlockSpec((Q_BLOCK, GQA_KV), lambda b, i: (b * nb + jnp.clip(i + shift, 0, nb - 1), col))

    return pl.pallas_call(
        _gqa_kernel,
        out_shape=_SDS((N_LAT, GQA_Q), BF16),
        grid=(BATCH, nb),
        in_specs=[pl.BlockSpec(memory_space=pltpu.SMEM),
                  pl.BlockSpec((Q_BLOCK, GQA_Q), lambda b, i: (b * nb + i, 0)),
                  win(kcol, -1), win(kcol, 0), win(kcol, 1), win(vcol, -1), win(vcol, 0), win(vcol, 1),
                  pl.BlockSpec((CTX_LEN, GQA_KV), lambda b, i: (ctx_blk + b, kcol)),
                  pl.BlockSpec((CTX_LEN, GQA_KV), lambda b, i: (ctx_blk + b, vcol))],
        out_specs=pl.BlockSpec((Q_BLOCK, GQA_Q), lambda b, i: (b * nb + i, 0)),
        compiler_params=_cparams(("parallel", "arbitrary")),
        name="gqa_window_attn",
    )(sink.astype(F32), qkv, qkv, qkv, qkv, qkv, qkv, qkv, qkv, qkv)


GATHER_TILE = 256


def _row_copy(src_hbm, row, dst, r, sem):
    return pltpu.make_async_copy(src_hbm.at[pl.ds(row, 1), :], dst.at[pl.ds(r, 1), :], sem)


def _tile_copy(src_hbm, dst, sem):
    return pltpu.make_async_copy(src_hbm.at[pl.ds(0, GATHER_TILE), :], dst, sem)


def _gather_kernel(tok_ref, tv_ref, u_hbm, valid_ref, o_ref, buf, sem):
    i = pl.program_id(0)
    n = pl.num_programs(0)

    def start_tile(t, slot):
        @pl.when(tv_ref[t] > 0)
        def _():
            def issue(r8, carry):
                for j in range(8):
                    r = r8 * 8 + j
                    _row_copy(u_hbm, tok_ref[t * GATHER_TILE + r], buf.at[slot], r, sem.at[slot]).start(priority=j % 2)
                return carry

            lax.fori_loop(0, GATHER_TILE // 8, issue, 0)

    @pl.when(i == 0)
    def _():
        start_tile(0, 0)

    @pl.when(i + 1 < n)
    def _():
        start_tile(i + 1, (i + 1) % 2)

    slot = i % 2

    @pl.when(tv_ref[i] > 0)
    def _():
        _tile_copy(u_hbm, buf.at[slot], sem.at[slot]).wait()
        lo, hi = _unpack_bf16_halves(buf[slot])
        keep = valid_ref[...] > 0.0
        half = D_MODEL // 2
        o_ref[:, :half] = jnp.where(keep, lo, jnp.zeros_like(lo))
        o_ref[:, half:] = jnp.where(keep, hi, jnp.zeros_like(hi))

    @pl.when(tv_ref[i] == 0)
    def _():
        o_ref[...] = jnp.zeros_like(o_ref)


def _moe_gather(slot_tok, slot_valid, u):
    n_slots = slot_tok.shape[0]
    n_tiles = n_slots // GATHER_TILE
    tile_valid = (jnp.max(slot_valid.reshape(n_tiles, GATHER_TILE), axis=1) > 0.0).astype(I32)
    return pl.pallas_call(
        _gather_kernel,
        out_shape=_SDS((n_slots, D_MODEL), BF16),
        grid_spec=pltpu.PrefetchScalarGridSpec(
            num_scalar_prefetch=2,
            grid=(n_tiles,),
            in_specs=[pl.BlockSpec(memory_space=pl.ANY),
                      pl.BlockSpec((GATHER_TILE, 1), lambda i, tok, tv: (i, 0))],
            out_specs=pl.BlockSpec((GATHER_TILE, D_MODEL), lambda i, tok, tv: (i, 0)),
            scratch_shapes=[pltpu.VMEM((2, GATHER_TILE, D_MODEL // 2), U32), pltpu.SemaphoreType.DMA((2,))]),
        compiler_params=_cparams(("arbitrary",)),
        name="moe_gather",
    )(slot_tok, tile_valid, u, slot_valid.reshape(n_slots, 1))


def _moe_ffn_kernel(be_ref, bv_ref, x_ref, wg_ref, wu_ref, wd_ref, o_ref, h_ref):
    i, s = pl.program_id(0), pl.program_id(1)
    n_sub = bv_ref[i]
    for k in range(1, MOE_BLOCK // MOE_SUB + 1):
        _swiglu_step(s, x_ref, wg_ref.at[0], wu_ref.at[0], wd_ref.at[0], o_ref, h_ref,
                     rows=k * MOE_SUB, guard=n_sub == k)

    @pl.when((n_sub == 0) & (s >= FF_STEPS))
    def _():
        o_ref[...] = jnp.zeros_like(o_ref)


def _moe_ffn(block_expert, block_valid, xs, w_gate, w_up, w_down):
    n_slots = xs.shape[0]
    tm = MOE_BLOCK

    def ff(i, s, be, bv):
        return (be[i], 0, jnp.where(bv[i] > 0, jnp.minimum(s, FF_STEPS - 1), FF_STEPS - 1))

    def down(i, s, be, bv):
        return (be[i], 0, jnp.where(bv[i] > 0, jnp.maximum(s - FF_STEPS, 0), OUT_STEPS - 1))

    return pl.pallas_call(
        _moe_ffn_kernel,
        out_shape=_SDS((n_slots, D_MODEL), F32),
        grid_spec=pltpu.PrefetchScalarGridSpec(
            num_scalar_prefetch=2,
            grid=(n_slots // tm, FF_STEPS + OUT_STEPS),
            in_specs=[pl.BlockSpec((tm, D_MODEL), lambda i, s, be, bv: (i, 0), pipeline_mode=pl.Buffered(1)),
                      pl.BlockSpec((1, D_MODEL, FF_TILE), ff),
                      pl.BlockSpec((1, D_MODEL, FF_TILE), ff),
                      pl.BlockSpec((1, D_FF, OUT_TILE), down)],
            out_specs=pl.BlockSpec((tm, OUT_TILE), lambda i, s, be, bv: (i, jnp.maximum(s - FF_STEPS, 0))),
            scratch_shapes=[pltpu.VMEM((tm, D_FF), BF16)]),
        compiler_params=_cparams(("arbitrary", "arbitrary"), SWIGLU_VMEM_LIMIT),
        name="moe_grouped_swiglu",
    )(block_expert, block_valid, xs, w_gate, w_up, w_down)


def _combine_kernel(pos_ref, y_hbm, wt_ref, h_ref, gp_ref, gate_ref, o_ref, buf_a, buf_b, sem):
    i = pl.program_id(0)
    n = pl.num_programs(0)
    base = i * GATHER_TILE

    def start_tile(t, slot):
        def issue(r, carry):
            a = 2 * (t * GATHER_TILE + r)
            _row_copy(y_hbm, pos_ref[a], buf_a.at[slot], r, sem.at[slot]).start(priority=0)
            _row_copy(y_hbm, pos_ref[a + 1], buf_b.at[slot], r, sem.at[slot]).start(priority=1)
            return carry

        lax.fori_loop(0, GATHER_TILE, issue, 0, unroll=8)

    @pl.when(i == 0)
    def _():
        start_tile(0, 0)

    @pl.when(i + 1 < n)
    def _():
        start_tile(i + 1, (i + 1) % 2)

    slot = i % 2

    _tile_copy(y_hbm, buf_a.at[slot], sem.at[slot]).wait()
    _tile_copy(y_hbm, buf_b.at[slot], sem.at[slot]).wait()
    wt = wt_ref[...]
    y = wt[:, 0:1] * buf_a[slot] + wt[:, 1:2] * buf_b[slot]
    grp = _group_of_row(base)
    o_ref[...] = h_ref[...] + _mod_row(gate_ref, grp) * _rms(y, gp_ref[...])


def _moe_combine(pos, ys, wts, h, g_post, mod, gate_lk):
    n = wts.shape[0]
    tm = GATHER_TILE
    return pl.pallas_call(
        _combine_kernel,
        out_shape=_SDS((n, D_MODEL), F32),
        grid_spec=pltpu.PrefetchScalarGridSpec(
            num_scalar_prefetch=1,
            grid=(n // tm,),
            in_specs=[pl.BlockSpec(memory_space=pl.ANY),
                      pl.BlockSpec((tm, 128), lambda i, pos: (i, 0)),
                      pl.BlockSpec((tm, D_MODEL), lambda i, pos: (i, 0)),
                      pl.BlockSpec((1, D_MODEL), lambda i, pos: (0, 0)),
                      pl.BlockSpec((1, 8, D_MODEL), lambda i, pos: (gate_lk[0], 0, gate_lk[1]))],
            out_specs=pl.BlockSpec((tm, D_MODEL), lambda i, pos: (i, 0)),
            scratch_shapes=[pltpu.VMEM((2, tm, D_MODEL), F32), pltpu.VMEM((2, tm, D_MODEL), F32),
                            pltpu.SemaphoreType.DMA((2,))]),
        compiler_params=_cparams(("arbitrary",)),
        name="moe_combine",
    )(pos, ys, wts, h, g_post.reshape(1, D_MODEL), mod)


def _moe_routing(idx):
    flat_e = idx[:, :TOP_K].reshape(-1)
    n_assign = flat_e.shape[0]
    onehot = (flat_e[:, None] == jnp.arange(N_EXPERTS, dtype=I32)[None, :]).astype(I32)
    csum = jnp.cumsum(onehot, axis=0)
    counts = csum[-1]
    padded = (counts + MOE_BLOCK - 1) // MOE_BLOCK * MOE_BLOCK
    pad_end = jnp.cumsum(padded)
    pad_start = pad_end - padded
    pos = jnp.sum(onehot * (pad_start[None, :] + csum - 1), axis=1).astype(I32)
    slot_tok = jnp.zeros((MOE_SLOTS,), I32).at[pos].set(jnp.arange(n_assign, dtype=I32) // TOP_K)
    slot = jnp.arange(MOE_SLOTS, dtype=I32)
    used_end = pad_start + counts
    slot_valid = jnp.any((slot[:, None] >= pad_start[None, :]) & (slot[:, None] < used_end[None, :]), axis=1)
    blk_start = jnp.arange(MOE_SLOTS // MOE_BLOCK, dtype=I32) * MOE_BLOCK
    last_start = jnp.maximum(pad_end[-1] - MOE_BLOCK, 0)
    blk_e = jnp.searchsorted(pad_end, jnp.minimum(blk_start, last_start), side='right').astype(I32)
    blk_e = jnp.minimum(blk_e, N_EXPERTS - 1)
    rows_used = jnp.clip(used_end[blk_e] - blk_start, 0, MOE_BLOCK)
    blk_sub = ((rows_used + MOE_SUB - 1) // MOE_SUB).astype(I32)
    return pos, slot_tok, slot_valid.astype(F32), blk_e, blk_sub


def _rest_of_w_in_t(w_in_t):
    dn_in = P_MAIN + 4 * DN_HEADS
    ba = w_in_t[P_MAIN:dn_in]
    cq = w_in_t[dn_in:dn_in + MLA_Q_RANK]
    ckv = w_in_t[dn_in + MLA_Q_RANK:dn_in + MLA_Q_RANK + MLA_KV_RANK]
    k_rope = w_in_t[dn_in + MLA_Q_RANK + MLA_KV_RANK:]
    pad = jnp.zeros((R_CQ - R_MISC - MLA_ROPE - 4 * DN_HEADS, D_MODEL), w_in_t.dtype)
    return jnp.concatenate([ckv, k_rope, ba, pad, cq], axis=0)


def _rearrange_w_qb(w_qb):
    w = w_qb.reshape(MLA_Q_RANK, MLA_HEADS, MLA_NOPE + MLA_ROPE)
    w = jnp.pad(w, ((0, 0), (0, 0), (0, MLA_QK - MLA_NOPE - MLA_ROPE)))
    return w.reshape(MLA_Q_RANK, MLA_HEADS * MLA_QK).astype(BF16)


def kernel(x, c, ctx, c_ctx, ada_w, ada_b, norm_g, ab_w_in, dn_conv_w, dn_a_log, dn_dt_bias, dn_norm_g, mla_q_norm_g, mla_w_qb, mla_kv_norm_g, mla_w_kvb, ab_w_out, ffn_w_gate, ffn_w_up, ffn_w_down, gqa_w_qkv, gqa_b_qkv, gqa_sink, gqa_w_out, gqa_b_out, moe_w_router, moe_w_gate, moe_w_up, moe_w_down):
    assert x.shape == (BATCH, SEQ, D_MODEL) and ctx.shape == (BATCH, CTX_LEN, D_MODEL) and ada_w.shape[0] == 2
    h = (x.reshape(N_LAT, D_MODEL), ctx.reshape(N_CTX, D_MODEL))
    cvec = jnp.zeros((8, D_MODEL), F32).at[:BATCH].set(c).at[BATCH].set(c_ctx)
    mod = _ada(cvec, ada_w, ada_b)
    (cos1, sin1), (cos2, sin2) = _rope_tables()

    u = _modulate_in(*h, norm_g[0, 0], mod, 0, 0, 1)
    w_in_t = jnp.swapaxes(ab_w_in, 1, 2)
    p = _matmul(u, w_in_t, None, BF16, MM_ROW_TILE, 1024, n=P_MAIN, w_is_transposed=True)
    p_rest = _matmul(u, _rest_of_w_in_t(w_in_t[0]), None, F32, MM_ROW_TILE, R_WIDTH // 2, w_is_transposed=True)
    o_f, o_b = _deltanet(_dn_conv(p, dn_conv_w[0]), _dn_gates(p_rest, dn_a_log[0], dn_dt_bias[0]))
    q, k, v = _mla_proj(p_rest, mla_q_norm_g[0], mla_kv_norm_g[0], _rearrange_w_qb(mla_w_qb[0]),
                        mla_w_kvb[0].astype(BF16), cos1, sin1)
    a = _ab_mix(o_f, o_b, p, dn_norm_g[0], _mla_attention(q, k, v))
    y = _matmul(a, ab_w_out[0].astype(BF16), None, BF16, MM_ROW_TILE, 1024)
    h, u = _post(h, y, norm_g[0, 1], norm_g[0, 2], mod, (0, 2), (0, 3), (0, 4), BF16)
    y = _ffn(u, ffn_w_gate[0], ffn_w_up[0], ffn_w_down[0])
    h, u = _post(h, y, norm_g[0, 3], norm_g[1, 0], mod, (0, 5), (1, 0), (1, 1), BF16)

    qkv = _gqa_qkv(u, gqa_w_qkv[0].astype(BF16), gqa_b_qkv[0], cos2, sin2)
    o = _gqa_attention(qkv, gqa_sink[0])
    y = _matmul(o, gqa_w_out[0].astype(BF16), gqa_b_out[0], BF16, 1024, 1024)
    h, u, idx, wts = _post(h, y, norm_g[1, 1], norm_g[1, 2], mod, (1, 2), (1, 3), (1, 4), F32, moe_w_router[0])
    pos, slot_tok, slot_valid, blk_e, blk_sub = _moe_routing(idx)
    xs = _moe_gather(slot_tok, slot_valid, u)
    ys = _moe_ffn(blk_e, blk_sub, xs, moe_w_gate[0], moe_w_up[0], moe_w_down[0])
    out = _moe_combine(pos, ys, wts, h, norm_g[1, 3], mod, (1, 5))
    return out.reshape(BATCH, SEQ, D_MODEL)
```

```python
import functools
import math

import jax
import jax.numpy as jnp
from jax import lax
from jax.experimental import pallas as pl
from jax.experimental.pallas import tpu as pltpu

F32 = jnp.float32
BF16 = jnp.bfloat16
I32 = jnp.int32

D_MODEL = 2048
BATCH = 2
SEQ = 4096
CTX_LEN = 256
GRID_W = 64
EPS = 1e-6
ROPE_THETA = 10000.0

DN_HEADS = 8
DN_HEAD_DIM = 128
DN_CONV = 5
DN_CHUNK = 64
MLA_HEADS = 8
MLA_Q_RANK = 768
MLA_KV_RANK = 512
MLA_NOPE = 128
MLA_ROPE = 64
MLA_V = 128
GQA_HEADS = 32
GQA_KV_HEADS = 4
GQA_HEAD_DIM = 64
WINDOW = 128
Q_BLOCK = 128
D_FF = 7168
N_EXPERTS = 8
TOP_K = 2

DN_W = DN_HEADS * DN_HEAD_DIM
DN_QKV = 3 * DN_W
N_LAT = BATCH * SEQ
N_CTX = BATCH * CTX_LEN
N_TOK = N_LAT + N_CTX
GQA_Q = GQA_HEADS * GQA_HEAD_DIM
GQA_KV = GQA_KV_HEADS * GQA_HEAD_DIM

P_QKV = 0
P_Z = DN_QKV
P_MAIN = P_Z + DN_W
R_CKV = 0
R_MISC = MLA_KV_RANK
R_CQ = MLA_Q_RANK
R_WIDTH = R_CQ + MLA_Q_RANK

ROW_TILE = 512
MM_ROW_TILE = 1088
MOE_BLOCK = 1024
MOE_SUB = 128
MOE_SLOTS = N_LAT * TOP_K + N_EXPERTS * MOE_BLOCK
VMEM_LIMIT = 56 * 1024 * 1024

_SDS = jax.ShapeDtypeStruct


def _cparams(sem, vmem=None):
    return pltpu.CompilerParams(dimension_semantics=sem, vmem_limit_bytes=vmem)


def _dot(a, b):
    return jnp.dot(a, b, preferred_element_type=F32)


def _dot_nt(a, b):
    return lax.dot_general(a, b, (((1,), (1,)), ((), ())), preferred_element_type=F32)


def _dot_tn(a, b):
    return lax.dot_general(a, b, (((0,), (0,)), ((), ())), preferred_element_type=F32)


def _silu(x):
    return x * jax.nn.sigmoid(x)


def _rms(x, g):
    return x * lax.rsqrt(jnp.mean(x * x, axis=-1, keepdims=True) + EPS) * g


def _group_of_row(r0):
    return (r0 >= SEQ).astype(I32) + (r0 >= 2 * SEQ).astype(I32)


def _mod_spec(layer, k):
    return pl.BlockSpec((1, 8, D_MODEL), lambda *_: (layer, 0, k))


def _mod_row(ref, grp):
    return ref[0, pl.ds(grp, 1), :]


def _ada_kernel(c_ref, w_ref, b_ref, o_ref):
    s = _silu(c_ref[...]).astype(BF16)
    o_ref[0] = _dot(s, w_ref[0].astype(BF16)) + b_ref[0]


def _ada(cvec, ada_w, ada_b):
    n_layers, _, n = ada_w.shape
    tn = 1024
    return pl.pallas_call(
        _ada_kernel,
        out_shape=_SDS((n_layers, 8, n), F32),
        grid=(n_layers, n // tn),
        in_specs=[pl.BlockSpec((8, D_MODEL), lambda l, j: (0, 0)),
                  pl.BlockSpec((1, D_MODEL, tn), lambda l, j: (l, 0, j)),
                  pl.BlockSpec((1, 1, tn), lambda l, j: (l, 0, j))],
        out_specs=pl.BlockSpec((1, 8, tn), lambda l, j: (l, 0, j)),
        compiler_params=_cparams(("parallel", "parallel")),
        name="ada_mod",
    )(cvec, ada_w, ada_b.reshape(n_layers, 1, n))


_LAT_TILES = N_LAT // ROW_TILE
assert N_CTX == ROW_TILE


def _split_specs(width, col=0):
    lat = pl.BlockSpec((ROW_TILE, width), lambda i, *_: (jnp.minimum(i, _LAT_TILES - 1), col))
    ctx = pl.BlockSpec((ROW_TILE, width), lambda i, *_: (0, col))
    return [lat, ctx]


def _split_read(lat_ref, ctx_ref, sl=slice(None)):
    return jnp.where(pl.program_id(0) == _LAT_TILES, ctx_ref[:, sl], lat_ref[:, sl])


def _modin_kernel(x_ref, c_ref, g_ref, sh_ref, sc_ref, u_ref):
    grp = _group_of_row(pl.program_id(0) * ROW_TILE)
    h = _split_read(x_ref, c_ref)
    u = _rms(h, g_ref[...]) * (1.0 + _mod_row(sc_ref, grp)) + _mod_row(sh_ref, grp)
    u_ref[...] = u.astype(u_ref.dtype)


def _modulate_in(x, ctx, g, mod, layer, k_shift, k_scale):
    row = pl.BlockSpec((ROW_TILE, D_MODEL), lambda i: (i, 0))
    vec = pl.BlockSpec((1, D_MODEL), lambda i: (0, 0))
    return pl.pallas_call(
        _modin_kernel,
        out_shape=_SDS((N_TOK, D_MODEL), BF16),
        grid=(N_TOK // ROW_TILE,),
        in_specs=_split_specs(D_MODEL) + [vec, _mod_spec(layer, k_shift), _mod_spec(layer, k_scale)],
        out_specs=row,
        compiler_params=_cparams(("parallel",)),
        name="modulate_in",
    )(x, ctx, g.reshape(1, D_MODEL), mod, mod)


def _route_top2(logits):
    lane = lax.broadcasted_iota(I32, logits.shape, 1)
    neg = -jnp.inf
    l0 = jnp.where(lane < N_EXPERTS, logits, neg)
    m1 = jnp.max(l0, axis=-1, keepdims=True)
    i1 = jnp.min(jnp.where(l0 == m1, lane, 128), axis=-1, keepdims=True)
    l1 = jnp.where(lane == i1, neg, l0)
    m2 = jnp.max(l1, axis=-1, keepdims=True)
    i2 = jnp.min(jnp.where(l1 == m2, lane, 128), axis=-1, keepdims=True)
    e = jnp.exp(m2 - m1)
    w1 = 1.0 / (1.0 + e)
    idx = jnp.where(lane == 0, i1, jnp.where(lane == 1, i2, 0))
    wts = jnp.where(lane == 0, w1, jnp.where(lane == 1, e * w1, 0.0))
    return idx, wts


U32 = jnp.uint32


def _pack_bf16_halves(x):
    n = x.shape[1] // 2
    lo = lax.bitcast_convert_type(x[:, :n].astype(F32), U32) >> 16
    hi = (lax.bitcast_convert_type(x[:, n:].astype(F32), U32) >> 16) << 16
    return hi | lo


def _unpack_bf16_halves(w):
    lo = lax.bitcast_convert_type(w << 16, F32).astype(BF16)
    hi = lax.bitcast_convert_type((w >> 16) << 16, F32).astype(BF16)
    return lo, hi


def _post_kernel(*refs, split, route):
    refs = list(refs)
    if split:
        h = _split_read(refs.pop(0), refs.pop(0))
    else:
        h = refs.pop(0)[...]
    y_ref, gp_ref, gate_ref, gn_ref, sh_ref, sc_ref = refs[:6]
    refs = refs[6:]
    wr_ref = refs.pop(0) if route else None
    hn_ref, u_ref = refs[:2]
    grp = _group_of_row(pl.program_id(0) * ROW_TILE)
    hn = h + _mod_row(gate_ref, grp) * _rms(y_ref[...].astype(F32), gp_ref[...])
    hn_ref[...] = hn
    u = _rms(hn, gn_ref[...]) * (1.0 + _mod_row(sc_ref, grp)) + _mod_row(sh_ref, grp)
    if route:
        ub = u.astype(BF16)
        u_ref[...] = _pack_bf16_halves(ub)
        idx_ref, wt_ref = refs[2:4]
        idx_ref[...], wt_ref[...] = _route_top2(_dot(ub, wr_ref[...]))
    else:
        u_ref[...] = u.astype(u_ref.dtype)


def _post(h, y, g_post, g_next, mod, gate_lk, shift_lk, scale_lk, u_dtype, w_router=None):
    n = y.shape[0]
    split = isinstance(h, tuple)
    route = w_router is not None
    row = pl.BlockSpec((ROW_TILE, D_MODEL), lambda i: (i, 0))
    vec = pl.BlockSpec((1, D_MODEL), lambda i: (0, 0))
    slab = pl.BlockSpec((ROW_TILE, 128), lambda i: (i, 0))
    h_specs, h_args = (_split_specs(D_MODEL), list(h)) if split else ([row], [h])
    extra_specs, extra_args, extra_out, extra_out_specs = [], [], (), ()
    u_shape, u_spec = _SDS((n, D_MODEL), u_dtype), row
    if route:
        w = jnp.zeros((D_MODEL, 128), BF16).at[:, :N_EXPERTS].set(w_router.astype(BF16))
        extra_specs, extra_args = [pl.BlockSpec((D_MODEL, 128), lambda i: (0, 0))], [w]
        extra_out, extra_out_specs = (_SDS((n, 128), I32), _SDS((n, 128), F32)), (slab, slab)
        u_shape, u_spec = _SDS((n, D_MODEL // 2), U32), pl.BlockSpec((ROW_TILE, D_MODEL // 2), lambda i: (i, 0))
    return pl.pallas_call(
        functools.partial(_post_kernel, split=split, route=route),
        out_shape=(_SDS((n, D_MODEL), F32), u_shape) + extra_out,
        grid=(n // ROW_TILE,),
        in_specs=h_specs + [row, vec, _mod_spec(*gate_lk), vec, _mod_spec(*shift_lk), _mod_spec(*scale_lk)] + extra_specs,
        out_specs=(row, u_spec) + extra_out_specs,
        compiler_params=_cparams(("parallel",)),
        name="residual_post",
    )(*h_args, y, g_post.reshape(1, D_MODEL), mod, g_next.reshape(1, D_MODEL), mod, mod, *extra_args)


def _mm_kernel(a_ref, w_ref, b_ref, o_ref, *, w_is_transposed):
    dot = _dot_nt if w_is_transposed else _dot
    o_ref[...] = (dot(a_ref[...], w_ref[...].astype(BF16)) + b_ref[...]).astype(o_ref.dtype)


def _matmul(a, w, bias, out_dtype, tm, tn, n=None, w_is_transposed=False):
    m, k = a.shape
    if w_is_transposed:
        n = w.shape[-2] if n is None else n
        if w.ndim == 3:
            w_spec = pl.BlockSpec((None, tn, k), lambda i, j: (0, j, 0))
        else:
            w_spec = pl.BlockSpec((tn, k), lambda i, j: (j, 0))
    else:
        n = w.shape[-1] if n is None else n
        w_spec = pl.BlockSpec((k, tn), lambda i, j: (0, j))
    if bias is None:
        bias = jnp.zeros((n,), F32)
    return pl.pallas_call(
        functools.partial(_mm_kernel, w_is_transposed=w_is_transposed),
        out_shape=_SDS((m, n), out_dtype),
        grid=(m // tm, n // tn),
        in_specs=[pl.BlockSpec((tm, k), lambda i, j: (i, 0)),
                  w_spec,
                  pl.BlockSpec((1, tn), lambda i, j: (0, j))],
        out_specs=pl.BlockSpec((tm, tn), lambda i, j: (i, j)),
        compiler_params=_cparams(("parallel", "arbitrary"), VMEM_LIMIT),
        name="matmul",
    )(a, w, bias.reshape(1, n))


FF_TILE = 512
SWIGLU_VMEM_LIMIT = 60 * 1024 * 1024
OUT_TILE = 256
FF_STEPS = D_FF // FF_TILE
OUT_STEPS = D_MODEL // OUT_TILE


def _swiglu_step(s, x_ref, wg_ref, wu_ref, wd_ref, o_ref, h_ref, rows=None, guard=True):
    total = x_ref.shape[0]
    rows = total if rows is None else rows

    @pl.when(guard & (s < FF_STEPS))
    def _():
        x = x_ref[0:rows, :]
        a = _dot(x, wg_ref[...].astype(BF16))
        b = _dot(x, wu_ref[...].astype(BF16))
        h_ref[0:rows, pl.ds(pl.multiple_of(s * FF_TILE, FF_TILE), FF_TILE)] = (_silu(a) * b).astype(BF16)

    @pl.when(guard & (s >= FF_STEPS))
    def _():
        o_ref[0:rows, :] = _dot(h_ref[0:rows, :], wd_ref[...].astype(BF16)).astype(o_ref.dtype)
        if rows < total:
            o_ref[rows:total, :] = jnp.zeros((total - rows, o_ref.shape[1]), o_ref.dtype)


def _ffn_kernel(u_ref, wg_ref, wu_ref, wd_ref, o_ref, h_ref):
    _swiglu_step(pl.program_id(1), u_ref, wg_ref, wu_ref, wd_ref, o_ref, h_ref)


def _ffn(u, w_gate, w_up, w_down):
    m = u.shape[0]
    tm = MM_ROW_TILE

    def ff(i, s):
        return (0, jnp.minimum(s, FF_STEPS - 1))

    def out(s):
        return jnp.maximum(s - FF_STEPS, 0)

    return pl.pallas_call(
        _ffn_kernel,
        out_shape=_SDS((m, D_MODEL), BF16),
        grid=(m // tm, FF_STEPS + OUT_STEPS),
        in_specs=[pl.BlockSpec((tm, D_MODEL), lambda i, s: (i, 0), pipeline_mode=pl.Buffered(1)),
                  pl.BlockSpec((D_MODEL, FF_TILE), ff),
                  pl.BlockSpec((D_MODEL, FF_TILE), ff),
                  pl.BlockSpec((D_FF, OUT_TILE), lambda i, s: (0, out(s)))],
        out_specs=pl.BlockSpec((tm, OUT_TILE), lambda i, s: (i, out(s))),
        scratch_shapes=[pltpu.VMEM((tm, D_FF), BF16)],
        compiler_params=_cparams(("parallel", "arbitrary"), SWIGLU_VMEM_LIMIT),
        name="swiglu_ffn",
    )(u, w_gate, w_up, w_down)


CONV_TILE = 256
CONV_HALO = 16
_SEQ_STARTS = (0, SEQ // CONV_TILE, 2 * SEQ // CONV_TILE, 2 * SEQ // CONV_TILE + 1)
_SEQ_LASTS = (SEQ // CONV_TILE - 1, 2 * SEQ // CONV_TILE - 1, 2 * SEQ // CONV_TILE, 2 * SEQ // CONV_TILE + 1)


def _conv_kernel(prev_ref, cur_ref, next_ref, w_ref, o_ref, buf):
    i = pl.program_id(0)
    j = pl.program_id(1)
    tm = CONV_TILE
    first = functools.reduce(jnp.logical_or, [i == s for s in _SEQ_STARTS])
    last = functools.reduce(jnp.logical_or, [i == s for s in _SEQ_LASTS])
    hb = CONV_HALO
    buf[0:hb, :] = jnp.where(first, 0.0, prev_ref[...].astype(F32))
    buf[hb:hb + tm, :] = cur_ref[...].astype(F32)
    buf[hb + tm:2 * hb + tm, :] = jnp.where(last, 0.0, next_ref[...].astype(F32))
    pad = DN_CONV // 2
    acc = buf[hb - pad:hb - pad + tm, :] * w_ref[0:1, :]
    for d in range(1, DN_CONV):
        acc = acc + buf[hb - pad + d:hb - pad + d + tm, :] * w_ref[d:d + 1, :]
    x = _silu(acc)
    q_scale = jnp.where(j == 0, DN_HEAD_DIM ** -0.5, 1.0)
    for hh in range(DN_HEADS):
        xh = x[:, hh * DN_HEAD_DIM:(hh + 1) * DN_HEAD_DIM]
        inv = lax.rsqrt(jnp.sum(xh * xh, axis=-1, keepdims=True) + EPS) * q_scale
        o_ref[:, hh * DN_HEAD_DIM:(hh + 1) * DN_HEAD_DIM] = xh * jnp.where(j == 2, 1.0, inv)


def _dn_conv(p, conv_w):
    n = p.shape[0]
    tm = CONV_TILE
    hb = CONV_HALO
    per_tile = tm // hb
    n_halo = n // hb
    return pl.pallas_call(
        _conv_kernel,
        out_shape=_SDS((n, DN_QKV), F32),
        grid=(n // tm, 3),
        in_specs=[pl.BlockSpec((hb, DN_W), lambda i, j: (jnp.maximum(i * per_tile - 1, 0), j)),
                  pl.BlockSpec((tm, DN_W), lambda i, j: (i, j)),
                  pl.BlockSpec((hb, DN_W), lambda i, j: (jnp.minimum((i + 1) * per_tile, n_halo - 1), j)),
                  pl.BlockSpec((DN_CONV, DN_W), lambda i, j: (0, j))],
        out_specs=pl.BlockSpec((tm, DN_W), lambda i, j: (i, j)),
        scratch_shapes=[pltpu.VMEM((tm + 2 * hb, DN_W), F32)],
        compiler_params=_cparams(("parallel", "parallel")),
        name="dn_conv",
    )(p, p, p, conv_w)


def _gates_kernel(m_ref, a_ref, dt_ref, o_ref):
    x = pltpu.roll(m_ref[...], 64, 1)
    lane = lax.broadcasted_iota(I32, x.shape, 1)
    z = x + dt_ref[...]
    softplus = jnp.maximum(z, 0.0) + jnp.log1p(jnp.exp(-jnp.abs(z)))
    g = a_ref[...] * softplus
    o_ref[...] = jnp.where(lane < 2 * DN_HEADS, jax.nn.sigmoid(x), jnp.where(lane < 4 * DN_HEADS, g, 0.0))


def _dn_gates(p, a_log, dt_bias):
    n = p.shape[0]
    neg_a = jnp.zeros((1, 128), F32).at[0, 2 * DN_HEADS:4 * DN_HEADS].set(-jnp.exp(a_log.astype(F32)).reshape(-1))
    dtb = jnp.zeros((1, 128), F32).at[0, 2 * DN_HEADS:4 * DN_HEADS].set(dt_bias.astype(F32).reshape(-1))
    tm = ROW_TILE
    vec = pl.BlockSpec((1, 128), lambda i: (0, 0))
    return pl.pallas_call(
        _gates_kernel,
        out_shape=_SDS((n, 128), F32),
        grid=(n // tm,),
        in_specs=[pl.BlockSpec((tm, 128), lambda i: (i, R_MISC // 128)), vec, vec],
        out_specs=pl.BlockSpec((tm, 128), lambda i: (i, 0)),
        compiler_params=_cparams(("parallel",)),
        name="dn_gates",
    )(p, neg_a, dtb)


_CTX_CHUNKS = CTX_LEN // DN_CHUNK
_LAT_CHUNKS = SEQ // DN_CHUNK
_DN_STEPS = _CTX_CHUNKS + _LAT_CHUNKS


def _dn_fwd_block(b, t):
    return jnp.where(t < _CTX_CHUNKS, (2 * SEQ + b * CTX_LEN) // DN_CHUNK + t, b * _LAT_CHUNKS + (t - _CTX_CHUNKS))


def _dn_bwd_block(b, t):
    return jnp.where(t < _CTX_CHUNKS, (2 * SEQ + b * CTX_LEN) // DN_CHUNK + (_CTX_CHUNKS - 1 - t),
                     b * _LAT_CHUNKS + (_DN_STEPS - 1 - t))


DN_GROUP = 2
_GROUP_ROWS = DN_GROUP * DN_CHUNK
_GROUP_STATE = DN_GROUP * DN_HEAD_DIM


_DN_BASE = 8


def _block_diag(x):
    zero = jnp.zeros((DN_CHUNK, DN_HEAD_DIM), x.dtype)
    rows = []
    for hh in range(DN_GROUP):
        piece = x[hh * DN_CHUNK:(hh + 1) * DN_CHUNK]
        rows.append(jnp.concatenate([piece if j == hh else zero for j in range(DN_GROUP)], axis=1))
    return jnp.concatenate(rows, axis=0)


def _dn_groups(items, eye, base_mask):
    it = items
    n = range(len(it))
    gx = [jnp.broadcast_to(it[g]["gc"], (_GROUP_ROWS, _GROUP_ROWS)) for g in n]
    gamma = [jnp.where(it[g]["incl"], jnp.exp(jnp.where(it[g]["incl"], gx[g] - gx[g].T, 0.0)), 0.0) for g in n]
    kb = [it[g]["k"].astype(BF16) for g in n]
    kk = [_dot_nt(kb[g], kb[g]) for g in n]
    a = [jnp.where(it[g]["strict"], it[g]["beta"] * kk[g] * gamma[g], 0.0) for g in n]
    d = [jnp.where(base_mask, a[g], 0.0) for g in n]
    tinv = [eye - d[g] for g in n]
    pw = [d[g].astype(BF16) for g in n]
    for _ in range(int(math.log2(_DN_BASE)) - 1):
        pw = [_dot(pw[g], pw[g]).astype(BF16) for g in n]
        tinv = [tinv[g] + _dot(tinv[g].astype(BF16), pw[g]) for g in n]
    for lvl in range(len(it[0]["levels"])):
        tb = [tinv[g].astype(BF16) for g in n]
        mt = [_dot(jnp.where(it[g]["levels"][lvl], a[g], 0.0).astype(BF16), tb[g]).astype(BF16) for g in n]
        tinv = [tinv[g] - _dot(tb[g], mt[g]) for g in n]
    tb = [tinv[g].astype(BF16) for g in n]
    egc = [jnp.exp(it[g]["gc"]) for g in n]
    rhs = [jnp.concatenate([it[g]["v"] * it[g]["beta"], it[g]["k"] * (it[g]["beta"] * egc[g])], axis=1) for g in n]
    rhs_hi = [rhs[g].astype(BF16) for g in n]
    rhs_lo = [(rhs[g] - rhs_hi[g].astype(F32)).astype(BF16) for g in n]
    sol = [_dot(jnp.concatenate([tb[g], tb[g]], axis=1), jnp.concatenate([rhs_hi[g], rhs_lo[g]], axis=0)) for g in n]
    qk = [(_dot_nt(it[g]["q"].astype(BF16), kb[g]) * gamma[g]).astype(BF16) for g in n]
    s = [it[g]["s_ref"][...] for g in n]
    sb = [s[g].astype(BF16) for g in n]
    v_new = [sol[g][:, :DN_HEAD_DIM] - _dot(_block_diag(sol[g][:, DN_HEAD_DIM:].astype(BF16)), sb[g]) for g in n]
    vb = [v_new[g].astype(BF16) for g in n]
    o = [_dot(_block_diag((it[g]["q"] * egc[g]).astype(BF16)), sb[g]) + _dot(qk[g], vb[g]) for g in n]
    for g in n:
        k_dec = _block_diag((it[g]["k"] * jnp.exp(it[g]["g_last"] - it[g]["gc"])).astype(BF16))
        it[g]["s_ref"][...] = s[g] * jnp.exp(it[g]["g_last_state"]) + _dot_tn(k_dec, vb[g])
    return o


def _stack_cols(x, lanes, rows_each):
    pieces = [x[:, l:l + 1] for l in lanes]
    if rows_each is not None:
        pieces = [jnp.broadcast_to(pc, (rows_each, 1)) for pc in pieces]
    return jnp.concatenate(pieces, axis=0)


def _dn_kernel(*refs):
    n_in = BATCH * 2 * 4
    in_refs, (ofl_ref, ofc_ref, obl_ref, obc_ref), (s_ref, o_scr) = refs[:n_in], refs[n_in:n_in + 4], refs[n_in + 4:]
    t = pl.program_id(0)

    @pl.when(t == 0)
    def _():
        s_ref[...] = jnp.zeros_like(s_ref)

    c = DN_CHUNK
    ri = lax.broadcasted_iota(I32, (_GROUP_ROWS, _GROUP_ROWS), 0)
    ci = lax.broadcasted_iota(I32, (_GROUP_ROWS, _GROUP_ROWS), 1)
    same_head = (ri // c) == (ci // c)
    eye = (ri == ci).astype(F32)
    r1 = lax.broadcasted_iota(I32, (c, c), 0)
    c1 = lax.broadcasted_iota(I32, (c, c), 1)
    base_mask = (ri // _DN_BASE) == (ci // _DN_BASE)
    masks = []
    for d in range(2):
        incl = same_head & ((ri >= ci) if d == 0 else (ri <= ci))
        strict = same_head & ((ri > ci) if d == 0 else (ri < ci))
        levels = []
        size = _DN_BASE
        while size < c:
            lo_blk, hi_blk = ((ci, ri) if d == 0 else (ri, ci))
            levels.append(((ri // (2 * size)) == (ci // (2 * size)))
                          & ((lo_blk // size) % 2 == 0) & ((hi_blk // size) % 2 == 1))
            size *= 2
        masks.append((incl, strict, levels))
    items, where = [], []
    for b, d in [(b, d) for b in range(BATCH) for d in range(2)]:
        q_ref, k_ref, v_ref, g_ref = in_refs[(b * 2 + d) * 4:(b * 2 + d) * 4 + 4]
        incl, strict, levels = masks[d]
        gates = g_ref[...]
        tri = ((r1 >= c1) if d == 0 else (r1 <= c1)).astype(BF16)
        g_hi = gates.astype(BF16)
        g_r1 = gates - g_hi.astype(F32)
        g_mid = g_r1.astype(BF16)
        g_lo = (g_r1 - g_mid.astype(F32)).astype(BF16)
        gc = _dot(tri, g_hi) + _dot(tri, g_mid) + _dot(tri, g_lo)
        last_row = c - 1 if d == 0 else 0
        tot = gc[last_row:last_row + 1, :]
        for grp in range(DN_HEADS // DN_GROUP):
            heads = range(grp * DN_GROUP, (grp + 1) * DN_GROUP)
            lanes_b = [d * DN_HEADS + hh for hh in heads]
            lanes_g = [2 * DN_HEADS + lb for lb in lanes_b]

            def stack(ref):
                return jnp.concatenate([ref[:, hh * DN_HEAD_DIM:(hh + 1) * DN_HEAD_DIM] for hh in heads], axis=0)

            items.append(dict(q=stack(q_ref), k=stack(k_ref), v=stack(v_ref),
                              beta=_stack_cols(gates, lanes_b, None), gc=_stack_cols(gc, lanes_g, None),
                              g_last=_stack_cols(tot, lanes_g, c), g_last_state=_stack_cols(tot, lanes_g, DN_HEAD_DIM),
                              incl=incl, strict=strict, levels=levels, s_ref=s_ref.at[b, d, grp]))
            where.append((d, b, heads))
    outs = _dn_groups(items, eye, base_mask)
    for o, (d, b, heads) in zip(outs, where):
        for j, hh in enumerate(heads):
            o_scr[d, b, :, hh * DN_HEAD_DIM:(hh + 1) * DN_HEAD_DIM] = o[j * c:(j + 1) * c]

    @pl.when(t < _CTX_CHUNKS)
    def _():
        ofc_ref[...] = o_scr[0]
        obc_ref[...] = o_scr[1]

    @pl.when(t >= _CTX_CHUNKS)
    def _():
        ofl_ref[...] = o_scr[0]
        obl_ref[...] = o_scr[1]


def _deltanet(qkv, gates):
    c = DN_CHUNK
    ins, args = [], []
    for b in range(BATCH):
        for fn in (_dn_fwd_block, _dn_bwd_block):
            for width, col, arr in ((DN_W, 0, qkv), (DN_W, 1, qkv), (DN_W, 2, qkv), (128, 0, gates)):
                ins.append(pl.BlockSpec((c, width), functools.partial(lambda t, b, fn, col: (fn(b, t), col),
                                                                      b=b, fn=fn, col=col)))
                args.append(arr)
    last = _DN_STEPS - 1
    lat = _SDS((BATCH, SEQ, DN_W), F32)
    ctx = _SDS((BATCH, CTX_LEN, DN_W), F32)
    blk = (BATCH, c, DN_W)
    outs = pl.pallas_call(
        _dn_kernel,
        out_shape=(lat, ctx, lat, ctx),
        grid=(_DN_STEPS,),
        in_specs=ins,
        out_specs=(pl.BlockSpec(blk, lambda t: (0, jnp.maximum(t - _CTX_CHUNKS, 0), 0)),
                   pl.BlockSpec(blk, lambda t: (0, jnp.minimum(t, _CTX_CHUNKS - 1), 0)),
                   pl.BlockSpec(blk, lambda t: (0, jnp.minimum(last - t, _LAT_CHUNKS - 1), 0)),
                   pl.BlockSpec(blk, lambda t: (0, jnp.maximum(_CTX_CHUNKS - 1 - t, 0), 0))),
        scratch_shapes=[pltpu.VMEM((BATCH, 2, DN_HEADS // DN_GROUP, _GROUP_STATE, DN_HEAD_DIM), F32),
                        pltpu.VMEM((2, BATCH, c, DN_W), F32)],
        compiler_params=_cparams(("arbitrary",)),
        name="deltanet",
    )(*args)
    ofl, ofc, obl, obc = outs
    return ((ofl.reshape(N_LAT, DN_W), ofc.reshape(N_CTX, DN_W)),
            (obl.reshape(N_LAT, DN_W), obc.reshape(N_CTX, DN_W)))


def _rope_tables():
    t = jnp.arange(SEQ)
    row = (t // GRID_W).astype(F32)
    col = (t % GRID_W).astype(F32)
    n_freq = MLA_ROPE // 4
    inv_freq = ROPE_THETA ** (-jnp.arange(n_freq, dtype=F32) / n_freq)
    ang = jnp.concatenate([row[:, None] * inv_freq, col[:, None] * inv_freq], axis=-1)
    cos, sin = jnp.cos(ang), jnp.sin(ang)
    cos64 = jnp.concatenate([cos, cos], axis=-1)
    sin64 = jnp.concatenate([-sin, sin], axis=-1)

    def assemble(lat_cos, lat_sin):
        c = jnp.concatenate([jnp.tile(lat_cos, (BATCH, 1)), jnp.ones((N_CTX, 128), F32)], axis=0)
        s = jnp.concatenate([jnp.tile(lat_sin, (BATCH, 1)), jnp.zeros((N_CTX, 128), F32)], axis=0)
        return c, s

    cos_l = jnp.concatenate([cos64, jnp.ones((SEQ, 64), F32)], axis=-1)
    sin_l = jnp.concatenate([sin64, jnp.zeros((SEQ, 64), F32)], axis=-1)
    one_head = assemble(cos_l, sin_l)
    two_heads = assemble(jnp.tile(cos64, (1, 2)), jnp.tile(sin64, (1, 2)))
    return one_head, two_heads


def _rope128(x, cosv, sinv):
    lane = lax.broadcasted_iota(I32, x.shape, 1)
    swapped = jnp.where((lane % 64) < 32, pltpu.roll(x, 96, 1), pltpu.roll(x, 32, 1))
    return x * cosv + swapped * sinv


MLA_QK = 256
MLA_VX = 256


def _mla_proj_kernel(ckv_ref, cq_ref, misc_ref, gq_ref, gkv_ref, wq_ref, wkv_ref, cos_ref, sin_ref,
                     q_ref, k_ref, v_ref):
    cq = _rms(cq_ref[...].astype(F32), gq_ref[...]).astype(BF16)
    ckv = _rms(ckv_ref[...].astype(F32), gkv_ref[...]).astype(BF16)
    qf = _dot(cq, wq_ref[...])
    kvf = _dot(ckv, wkv_ref[...])
    cosv, sinv = cos_ref[...], sin_ref[...]
    lane = lax.broadcasted_iota(I32, cosv.shape, 1)
    k_rope = _rope128(jnp.where(lane < MLA_ROPE, misc_ref[...], 0.0), cosv, sinv).astype(BF16)
    scale = (MLA_NOPE + MLA_ROPE) ** -0.5
    for hh in range(MLA_HEADS):
        lo, mid, hi = hh * MLA_QK, hh * MLA_QK + 128, (hh + 1) * MLA_QK
        q_ref[:, lo:mid] = (qf[:, lo:mid] * scale).astype(BF16)
        q_ref[:, mid:hi] = (_rope128(qf[:, mid:hi], cosv, sinv) * scale).astype(BF16)
        k_ref[:, lo:mid] = kvf[:, lo:mid].astype(BF16)
        k_ref[:, mid:hi] = k_rope
        v_ref[:, lo:mid] = kvf[:, mid:hi].astype(BF16)
        v_ref[:, mid:hi] = jnp.ones((kvf.shape[0], MLA_VX - MLA_V), BF16)


def _mla_proj(p_rest, q_norm_g, kv_norm_g, w_q, w_kv, cosv, sinv):
    n = p_rest.shape[0]
    tm = ROW_TILE
    wide = MLA_HEADS * MLA_QK

    def rows(width, col):
        return pl.BlockSpec((tm, width), lambda i: (i, col))

    def whole(shape):
        return pl.BlockSpec(shape, lambda i: (0, 0))

    return pl.pallas_call(
        _mla_proj_kernel,
        out_shape=(_SDS((n, wide), BF16), _SDS((n, wide), BF16), _SDS((n, MLA_HEADS * MLA_VX), BF16)),
        grid=(n // tm,),
        in_specs=[rows(MLA_KV_RANK, R_CKV // MLA_KV_RANK), rows(MLA_Q_RANK, R_CQ // MLA_Q_RANK),
                  rows(128, R_MISC // 128), whole((1, MLA_Q_RANK)), whole((1, MLA_KV_RANK)),
                  whole((MLA_Q_RANK, wide)), whole((MLA_KV_RANK, wide)), rows(128, 0), rows(128, 0)],
        out_specs=(rows(wide, 0), rows(wide, 0), rows(MLA_HEADS * MLA_VX, 0)),
        compiler_params=_cparams(("parallel",), VMEM_LIMIT),
        name="mla_proj",
    )(p_rest, p_rest, p_rest, q_norm_g.reshape(1, -1), kv_norm_g.reshape(1, -1), w_q, w_kv, cosv, sinv)


MLA_KEY_CHUNK = 2048


def _flash_kernel(*refs, chunks):
    n_seg = (len(refs) - 2) // 2
    q_ref, o_ref = refs[0], refs[-1]
    k_refs, v_refs = refs[1:1 + n_seg], refs[1 + n_seg:1 + 2 * n_seg]
    q = q_ref[...]

    def scores(c):
        seg, off, size = c
        return _dot_nt(q, k_refs[seg][off:off + size, :])

    m = jnp.full((q.shape[0], 1), -jnp.inf, F32)
    acc = jnp.zeros((q.shape[0], MLA_VX), F32)
    s_next = scores(chunks[0])
    for j, (seg, off, size) in enumerate(chunks):
        s = s_next
        if j + 1 < len(chunks):
            s_next = scores(chunks[j + 1])
        m_new = jnp.maximum(m, jnp.max(s, axis=-1, keepdims=True))
        p = jnp.exp(s - m_new).astype(BF16)
        acc = jnp.exp(m - m_new) * acc + _dot(p, v_refs[seg][off:off + size, :])
        m = m_new
    o_ref[...] = (acc[:, :MLA_V] / acc[:, MLA_V:]).astype(o_ref.dtype)


def _mla_attention(q, k, v):
    tq = 1024
    nq = SEQ // tq
    ctx_blk = 2 * SEQ // CTX_LEN
    lat_chunks = ((0, 0, CTX_LEN),) + tuple((1, off, MLA_KEY_CHUNK) for off in range(0, SEQ, MLA_KEY_CHUNK))
    lat = pl.pallas_call(
        functools.partial(_flash_kernel, chunks=lat_chunks),
        out_shape=_SDS((N_LAT, MLA_HEADS * MLA_V), BF16),
        grid=(BATCH, MLA_HEADS, nq),
        in_specs=[pl.BlockSpec((tq, MLA_QK), lambda b, h, i: (b * nq + i, h)),
                  pl.BlockSpec((CTX_LEN, MLA_QK), lambda b, h, i: (ctx_blk + b, h)),
                  pl.BlockSpec((SEQ, MLA_QK), lambda b, h, i: (b, h)),
                  pl.BlockSpec((CTX_LEN, MLA_VX), lambda b, h, i: (ctx_blk + b, h)),
                  pl.BlockSpec((SEQ, MLA_VX), lambda b, h, i: (b, h))],
        out_specs=pl.BlockSpec((tq, MLA_V), lambda b, h, i: (b * nq + i, h)),
        compiler_params=_cparams(("parallel", "parallel", "arbitrary"), VMEM_LIMIT),
        name="mla_attn_latent",
    )(q, k, k, v, v)
    ctx = pl.pallas_call(
        functools.partial(_flash_kernel, chunks=((0, 0, CTX_LEN),)),
        out_shape=_SDS((N_CTX, MLA_HEADS * MLA_V), BF16),
        grid=(BATCH, MLA_HEADS),
        in_specs=[pl.BlockSpec((CTX_LEN, MLA_QK), lambda b, h: (ctx_blk + b, h)),
                  pl.BlockSpec((CTX_LEN, MLA_QK), lambda b, h: (ctx_blk + b, h)),
                  pl.BlockSpec((CTX_LEN, MLA_VX), lambda b, h: (ctx_blk + b, h))],
        out_specs=pl.BlockSpec((CTX_LEN, MLA_V), lambda b, h: (b, h)),
        compiler_params=_cparams(("parallel", "parallel")),
        name="mla_attn_context",
    )(q, k, v)
    return lat, ctx


def _ab_mix_kernel(ofl_ref, ofc_ref, obl_ref, obc_ref, ml_ref, mc_ref, z_ref, g_ref, a_ref):
    for hh in range(DN_HEADS):
        sl = slice(hh * DN_HEAD_DIM, (hh + 1) * DN_HEAD_DIM)
        o = _split_read(ofl_ref, ofc_ref, sl) + _split_read(obl_ref, obc_ref, sl)
        a_ref[:, sl] = (_rms(o, g_ref[...]) * _silu(z_ref[:, sl].astype(F32))).astype(BF16)
    a_ref[:, DN_W:] = _split_read(ml_ref, mc_ref)


def _ab_mix(o_f, o_b, p, dn_norm_g, mla_o):
    tm = ROW_TILE
    half = DN_W
    return pl.pallas_call(
        _ab_mix_kernel,
        out_shape=_SDS((N_TOK, 2 * half), BF16),
        grid=(N_TOK // tm,),
        in_specs=_split_specs(half) + _split_specs(half) + _split_specs(half) + [
            pl.BlockSpec((tm, half), lambda i: (i, P_Z // half)),
            pl.BlockSpec((1, DN_HEAD_DIM), lambda i: (0, 0))],
        out_specs=pl.BlockSpec((tm, 2 * half), lambda i: (i, 0)),
        compiler_params=_cparams(("parallel",)),
        name="ab_mix",
    )(*o_f, *o_b, *mla_o, p, dn_norm_g.reshape(1, -1))


def _mm_rope_kernel(a_ref, w_ref, b_ref, cos_ref, sin_ref, o_ref, *, tn, rope_cols):
    y = _dot(a_ref[...], w_ref[...].astype(BF16)) + b_ref[...]
    col0 = pl.program_id(1) * tn
    cosv, sinv = cos_ref[...], sin_ref[...]
    for s in range(tn // 128):
        ys = y[:, s * 128:(s + 1) * 128]
        roped = _rope128(ys, cosv, sinv)
        o_ref[:, s * 128:(s + 1) * 128] = jnp.where(col0 + s * 128 < rope_cols, roped, ys).astype(o_ref.dtype)


def _gqa_qkv(u, w, bias, cosv, sinv):
    m, k = u.shape
    n = w.shape[1]
    tm, tn = MM_ROW_TILE, 1280
    return pl.pallas_call(
        functools.partial(_mm_rope_kernel, tn=tn, rope_cols=GQA_Q + GQA_KV),
        out_shape=_SDS((m, n), BF16),
        grid=(m // tm, n // tn),
        in_specs=[pl.BlockSpec((tm, k), lambda i, j: (i, 0)),
                  pl.BlockSpec((k, tn), lambda i, j: (0, j)),
                  pl.BlockSpec((1, tn), lambda i, j: (0, j)),
                  pl.BlockSpec((tm, 128), lambda i, j: (i, 0)),
                  pl.BlockSpec((tm, 128), lambda i, j: (i, 0))],
        out_specs=pl.BlockSpec((tm, tn), lambda i, j: (i, j)),
        compiler_params=_cparams(("parallel", "arbitrary"), VMEM_LIMIT),
        name="gqa_qkv_rope",
    )(u, w, bias.reshape(1, n), cosv, sinv)


def _gqa_kernel(sink_ref, q_ref, kp_ref, kc_ref, kn_ref, vp_ref, vc_ref, vn_ref, kx_ref, vx_ref, o_ref):
    i = pl.program_id(1)
    nb = SEQ // Q_BLOCK
    n_keys = CTX_LEN + 3 * Q_BLOCK
    groups = GQA_HEADS // GQA_KV_HEADS
    slabs = groups // 2
    dh = GQA_HEAD_DIM
    qi = lax.broadcasted_iota(I32, (Q_BLOCK, n_keys), 0)
    kj = lax.broadcasted_iota(I32, (Q_BLOCK, n_keys), 1) - CTX_LEN
    rel = kj - Q_BLOCK - qi
    valid = (kj < 0) | ((jnp.abs(rel) <= WINDOW) & ((kj >= Q_BLOCK) | (i > 0)) & ((kj < 2 * Q_BLOCK) | (i < nb - 1)))
    bias = jnp.where(valid, 0.0, -jnp.inf).astype(F32)
    bias = jnp.concatenate([bias] * slabs, axis=0)
    k_all = jnp.concatenate([kx_ref[...], kp_ref[...], kc_ref[...], kn_ref[...]], axis=0)
    v_all = jnp.concatenate([vx_ref[...], vp_ref[...], vc_ref[...], vn_ref[...]], axis=0)
    low = lax.broadcasted_iota(I32, (n_keys, 128), 1) < dh

    def block_diag(x_all, kvh):
        pair = x_all[:, (kvh // 2) * 128:(kvh // 2 + 1) * 128].astype(F32)
        other = pltpu.roll(pair, dh, 1)
        first, second = (pair, other) if kvh % 2 == 0 else (other, pair)
        return jnp.concatenate([jnp.where(low, first, 0.0), jnp.where(low, 0.0, second)], axis=0).astype(BF16)

    def scores(kvh):
        q = jnp.concatenate([q_ref[:, (kvh * slabs + r) * 128:(kvh * slabs + r + 1) * 128] for r in range(slabs)], axis=0)
        return _dot_nt(q * (dh ** -0.5), block_diag(k_all, kvh))

    row_lo = lax.broadcasted_iota(I32, (2 * n_keys, 128), 0) < n_keys
    lane_lo = lax.broadcasted_iota(I32, (2 * n_keys, 128), 1) < dh
    ones_cols = jnp.where(row_lo == lane_lo, 1.0, 0.0).astype(BF16)

    def softmax(kvh, s):
        ps, sink_terms = [], []
        for half in range(2):
            sh = s[:, half * n_keys:(half + 1) * n_keys] + bias
            sink = jnp.concatenate([jnp.full((Q_BLOCK, 1), sink_ref[(kvh * slabs + r) * 2 + half], F32)
                                    for r in range(slabs)], axis=0)
            m = jnp.maximum(jnp.max(sh, axis=-1, keepdims=True), sink)
            ps.append(jnp.exp(sh - m).astype(BF16))
            sink_terms.append(jnp.exp(sink - m))
        return jnp.concatenate(ps, axis=1), sink_terms

    def finish(kvh, p, sink_terms):
        o = _dot(p, jnp.concatenate([block_diag(v_all, kvh), ones_cols], axis=1))
        num, den = o[:, :128], o[:, 128:]
        lane_low = lax.broadcasted_iota(I32, num.shape, 1) < dh
        o = num / (den + jnp.where(lane_low, sink_terms[0], sink_terms[1]))
        for r in range(slabs):
            o_ref[:, (kvh * slabs + r) * 128:(kvh * slabs + r + 1) * 128] = o[r * Q_BLOCK:(r + 1) * Q_BLOCK].astype(o_ref.dtype)

    s_next = scores(0)
    for kvh in range(GQA_KV_HEADS):
        s_cur = s_next
        if kvh + 1 < GQA_KV_HEADS:
            s_next = scores(kvh + 1)
        p, dens = softmax(kvh, s_cur)
        finish(kvh, p, dens)


def _gqa_attention(qkv, sink):
    nb = SEQ // Q_BLOCK
    kcol, vcol = GQA_Q // GQA_KV, GQA_Q // GQA_KV + 1
    ctx_blk = 2 * SEQ // CTX_LEN

    def win(col, shift):
        return pl.BlockSpec((Q_BLOCK, GQA_KV), lambda b, i: (b * nb + jnp.clip(i + shift, 0, nb - 1), col))

    return pl.pallas_call(
        _gqa_kernel,
        out_shape=_SDS((N_LAT, GQA_Q), BF16),
        grid=(BATCH, nb),
        in_specs=[pl.BlockSpec(memory_space=pltpu.SMEM),
                  pl.BlockSpec((Q_BLOCK, GQA_Q), lambda b, i: (b * nb + i, 0)),
                  win(kcol, -1), win(kcol, 0), win(kcol, 1), win(vcol, -1), win(vcol, 0), win(vcol, 1),
                  pl.BlockSpec((CTX_LEN, GQA_KV), lambda b, i: (ctx_blk + b, kcol)),
                  pl.BlockSpec((CTX_LEN, GQA_KV), lambda b, i: (ctx_blk + b, vcol))],
        out_specs=pl.BlockSpec((Q_BLOCK, GQA_Q), lambda b, i: (b * nb + i, 0)),
        compiler_params=_cparams(("parallel", "arbitrary")),
        name="gqa_window_attn",
    )(sink.astype(F32), qkv, qkv, qkv, qkv, qkv, qkv, qkv, qkv, qkv)


GATHER_TILE = 512
COMBINE_TILE = 256


def _row_copy(src_hbm, row, dst, r, sem):
    return pltpu.make_async_copy(src_hbm.at[pl.ds(row, 1), :], dst.at[pl.ds(r, 1), :], sem)


def _tile_copy(src_hbm, dst, sem):
    return pltpu.make_async_copy(src_hbm.at[pl.ds(0, dst.shape[0]), :], dst, sem)


def _gather_kernel(tok_ref, tv_ref, u_hbm, valid_ref, o_ref, buf, sem):
    i = pl.program_id(0)
    n = pl.num_programs(0)

    def start_tile(t, slot):
        @pl.when(tv_ref[t] > 0)
        def _():
            def issue(r8, carry):
                for j in range(8):
                    r = r8 * 8 + j
                    _row_copy(u_hbm, tok_ref[t * GATHER_TILE + r], buf.at[slot], r, sem.at[slot]).start(priority=j % 2)
                return carry

            lax.fori_loop(0, GATHER_TILE // 8, issue, 0)

    @pl.when(i == 0)
    def _():
        start_tile(0, 0)

    @pl.when(i + 1 < n)
    def _():
        start_tile(i + 1, (i + 1) % 2)

    slot = i % 2

    @pl.when(tv_ref[i] > 0)
    def _():
        _tile_copy(u_hbm, buf.at[slot], sem.at[slot]).wait()
        lo, hi = _unpack_bf16_halves(buf[slot])
        keep = valid_ref[...] > 0.0
        half = D_MODEL // 2
        o_ref[:, :half] = jnp.where(keep, lo, jnp.zeros_like(lo))
        o_ref[:, half:] = jnp.where(keep, hi, jnp.zeros_like(hi))

    @pl.when(tv_ref[i] == 0)
    def _():
        o_ref[...] = jnp.zeros_like(o_ref)


def _moe_gather(slot_tok, slot_valid, u):
    n_slots = slot_tok.shape[0]
    n_tiles = n_slots // GATHER_TILE
    tile_valid = (jnp.max(slot_valid.reshape(n_tiles, GATHER_TILE), axis=1) > 0.0).astype(I32)
    return pl.pallas_call(
        _gather_kernel,
        out_shape=_SDS((n_slots, D_MODEL), BF16),
        grid_spec=pltpu.PrefetchScalarGridSpec(
            num_scalar_prefetch=2,
            grid=(n_tiles,),
            in_specs=[pl.BlockSpec(memory_space=pl.ANY),
                      pl.BlockSpec((GATHER_TILE, 1), lambda i, tok, tv: (i, 0))],
            out_specs=pl.BlockSpec((GATHER_TILE, D_MODEL), lambda i, tok, tv: (i, 0)),
            scratch_shapes=[pltpu.VMEM((2, GATHER_TILE, D_MODEL // 2), U32), pltpu.SemaphoreType.DMA((2,))]),
        compiler_params=_cparams(("arbitrary",)),
        name="moe_gather",
    )(slot_tok, tile_valid, u, slot_valid.reshape(n_slots, 1))


def _moe_ffn_kernel(be_ref, bv_ref, x_ref, wg_ref, wu_ref, wd_ref, o_ref, h_ref):
    i, s = pl.program_id(0), pl.program_id(1)
    n_sub = bv_ref[i]
    for k in range(1, MOE_BLOCK // MOE_SUB + 1):
        _swiglu_step(s, x_ref, wg_ref.at[0], wu_ref.at[0], wd_ref.at[0], o_ref, h_ref,
                     rows=k * MOE_SUB, guard=n_sub == k)

    @pl.when((n_sub == 0) & (s >= FF_STEPS))
    def _():
        o_ref[...] = jnp.zeros_like(o_ref)


def _moe_ffn(block_expert, block_valid, xs, w_gate, w_up, w_down):
    n_slots = xs.shape[0]
    tm = MOE_BLOCK

    def ff(i, s, be, bv):
        return (be[i], 0, jnp.where(bv[i] > 0, jnp.minimum(s, FF_STEPS - 1), FF_STEPS - 1))

    def down(i, s, be, bv):
        return (be[i], 0, jnp.where(bv[i] > 0, jnp.maximum(s - FF_STEPS, 0), OUT_STEPS - 1))

    return pl.pallas_call(
        _moe_ffn_kernel,
        out_shape=_SDS((n_slots, D_MODEL), F32),
        grid_spec=pltpu.PrefetchScalarGridSpec(
            num_scalar_prefetch=2,
            grid=(n_slots // tm, FF_STEPS + OUT_STEPS),
            in_specs=[pl.BlockSpec((tm, D_MODEL), lambda i, s, be, bv: (i, 0), pipeline_mode=pl.Buffered(1)),
                      pl.BlockSpec((1, D_MODEL, FF_TILE), ff),
                      pl.BlockSpec((1, D_MODEL, FF_TILE), ff),
                      pl.BlockSpec((1, D_FF, OUT_TILE), down)],
            out_specs=pl.BlockSpec((tm, OUT_TILE), lambda i, s, be, bv: (i, jnp.maximum(s - FF_STEPS, 0))),
            scratch_shapes=[pltpu.VMEM((tm, D_FF), BF16)]),
        compiler_params=_cparams(("arbitrary", "arbitrary"), SWIGLU_VMEM_LIMIT),
        name="moe_grouped_swiglu",
    )(block_expert, block_valid, xs, w_gate, w_up, w_down)


def _combine_kernel(pos_ref, y_hbm, wt_ref, h_ref, gp_ref, gate_ref, o_ref, buf_a, buf_b, sem):
    i = pl.program_id(0)
    n = pl.num_programs(0)
    base = i * COMBINE_TILE

    def start_tile(t, slot):
        def issue(r, carry):
            a = 2 * (t * COMBINE_TILE + r)
            _row_copy(y_hbm, pos_ref[a], buf_a.at[slot], r, sem.at[slot]).start(priority=0)
            _row_copy(y_hbm, pos_ref[a + 1], buf_b.at[slot], r, sem.at[slot]).start(priority=1)
            return carry

        lax.fori_loop(0, COMBINE_TILE, issue, 0, unroll=8)

    @pl.when(i == 0)
    def _():
        start_tile(0, 0)

    @pl.when(i + 1 < n)
    def _():
        start_tile(i + 1, (i + 1) % 2)

    slot = i % 2

    _tile_copy(y_hbm, buf_a.at[slot], sem.at[slot]).wait()
    _tile_copy(y_hbm, buf_b.at[slot], sem.at[slot]).wait()
    wt = wt_ref[...]
    y = wt[:, 0:1] * buf_a[slot] + wt[:, 1:2] * buf_b[slot]
    grp = _group_of_row(base)
    o_ref[...] = h_ref[...] + _mod_row(gate_ref, grp) * _rms(y, gp_ref[...])


def _moe_combine(pos, ys, wts, h, g_post, mod, gate_lk):
    n = wts.shape[0]
    tm = COMBINE_TILE
    return pl.pallas_call(
        _combine_kernel,
        out_shape=_SDS((n, D_MODEL), F32),
        grid_spec=pltpu.PrefetchScalarGridSpec(
            num_scalar_prefetch=1,
            grid=(n // tm,),
            in_specs=[pl.BlockSpec(memory_space=pl.ANY),
                      pl.BlockSpec((tm, 128), lambda i, pos: (i, 0)),
                      pl.BlockSpec((tm, D_MODEL), lambda i, pos: (i, 0)),
                      pl.BlockSpec((1, D_MODEL), lambda i, pos: (0, 0)),
                      pl.BlockSpec((1, 8, D_MODEL), lambda i, pos: (gate_lk[0], 0, gate_lk[1]))],
            out_specs=pl.BlockSpec((tm, D_MODEL), lambda i, pos: (i, 0)),
            scratch_shapes=[pltpu.VMEM((2, tm, D_MODEL), F32), pltpu.VMEM((2, tm, D_MODEL), F32),
                            pltpu.SemaphoreType.DMA((2,))]),
        compiler_params=_cparams(("arbitrary",)),
        name="moe_combine",
    )(pos, ys, wts, h, g_post.reshape(1, D_MODEL), mod)


def _moe_routing(idx):
    flat_e = idx[:, :TOP_K].reshape(-1)
    n_assign = flat_e.shape[0]
    onehot = (flat_e[:, None] == jnp.arange(N_EXPERTS, dtype=I32)[None, :]).astype(I32)
    csum = jnp.cumsum(onehot, axis=0)
    counts = csum[-1]
    padded = (counts + MOE_BLOCK - 1) // MOE_BLOCK * MOE_BLOCK
    pad_end = jnp.cumsum(padded)
    pad_start = pad_end - padded
    pos = jnp.sum(onehot * (pad_start[None, :] + csum - 1), axis=1).astype(I32)
    slot_tok = jnp.zeros((MOE_SLOTS,), I32).at[pos].set(jnp.arange(n_assign, dtype=I32) // TOP_K)
    slot = jnp.arange(MOE_SLOTS, dtype=I32)
    used_end = pad_start + counts
    slot_valid = jnp.any((slot[:, None] >= pad_start[None, :]) & (slot[:, None] < used_end[None, :]), axis=1)
    blk_start = jnp.arange(MOE_SLOTS // MOE_BLOCK, dtype=I32) * MOE_BLOCK
    last_start = jnp.maximum(pad_end[-1] - MOE_BLOCK, 0)
    blk_e = jnp.searchsorted(pad_end, jnp.minimum(blk_start, last_start), side='right').astype(I32)
    blk_e = jnp.minimum(blk_e, N_EXPERTS - 1)
    rows_used = jnp.clip(used_end[blk_e] - blk_start, 0, MOE_BLOCK)
    blk_sub = ((rows_used + MOE_SUB - 1) // MOE_SUB).astype(I32)
    return pos, slot_tok, slot_valid.astype(F32), blk_e, blk_sub


def _rest_of_w_in_t(w_in_t):
    dn_in = P_MAIN + 4 * DN_HEADS
    ba = w_in_t[P_MAIN:dn_in]
    cq = w_in_t[dn_in:dn_in + MLA_Q_RANK]
    ckv = w_in_t[dn_in + MLA_Q_RANK:dn_in + MLA_Q_RANK + MLA_KV_RANK]
    k_rope = w_in_t[dn_in + MLA_Q_RANK + MLA_KV_RANK:]
    pad = jnp.zeros((R_CQ - R_MISC - MLA_ROPE - 4 * DN_HEADS, D_MODEL), w_in_t.dtype)
    return jnp.concatenate([ckv, k_rope, ba, pad, cq], axis=0)


def _rearrange_w_qb(w_qb):
    w = w_qb.reshape(MLA_Q_RANK, MLA_HEADS, MLA_NOPE + MLA_ROPE)
    w = jnp.pad(w, ((0, 0), (0, 0), (0, MLA_QK - MLA_NOPE - MLA_ROPE)))
    return w.reshape(MLA_Q_RANK, MLA_HEADS * MLA_QK).astype(BF16)


def kernel(x, c, ctx, c_ctx, ada_w, ada_b, norm_g, ab_w_in, dn_conv_w, dn_a_log, dn_dt_bias, dn_norm_g, mla_q_norm_g, mla_w_qb, mla_kv_norm_g, mla_w_kvb, ab_w_out, ffn_w_gate, ffn_w_up, ffn_w_down, gqa_w_qkv, gqa_b_qkv, gqa_sink, gqa_w_out, gqa_b_out, moe_w_router, moe_w_gate, moe_w_up, moe_w_down):
    assert x.shape == (BATCH, SEQ, D_MODEL) and ctx.shape == (BATCH, CTX_LEN, D_MODEL) and ada_w.shape[0] == 2
    h = (x.reshape(N_LAT, D_MODEL), ctx.reshape(N_CTX, D_MODEL))
    cvec = jnp.zeros((8, D_MODEL), F32).at[:BATCH].set(c).at[BATCH].set(c_ctx)
    mod = _ada(cvec, ada_w, ada_b)
    (cos1, sin1), (cos2, sin2) = _rope_tables()

    u = _modulate_in(*h, norm_g[0, 0], mod, 0, 0, 1)
    w_in_t = jnp.swapaxes(ab_w_in, 1, 2)
    p = _matmul(u, w_in_t, None, BF16, MM_ROW_TILE, 1024, n=P_MAIN, w_is_transposed=True)
    p_rest = _matmul(u, _rest_of_w_in_t(w_in_t[0]), None, F32, MM_ROW_TILE, R_WIDTH // 2, w_is_transposed=True)
    o_f, o_b = _deltanet(_dn_conv(p, dn_conv_w[0]), _dn_gates(p_rest, dn_a_log[0], dn_dt_bias[0]))
    q, k, v = _mla_proj(p_rest, mla_q_norm_g[0], mla_kv_norm_g[0], _rearrange_w_qb(mla_w_qb[0]),
                        mla_w_kvb[0].astype(BF16), cos1, sin1)
    a = _ab_mix(o_f, o_b, p, dn_norm_g[0], _mla_attention(q, k, v))
    y = _matmul(a, ab_w_out[0].astype(BF16), None, BF16, MM_ROW_TILE, 1024)
    h, u = _post(h, y, norm_g[0, 1], norm_g[0, 2], mod, (0, 2), (0, 3), (0, 4), BF16)
    y = _ffn(u, ffn_w_gate[0], ffn_w_up[0], ffn_w_down[0])
    h, u = _post(h, y, norm_g[0, 3], norm_g[1, 0], mod, (0, 5), (1, 0), (1, 1), BF16)

    qkv = _gqa_qkv(u, gqa_w_qkv[0].astype(BF16), gqa_b_qkv[0], cos2, sin2)
    o = _gqa_attention(qkv, gqa_sink[0])
    y = _matmul(o, gqa_w_out[0].astype(BF16), gqa_b_out[0], BF16, 1024, 1024)
    h, u, idx, wts = _post(h, y, norm_g[1, 1], norm_g[1, 2], mod, (1, 2), (1, 3), (1, 4), F32, moe_w_router[0])
    pos, slot_tok, slot_valid, blk_e, blk_sub = _moe_routing(idx)
    xs = _moe_gather(slot_tok, slot_valid, u)
    ys = _moe_ffn(blk_e, blk_sub, xs, moe_w_gate[0], moe_w_up[0], moe_w_down[0])
    out = _moe_combine(pos, ys, wts, h, norm_g[1, 3], mod, (1, 5))
    return out.reshape(BATCH, SEQ, D_MODEL)
```

```python
import functools
import math

import jax
import jax.numpy as jnp
from jax import lax
from jax.experimental import pallas as pl
from jax.experimental.pallas import tpu as pltpu

F32 = jnp.float32
BF16 = jnp.bfloat16
I32 = jnp.int32

D_MODEL = 2048
BATCH = 2
SEQ = 4096
CTX_LEN = 256
GRID_W = 64
EPS = 1e-6
ROPE_THETA = 10000.0

DN_HEADS = 8
DN_HEAD_DIM = 128
DN_CONV = 5
DN_CHUNK = 64
MLA_HEADS = 8
MLA_Q_RANK = 768
MLA_KV_RANK = 512
MLA_NOPE = 128
MLA_ROPE = 64
MLA_V = 128
GQA_HEADS = 32
GQA_KV_HEADS = 4
GQA_HEAD_DIM = 64
WINDOW = 128
Q_BLOCK = 128
D_FF = 7168
N_EXPERTS = 8
TOP_K = 2

DN_W = DN_HEADS * DN_HEAD_DIM
DN_QKV = 3 * DN_W
N_LAT = BATCH * SEQ
N_CTX = BATCH * CTX_LEN
N_TOK = N_LAT + N_CTX
GQA_Q = GQA_HEADS * GQA_HEAD_DIM
GQA_KV = GQA_KV_HEADS * GQA_HEAD_DIM

P_QKV = 0
P_Z = DN_QKV
P_MAIN = P_Z + DN_W
R_CKV = 0
R_MISC = MLA_KV_RANK
R_CQ = MLA_Q_RANK
R_WIDTH = R_CQ + MLA_Q_RANK

ROW_TILE = 512
MM_ROW_TILE = 1088
MOE_BLOCK = 1024
MOE_SUB = 128
MOE_SLOTS = N_LAT * TOP_K + N_EXPERTS * MOE_BLOCK
VMEM_LIMIT = 56 * 1024 * 1024

_SDS = jax.ShapeDtypeStruct


def _cparams(sem, vmem=None):
    return pltpu.CompilerParams(dimension_semantics=sem, vmem_limit_bytes=vmem)


def _dot(a, b):
    return jnp.dot(a, b, preferred_element_type=F32)


def _dot_nt(a, b):
    return lax.dot_general(a, b, (((1,), (1,)), ((), ())), preferred_element_type=F32)


def _dot_tn(a, b):
    return lax.dot_general(a, b, (((0,), (0,)), ((), ())), preferred_element_type=F32)


def _silu(x):
    return x * jax.nn.sigmoid(x)


def _rms(x, g):
    return x * lax.rsqrt(jnp.mean(x * x, axis=-1, keepdims=True) + EPS) * g


def _group_of_row(r0):
    return (r0 >= SEQ).astype(I32) + (r0 >= 2 * SEQ).astype(I32)


def _mod_spec(layer, k):
    return pl.BlockSpec((1, 8, D_MODEL), lambda *_: (layer, 0, k))


def _mod_row(ref, grp):
    return ref[0, pl.ds(grp, 1), :]


def _ada_kernel(c_ref, w_ref, b_ref, o_ref):
    s = _silu(c_ref[...]).astype(BF16)
    o_ref[0] = _dot(s, w_ref[0].astype(BF16)) + b_ref[0]


def _ada(cvec, ada_w, ada_b):
    n_layers, _, n = ada_w.shape
    tn = 1024
    return pl.pallas_call(
        _ada_kernel,
        out_shape=_SDS((n_layers, 8, n), F32),
        grid=(n_layers, n // tn),
        in_specs=[pl.BlockSpec((8, D_MODEL), lambda l, j: (0, 0)),
                  pl.BlockSpec((1, D_MODEL, tn), lambda l, j: (l, 0, j)),
                  pl.BlockSpec((1, 1, tn), lambda l, j: (l, 0, j))],
        out_specs=pl.BlockSpec((1, 8, tn), lambda l, j: (l, 0, j)),
        compiler_params=_cparams(("parallel", "parallel")),
        name="ada_mod",
    )(cvec, ada_w, ada_b.reshape(n_layers, 1, n))


_LAT_TILES = N_LAT // ROW_TILE
assert N_CTX == ROW_TILE


def _split_specs(width, col=0):
    lat = pl.BlockSpec((ROW_TILE, width), lambda i, *_: (jnp.minimum(i, _LAT_TILES - 1), col))
    ctx = pl.BlockSpec((ROW_TILE, width), lambda i, *_: (0, col))
    return [lat, ctx]


def _split_read(lat_ref, ctx_ref, sl=slice(None)):
    return jnp.where(pl.program_id(0) == _LAT_TILES, ctx_ref[:, sl], lat_ref[:, sl])


def _modin_kernel(x_ref, c_ref, g_ref, sh_ref, sc_ref, u_ref):
    grp = _group_of_row(pl.program_id(0) * ROW_TILE)
    h = _split_read(x_ref, c_ref)
    u = _rms(h, g_ref[...]) * (1.0 + _mod_row(sc_ref, grp)) + _mod_row(sh_ref, grp)
    u_ref[...] = u.astype(u_ref.dtype)


def _modulate_in(x, ctx, g, mod, layer, k_shift, k_scale):
    row = pl.BlockSpec((ROW_TILE, D_MODEL), lambda i: (i, 0))
    vec = pl.BlockSpec((1, D_MODEL), lambda i: (0, 0))
    return pl.pallas_call(
        _modin_kernel,
        out_shape=_SDS((N_TOK, D_MODEL), BF16),
        grid=(N_TOK // ROW_TILE,),
        in_specs=_split_specs(D_MODEL) + [vec, _mod_spec(layer, k_shift), _mod_spec(layer, k_scale)],
        out_specs=row,
        compiler_params=_cparams(("parallel",)),
        name="modulate_in",
    )(x, ctx, g.reshape(1, D_MODEL), mod, mod)


def _route_top2(logits):
    lane = lax.broadcasted_iota(I32, logits.shape, 1)
    neg = -jnp.inf
    l0 = jnp.where(lane < N_EXPERTS, logits, neg)
    m1 = jnp.max(l0, axis=-1, keepdims=True)
    i1 = jnp.min(jnp.where(l0 == m1, lane, 128), axis=-1, keepdims=True)
    l1 = jnp.where(lane == i1, neg, l0)
    m2 = jnp.max(l1, axis=-1, keepdims=True)
    i2 = jnp.min(jnp.where(l1 == m2, lane, 128), axis=-1, keepdims=True)
    e = jnp.exp(m2 - m1)
    w1 = 1.0 / (1.0 + e)
    idx = jnp.where(lane == 0, i1, jnp.where(lane == 1, i2, 0))
    wts = jnp.where(lane == 0, w1, jnp.where(lane == 1, e * w1, 0.0))
    return idx, wts


U32 = jnp.uint32


def _pack_bf16_halves(x):
    n = x.shape[1] // 2
    lo = lax.bitcast_convert_type(x[:, :n].astype(F32), U32) >> 16
    hi = (lax.bitcast_convert_type(x[:, n:].astype(F32), U32) >> 16) << 16
    return hi | lo


def _unpack_bf16_halves(w):
    lo = lax.bitcast_convert_type(w << 16, F32).astype(BF16)
    hi = lax.bitcast_convert_type((w >> 16) << 16, F32).astype(BF16)
    return lo, hi


def _post_kernel(*refs, split, route):
    refs = list(refs)
    if split:
        h = _split_read(refs.pop(0), refs.pop(0))
    else:
        h = refs.pop(0)[...]
    y_ref, gp_ref, gate_ref, gn_ref, sh_ref, sc_ref = refs[:6]
    refs = refs[6:]
    wr_ref = refs.pop(0) if route else None
    hn_ref, u_ref = refs[:2]
    grp = _group_of_row(pl.program_id(0) * ROW_TILE)
    hn = h + _mod_row(gate_ref, grp) * _rms(y_ref[...].astype(F32), gp_ref[...])
    hn_ref[...] = hn
    u = _rms(hn, gn_ref[...]) * (1.0 + _mod_row(sc_ref, grp)) + _mod_row(sh_ref, grp)
    if route:
        ub = u.astype(BF16)
        u_ref[...] = _pack_bf16_halves(ub)
        idx_ref, wt_ref = refs[2:4]
        idx_ref[...], wt_ref[...] = _route_top2(_dot(ub, wr_ref[...]))
    else:
        u_ref[...] = u.astype(u_ref.dtype)


def _post(h, y, g_post, g_next, mod, gate_lk, shift_lk, scale_lk, u_dtype, w_router=None):
    n = y.shape[0]
    split = isinstance(h, tuple)
    route = w_router is not None
    row = pl.BlockSpec((ROW_TILE, D_MODEL), lambda i: (i, 0))
    vec = pl.BlockSpec((1, D_MODEL), lambda i: (0, 0))
    slab = pl.BlockSpec((ROW_TILE, 128), lambda i: (i, 0))
    h_specs, h_args = (_split_specs(D_MODEL), list(h)) if split else ([row], [h])
    extra_specs, extra_args, extra_out, extra_out_specs = [], [], (), ()
    u_shape, u_spec = _SDS((n, D_MODEL), u_dtype), row
    if route:
        w = jnp.zeros((D_MODEL, 128), BF16).at[:, :N_EXPERTS].set(w_router.astype(BF16))
        extra_specs, extra_args = [pl.BlockSpec((D_MODEL, 128), lambda i: (0, 0))], [w]
        extra_out, extra_out_specs = (_SDS((n, 128), I32), _SDS((n, 128), F32)), (slab, slab)
        u_shape, u_spec = _SDS((n, D_MODEL // 2), U32), pl.BlockSpec((ROW_TILE, D_MODEL // 2), lambda i: (i, 0))
    return pl.pallas_call(
        functools.partial(_post_kernel, split=split, route=route),
        out_shape=(_SDS((n, D_MODEL), F32), u_shape) + extra_out,
        grid=(n // ROW_TILE,),
        in_specs=h_specs + [row, vec, _mod_spec(*gate_lk), vec, _mod_spec(*shift_lk), _mod_spec(*scale_lk)] + extra_specs,
        out_specs=(row, u_spec) + extra_out_specs,
        compiler_params=_cparams(("parallel",)),
        name="residual_post",
    )(*h_args, y, g_post.reshape(1, D_MODEL), mod, g_next.reshape(1, D_MODEL), mod, mod, *extra_args)


def _mm_kernel(a_ref, w_ref, b_ref, o_ref, *, w_is_transposed):
    dot = _dot_nt if w_is_transposed else _dot
    o_ref[...] = (dot(a_ref[...], w_ref[...].astype(BF16)) + b_ref[...]).astype(o_ref.dtype)


def _matmul(a, w, bias, out_dtype, tm, tn, n=None, w_is_transposed=False):
    m, k = a.shape
    if w_is_transposed:
        n = w.shape[-2] if n is None else n
        if w.ndim == 3:
            w_spec = pl.BlockSpec((None, tn, k), lambda i, j: (0, j, 0))
        else:
            w_spec = pl.BlockSpec((tn, k), lambda i, j: (j, 0))
    else:
        n = w.shape[-1] if n is None else n
        w_spec = pl.BlockSpec((k, tn), lambda i, j: (0, j))
    if bias is None:
        bias = jnp.zeros((n,), F32)
    return pl.pallas_call(
        functools.partial(_mm_kernel, w_is_transposed=w_is_transposed),
        out_shape=_SDS((m, n), out_dtype),
        grid=(m // tm, n // tn),
        in_specs=[pl.BlockSpec((tm, k), lambda i, j: (i, 0)),
                  w_spec,
                  pl.BlockSpec((1, tn), lambda i, j: (0, j))],
        out_specs=pl.BlockSpec((tm, tn), lambda i, j: (i, j)),
        compiler_params=_cparams(("parallel", "arbitrary"), VMEM_LIMIT),
        name="matmul",
    )(a, w, bias.reshape(1, n))


FF_TILE = 512
SWIGLU_VMEM_LIMIT = 60 * 1024 * 1024
OUT_TILE = 256
FF_STEPS = D_FF // FF_TILE
OUT_STEPS = D_MODEL // OUT_TILE


def _swiglu_step(s, x_ref, wg_ref, wu_ref, wd_ref, o_ref, h_ref, rows=None, guard=True):
    total = x_ref.shape[0]
    rows = total if rows is None else rows

    @pl.when(guard & (s < FF_STEPS))
    def _():
        x = x_ref[0:rows, :]
        a = _dot(x, wg_ref[...].astype(BF16))
        b = _dot(x, wu_ref[...].astype(BF16))
        h_ref[0:rows, pl.ds(pl.multiple_of(s * FF_TILE, FF_TILE), FF_TILE)] = (_silu(a) * b).astype(BF16)

    @pl.when(guard & (s >= FF_STEPS))
    def _():
        o_ref[0:rows, :] = _dot(h_ref[0:rows, :], wd_ref[...].astype(BF16)).astype(o_ref.dtype)
        if rows < total:
            o_ref[rows:total, :] = jnp.zeros((total - rows, o_ref.shape[1]), o_ref.dtype)


def _ffn_kernel(u_ref, wg_ref, wu_ref, wd_ref, o_ref, h_ref):
    _swiglu_step(pl.program_id(1), u_ref, wg_ref, wu_ref, wd_ref, o_ref, h_ref)


def _ffn(u, w_gate, w_up, w_down):
    m = u.shape[0]
    tm = MM_ROW_TILE

    def ff(i, s):
        return (0, jnp.minimum(s, FF_STEPS - 1))

    def out(s):
        return jnp.maximum(s - FF_STEPS, 0)

    return pl.pallas_call(
        _ffn_kernel,
        out_shape=_SDS((m, D_MODEL), BF16),
        grid=(m // tm, FF_STEPS + OUT_STEPS),
        in_specs=[pl.BlockSpec((tm, D_MODEL), lambda i, s: (i, 0), pipeline_mode=pl.Buffered(1)),
                  pl.BlockSpec((D_MODEL, FF_TILE), ff),
                  pl.BlockSpec((D_MODEL, FF_TILE), ff),
                  pl.BlockSpec((D_FF, OUT_TILE), lambda i, s: (0, out(s)))],
        out_specs=pl.BlockSpec((tm, OUT_TILE), lambda i, s: (i, out(s))),
        scratch_shapes=[pltpu.VMEM((tm, D_FF), BF16)],
        compiler_params=_cparams(("parallel", "arbitrary"), SWIGLU_VMEM_LIMIT),
        name="swiglu_ffn",
    )(u, w_gate, w_up, w_down)


CONV_TILE = 256
CONV_HALO = 16
_SEQ_STARTS = (0, SEQ // CONV_TILE, 2 * SEQ // CONV_TILE, 2 * SEQ // CONV_TILE + 1)
_SEQ_LASTS = (SEQ // CONV_TILE - 1, 2 * SEQ // CONV_TILE - 1, 2 * SEQ // CONV_TILE, 2 * SEQ // CONV_TILE + 1)


def _conv_kernel(prev_ref, cur_ref, next_ref, w_ref, o_ref, buf):
    i = pl.program_id(0)
    j = pl.program_id(1)
    tm = CONV_TILE
    first = functools.reduce(jnp.logical_or, [i == s for s in _SEQ_STARTS])
    last = functools.reduce(jnp.logical_or, [i == s for s in _SEQ_LASTS])
    hb = CONV_HALO
    buf[0:hb, :] = jnp.where(first, 0.0, prev_ref[...].astype(F32))
    buf[hb:hb + tm, :] = cur_ref[...].astype(F32)
    buf[hb + tm:2 * hb + tm, :] = jnp.where(last, 0.0, next_ref[...].astype(F32))
    pad = DN_CONV // 2
    acc = buf[hb - pad:hb - pad + tm, :] * w_ref[0:1, :]
    for d in range(1, DN_CONV):
        acc = acc + buf[hb - pad + d:hb - pad + d + tm, :] * w_ref[d:d + 1, :]
    x = _silu(acc)
    q_scale = jnp.where(j == 0, DN_HEAD_DIM ** -0.5, 1.0)
    for hh in range(DN_HEADS):
        xh = x[:, hh * DN_HEAD_DIM:(hh + 1) * DN_HEAD_DIM]
        inv = lax.rsqrt(jnp.sum(xh * xh, axis=-1, keepdims=True) + EPS) * q_scale
        o_ref[:, hh * DN_HEAD_DIM:(hh + 1) * DN_HEAD_DIM] = xh * jnp.where(j == 2, 1.0, inv)


def _dn_conv(p, conv_w):
    n = p.shape[0]
    tm = CONV_TILE
    hb = CONV_HALO
    per_tile = tm // hb
    n_halo = n // hb
    return pl.pallas_call(
        _conv_kernel,
        out_shape=_SDS((n, DN_QKV), F32),
        grid=(n // tm, 3),
        in_specs=[pl.BlockSpec((hb, DN_W), lambda i, j: (jnp.maximum(i * per_tile - 1, 0), j)),
                  pl.BlockSpec((tm, DN_W), lambda i, j: (i, j)),
                  pl.BlockSpec((hb, DN_W), lambda i, j: (jnp.minimum((i + 1) * per_tile, n_halo - 1), j)),
                  pl.BlockSpec((DN_CONV, DN_W), lambda i, j: (0, j))],
        out_specs=pl.BlockSpec((tm, DN_W), lambda i, j: (i, j)),
        scratch_shapes=[pltpu.VMEM((tm + 2 * hb, DN_W), F32)],
        compiler_params=_cparams(("parallel", "parallel")),
        name="dn_conv",
    )(p, p, p, conv_w)


def _gates_kernel(m_ref, a_ref, dt_ref, o_ref):
    x = pltpu.roll(m_ref[...], 64, 1)
    lane = lax.broadcasted_iota(I32, x.shape, 1)
    z = x + dt_ref[...]
    softplus = jnp.maximum(z, 0.0) + jnp.log1p(jnp.exp(-jnp.abs(z)))
    g = a_ref[...] * softplus
    o_ref[...] = jnp.where(lane < 2 * DN_HEADS, jax.nn.sigmoid(x), jnp.where(lane < 4 * DN_HEADS, g, 0.0))


def _dn_gates(p, a_log, dt_bias):
    n = p.shape[0]
    neg_a = jnp.zeros((1, 128), F32).at[0, 2 * DN_HEADS:4 * DN_HEADS].set(-jnp.exp(a_log.astype(F32)).reshape(-1))
    dtb = jnp.zeros((1, 128), F32).at[0, 2 * DN_HEADS:4 * DN_HEADS].set(dt_bias.astype(F32).reshape(-1))
    tm = ROW_TILE
    vec = pl.BlockSpec((1, 128), lambda i: (0, 0))
    return pl.pallas_call(
        _gates_kernel,
        out_shape=_SDS((n, 128), F32),
        grid=(n // tm,),
        in_specs=[pl.BlockSpec((tm, 128), lambda i: (i, R_MISC // 128)), vec, vec],
        out_specs=pl.BlockSpec((tm, 128), lambda i: (i, 0)),
        compiler_params=_cparams(("parallel",)),
        name="dn_gates",
    )(p, neg_a, dtb)


_CTX_CHUNKS = CTX_LEN // DN_CHUNK
_LAT_CHUNKS = SEQ // DN_CHUNK
_DN_STEPS = _CTX_CHUNKS + _LAT_CHUNKS


def _dn_fwd_block(b, t):
    return jnp.where(t < _CTX_CHUNKS, (2 * SEQ + b * CTX_LEN) // DN_CHUNK + t, b * _LAT_CHUNKS + (t - _CTX_CHUNKS))


def _dn_bwd_block(b, t):
    return jnp.where(t < _CTX_CHUNKS, (2 * SEQ + b * CTX_LEN) // DN_CHUNK + (_CTX_CHUNKS - 1 - t),
                     b * _LAT_CHUNKS + (_DN_STEPS - 1 - t))


DN_GROUP = 2
_GROUP_ROWS = DN_GROUP * DN_CHUNK
_GROUP_STATE = DN_GROUP * DN_HEAD_DIM


_DN_BASE = 8


def _block_diag(x):
    zero = jnp.zeros((DN_CHUNK, DN_HEAD_DIM), x.dtype)
    rows = []
    for hh in range(DN_GROUP):
        piece = x[hh * DN_CHUNK:(hh + 1) * DN_CHUNK]
        rows.append(jnp.concatenate([piece if j == hh else zero for j in range(DN_GROUP)], axis=1))
    return jnp.concatenate(rows, axis=0)


def _dn_groups(items, eye, base_mask):
    it = items
    n = range(len(it))
    gx = [jnp.broadcast_to(it[g]["gc"], (_GROUP_ROWS, _GROUP_ROWS)) for g in n]
    gamma = [jnp.where(it[g]["incl"], jnp.exp(jnp.where(it[g]["incl"], gx[g] - gx[g].T, 0.0)), 0.0) for g in n]
    kb = [it[g]["k"].astype(BF16) for g in n]
    kk = [_dot_nt(kb[g], kb[g]) for g in n]
    a = [jnp.where(it[g]["strict"], it[g]["beta"] * kk[g] * gamma[g], 0.0) for g in n]
    d = [jnp.where(base_mask, a[g], 0.0) for g in n]
    tinv = [eye - d[g] for g in n]
    pw = [d[g].astype(BF16) for g in n]
    for _ in range(int(math.log2(_DN_BASE)) - 1):
        pw = [_dot(pw[g], pw[g]).astype(BF16) for g in n]
        tinv = [tinv[g] + _dot(tinv[g].astype(BF16), pw[g]) for g in n]
    for lvl in range(len(it[0]["levels"])):
        tb = [tinv[g].astype(BF16) for g in n]
        mt = [_dot(jnp.where(it[g]["levels"][lvl], a[g], 0.0).astype(BF16), tb[g]).astype(BF16) for g in n]
        tinv = [tinv[g] - _dot(tb[g], mt[g]) for g in n]
    tb = [tinv[g].astype(BF16) for g in n]
    egc = [jnp.exp(it[g]["gc"]) for g in n]
    rhs = [jnp.concatenate([it[g]["v"] * it[g]["beta"], it[g]["k"] * (it[g]["beta"] * egc[g])], axis=1) for g in n]
    rhs_hi = [rhs[g].astype(BF16) for g in n]
    rhs_lo = [(rhs[g] - rhs_hi[g].astype(F32)).astype(BF16) for g in n]
    sol = [_dot(jnp.concatenate([tb[g], tb[g]], axis=1), jnp.concatenate([rhs_hi[g], rhs_lo[g]], axis=0)) for g in n]
    qk = [(_dot_nt(it[g]["q"].astype(BF16), kb[g]) * gamma[g]).astype(BF16) for g in n]
    s = [it[g]["s_ref"][...] for g in n]
    sb = [s[g].astype(BF16) for g in n]
    v_new = [sol[g][:, :DN_HEAD_DIM] - _dot(_block_diag(sol[g][:, DN_HEAD_DIM:].astype(BF16)), sb[g]) for g in n]
    vb = [v_new[g].astype(BF16) for g in n]
    o = [_dot(_block_diag((it[g]["q"] * egc[g]).astype(BF16)), sb[g]) + _dot(qk[g], vb[g]) for g in n]
    for g in n:
        k_dec = _block_diag((it[g]["k"] * jnp.exp(it[g]["g_last"] - it[g]["gc"])).astype(BF16))
        it[g]["s_ref"][...] = s[g] * jnp.exp(it[g]["g_last_state"]) + _dot_tn(k_dec, vb[g])
    return o


def _stack_cols(x, lanes, rows_each):
    pieces = [x[:, l:l + 1] for l in lanes]
    if rows_each is not None:
        pieces = [jnp.broadcast_to(pc, (rows_each, 1)) for pc in pieces]
    return jnp.concatenate(pieces, axis=0)


def _dn_kernel(*refs):
    n_in = BATCH * 2 * 4
    in_refs, (ofl_ref, ofc_ref, obl_ref, obc_ref), (s_ref, o_scr) = refs[:n_in], refs[n_in:n_in + 4], refs[n_in + 4:]
    t = pl.program_id(0)

    @pl.when(t == 0)
    def _():
        s_ref[...] = jnp.zeros_like(s_ref)

    c = DN_CHUNK
    ri = lax.broadcasted_iota(I32, (_GROUP_ROWS, _GROUP_ROWS), 0)
    ci = lax.broadcasted_iota(I32, (_GROUP_ROWS, _GROUP_ROWS), 1)
    same_head = (ri // c) == (ci // c)
    eye = (ri == ci).astype(F32)
    r1 = lax.broadcasted_iota(I32, (c, c), 0)
    c1 = lax.broadcasted_iota(I32, (c, c), 1)
    base_mask = (ri // _DN_BASE) == (ci // _DN_BASE)
    masks = []
    for d in range(2):
        incl = same_head & ((ri >= ci) if d == 0 else (ri <= ci))
        strict = same_head & ((ri > ci) if d == 0 else (ri < ci))
        levels = []
        size = _DN_BASE
        while size < c:
            lo_blk, hi_blk = ((ci, ri) if d == 0 else (ri, ci))
            levels.append(((ri // (2 * size)) == (ci // (2 * size)))
                          & ((lo_blk // size) % 2 == 0) & ((hi_blk // size) % 2 == 1))
            size *= 2
        masks.append((incl, strict, levels))
    items, where = [], []
    for b, d in [(b, d) for b in range(BATCH) for d in range(2)]:
        q_ref, k_ref, v_ref, g_ref = in_refs[(b * 2 + d) * 4:(b * 2 + d) * 4 + 4]
        incl, strict, levels = masks[d]
        gates = g_ref[...]
        tri = ((r1 >= c1) if d == 0 else (r1 <= c1)).astype(BF16)
        g_hi = gates.astype(BF16)
        g_r1 = gates - g_hi.astype(F32)
        g_mid = g_r1.astype(BF16)
        g_lo = (g_r1 - g_mid.astype(F32)).astype(BF16)
        gc = _dot(tri, g_hi) + _dot(tri, g_mid) + _dot(tri, g_lo)
        last_row = c - 1 if d == 0 else 0
        tot = gc[last_row:last_row + 1, :]
        for grp in range(DN_HEADS // DN_GROUP):
            heads = range(grp * DN_GROUP, (grp + 1) * DN_GROUP)
            lanes_b = [d * DN_HEADS + hh for hh in heads]
            lanes_g = [2 * DN_HEADS + lb for lb in lanes_b]

            def stack(ref):
                return jnp.concatenate([ref[:, hh * DN_HEAD_DIM:(hh + 1) * DN_HEAD_DIM] for hh in heads], axis=0)

            items.append(dict(q=stack(q_ref), k=stack(k_ref), v=stack(v_ref),
                              beta=_stack_cols(gates, lanes_b, None), gc=_stack_cols(gc, lanes_g, None),
                              g_last=_stack_cols(tot, lanes_g, c), g_last_state=_stack_cols(tot, lanes_g, DN_HEAD_DIM),
                              incl=incl, strict=strict, levels=levels, s_ref=s_ref.at[b, d, grp]))
            where.append((d, b, heads))
    outs = _dn_groups(items, eye, base_mask)
    for o, (d, b, heads) in zip(outs, where):
        for j, hh in enumerate(heads):
            o_scr[d, b, :, hh * DN_HEAD_DIM:(hh + 1) * DN_HEAD_DIM] = o[j * c:(j + 1) * c]

    @pl.when(t < _CTX_CHUNKS)
    def _():
        ofc_ref[...] = o_scr[0]
        obc_ref[...] = o_scr[1]

    @pl.when(t >= _CTX_CHUNKS)
    def _():
        ofl_ref[...] = o_scr[0]
        obl_ref[...] = o_scr[1]


def _deltanet(qkv, gates):
    c = DN_CHUNK
    ins, args = [], []
    for b in range(BATCH):
        for fn in (_dn_fwd_block, _dn_bwd_block):
            for width, col, arr in ((DN_W, 0, qkv), (DN_W, 1, qkv), (DN_W, 2, qkv), (128, 0, gates)):
                ins.append(pl.BlockSpec((c, width), functools.partial(lambda t, b, fn, col: (fn(b, t), col),
                                                                      b=b, fn=fn, col=col)))
                args.append(arr)
    last = _DN_STEPS - 1
    lat = _SDS((BATCH, SEQ, DN_W), F32)
    ctx = _SDS((BATCH, CTX_LEN, DN_W), F32)
    blk = (BATCH, c, DN_W)
    outs = pl.pallas_call(
        _dn_kernel,
        out_shape=(lat, ctx, lat, ctx),
        grid=(_DN_STEPS,),
        in_specs=ins,
        out_specs=(pl.BlockSpec(blk, lambda t: (0, jnp.maximum(t - _CTX_CHUNKS, 0), 0)),
                   pl.BlockSpec(blk, lambda t: (0, jnp.minimum(t, _CTX_CHUNKS - 1), 0)),
                   pl.BlockSpec(blk, lambda t: (0, jnp.minimum(last - t, _LAT_CHUNKS - 1), 0)),
                   pl.BlockSpec(blk, lambda t: (0, jnp.maximum(_CTX_CHUNKS - 1 - t, 0), 0))),
        scratch_shapes=[pltpu.VMEM((BATCH, 2, DN_HEADS // DN_GROUP, _GROUP_STATE, DN_HEAD_DIM), F32),
                        pltpu.VMEM((2, BATCH, c, DN_W), F32)],
        compiler_params=_cparams(("arbitrary",)),
        name="deltanet",
    )(*args)
    ofl, ofc, obl, obc = outs
    return ((ofl.reshape(N_LAT, DN_W), ofc.reshape(N_CTX, DN_W)),
            (obl.reshape(N_LAT, DN_W), obc.reshape(N_CTX, DN_W)))


def _rope_tables():
    t = jnp.arange(SEQ)
    row = (t // GRID_W).astype(F32)
    col = (t % GRID_W).astype(F32)
    n_freq = MLA_ROPE // 4
    inv_freq = ROPE_THETA ** (-jnp.arange(n_freq, dtype=F32) / n_freq)
    ang = jnp.concatenate([row[:, None] * inv_freq, col[:, None] * inv_freq], axis=-1)
    cos, sin = jnp.cos(ang), jnp.sin(ang)
    cos64 = jnp.concatenate([cos, cos], axis=-1)
    sin64 = jnp.concatenate([-sin, sin], axis=-1)

    def assemble(lat_cos, lat_sin):
        c = jnp.concatenate([jnp.tile(lat_cos, (BATCH, 1)), jnp.ones((N_CTX, 128), F32)], axis=0)
        s = jnp.concatenate([jnp.tile(lat_sin, (BATCH, 1)), jnp.zeros((N_CTX, 128), F32)], axis=0)
        return c, s

    cos_l = jnp.concatenate([cos64, jnp.ones((SEQ, 64), F32)], axis=-1)
    sin_l = jnp.concatenate([sin64, jnp.zeros((SEQ, 64), F32)], axis=-1)
    one_head = assemble(cos_l, sin_l)
    two_heads = assemble(jnp.tile(cos64, (1, 2)), jnp.tile(sin64, (1, 2)))
    return one_head, two_heads


def _rope128(x, cosv, sinv):
    lane = lax.broadcasted_iota(I32, x.shape, 1)
    swapped = jnp.where((lane % 64) < 32, pltpu.roll(x, 96, 1), pltpu.roll(x, 32, 1))
    return x * cosv + swapped * sinv


MLA_QK = 256
MLA_VX = 256


def _mla_proj_kernel(ckv_ref, cq_ref, misc_ref, gq_ref, gkv_ref, wq_ref, wkv_ref, cos_ref, sin_ref,
                     q_ref, k_ref, v_ref):
    cq = _rms(cq_ref[...].astype(F32), gq_ref[...]).astype(BF16)
    ckv = _rms(ckv_ref[...].astype(F32), gkv_ref[...]).astype(BF16)
    qf = _dot(cq, wq_ref[...])
    kvf = _dot(ckv, wkv_ref[...])
    cosv, sinv = cos_ref[...], sin_ref[...]
    lane = lax.broadcasted_iota(I32, cosv.shape, 1)
    k_rope = _rope128(jnp.where(lane < MLA_ROPE, misc_ref[...], 0.0), cosv, sinv).astype(BF16)
    scale = (MLA_NOPE + MLA_ROPE) ** -0.5
    for hh in range(MLA_HEADS):
        lo, mid, hi = hh * MLA_QK, hh * MLA_QK + 128, (hh + 1) * MLA_QK
        q_ref[:, lo:mid] = (qf[:, lo:mid] * scale).astype(BF16)
        q_ref[:, mid:hi] = (_rope128(qf[:, mid:hi], cosv, sinv) * scale).astype(BF16)
        k_ref[:, lo:mid] = kvf[:, lo:mid].astype(BF16)
        k_ref[:, mid:hi] = k_rope
        v_ref[:, lo:mid] = kvf[:, mid:hi].astype(BF16)
        v_ref[:, mid:hi] = jnp.ones((kvf.shape[0], MLA_VX - MLA_V), BF16)


def _mla_proj(p_rest, q_norm_g, kv_norm_g, w_q, w_kv, cosv, sinv):
    n = p_rest.shape[0]
    tm = ROW_TILE
    wide = MLA_HEADS * MLA_QK

    def rows(width, col):
        return pl.BlockSpec((tm, width), lambda i: (i, col))

    def whole(shape):
        return pl.BlockSpec(shape, lambda i: (0, 0))

    return pl.pallas_call(
        _mla_proj_kernel,
        out_shape=(_SDS((n, wide), BF16), _SDS((n, wide), BF16), _SDS((n, MLA_HEADS * MLA_VX), BF16)),
        grid=(n // tm,),
        in_specs=[rows(MLA_KV_RANK, R_CKV // MLA_KV_RANK), rows(MLA_Q_RANK, R_CQ // MLA_Q_RANK),
                  rows(128, R_MISC // 128), whole((1, MLA_Q_RANK)), whole((1, MLA_KV_RANK)),
                  whole((MLA_Q_RANK, wide)), whole((MLA_KV_RANK, wide)), rows(128, 0), rows(128, 0)],
        out_specs=(rows(wide, 0), rows(wide, 0), rows(MLA_HEADS * MLA_VX, 0)),
        compiler_params=_cparams(("parallel",), VMEM_LIMIT),
        name="mla_proj",
    )(p_rest, p_rest, p_rest, q_norm_g.reshape(1, -1), kv_norm_g.reshape(1, -1), w_q, w_kv, cosv, sinv)


MLA_KEY_CHUNK = 2048


def _flash_kernel(*refs, chunks):
    n_seg = (len(refs) - 2) // 2
    q_ref, o_ref = refs[0], refs[-1]
    k_refs, v_refs = refs[1:1 + n_seg], refs[1 + n_seg:1 + 2 * n_seg]
    q = q_ref[...]

    def scores(c):
        seg, off, size = c
        return _dot_nt(q, k_refs[seg][off:off + size, :])

    m = jnp.full((q.shape[0], 1), -jnp.inf, F32)
    acc = jnp.zeros((q.shape[0], MLA_VX), F32)
    s_next = scores(chunks[0])
    for j, (seg, off, size) in enumerate(chunks):
        s = s_next
        if j + 1 < len(chunks):
            s_next = scores(chunks[j + 1])
        m_new = jnp.maximum(m, jnp.max(s, axis=-1, keepdims=True))
        p = jnp.exp(s - m_new).astype(BF16)
        acc = jnp.exp(m - m_new) * acc + _dot(p, v_refs[seg][off:off + size, :])
        m = m_new
    o_ref[...] = (acc[:, :MLA_V] / acc[:, MLA_V:]).astype(o_ref.dtype)


def _mla_attention(q, k, v):
    tq = 1024
    nq = SEQ // tq
    ctx_blk = 2 * SEQ // CTX_LEN
    lat_chunks = ((0, 0, CTX_LEN),) + tuple((1, off, MLA_KEY_CHUNK) for off in range(0, SEQ, MLA_KEY_CHUNK))
    lat = pl.pallas_call(
        functools.partial(_flash_kernel, chunks=lat_chunks),
        out_shape=_SDS((N_LAT, MLA_HEADS * MLA_V), BF16),
        grid=(BATCH, MLA_HEADS, nq),
        in_specs=[pl.BlockSpec((tq, MLA_QK), lambda b, h, i: (b * nq + i, h)),
                  pl.BlockSpec((CTX_LEN, MLA_QK), lambda b, h, i: (ctx_blk + b, h)),
                  pl.BlockSpec((SEQ, MLA_QK), lambda b, h, i: (b, h)),
                  pl.BlockSpec((CTX_LEN, MLA_VX), lambda b, h, i: (ctx_blk + b, h)),
                  pl.BlockSpec((SEQ, MLA_VX), lambda b, h, i: (b, h))],
        out_specs=pl.BlockSpec((tq, MLA_V), lambda b, h, i: (b * nq + i, h)),
        compiler_params=_cparams(("parallel", "parallel", "arbitrary"), VMEM_LIMIT),
        name="mla_attn_latent",
    )(q, k, k, v, v)
    ctx = pl.pallas_call(
        functools.partial(_flash_kernel, chunks=((0, 0, CTX_LEN),)),
        out_shape=_SDS((N_CTX, MLA_HEADS * MLA_V), BF16),
        grid=(BATCH, MLA_HEADS),
        in_specs=[pl.BlockSpec((CTX_LEN, MLA_QK), lambda b, h: (ctx_blk + b, h)),
                  pl.BlockSpec((CTX_LEN, MLA_QK), lambda b, h: (ctx_blk + b, h)),
                  pl.BlockSpec((CTX_LEN, MLA_VX), lambda b, h: (ctx_blk + b, h))],
        out_specs=pl.BlockSpec((CTX_LEN, MLA_V), lambda b, h: (b, h)),
        compiler_params=_cparams(("parallel", "parallel")),
        name="mla_attn_context",
    )(q, k, v)
    return lat, ctx


def _ab_mix_kernel(ofl_ref, ofc_ref, obl_ref, obc_ref, ml_ref, mc_ref, z_ref, g_ref, a_ref):
    for hh in range(DN_HEADS):
        sl = slice(hh * DN_HEAD_DIM, (hh + 1) * DN_HEAD_DIM)
        o = _split_read(ofl_ref, ofc_ref, sl) + _split_read(obl_ref, obc_ref, sl)
        a_ref[:, sl] = (_rms(o, g_ref[...]) * _silu(z_ref[:, sl].astype(F32))).astype(BF16)
    a_ref[:, DN_W:] = _split_read(ml_ref, mc_ref)


def _ab_mix(o_f, o_b, p, dn_norm_g, mla_o):
    tm = ROW_TILE
    half = DN_W
    return pl.pallas_call(
        _ab_mix_kernel,
        out_shape=_SDS((N_TOK, 2 * half), BF16),
        grid=(N_TOK // tm,),
        in_specs=_split_specs(half) + _split_specs(half) + _split_specs(half) + [
            pl.BlockSpec((tm, half), lambda i: (i, P_Z // half)),
            pl.BlockSpec((1, DN_HEAD_DIM), lambda i: (0, 0))],
        out_specs=pl.BlockSpec((tm, 2 * half), lambda i: (i, 0)),
        compiler_params=_cparams(("parallel",)),
        name="ab_mix",
    )(*o_f, *o_b, *mla_o, p, dn_norm_g.reshape(1, -1))


def _mm_rope_kernel(a_ref, w_ref, b_ref, cos_ref, sin_ref, o_ref, *, tn, rope_cols):
    y = _dot(a_ref[...], w_ref[...].astype(BF16)) + b_ref[...]
    col0 = pl.program_id(1) * tn
    cosv, sinv = cos_ref[...], sin_ref[...]
    for s in range(tn // 128):
        ys = y[:, s * 128:(s + 1) * 128]
        roped = _rope128(ys, cosv, sinv)
        o_ref[:, s * 128:(s + 1) * 128] = jnp.where(col0 + s * 128 < rope_cols, roped, ys).astype(o_ref.dtype)


def _gqa_qkv(u, w, bias, cosv, sinv):
    m, k = u.shape
    n = w.shape[1]
    tm, tn = MM_ROW_TILE, 1280
    return pl.pallas_call(
        functools.partial(_mm_rope_kernel, tn=tn, rope_cols=GQA_Q + GQA_KV),
        out_shape=_SDS((m, n), BF16),
        grid=(m // tm, n // tn),
        in_specs=[pl.BlockSpec((tm, k), lambda i, j: (i, 0)),
                  pl.BlockSpec((k, tn), lambda i, j: (0, j)),
                  pl.BlockSpec((1, tn), lambda i, j: (0, j)),
                  pl.BlockSpec((tm, 128), lambda i, j: (i, 0)),
                  pl.BlockSpec((tm, 128), lambda i, j: (i, 0))],
        out_specs=pl.BlockSpec((tm, tn), lambda i, j: (i, j)),
        compiler_params=_cparams(("parallel", "arbitrary"), VMEM_LIMIT),
        name="gqa_qkv_rope",
    )(u, w, bias.reshape(1, n), cosv, sinv)


def _gqa_kernel(sink_ref, q_ref, kp_ref, kc_ref, kn_ref, vp_ref, vc_ref, vn_ref, kx_ref, vx_ref, o_ref):
    i = pl.program_id(1)
    nb = SEQ // Q_BLOCK
    n_keys = CTX_LEN + 3 * Q_BLOCK
    groups = GQA_HEADS // GQA_KV_HEADS
    slabs = groups // 2
    dh = GQA_HEAD_DIM
    qi = lax.broadcasted_iota(I32, (Q_BLOCK, n_keys), 0)
    kj = lax.broadcasted_iota(I32, (Q_BLOCK, n_keys), 1) - CTX_LEN
    rel = kj - Q_BLOCK - qi
    valid = (kj < 0) | ((jnp.abs(rel) <= WINDOW) & ((kj >= Q_BLOCK) | (i > 0)) & ((kj < 2 * Q_BLOCK) | (i < nb - 1)))
    bias = jnp.where(valid, 0.0, -jnp.inf).astype(F32)
    bias = jnp.concatenate([bias] * slabs, axis=0)
    k_all = jnp.concatenate([kx_ref[...], kp_ref[...], kc_ref[...], kn_ref[...]], axis=0)
    v_all = jnp.concatenate([vx_ref[...], vp_ref[...], vc_ref[...], vn_ref[...]], axis=0)
    low = lax.broadcasted_iota(I32, (n_keys, 128), 1) < dh

    def block_diag(x_all, kvh):
        pair = x_all[:, (kvh // 2) * 128:(kvh // 2 + 1) * 128].astype(F32)
        other = pltpu.roll(pair, dh, 1)
        first, second = (pair, other) if kvh % 2 == 0 else (other, pair)
        return jnp.concatenate([jnp.where(low, first, 0.0), jnp.where(low, 0.0, second)], axis=0).astype(BF16)

    def scores(kvh):
        q = jnp.concatenate([q_ref[:, (kvh * slabs + r) * 128:(kvh * slabs + r + 1) * 128] for r in range(slabs)], axis=0)
        return _dot_nt(q * (dh ** -0.5), block_diag(k_all, kvh))

    row_lo = lax.broadcasted_iota(I32, (2 * n_keys, 128), 0) < n_keys
    lane_lo = lax.broadcasted_iota(I32, (2 * n_keys, 128), 1) < dh
    ones_cols = jnp.where(row_lo == lane_lo, 1.0, 0.0).astype(BF16)

    def softmax(kvh, s):
        ps, sink_terms = [], []
        for half in range(2):
            sh = s[:, half * n_keys:(half + 1) * n_keys] + bias
            sink = jnp.concatenate([jnp.full((Q_BLOCK, 1), sink_ref[(kvh * slabs + r) * 2 + half], F32)
                                    for r in range(slabs)], axis=0)
            m = jnp.maximum(jnp.max(sh, axis=-1, keepdims=True), sink)
            ps.append(jnp.exp(sh - m).astype(BF16))
            sink_terms.append(jnp.exp(sink - m))
        return jnp.concatenate(ps, axis=1), sink_terms

    def finish(kvh, p, sink_terms):
        o = _dot(p, jnp.concatenate([block_diag(v_all, kvh), ones_cols], axis=1))
        num, den = o[:, :128], o[:, 128:]
        lane_low = lax.broadcasted_iota(I32, num.shape, 1) < dh
        o = num / (den + jnp.where(lane_low, sink_terms[0], sink_terms[1]))
        for r in range(slabs):
            o_ref[:, (kvh * slabs + r) * 128:(kvh * slabs + r + 1) * 128] = o[r * Q_BLOCK:(r + 1) * Q_BLOCK].astype(o_ref.dtype)

    s_next = scores(0)
    for kvh in range(GQA_KV_HEADS):
        s_cur = s_next
        if kvh + 1 < GQA_KV_HEADS:
            s_next = scores(kvh + 1)
        p, dens = softmax(kvh, s_cur)
        finish(kvh, p, dens)


def _gqa_attention(qkv, sink):
    nb = SEQ // Q_BLOCK
    kcol, vcol = GQA_Q // GQA_KV, GQA_Q // GQA_KV + 1
    ctx_blk = 2 * SEQ // CTX_LEN

    def win(col, shift):
        return pl.BlockSpec((Q_BLOCK, GQA_KV), lambda b, i: (b * nb + jnp.clip(i + shift, 0, nb - 1), col))

    return pl.pallas_call(
        _gqa_kernel,
        out_shape=_SDS((N_LAT, GQA_Q), BF16),
        grid=(BATCH, nb),
        in_specs=[pl.BlockSpec(memory_space=pltpu.SMEM),
                  pl.BlockSpec((Q_BLOCK, GQA_Q), lambda b, i: (b * nb + i, 0)),
                  win(kcol, -1), win(kcol, 0), win(kcol, 1), win(vcol, -1), win(vcol, 0), win(vcol, 1),
                  pl.BlockSpec((CTX_LEN, GQA_KV), lambda b, i: (ctx_blk + b, kcol)),
                  pl.BlockSpec((CTX_LEN, GQA_KV), lambda b, i: (ctx_blk + b, vcol))],
        out_specs=pl.BlockSpec((Q_BLOCK, GQA_Q), lambda b, i: (b * nb + i, 0)),
        compiler_params=_cparams(("parallel", "arbitrary")),
        name="gqa_window_attn",
    )(sink.astype(F32), qkv, qkv, qkv, qkv, qkv, qkv, qkv, qkv, qkv)


GATHER_TILE = 256
COMBINE_TILE = 256


def _row_copy(src_hbm, row, dst, r, sem):
    return pltpu.make_async_copy(src_hbm.at[pl.ds(row, 1), :], dst.at[pl.ds(r, 1), :], sem)


def _tile_copy(src_hbm, dst, sem):
    return pltpu.make_async_copy(src_hbm.at[pl.ds(0, dst.shape[0]), :], dst, sem)


def _gather_kernel(tok_ref, tv_ref, u_hbm, valid_ref, o_ref, buf, sem):
    i = pl.program_id(0)
    n = pl.num_programs(0)

    def start_tile(t, slot):
        @pl.when(tv_ref[t] > 0)
        def _():
            def issue(r8, carry):
                for j in range(8):
                    r = r8 * 8 + j
                    _row_copy(u_hbm, tok_ref[t * GATHER_TILE + r], buf.at[slot], r, sem.at[slot]).start(priority=j % 2)
                return carry

            lax.fori_loop(0, GATHER_TILE // 8, issue, 0)

    @pl.when(i == 0)
    def _():
        start_tile(0, 0)

    @pl.when(i + 1 < n)
    def _():
        start_tile(i + 1, (i + 1) % 2)

    slot = i % 2

    @pl.when(tv_ref[i] > 0)
    def _():
        _tile_copy(u_hbm, buf.at[slot], sem.at[slot]).wait()
        lo, hi = _unpack_bf16_halves(buf[slot])
        keep = valid_ref[...] > 0.0
        half = D_MODEL // 2
        o_ref[:, :half] = jnp.where(keep, lo, jnp.zeros_like(lo))
        o_ref[:, half:] = jnp.where(keep, hi, jnp.zeros_like(hi))

    @pl.when(tv_ref[i] == 0)
    def _():
        o_ref[...] = jnp.zeros_like(o_ref)


def _moe_gather(slot_tok, slot_valid, u):
    n_slots = slot_tok.shape[0]
    n_tiles = n_slots // GATHER_TILE
    tile_valid = (jnp.max(slot_valid.reshape(n_tiles, GATHER_TILE), axis=1) > 0.0).astype(I32)
    return pl.pallas_call(
        _gather_kernel,
        out_shape=_SDS((n_slots, D_MODEL), BF16),
        grid_spec=pltpu.PrefetchScalarGridSpec(
            num_scalar_prefetch=2,
            grid=(n_tiles,),
            in_specs=[pl.BlockSpec(memory_space=pl.ANY),
                      pl.BlockSpec((GATHER_TILE, 1), lambda i, tok, tv: (i, 0))],
            out_specs=pl.BlockSpec((GATHER_TILE, D_MODEL), lambda i, tok, tv: (i, 0)),
            scratch_shapes=[pltpu.VMEM((2, GATHER_TILE, D_MODEL // 2), U32), pltpu.SemaphoreType.DMA((2,))]),
        compiler_params=_cparams(("arbitrary",)),
        name="moe_gather",
    )(slot_tok, tile_valid, u, slot_valid.reshape(n_slots, 1))


def _moe_ffn_kernel(be_ref, bv_ref, x_ref, wg_ref, wu_ref, wd_ref, o_ref, h_ref):
    i, s = pl.program_id(0), pl.program_id(1)
    n_sub = bv_ref[i]
    for k in range(1, MOE_BLOCK // MOE_SUB + 1):
        _swiglu_step(s, x_ref, wg_ref.at[0], wu_ref.at[0], wd_ref.at[0], o_ref, h_ref,
                     rows=k * MOE_SUB, guard=n_sub == k)

    @pl.when((n_sub == 0) & (s >= FF_STEPS))
    def _():
        o_ref[...] = jnp.zeros_like(o_ref)


def _moe_ffn(block_expert, block_valid, xs, w_gate, w_up, w_down):
    n_slots = xs.shape[0]
    tm = MOE_BLOCK

    def ff(i, s, be, bv):
        return (be[i], 0, jnp.where(bv[i] > 0, jnp.minimum(s, FF_STEPS - 1), FF_STEPS - 1))

    def down(i, s, be, bv):
        return (be[i], 0, jnp.where(bv[i] > 0, jnp.maximum(s - FF_STEPS, 0), OUT_STEPS - 1))

    return pl.pallas_call(
        _moe_ffn_kernel,
        out_shape=_SDS((n_slots, D_MODEL), F32),
        grid_spec=pltpu.PrefetchScalarGridSpec(
            num_scalar_prefetch=2,
            grid=(n_slots // tm, FF_STEPS + OUT_STEPS),
            in_specs=[pl.BlockSpec((tm, D_MODEL), lambda i, s, be, bv: (i, 0), pipeline_mode=pl.Buffered(1)),
                      pl.BlockSpec((1, D_MODEL, FF_TILE), ff),
                      pl.BlockSpec((1, D_MODEL, FF_TILE), ff),
                      pl.BlockSpec((1, D_FF, OUT_TILE), down)],
            out_specs=pl.BlockSpec((tm, OUT_TILE), lambda i, s, be, bv: (i, jnp.maximum(s - FF_STEPS, 0))),
            scratch_shapes=[pltpu.VMEM((tm, D_FF), BF16)]),
        compiler_params=_cparams(("arbitrary", "arbitrary"), SWIGLU_VMEM_LIMIT),
        name="moe_grouped_swiglu",
    )(block_expert, block_valid, xs, w_gate, w_up, w_down)


def _combine_kernel(pos_ref, y_hbm, wt_ref, h_ref, gp_ref, gate_ref, o_ref, buf_a, buf_b, sem):
    i = pl.program_id(0)
    n = pl.num_programs(0)
    base = i * COMBINE_TILE

    def start_tile(t, slot):
        def issue(r, carry):
            a = 2 * (t * COMBINE_TILE + r)
            _row_copy(y_hbm, pos_ref[a], buf_a.at[slot], r, sem.at[slot]).start(priority=0)
            _row_copy(y_hbm, pos_ref[a + 1], buf_b.at[slot], r, sem.at[slot]).start(priority=1)
            return carry

        lax.fori_loop(0, COMBINE_TILE, issue, 0, unroll=8)

    @pl.when(i == 0)
    def _():
        start_tile(0, 0)

    @pl.when(i + 1 < n)
    def _():
        start_tile(i + 1, (i + 1) % 2)

    slot = i % 2

    _tile_copy(y_hbm, buf_a.at[slot], sem.at[slot]).wait()
    _tile_copy(y_hbm, buf_b.at[slot], sem.at[slot]).wait()
    wt = wt_ref[...]
    y = wt[:, 0:1] * buf_a[slot] + wt[:, 1:2] * buf_b[slot]
    grp = _group_of_row(base)
    o_ref[...] = h_ref[...] + _mod_row(gate_ref, grp) * _rms(y, gp_ref[...])


def _moe_combine(pos, ys, wts, h, g_post, mod, gate_lk):
    n = wts.shape[0]
    tm = COMBINE_TILE
    return pl.pallas_call(
        _combine_kernel,
        out_shape=_SDS((n, D_MODEL), F32),
        grid_spec=pltpu.PrefetchScalarGridSpec(
            num_scalar_prefetch=1,
            grid=(n // tm,),
            in_specs=[pl.BlockSpec(memory_space=pl.ANY),
                      pl.BlockSpec((tm, 128), lambda i, pos: (i, 0)),
                      pl.BlockSpec((tm, D_MODEL), lambda i, pos: (i, 0)),
                      pl.BlockSpec((1, D_MODEL), lambda i, pos: (0, 0)),
                      pl.BlockSpec((1, 8, D_MODEL), lambda i, pos: (gate_lk[0], 0, gate_lk[1]))],
            out_specs=pl.BlockSpec((tm, D_MODEL), lambda i, pos: (i, 0)),
            scratch_shapes=[pltpu.VMEM((2, tm, D_MODEL), F32), pltpu.VMEM((2, tm, D_MODEL), F32),
                            pltpu.SemaphoreType.DMA((2,))]),
        compiler_params=_cparams(("arbitrary",)),
        name="moe_combine",
    )(pos, ys, wts, h, g_post.reshape(1, D_MODEL), mod)


def _moe_routing(idx):
    flat_e = idx[:, :TOP_K].reshape(-1)
    n_assign = flat_e.shape[0]
    onehot = (flat_e[:, None] == jnp.arange(N_EXPERTS, dtype=I32)[None, :]).astype(I32)
    csum = jnp.cumsum(onehot, axis=0)
    counts = csum[-1]
    padded = (counts + MOE_BLOCK - 1) // MOE_BLOCK * MOE_BLOCK
    pad_end = jnp.cumsum(padded)
    pad_start = pad_end - padded
    pos = jnp.sum(onehot * (pad_start[None, :] + csum - 1), axis=1).astype(I32)
    slot_tok = jnp.zeros((MOE_SLOTS,), I32).at[pos].set(jnp.arange(n_assign, dtype=I32) // TOP_K)
    slot = jnp.arange(MOE_SLOTS, dtype=I32)
    used_end = pad_start + counts
    slot_valid = jnp.any((slot[:, None] >= pad_start[None, :]) & (slot[:, None] < used_end[None, :]), axis=1)
    blk_start = jnp.arange(MOE_SLOTS // MOE_BLOCK, dtype=I32) * MOE_BLOCK
    last_start = jnp.maximum(pad_end[-1] - MOE_BLOCK, 0)
    blk_e = jnp.searchsorted(pad_end, jnp.minimum(blk_start, last_start), side='right').astype(I32)
    blk_e = jnp.minimum(blk_e, N_EXPERTS - 1)
    rows_used = jnp.clip(used_end[blk_e] - blk_start, 0, MOE_BLOCK)
    blk_sub = ((rows_used + MOE_SUB - 1) // MOE_SUB).astype(I32)
    return pos, slot_tok, slot_valid.astype(F32), blk_e, blk_sub


def _rest_of_w_in_t(w_in_t):
    dn_in = P_MAIN + 4 * DN_HEADS
    ba = w_in_t[P_MAIN:dn_in]
    cq = w_in_t[dn_in:dn_in + MLA_Q_RANK]
    ckv = w_in_t[dn_in + MLA_Q_RANK:dn_in + MLA_Q_RANK + MLA_KV_RANK]
    k_rope = w_in_t[dn_in + MLA_Q_RANK + MLA_KV_RANK:]
    pad = jnp.zeros((R_CQ - R_MISC - MLA_ROPE - 4 * DN_HEADS, D_MODEL), w_in_t.dtype)
    return jnp.concatenate([ckv, k_rope, ba, pad, cq], axis=0)


def _rearrange_w_qb(w_qb):
    w = w_qb.reshape(MLA_Q_RANK, MLA_HEADS, MLA_NOPE + MLA_ROPE)
    w = jnp.pad(w, ((0, 0), (0, 0), (0, MLA_QK - MLA_NOPE - MLA_ROPE)))
    return w.reshape(MLA_Q_RANK, MLA_HEADS * MLA_QK).astype(BF16)


def kernel(x, c, ctx, c_ctx, ada_w, ada_b, norm_g, ab_w_in, dn_conv_w, dn_a_log, dn_dt_bias, dn_norm_g, mla_q_norm_g, mla_w_qb, mla_kv_norm_g, mla_w_kvb, ab_w_out, ffn_w_gate, ffn_w_up, ffn_w_down, gqa_w_qkv, gqa_b_qkv, gqa_sink, gqa_w_out, gqa_b_out, moe_w_router, moe_w_gate, moe_w_up, moe_w_down):
    assert x.shape == (BATCH, SEQ, D_MODEL) and ctx.shape == (BATCH, CTX_LEN, D_MODEL) and ada_w.shape[0] == 2
    h = (x.reshape(N_LAT, D_MODEL), ctx.reshape(N_CTX, D_MODEL))
    cvec = jnp.zeros((8, D_MODEL), F32).at[:BATCH].set(c).at[BATCH].set(c_ctx)
    mod = _ada(cvec, ada_w, ada_b)
    (cos1, sin1), (cos2, sin2) = _rope_tables()

    u = _modulate_in(*h, norm_g[0, 0], mod, 0, 0, 1)
    w_in_t = jnp.swapaxes(ab_w_in, 1, 2)
    p = _matmul(u, w_in_t, None, BF16, MM_ROW_TILE, 1024, n=P_MAIN, w_is_transposed=True)
    p_rest = _matmul(u, _rest_of_w_in_t(w_in_t[0]), None, F32, MM_ROW_TILE, R_WIDTH // 2, w_is_transposed=True)
    o_f, o_b = _deltanet(_dn_conv(p, dn_conv_w[0]), _dn_gates(p_rest, dn_a_log[0], dn_dt_bias[0]))
    q, k, v = _mla_proj(p_rest, mla_q_norm_g[0], mla_kv_norm_g[0], _rearrange_w_qb(mla_w_qb[0]),
                        mla_w_kvb[0].astype(BF16), cos1, sin1)
    a = _ab_mix(o_f, o_b, p, dn_norm_g[0], _mla_attention(q, k, v))
    y = _matmul(a, ab_w_out[0].astype(BF16), None, BF16, MM_ROW_TILE, 1024)
    h, u = _post(h, y, norm_g[0, 1], norm_g[0, 2], mod, (0, 2), (0, 3), (0, 4), BF16)
    y = _ffn(u, ffn_w_gate[0], ffn_w_up[0], ffn_w_down[0])
    h, u = _post(h, y, norm_g[0, 3], norm_g[1, 0], mod, (0, 5), (1, 0), (1, 1), BF16)

    qkv = _gqa_qkv(u, gqa_w_qkv[0].astype(BF16), gqa_b_qkv[0], cos2, sin2)
    o = _gqa_attention(qkv, gqa_sink[0])
    y = _matmul(o, gqa_w_out[0].astype(BF16), gqa_b_out[0], BF16, 1024, 1024)
    h, u, idx, wts = _post(h, y, norm_g[1, 1], norm_g[1, 2], mod, (1, 2), (1, 3), (1, 4), F32, moe_w_router[0])
    pos, slot_tok, slot_valid, blk_e, blk_sub = _moe_routing(idx)
    xs = _moe_gather(slot_tok, slot_valid, u)
    ys = _moe_ffn(blk_e, blk_sub, xs, moe_w_gate[0], moe_w_up[0], moe_w_down[0])
    out = _moe_combine(pos, ys, wts, h, norm_g[1, 3], mod, (1, 5))
    return out.reshape(BATCH, SEQ, D_MODEL)
```

```python
import functools
import math

import jax
import jax.numpy as jnp
from jax import lax
from jax.experimental import pallas as pl
from jax.experimental.pallas import tpu as pltpu

F32 = jnp.float32
BF16 = jnp.bfloat16
I32 = jnp.int32

D_MODEL = 2048
BATCH = 2
SEQ = 4096
CTX_LEN = 256
GRID_W = 64
EPS = 1e-6
ROPE_THETA = 10000.0

DN_HEADS = 8
DN_HEAD_DIM = 128
DN_CONV = 5
DN_CHUNK = 64
MLA_HEADS = 8
MLA_Q_RANK = 768
MLA_KV_RANK = 512
MLA_NOPE = 128
MLA_ROPE = 64
MLA_V = 128
GQA_HEADS = 32
GQA_KV_HEADS = 4
GQA_HEAD_DIM = 64
WINDOW = 128
Q_BLOCK = 128
D_FF = 7168
N_EXPERTS = 8
TOP_K = 2

DN_W = DN_HEADS * DN_HEAD_DIM
DN_QKV = 3 * DN_W
N_LAT = BATCH * SEQ
N_CTX = BATCH * CTX_LEN
N_TOK = N_LAT + N_CTX
GQA_Q = GQA_HEADS * GQA_HEAD_DIM
GQA_KV = GQA_KV_HEADS * GQA_HEAD_DIM

P_QKV = 0
P_Z = DN_QKV
P_MAIN = P_Z + DN_W
R_CKV = 0
R_MISC = MLA_KV_RANK
R_CQ = MLA_Q_RANK
R_WIDTH = R_CQ + MLA_Q_RANK

ROW_TILE = 512
MM_ROW_TILE = 1088
MOE_BLOCK = 1024
MOE_SUB = 128
MOE_SLOTS = N_LAT * TOP_K + N_EXPERTS * MOE_BLOCK
VMEM_LIMIT = 56 * 1024 * 1024

_SDS = jax.ShapeDtypeStruct


def _cparams(sem, vmem=None):
    return pltpu.CompilerParams(dimension_semantics=sem, vmem_limit_bytes=vmem)


def _dot(a, b):
    return jnp.dot(a, b, preferred_element_type=F32)


def _dot_nt(a, b):
    return lax.dot_general(a, b, (((1,), (1,)), ((), ())), preferred_element_type=F32)


def _dot_tn(a, b):
    return lax.dot_general(a, b, (((0,), (0,)), ((), ())), preferred_element_type=F32)


def _silu(x):
    return x * jax.nn.sigmoid(x)


def _rms(x, g):
    return x * lax.rsqrt(jnp.mean(x * x, axis=-1, keepdims=True) + EPS) * g


def _group_of_row(r0):
    return (r0 >= SEQ).astype(I32) + (r0 >= 2 * SEQ).astype(I32)


def _mod_spec(layer, k):
    return pl.BlockSpec((1, 8, D_MODEL), lambda *_: (layer, 0, k))


def _mod_row(ref, grp):
    return ref[0, pl.ds(grp, 1), :]


def _ada_kernel(c_ref, w_ref, b_ref, o_ref):
    s = _silu(c_ref[...]).astype(BF16)
    o_ref[0] = _dot(s, w_ref[0].astype(BF16)) + b_ref[0]


def _ada(cvec, ada_w, ada_b):
    n_layers, _, n = ada_w.shape
    tn = 1024
    return pl.pallas_call(
        _ada_kernel,
        out_shape=_SDS((n_layers, 8, n), F32),
        grid=(n_layers, n // tn),
        in_specs=[pl.BlockSpec((8, D_MODEL), lambda l, j: (0, 0)),
                  pl.BlockSpec((1, D_MODEL, tn), lambda l, j: (l, 0, j)),
                  pl.BlockSpec((1, 1, tn), lambda l, j: (l, 0, j))],
        out_specs=pl.BlockSpec((1, 8, tn), lambda l, j: (l, 0, j)),
        compiler_params=_cparams(("parallel", "parallel")),
        name="ada_mod",
    )(cvec, ada_w, ada_b.reshape(n_layers, 1, n))


_LAT_TILES = N_LAT // ROW_TILE
assert N_CTX == ROW_TILE


def _split_specs(width, col=0):
    lat = pl.BlockSpec((ROW_TILE, width), lambda i, *_: (jnp.minimum(i, _LAT_TILES - 1), col))
    ctx = pl.BlockSpec((ROW_TILE, width), lambda i, *_: (0, col))
    return [lat, ctx]


def _split_read(lat_ref, ctx_ref, sl=slice(None)):
    return jnp.where(pl.program_id(0) == _LAT_TILES, ctx_ref[:, sl], lat_ref[:, sl])


def _modin_kernel(x_ref, c_ref, g_ref, sh_ref, sc_ref, u_ref):
    grp = _group_of_row(pl.program_id(0) * ROW_TILE)
    h = _split_read(x_ref, c_ref)
    u = _rms(h, g_ref[...]) * (1.0 + _mod_row(sc_ref, grp)) + _mod_row(sh_ref, grp)
    u_ref[...] = u.astype(u_ref.dtype)


def _modulate_in(x, ctx, g, mod, layer, k_shift, k_scale):
    row = pl.BlockSpec((ROW_TILE, D_MODEL), lambda i: (i, 0))
    vec = pl.BlockSpec((1, D_MODEL), lambda i: (0, 0))
    return pl.pallas_call(
        _modin_kernel,
        out_shape=_SDS((N_TOK, D_MODEL), BF16),
        grid=(N_TOK // ROW_TILE,),
        in_specs=_split_specs(D_MODEL) + [vec, _mod_spec(layer, k_shift), _mod_spec(layer, k_scale)],
        out_specs=row,
        compiler_params=_cparams(("parallel",)),
        name="modulate_in",
    )(x, ctx, g.reshape(1, D_MODEL), mod, mod)


def _route_top2(logits):
    lane = lax.broadcasted_iota(I32, logits.shape, 1)
    neg = -jnp.inf
    l0 = jnp.where(lane < N_EXPERTS, logits, neg)
    m1 = jnp.max(l0, axis=-1, keepdims=True)
    i1 = jnp.min(jnp.where(l0 == m1, lane, 128), axis=-1, keepdims=True)
    l1 = jnp.where(lane == i1, neg, l0)
    m2 = jnp.max(l1, axis=-1, keepdims=True)
    i2 = jnp.min(jnp.where(l1 == m2, lane, 128), axis=-1, keepdims=True)
    e = jnp.exp(m2 - m1)
    w1 = 1.0 / (1.0 + e)
    idx = jnp.where(lane == 0, i1, jnp.where(lane == 1, i2, 0))
    wts = jnp.where(lane == 0, w1, jnp.where(lane == 1, e * w1, 0.0))
    return idx, wts


U32 = jnp.uint32


def _pack_bf16_halves(x):
    n = x.shape[1] // 2
    lo = lax.bitcast_convert_type(x[:, :n].astype(F32), U32) >> 16
    hi = (lax.bitcast_convert_type(x[:, n:].astype(F32), U32) >> 16) << 16
    return hi | lo


def _unpack_bf16_halves(w):
    lo = lax.bitcast_convert_type(w << 16, F32).astype(BF16)
    hi = lax.bitcast_convert_type((w >> 16) << 16, F32).astype(BF16)
    return lo, hi


def _post_kernel(*refs, split, route):
    refs = list(refs)
    if split:
        h = _split_read(refs.pop(0), refs.pop(0))
    else:
        h = refs.pop(0)[...]
    y_ref, gp_ref, gate_ref, gn_ref, sh_ref, sc_ref = refs[:6]
    refs = refs[6:]
    wr_ref = refs.pop(0) if route else None
    hn_ref, u_ref = refs[:2]
    grp = _group_of_row(pl.program_id(0) * ROW_TILE)
    hn = h + _mod_row(gate_ref, grp) * _rms(y_ref[...].astype(F32), gp_ref[...])
    hn_ref[...] = hn
    u = _rms(hn, gn_ref[...]) * (1.0 + _mod_row(sc_ref, grp)) + _mod_row(sh_ref, grp)
    if route:
        ub = u.astype(BF16)
        u_ref[...] = _pack_bf16_halves(ub)
        idx_ref, wt_ref = refs[2:4]
        idx_ref[...], wt_ref[...] = _route_top2(_dot(ub, wr_ref[...]))
    else:
        u_ref[...] = u.astype(u_ref.dtype)


def _post(h, y, g_post, g_next, mod, gate_lk, shift_lk, scale_lk, u_dtype, w_router=None):
    n = y.shape[0]
    split = isinstance(h, tuple)
    route = w_router is not None
    row = pl.BlockSpec((ROW_TILE, D_MODEL), lambda i: (i, 0))
    vec = pl.BlockSpec((1, D_MODEL), lambda i: (0, 0))
    slab = pl.BlockSpec((ROW_TILE, 128), lambda i: (i, 0))
    h_specs, h_args = (_split_specs(D_MODEL), list(h)) if split else ([row], [h])
    extra_specs, extra_args, extra_out, extra_out_specs = [], [], (), ()
    u_shape, u_spec = _SDS((n, D_MODEL), u_dtype), row
    if route:
        w = jnp.zeros((D_MODEL, 128), BF16).at[:, :N_EXPERTS].set(w_router.astype(BF16))
        extra_specs, extra_args = [pl.BlockSpec((D_MODEL, 128), lambda i: (0, 0))], [w]
        extra_out, extra_out_specs = (_SDS((n, 128), I32), _SDS((n, 128), F32)), (slab, slab)
        u_shape, u_spec = _SDS((n, D_MODEL // 2), U32), pl.BlockSpec((ROW_TILE, D_MODEL // 2), lambda i: (i, 0))
    return pl.pallas_call(
        functools.partial(_post_kernel, split=split, route=route),
        out_shape=(_SDS((n, D_MODEL), F32), u_shape) + extra_out,
        grid=(n // ROW_TILE,),
        in_specs=h_specs + [row, vec, _mod_spec(*gate_lk), vec, _mod_spec(*shift_lk), _mod_spec(*scale_lk)] + extra_specs,
        out_specs=(row, u_spec) + extra_out_specs,
        compiler_params=_cparams(("parallel",)),
        name="residual_post",
    )(*h_args, y, g_post.reshape(1, D_MODEL), mod, g_next.reshape(1, D_MODEL), mod, mod, *extra_args)


def _mm_kernel(a_ref, w_ref, b_ref, o_ref, *, w_is_transposed):
    dot = _dot_nt if w_is_transposed else _dot
    o_ref[...] = (dot(a_ref[...], w_ref[...].astype(BF16)) + b_ref[...]).astype(o_ref.dtype)


def _matmul(a, w, bias, out_dtype, tm, tn, n=None, w_is_transposed=False):
    m, k = a.shape
    if w_is_transposed:
        n = w.shape[-2] if n is None else n
        if w.ndim == 3:
            w_spec = pl.BlockSpec((None, tn, k), lambda i, j: (0, j, 0))
        else:
            w_spec = pl.BlockSpec((tn, k), lambda i, j: (j, 0))
    else:
        n = w.shape[-1] if n is None else n
        w_spec = pl.BlockSpec((k, tn), lambda i, j: (0, j))
    if bias is None:
        bias = jnp.zeros((n,), F32)
    return pl.pallas_call(
        functools.partial(_mm_kernel, w_is_transposed=w_is_transposed),
        out_shape=_SDS((m, n), out_dtype),
        grid=(m // tm, n // tn),
        in_specs=[pl.BlockSpec((tm, k), lambda i, j: (i, 0)),
                  w_spec,
                  pl.BlockSpec((1, tn), lambda i, j: (0, j))],
        out_specs=pl.BlockSpec((tm, tn), lambda i, j: (i, j)),
        compiler_params=_cparams(("parallel", "arbitrary"), VMEM_LIMIT),
        name="matmul",
    )(a, w, bias.reshape(1, n))


FF_TILE = 512
SWIGLU_VMEM_LIMIT = 60 * 1024 * 1024
OUT_TILE = 256
FF_STEPS = D_FF // FF_TILE
OUT_STEPS = D_MODEL // OUT_TILE


def _swiglu_step(s, x_ref, wg_ref, wu_ref, wd_ref, o_ref, h_ref, rows=None, guard=True):
    total = x_ref.shape[0]
    rows = total if rows is None else rows

    @pl.when(guard & (s < FF_STEPS))
    def _():
        x = x_ref[0:rows, :]
        a = _dot(x, wg_ref[...].astype(BF16))
        b = _dot(x, wu_ref[...].astype(BF16))
        h_ref[0:rows, pl.ds(pl.multiple_of(s * FF_TILE, FF_TILE), FF_TILE)] = (_silu(a) * b).astype(BF16)

    @pl.when(guard & (s >= FF_STEPS))
    def _():
        o_ref[0:rows, :] = _dot(h_ref[0:rows, :], wd_ref[...].astype(BF16)).astype(o_ref.dtype)
        if rows < total:
            o_ref[rows:total, :] = jnp.zeros((total - rows, o_ref.shape[1]), o_ref.dtype)


def _ffn_kernel(u_ref, wg_ref, wu_ref, wd_ref, o_ref, h_ref):
    _swiglu_step(pl.program_id(1), u_ref, wg_ref, wu_ref, wd_ref, o_ref, h_ref)


def _ffn(u, w_gate, w_up, w_down):
    m = u.shape[0]
    tm = MM_ROW_TILE

    def ff(i, s):
        return (0, jnp.minimum(s, FF_STEPS - 1))

    def out(s):
        return jnp.maximum(s - FF_STEPS, 0)

    return pl.pallas_call(
        _ffn_kernel,
        out_shape=_SDS((m, D_MODEL), BF16),
        grid=(m // tm, FF_STEPS + OUT_STEPS),
        in_specs=[pl.BlockSpec((tm, D_MODEL), lambda i, s: (i, 0), pipeline_mode=pl.Buffered(1)),
                  pl.BlockSpec((D_MODEL, FF_TILE), ff),
                  pl.BlockSpec((D_MODEL, FF_TILE), ff),
                  pl.BlockSpec((D_FF, OUT_TILE), lambda i, s: (0, out(s)))],
        out_specs=pl.BlockSpec((tm, OUT_TILE), lambda i, s: (i, out(s))),
        scratch_shapes=[pltpu.VMEM((tm, D_FF), BF16)],
        compiler_params=_cparams(("parallel", "arbitrary"), SWIGLU_VMEM_LIMIT),
        name="swiglu_ffn",
    )(u, w_gate, w_up, w_down)


CONV_TILE = 256
CONV_HALO = 16
_SEQ_STARTS = (0, SEQ // CONV_TILE, 2 * SEQ // CONV_TILE, 2 * SEQ // CONV_TILE + 1)
_SEQ_LASTS = (SEQ // CONV_TILE - 1, 2 * SEQ // CONV_TILE - 1, 2 * SEQ // CONV_TILE, 2 * SEQ // CONV_TILE + 1)


def _conv_kernel(prev_ref, cur_ref, next_ref, w_ref, o_ref, buf):
    i = pl.program_id(0)
    j = pl.program_id(1)
    tm = CONV_TILE
    first = functools.reduce(jnp.logical_or, [i == s for s in _SEQ_STARTS])
    last = functools.reduce(jnp.logical_or, [i == s for s in _SEQ_LASTS])
    hb = CONV_HALO
    buf[0:hb, :] = jnp.where(first, 0.0, prev_ref[...].astype(F32))
    buf[hb:hb + tm, :] = cur_ref[...].astype(F32)
    buf[hb + tm:2 * hb + tm, :] = jnp.where(last, 0.0, next_ref[...].astype(F32))
    pad = DN_CONV // 2
    acc = buf[hb - pad:hb - pad + tm, :] * w_ref[0:1, :]
    for d in range(1, DN_CONV):
        acc = acc + buf[hb - pad + d:hb - pad + d + tm, :] * w_ref[d:d + 1, :]
    x = _silu(acc)
    q_scale = jnp.where(j == 0, DN_HEAD_DIM ** -0.5, 1.0)
    for hh in range(DN_HEADS):
        xh = x[:, hh * DN_HEAD_DIM:(hh + 1) * DN_HEAD_DIM]
        inv = lax.rsqrt(jnp.sum(xh * xh, axis=-1, keepdims=True) + EPS) * q_scale
        o_ref[:, hh * DN_HEAD_DIM:(hh + 1) * DN_HEAD_DIM] = xh * jnp.where(j == 2, 1.0, inv)


def _dn_conv(p, conv_w):
    n = p.shape[0]
    tm = CONV_TILE
    hb = CONV_HALO
    per_tile = tm // hb
    n_halo = n // hb
    return pl.pallas_call(
        _conv_kernel,
        out_shape=_SDS((n, DN_QKV), F32),
        grid=(n // tm, 3),
        in_specs=[pl.BlockSpec((hb, DN_W), lambda i, j: (jnp.maximum(i * per_tile - 1, 0), j)),
                  pl.BlockSpec((tm, DN_W), lambda i, j: (i, j)),
                  pl.BlockSpec((hb, DN_W), lambda i, j: (jnp.minimum((i + 1) * per_tile, n_halo - 1), j)),
                  pl.BlockSpec((DN_CONV, DN_W), lambda i, j: (0, j))],
        out_specs=pl.BlockSpec((tm, DN_W), lambda i, j: (i, j)),
        scratch_shapes=[pltpu.VMEM((tm + 2 * hb, DN_W), F32)],
        compiler_params=_cparams(("parallel", "parallel")),
        name="dn_conv",
    )(p, p, p, conv_w)


def _gates_kernel(m_ref, a_ref, dt_ref, o_ref):
    x = pltpu.roll(m_ref[...], 64, 1)
    lane = lax.broadcasted_iota(I32, x.shape, 1)
    z = x + dt_ref[...]
    softplus = jnp.maximum(z, 0.0) + jnp.log1p(jnp.exp(-jnp.abs(z)))
    g = a_ref[...] * softplus
    o_ref[...] = jnp.where(lane < 2 * DN_HEADS, jax.nn.sigmoid(x), jnp.where(lane < 4 * DN_HEADS, g, 0.0))


def _dn_gates(p, a_log, dt_bias):
    n = p.shape[0]
    neg_a = jnp.zeros((1, 128), F32).at[0, 2 * DN_HEADS:4 * DN_HEADS].set(-jnp.exp(a_log.astype(F32)).reshape(-1))
    dtb = jnp.zeros((1, 128), F32).at[0, 2 * DN_HEADS:4 * DN_HEADS].set(dt_bias.astype(F32).reshape(-1))
    tm = ROW_TILE
    vec = pl.BlockSpec((1, 128), lambda i: (0, 0))
    return pl.pallas_call(
        _gates_kernel,
        out_shape=_SDS((n, 128), F32),
        grid=(n // tm,),
        in_specs=[pl.BlockSpec((tm, 128), lambda i: (i, R_MISC // 128)), vec, vec],
        out_specs=pl.BlockSpec((tm, 128), lambda i: (i, 0)),
        compiler_params=_cparams(("parallel",)),
        name="dn_gates",
    )(p, neg_a, dtb)


_CTX_CHUNKS = CTX_LEN // DN_CHUNK
_LAT_CHUNKS = SEQ // DN_CHUNK
_DN_STEPS = _CTX_CHUNKS + _LAT_CHUNKS


def _dn_fwd_block(b, t):
    return jnp.where(t < _CTX_CHUNKS, (2 * SEQ + b * CTX_LEN) // DN_CHUNK + t, b * _LAT_CHUNKS + (t - _CTX_CHUNKS))


def _dn_bwd_block(b, t):
    return jnp.where(t < _CTX_CHUNKS, (2 * SEQ + b * CTX_LEN) // DN_CHUNK + (_CTX_CHUNKS - 1 - t),
                     b * _LAT_CHUNKS + (_DN_STEPS - 1 - t))


DN_GROUP = 2
_GROUP_ROWS = DN_GROUP * DN_CHUNK
_GROUP_STATE = DN_GROUP * DN_HEAD_DIM


_DN_BASE = 8


def _block_diag(x):
    zero = jnp.zeros((DN_CHUNK, DN_HEAD_DIM), x.dtype)
    rows = []
    for hh in range(DN_GROUP):
        piece = x[hh * DN_CHUNK:(hh + 1) * DN_CHUNK]
        rows.append(jnp.concatenate([piece if j == hh else zero for j in range(DN_GROUP)], axis=1))
    return jnp.concatenate(rows, axis=0)


def _dn_groups(items, eye, base_mask):
    it = items
    n = range(len(it))
    gx = [jnp.broadcast_to(it[g]["gc"], (_GROUP_ROWS, _GROUP_ROWS)) for g in n]
    gamma = [jnp.where(it[g]["incl"], jnp.exp(jnp.where(it[g]["incl"], gx[g] - gx[g].T, 0.0)), 0.0) for g in n]
    kb = [it[g]["k"].astype(BF16) for g in n]
    kk = [_dot_nt(kb[g], kb[g]) for g in n]
    a = [jnp.where(it[g]["strict"], it[g]["beta"] * kk[g] * gamma[g], 0.0) for g in n]
    d = [jnp.where(base_mask, a[g], 0.0) for g in n]
    tinv = [eye - d[g] for g in n]
    pw = [d[g].astype(BF16) for g in n]
    for _ in range(int(math.log2(_DN_BASE)) - 1):
        pw = [_dot(pw[g], pw[g]).astype(BF16) for g in n]
        tinv = [tinv[g] + _dot(tinv[g].astype(BF16), pw[g]) for g in n]
    for lvl in range(len(it[0]["levels"])):
        tb = [tinv[g].astype(BF16) for g in n]
        mt = [_dot(jnp.where(it[g]["levels"][lvl], a[g], 0.0).astype(BF16), tb[g]).astype(BF16) for g in n]
        tinv = [tinv[g] - _dot(tb[g], mt[g]) for g in n]
    tb = [tinv[g].astype(BF16) for g in n]
    egc = [jnp.exp(it[g]["gc"]) for g in n]
    rhs = [jnp.concatenate([it[g]["v"] * it[g]["beta"], it[g]["k"] * (it[g]["beta"] * egc[g])], axis=1) for g in n]
    rhs_hi = [rhs[g].astype(BF16) for g in n]
    rhs_lo = [(rhs[g] - rhs_hi[g].astype(F32)).astype(BF16) for g in n]
    sol = [_dot(jnp.concatenate([tb[g], tb[g]], axis=1), jnp.concatenate([rhs_hi[g], rhs_lo[g]], axis=0)) for g in n]
    qk = [(_dot_nt(it[g]["q"].astype(BF16), kb[g]) * gamma[g]).astype(BF16) for g in n]
    s = [it[g]["s_ref"][...] for g in n]
    sb = [s[g].astype(BF16) for g in n]
    v_new = [sol[g][:, :DN_HEAD_DIM] - _dot(_block_diag(sol[g][:, DN_HEAD_DIM:].astype(BF16)), sb[g]) for g in n]
    vb = [v_new[g].astype(BF16) for g in n]
    o = [_dot(_block_diag((it[g]["q"] * egc[g]).astype(BF16)), sb[g]) + _dot(qk[g], vb[g]) for g in n]
    for g in n:
        k_dec = _block_diag((it[g]["k"] * jnp.exp(it[g]["g_last"] - it[g]["gc"])).astype(BF16))
        it[g]["s_ref"][...] = s[g] * jnp.exp(it[g]["g_last_state"]) + _dot_tn(k_dec, vb[g])
    return o


def _stack_cols(x, lanes, rows_each):
    pieces = [x[:, l:l + 1] for l in lanes]
    if rows_each is not None:
        pieces = [jnp.broadcast_to(pc, (rows_each, 1)) for pc in pieces]
    return jnp.concatenate(pieces, axis=0)


def _dn_kernel(*refs):
    n_in = BATCH * 2 * 4
    in_refs, (ofl_ref, ofc_ref, obl_ref, obc_ref), (s_ref, o_scr) = refs[:n_in], refs[n_in:n_in + 4], refs[n_in + 4:]
    t = pl.program_id(0)

    @pl.when(t == 0)
    def _():
        s_ref[...] = jnp.zeros_like(s_ref)

    c = DN_CHUNK
    ri = lax.broadcasted_iota(I32, (_GROUP_ROWS, _GROUP_ROWS), 0)
    ci = lax.broadcasted_iota(I32, (_GROUP_ROWS, _GROUP_ROWS), 1)
    same_head = (ri // c) == (ci // c)
    eye = (ri == ci).astype(F32)
    r1 = lax.broadcasted_iota(I32, (c, c), 0)
    c1 = lax.broadcasted_iota(I32, (c, c), 1)
    base_mask = (ri // _DN_BASE) == (ci // _DN_BASE)
    masks = []
    for d in range(2):
        incl = same_head & ((ri >= ci) if d == 0 else (ri <= ci))
        strict = same_head & ((ri > ci) if d == 0 else (ri < ci))
        levels = []
        size = _DN_BASE
        while size < c:
            lo_blk, hi_blk = ((ci, ri) if d == 0 else (ri, ci))
            levels.append(((ri // (2 * size)) == (ci // (2 * size)))
                          & ((lo_blk // size) % 2 == 0) & ((hi_blk // size) % 2 == 1))
            size *= 2
        masks.append((incl, strict, levels))
    items, where = [], []
    for b, d in [(b, d) for b in range(BATCH) for d in range(2)]:
        q_ref, k_ref, v_ref, g_ref = in_refs[(b * 2 + d) * 4:(b * 2 + d) * 4 + 4]
        incl, strict, levels = masks[d]
        gates = g_ref[...]
        tri = ((r1 >= c1) if d == 0 else (r1 <= c1)).astype(BF16)
        g_hi = gates.astype(BF16)
        g_r1 = gates - g_hi.astype(F32)
        g_mid = g_r1.astype(BF16)
        g_lo = (g_r1 - g_mid.astype(F32)).astype(BF16)
        gc = _dot(tri, g_hi) + _dot(tri, g_mid) + _dot(tri, g_lo)
        last_row = c - 1 if d == 0 else 0
        tot = gc[last_row:last_row + 1, :]
        for grp in range(DN_HEADS // DN_GROUP):
            heads = range(grp * DN_GROUP, (grp + 1) * DN_GROUP)
            lanes_b = [d * DN_HEADS + hh for hh in heads]
            lanes_g = [2 * DN_HEADS + lb for lb in lanes_b]

            def stack(ref):
                return jnp.concatenate([ref[:, hh * DN_HEAD_DIM:(hh + 1) * DN_HEAD_DIM] for hh in heads], axis=0)

            items.append(dict(q=stack(q_ref), k=stack(k_ref), v=stack(v_ref),
                              beta=_stack_cols(gates, lanes_b, None), gc=_stack_cols(gc, lanes_g, None),
                              g_last=_stack_cols(tot, lanes_g, c), g_last_state=_stack_cols(tot, lanes_g, DN_HEAD_DIM),
                              incl=incl, strict=strict, levels=levels, s_ref=s_ref.at[b, d, grp]))
            where.append((d, b, heads))
    outs = _dn_groups(items, eye, base_mask)
    for o, (d, b, heads) in zip(outs, where):
        for j, hh in enumerate(heads):
            o_scr[d, b, :, hh * DN_HEAD_DIM:(hh + 1) * DN_HEAD_DIM] = o[j * c:(j + 1) * c]

    @pl.when(t < _CTX_CHUNKS)
    def _():
        ofc_ref[...] = o_scr[0]
        obc_ref[...] = o_scr[1]

    @pl.when(t >= _CTX_CHUNKS)
    def _():
        ofl_ref[...] = o_scr[0]
        obl_ref[...] = o_scr[1]


def _deltanet(qkv, gates):
    c = DN_CHUNK
    ins, args = [], []
    for b in range(BATCH):
        for fn in (_dn_fwd_block, _dn_bwd_block):
            for width, col, arr in ((DN_W, 0, qkv), (DN_W, 1, qkv), (DN_W, 2, qkv), (128, 0, gates)):
                ins.append(pl.BlockSpec((c, width), functools.partial(lambda t, b, fn, col: (fn(b, t), col),
                                                                      b=b, fn=fn, col=col)))
                args.append(arr)
    last = _DN_STEPS - 1
    lat = _SDS((BATCH, SEQ, DN_W), F32)
    ctx = _SDS((BATCH, CTX_LEN, DN_W), F32)
    blk = (BATCH, c, DN_W)
    outs = pl.pallas_call(
        _dn_kernel,
        out_shape=(lat, ctx, lat, ctx),
        grid=(_DN_STEPS,),
        in_specs=ins,
        out_specs=(pl.BlockSpec(blk, lambda t: (0, jnp.maximum(t - _CTX_CHUNKS, 0), 0)),
                   pl.BlockSpec(blk, lambda t: (0, jnp.minimum(t, _CTX_CHUNKS - 1), 0)),
                   pl.BlockSpec(blk, lambda t: (0, jnp.minimum(last - t, _LAT_CHUNKS - 1), 0)),
                   pl.BlockSpec(blk, lambda t: (0, jnp.maximum(_CTX_CHUNKS - 1 - t, 0), 0))),
        scratch_shapes=[pltpu.VMEM((BATCH, 2, DN_HEADS // DN_GROUP, _GROUP_STATE, DN_HEAD_DIM), F32),
                        pltpu.VMEM((2, BATCH, c, DN_W), F32)],
        compiler_params=_cparams(("arbitrary",)),
        name="deltanet",
    )(*args)
    ofl, ofc, obl, obc = outs
    return ((ofl.reshape(N_LAT, DN_W), ofc.reshape(N_CTX, DN_W)),
            (obl.reshape(N_LAT, DN_W), obc.reshape(N_CTX, DN_W)))


def _rope_tables():
    t = jnp.arange(SEQ)
    row = (t // GRID_W).astype(F32)
    col = (t % GRID_W).astype(F32)
    n_freq = MLA_ROPE // 4
    inv_freq = ROPE_THETA ** (-jnp.arange(n_freq, dtype=F32) / n_freq)
    ang = jnp.concatenate([row[:, None] * inv_freq, col[:, None] * inv_freq], axis=-1)
    cos, sin = jnp.cos(ang), jnp.sin(ang)
    cos64 = jnp.concatenate([cos, cos], axis=-1)
    sin64 = jnp.concatenate([-sin, sin], axis=-1)

    def assemble(lat_cos, lat_sin):
        c = jnp.concatenate([jnp.tile(lat_cos, (BATCH, 1)), jnp.ones((N_CTX, 128), F32)], axis=0)
        s = jnp.concatenate([jnp.tile(lat_sin, (BATCH, 1)), jnp.zeros((N_CTX, 128), F32)], axis=0)
        return c, s

    cos_l = jnp.concatenate([cos64, jnp.ones((SEQ, 64), F32)], axis=-1)
    sin_l = jnp.concatenate([sin64, jnp.zeros((SEQ, 64), F32)], axis=-1)
    one_head = assemble(cos_l, sin_l)
    two_heads = assemble(jnp.tile(cos64, (1, 2)), jnp.tile(sin64, (1, 2)))
    return one_head, two_heads


def _rope128(x, cosv, sinv):
    lane = lax.broadcasted_iota(I32, x.shape, 1)
    swapped = jnp.where((lane % 64) < 32, pltpu.roll(x, 96, 1), pltpu.roll(x, 32, 1))
    return x * cosv + swapped * sinv


MLA_QK = 256
MLA_VX = 256


def _mla_proj_kernel(ckv_ref, cq_ref, misc_ref, gq_ref, gkv_ref, wq_ref, wkv_ref, cos_ref, sin_ref,
                     q_ref, k_ref, v_ref):
    cq = _rms(cq_ref[...].astype(F32), gq_ref[...]).astype(BF16)
    ckv = _rms(ckv_ref[...].astype(F32), gkv_ref[...]).astype(BF16)
    qf = _dot(cq, wq_ref[...])
    kvf = _dot(ckv, wkv_ref[...])
    cosv, sinv = cos_ref[...], sin_ref[...]
    lane = lax.broadcasted_iota(I32, cosv.shape, 1)
    k_rope = _rope128(jnp.where(lane < MLA_ROPE, misc_ref[...], 0.0), cosv, sinv).astype(BF16)
    scale = (MLA_NOPE + MLA_ROPE) ** -0.5
    for hh in range(MLA_HEADS):
        lo, mid, hi = hh * MLA_QK, hh * MLA_QK + 128, (hh + 1) * MLA_QK
        q_ref[:, lo:mid] = (qf[:, lo:mid] * scale).astype(BF16)
        q_ref[:, mid:hi] = (_rope128(qf[:, mid:hi], cosv, sinv) * scale).astype(BF16)
        k_ref[:, lo:mid] = kvf[:, lo:mid].astype(BF16)
        k_ref[:, mid:hi] = k_rope
        v_ref[:, lo:mid] = kvf[:, mid:hi].astype(BF16)
        v_ref[:, mid:hi] = jnp.ones((kvf.shape[0], MLA_VX - MLA_V), BF16)


def _mla_proj(p_rest, q_norm_g, kv_norm_g, w_q, w_kv, cosv, sinv):
    n = p_rest.shape[0]
    tm = ROW_TILE
    wide = MLA_HEADS * MLA_QK

    def rows(width, col):
        return pl.BlockSpec((tm, width), lambda i: (i, col))

    def whole(shape):
        return pl.BlockSpec(shape, lambda i: (0, 0))

    return pl.pallas_call(
        _mla_proj_kernel,
        out_shape=(_SDS((n, wide), BF16), _SDS((n, wide), BF16), _SDS((n, MLA_HEADS * MLA_VX), BF16)),
        grid=(n // tm,),
        in_specs=[rows(MLA_KV_RANK, R_CKV // MLA_KV_RANK), rows(MLA_Q_RANK, R_CQ // MLA_Q_RANK),
                  rows(128, R_MISC // 128), whole((1, MLA_Q_RANK)), whole((1, MLA_KV_RANK)),
                  whole((MLA_Q_RANK, wide)), whole((MLA_KV_RANK, wide)), rows(128, 0), rows(128, 0)],
        out_specs=(rows(wide, 0), rows(wide, 0), rows(MLA_HEADS * MLA_VX, 0)),
        compiler_params=_cparams(("parallel",), VMEM_LIMIT),
        name="mla_proj",
    )(p_rest, p_rest, p_rest, q_norm_g.reshape(1, -1), kv_norm_g.reshape(1, -1), w_q, w_kv, cosv, sinv)


MLA_KEY_CHUNK = 2048


def _flash_kernel(*refs, chunks):
    n_seg = (len(refs) - 2) // 2
    q_ref, o_ref = refs[0], refs[-1]
    k_refs, v_refs = refs[1:1 + n_seg], refs[1 + n_seg:1 + 2 * n_seg]
    q = q_ref[...]

    def scores(c):
        seg, off, size = c
        return _dot_nt(q, k_refs[seg][off:off + size, :])

    m = jnp.full((q.shape[0], 1), -jnp.inf, F32)
    acc = jnp.zeros((q.shape[0], MLA_VX), F32)
    s_next = scores(chunks[0])
    for j, (seg, off, size) in enumerate(chunks):
        s = s_next
        if j + 1 < len(chunks):
            s_next = scores(chunks[j + 1])
        m_new = jnp.maximum(m, jnp.max(s, axis=-1, keepdims=True))
        p = jnp.exp(s - m_new).astype(BF16)
        acc = jnp.exp(m - m_new) * acc + _dot(p, v_refs[seg][off:off + size, :])
        m = m_new
    o_ref[...] = (acc[:, :MLA_V] / acc[:, MLA_V:]).astype(o_ref.dtype)


def _mla_attention(q, k, v):
    tq = 1024
    nq = SEQ // tq
    ctx_blk = 2 * SEQ // CTX_LEN
    lat_chunks = ((0, 0, CTX_LEN),) + tuple((1, off, MLA_KEY_CHUNK) for off in range(0, SEQ, MLA_KEY_CHUNK))
    lat = pl.pallas_call(
        functools.partial(_flash_kernel, chunks=lat_chunks),
        out_shape=_SDS((N_LAT, MLA_HEADS * MLA_V), BF16),
        grid=(BATCH, MLA_HEADS, nq),
        in_specs=[pl.BlockSpec((tq, MLA_QK), lambda b, h, i: (b * nq + i, h)),
                  pl.BlockSpec((CTX_LEN, MLA_QK), lambda b, h, i: (ctx_blk + b, h)),
                  pl.BlockSpec((SEQ, MLA_QK), lambda b, h, i: (b, h)),
                  pl.BlockSpec((CTX_LEN, MLA_VX), lambda b, h, i: (ctx_blk + b, h)),
                  pl.BlockSpec((SEQ, MLA_VX), lambda b, h, i: (b, h))],
        out_specs=pl.BlockSpec((tq, MLA_V), lambda b, h, i: (b * nq + i, h)),
        compiler_params=_cparams(("parallel", "parallel", "arbitrary"), VMEM_LIMIT),
        name="mla_attn_latent",
    )(q, k, k, v, v)
    ctx = pl.pallas_call(
        functools.partial(_flash_kernel, chunks=((0, 0, CTX_LEN),)),
        out_shape=_SDS((N_CTX, MLA_HEADS * MLA_V), BF16),
        grid=(BATCH, MLA_HEADS),
        in_specs=[pl.BlockSpec((CTX_LEN, MLA_QK), lambda b, h: (ctx_blk + b, h)),
                  pl.BlockSpec((CTX_LEN, MLA_QK), lambda b, h: (ctx_blk + b, h)),
                  pl.BlockSpec((CTX_LEN, MLA_VX), lambda b, h: (ctx_blk + b, h))],
        out_specs=pl.BlockSpec((CTX_LEN, MLA_V), lambda b, h: (b, h)),
        compiler_params=_cparams(("parallel", "parallel")),
        name="mla_attn_context",
    )(q, k, v)
    return lat, ctx


def _ab_mix_kernel(ofl_ref, ofc_ref, obl_ref, obc_ref, ml_ref, mc_ref, z_ref, g_ref, a_ref):
    for hh in range(DN_HEADS):
        sl = slice(hh * DN_HEAD_DIM, (hh + 1) * DN_HEAD_DIM)
        o = _split_read(ofl_ref, ofc_ref, sl) + _split_read(obl_ref, obc_ref, sl)
        a_ref[:, sl] = (_rms(o, g_ref[...]) * _silu(z_ref[:, sl].astype(F32))).astype(BF16)
    a_ref[:, DN_W:] = _split_read(ml_ref, mc_ref)


def _ab_mix(o_f, o_b, p, dn_norm_g, mla_o):
    tm = ROW_TILE
    half = DN_W
    return pl.pallas_call(
        _ab_mix_kernel,
        out_shape=_SDS((N_TOK, 2 * half), BF16),
        grid=(N_TOK // tm,),
        in_specs=_split_specs(half) + _split_specs(half) + _split_specs(half) + [
            pl.BlockSpec((tm, half), lambda i: (i, P_Z // half)),
            pl.BlockSpec((1, DN_HEAD_DIM), lambda i: (0, 0))],
        out_specs=pl.BlockSpec((tm, 2 * half), lambda i: (i, 0)),
        compiler_params=_cparams(("parallel",)),
        name="ab_mix",
    )(*o_f, *o_b, *mla_o, p, dn_norm_g.reshape(1, -1))


def _mm_rope_kernel(a_ref, w_ref, b_ref, cos_ref, sin_ref, o_ref, *, tn, rope_cols):
    y = _dot(a_ref[...], w_ref[...].astype(BF16)) + b_ref[...]
    col0 = pl.program_id(1) * tn
    cosv, sinv = cos_ref[...], sin_ref[...]
    for s in range(tn // 128):
        ys = y[:, s * 128:(s + 1) * 128]
        roped = _rope128(ys, cosv, sinv)
        o_ref[:, s * 128:(s + 1) * 128] = jnp.where(col0 + s * 128 < rope_cols, roped, ys).astype(o_ref.dtype)


def _gqa_qkv(u, w, bias, cosv, sinv):
    m, k = u.shape
    n = w.shape[1]
    tm, tn = MM_ROW_TILE, 1280
    return pl.pallas_call(
        functools.partial(_mm_rope_kernel, tn=tn, rope_cols=GQA_Q + GQA_KV),
        out_shape=_SDS((m, n), BF16),
        grid=(m // tm, n // tn),
        in_specs=[pl.BlockSpec((tm, k), lambda i, j: (i, 0)),
                  pl.BlockSpec((k, tn), lambda i, j: (0, j)),
                  pl.BlockSpec((1, tn), lambda i, j: (0, j)),
                  pl.BlockSpec((tm, 128), lambda i, j: (i, 0)),
                  pl.BlockSpec((tm, 128), lambda i, j: (i, 0))],
        out_specs=pl.BlockSpec((tm, tn), lambda i, j: (i, j)),
        compiler_params=_cparams(("parallel", "arbitrary"), VMEM_LIMIT),
        name="gqa_qkv_rope",
    )(u, w, bias.reshape(1, n), cosv, sinv)


def _gqa_kernel(sink_ref, q_ref, kp_ref, kc_ref, kn_ref, vp_ref, vc_ref, vn_ref, kx_ref, vx_ref, o_ref):
    i = pl.program_id(1)
    nb = SEQ // Q_BLOCK
    n_keys = CTX_LEN + 3 * Q_BLOCK
    groups = GQA_HEADS // GQA_KV_HEADS
    slabs = groups // 2
    dh = GQA_HEAD_DIM
    qi = lax.broadcasted_iota(I32, (Q_BLOCK, n_keys), 0)
    kj = lax.broadcasted_iota(I32, (Q_BLOCK, n_keys), 1) - CTX_LEN
    rel = kj - Q_BLOCK - qi
    valid = (kj < 0) | ((jnp.abs(rel) <= WINDOW) & ((kj >= Q_BLOCK) | (i > 0)) & ((kj < 2 * Q_BLOCK) | (i < nb - 1)))
    bias = jnp.where(valid, 0.0, -jnp.inf).astype(F32)
    bias = jnp.concatenate([bias] * slabs, axis=0)
    k_all = jnp.concatenate([kx_ref[...], kp_ref[...], kc_ref[...], kn_ref[...]], axis=0)
    v_all = jnp.concatenate([vx_ref[...], vp_ref[...], vc_ref[...], vn_ref[...]], axis=0)
    low = lax.broadcasted_iota(I32, (n_keys, 128), 1) < dh

    def block_diag(x_all, kvh):
        pair = x_all[:, (kvh // 2) * 128:(kvh // 2 + 1) * 128].astype(F32)
        other = pltpu.roll(pair, dh, 1)
        first, second = (pair, other) if kvh % 2 == 0 else (other, pair)
        return jnp.concatenate([jnp.where(low, first, 0.0), jnp.where(low, 0.0, second)], axis=0).astype(BF16)

    def scores(kvh):
        q = jnp.concatenate([q_ref[:, (kvh * slabs + r) * 128:(kvh * slabs + r + 1) * 128] for r in range(slabs)], axis=0)
        return _dot_nt(q * (dh ** -0.5), block_diag(k_all, kvh))

    row_lo = lax.broadcasted_iota(I32, (2 * n_keys, 128), 0) < n_keys
    lane_lo = lax.broadcasted_iota(I32, (2 * n_keys, 128), 1) < dh
    ones_cols = jnp.where(row_lo == lane_lo, 1.0, 0.0).astype(BF16)

    def softmax(kvh, s):
        ps, sink_terms = [], []
        for half in range(2):
            sh = s[:, half * n_keys:(half + 1) * n_keys] + bias
            sink = jnp.concatenate([jnp.full((Q_BLOCK, 1), sink_ref[(kvh * slabs + r) * 2 + half], F32)
                                    for r in range(slabs)], axis=0)
            m = jnp.maximum(jnp.max(sh, axis=-1, keepdims=True), sink)
            ps.append(jnp.exp(sh - m).astype(BF16))
            sink_terms.append(jnp.exp(sink - m))
        return jnp.concatenate(ps, axis=1), sink_terms

    def finish(kvh, p, sink_terms):
        o = _dot(p, jnp.concatenate([block_diag(v_all, kvh), ones_cols], axis=1))
        num, den = o[:, :128], o[:, 128:]
        lane_low = lax.broadcasted_iota(I32, num.shape, 1) < dh
        o = num / (den + jnp.where(lane_low, sink_terms[0], sink_terms[1]))
        for r in range(slabs):
            o_ref[:, (kvh * slabs + r) * 128:(kvh * slabs + r + 1) * 128] = o[r * Q_BLOCK:(r + 1) * Q_BLOCK].astype(o_ref.dtype)

    s_next = scores(0)
    for kvh in range(GQA_KV_HEADS):
        s_cur = s_next
        if kvh + 1 < GQA_KV_HEADS:
            s_next = scores(kvh + 1)
        p, dens = softmax(kvh, s_cur)
        finish(kvh, p, dens)


def _gqa_attention(qkv, sink):
    nb = SEQ // Q_BLOCK
    kcol, vcol = GQA_Q // GQA_KV, GQA_Q // GQA_KV + 1
    ctx_blk = 2 * SEQ // CTX_LEN

    def win(col, shift):
        return pl.BlockSpec((Q_BLOCK, GQA_KV), lambda b, i: (b * nb + jnp.clip(i + shift, 0, nb - 1), col))

    return pl.pallas_call(
        _gqa_kernel,
        out_shape=_SDS((N_LAT, GQA_Q), BF16),
        grid=(BATCH, nb),
        in_specs=[pl.BlockSpec(memory_space=pltpu.SMEM),
                  pl.BlockSpec((Q_BLOCK, GQA_Q), lambda b, i: (b * nb + i, 0)),
                  win(kcol, -1), win(kcol, 0), win(kcol, 1), win(vcol, -1), win(vcol, 0), win(vcol, 1),
                  pl.BlockSpec((CTX_LEN, GQA_KV), lambda b, i: (ctx_blk + b, kcol)),
                  pl.BlockSpec((CTX_LEN, GQA_KV), lambda b, i: (ctx_blk + b, vcol))],
        out_specs=pl.BlockSpec((Q_BLOCK, GQA_Q), lambda b, i: (b * nb + i, 0)),
        compiler_params=_cparams(("parallel", "arbitrary")),
        name="gqa_window_attn",
    )(sink.astype(F32), qkv, qkv, qkv, qkv, qkv, qkv, qkv, qkv, qkv)


GATHER_TILE = 256
COMBINE_TILE = 256


def _row_copy(src_hbm, row, dst, r, sem):
    return pltpu.make_async_copy(src_hbm.at[pl.ds(row, 1), :], dst.at[pl.ds(r, 1), :], sem)


def _wait_slot(src_hbm, slot_buf, sem):
    def drain(g, carry):
        pltpu.make_async_copy(src_hbm.at[pl.ds(0, 8), :], slot_buf.at[g], sem).wait()
        return carry

    lax.fori_loop(0, slot_buf.shape[0], drain, 0)


def _gather_kernel(tok_ref, tv_ref, u_hbm, valid_ref, o_ref, buf, sem):
    i = pl.program_id(0)
    n = pl.num_programs(0)

    def start_tile(t, slot):
        @pl.when(tv_ref[t] > 0)
        def _():
            def issue(r8, carry):
                for j in range(8):
                    row = tok_ref[t * GATHER_TILE + r8 * 8 + j]
                    _row_copy(u_hbm, row, buf.at[slot, r8], j, sem.at[slot]).start(priority=j % 2)
                return carry

            lax.fori_loop(0, GATHER_TILE // 8, issue, 0)

    @pl.when(i == 0)
    def _():
        start_tile(0, 0)

    @pl.when(i + 1 < n)
    def _():
        start_tile(i + 1, (i + 1) % 2)

    slot = i % 2

    @pl.when(tv_ref[i] > 0)
    def _():
        _wait_slot(u_hbm, buf.at[slot], sem.at[slot])
        lo, hi = _unpack_bf16_halves(buf[slot].reshape(GATHER_TILE, D_MODEL // 2))
        keep = valid_ref[...] > 0.0
        half = D_MODEL // 2
        o_ref[:, :half] = jnp.where(keep, lo, jnp.zeros_like(lo))
        o_ref[:, half:] = jnp.where(keep, hi, jnp.zeros_like(hi))

    @pl.when(tv_ref[i] == 0)
    def _():
        o_ref[...] = jnp.zeros_like(o_ref)


def _moe_gather(slot_tok, slot_valid, u):
    n_slots = slot_tok.shape[0]
    n_tiles = n_slots // GATHER_TILE
    tile_valid = (jnp.max(slot_valid.reshape(n_tiles, GATHER_TILE), axis=1) > 0.0).astype(I32)
    return pl.pallas_call(
        _gather_kernel,
        out_shape=_SDS((n_slots, D_MODEL), BF16),
        grid_spec=pltpu.PrefetchScalarGridSpec(
            num_scalar_prefetch=2,
            grid=(n_tiles,),
            in_specs=[pl.BlockSpec(memory_space=pl.ANY),
                      pl.BlockSpec((GATHER_TILE, 1), lambda i, tok, tv: (i, 0))],
            out_specs=pl.BlockSpec((GATHER_TILE, D_MODEL), lambda i, tok, tv: (i, 0)),
            scratch_shapes=[pltpu.VMEM((2, GATHER_TILE // 8, 8, D_MODEL // 2), U32), pltpu.SemaphoreType.DMA((2,))]),
        compiler_params=_cparams(("arbitrary",)),
        name="moe_gather",
    )(slot_tok, tile_valid, u, slot_valid.reshape(n_slots, 1))


def _moe_ffn_kernel(be_ref, bv_ref, x_ref, wg_ref, wu_ref, wd_ref, o_ref, h_ref):
    i, s = pl.program_id(0), pl.program_id(1)
    n_sub = bv_ref[i]
    for k in range(1, MOE_BLOCK // MOE_SUB + 1):
        _swiglu_step(s, x_ref, wg_ref.at[0], wu_ref.at[0], wd_ref.at[0], o_ref, h_ref,
                     rows=k * MOE_SUB, guard=n_sub == k)

    @pl.when((n_sub == 0) & (s >= FF_STEPS))
    def _():
        o_ref[...] = jnp.zeros_like(o_ref)


def _moe_ffn(block_expert, block_valid, xs, w_gate, w_up, w_down):
    n_slots = xs.shape[0]
    tm = MOE_BLOCK

    def ff(i, s, be, bv):
        return (be[i], 0, jnp.where(bv[i] > 0, jnp.minimum(s, FF_STEPS - 1), FF_STEPS - 1))

    def down(i, s, be, bv):
        return (be[i], 0, jnp.where(bv[i] > 0, jnp.maximum(s - FF_STEPS, 0), OUT_STEPS - 1))

    return pl.pallas_call(
        _moe_ffn_kernel,
        out_shape=_SDS((n_slots, D_MODEL), F32),
        grid_spec=pltpu.PrefetchScalarGridSpec(
            num_scalar_prefetch=2,
            grid=(n_slots // tm, FF_STEPS + OUT_STEPS),
            in_specs=[pl.BlockSpec((tm, D_MODEL), lambda i, s, be, bv: (i, 0), pipeline_mode=pl.Buffered(1)),
                      pl.BlockSpec((1, D_MODEL, FF_TILE), ff),
                      pl.BlockSpec((1, D_MODEL, FF_TILE), ff),
                      pl.BlockSpec((1, D_FF, OUT_TILE), down)],
            out_specs=pl.BlockSpec((tm, OUT_TILE), lambda i, s, be, bv: (i, jnp.maximum(s - FF_STEPS, 0))),
            scratch_shapes=[pltpu.VMEM((tm, D_FF), BF16)]),
        compiler_params=_cparams(("arbitrary", "arbitrary"), SWIGLU_VMEM_LIMIT),
        name="moe_grouped_swiglu",
    )(block_expert, block_valid, xs, w_gate, w_up, w_down)


def _combine_kernel(pos_ref, y_hbm, wt_ref, h_ref, gp_ref, gate_ref, o_ref, buf_a, buf_b, sem):
    i = pl.program_id(0)
    n = pl.num_programs(0)
    base = i * COMBINE_TILE

    def start_tile(t, slot):
        def issue(r8, carry):
            for j in range(8):
                a = 2 * (t * COMBINE_TILE + r8 * 8 + j)
                _row_copy(y_hbm, pos_ref[a], buf_a.at[slot, r8], j, sem.at[slot]).start(priority=0)
                _row_copy(y_hbm, pos_ref[a + 1], buf_b.at[slot, r8], j, sem.at[slot]).start(priority=1)
            return carry

        lax.fori_loop(0, COMBINE_TILE // 8, issue, 0)

    @pl.when(i == 0)
    def _():
        start_tile(0, 0)

    @pl.when(i + 1 < n)
    def _():
        start_tile(i + 1, (i + 1) % 2)

    slot = i % 2

    _wait_slot(y_hbm, buf_a.at[slot], sem.at[slot])
    _wait_slot(y_hbm, buf_b.at[slot], sem.at[slot])
    wt = wt_ref[...]
    rows = (COMBINE_TILE, D_MODEL)
    y = wt[:, 0:1] * buf_a[slot].reshape(rows) + wt[:, 1:2] * buf_b[slot].reshape(rows)
    grp = _group_of_row(base)
    o_ref[...] = h_ref[...] + _mod_row(gate_ref, grp) * _rms(y, gp_ref[...])


def _moe_combine(pos, ys, wts, h, g_post, mod, gate_lk):
    n = wts.shape[0]
    tm = COMBINE_TILE
    return pl.pallas_call(
        _combine_kernel,
        out_shape=_SDS((n, D_MODEL), F32),
        grid_spec=pltpu.PrefetchScalarGridSpec(
            num_scalar_prefetch=1,
            grid=(n // tm,),
            in_specs=[pl.BlockSpec(memory_space=pl.ANY),
                      pl.BlockSpec((tm, 128), lambda i, pos: (i, 0)),
                      pl.BlockSpec((tm, D_MODEL), lambda i, pos: (i, 0)),
                      pl.BlockSpec((1, D_MODEL), lambda i, pos: (0, 0)),
                      pl.BlockSpec((1, 8, D_MODEL), lambda i, pos: (gate_lk[0], 0, gate_lk[1]))],
            out_specs=pl.BlockSpec((tm, D_MODEL), lambda i, pos: (i, 0)),
            scratch_shapes=[pltpu.VMEM((2, tm // 8, 8, D_MODEL), F32), pltpu.VMEM((2, tm // 8, 8, D_MODEL), F32),
                            pltpu.SemaphoreType.DMA((2,))]),
        compiler_params=_cparams(("arbitrary",)),
        name="moe_combine",
    )(pos, ys, wts, h, g_post.reshape(1, D_MODEL), mod)


def _moe_routing(idx):
    flat_e = idx[:, :TOP_K].reshape(-1)
    n_assign = flat_e.shape[0]
    onehot = (flat_e[:, None] == jnp.arange(N_EXPERTS, dtype=I32)[None, :]).astype(I32)
    csum = jnp.cumsum(onehot, axis=0)
    counts = csum[-1]
    padded = (counts + MOE_BLOCK - 1) // MOE_BLOCK * MOE_BLOCK
    pad_end = jnp.cumsum(padded)
    pad_start = pad_end - padded
    pos = jnp.sum(onehot * (pad_start[None, :] + csum - 1), axis=1).astype(I32)
    slot_tok = jnp.zeros((MOE_SLOTS,), I32).at[pos].set(jnp.arange(n_assign, dtype=I32) // TOP_K)
    slot = jnp.arange(MOE_SLOTS, dtype=I32)
    used_end = pad_start + counts
    slot_valid = jnp.any((slot[:, None] >= pad_start[None, :]) & (slot[:, None] < used_end[None, :]), axis=1)
    blk_start = jnp.arange(MOE_SLOTS // MOE_BLOCK, dtype=I32) * MOE_BLOCK
    last_start = jnp.maximum(pad_end[-1] - MOE_BLOCK, 0)
    blk_e = jnp.searchsorted(pad_end, jnp.minimum(blk_start, last_start), side='right').astype(I32)
    blk_e = jnp.minimum(blk_e, N_EXPERTS - 1)
    rows_used = jnp.clip(used_end[blk_e] - blk_start, 0, MOE_BLOCK)
    blk_sub = ((rows_used + MOE_SUB - 1) // MOE_SUB).astype(I32)
    return pos, slot_tok, slot_valid.astype(F32), blk_e, blk_sub


def _rest_of_w_in_t(w_in_t):
    dn_in = P_MAIN + 4 * DN_HEADS
    ba = w_in_t[P_MAIN:dn_in]
    cq = w_in_t[dn_in:dn_in + MLA_Q_RANK]
    ckv = w_in_t[dn_in + MLA_Q_RANK:dn_in + MLA_Q_RANK + MLA_KV_RANK]
    k_rope = w_in_t[dn_in + MLA_Q_RANK + MLA_KV_RANK:]
    pad = jnp.zeros((R_CQ - R_MISC - MLA_ROPE - 4 * DN_HEADS, D_MODEL), w_in_t.dtype)
    return jnp.concatenate([ckv, k_rope, ba, pad, cq], axis=0)


def _rearrange_w_qb(w_qb):
    w = w_qb.reshape(MLA_Q_RANK, MLA_HEADS, MLA_NOPE + MLA_ROPE)
    w = jnp.pad(w, ((0, 0), (0, 0), (0, MLA_QK - MLA_NOPE - MLA_ROPE)))
    return w.reshape(MLA_Q_RANK, MLA_HEADS * MLA_QK).astype(BF16)


def kernel(x, c, ctx, c_ctx, ada_w, ada_b, norm_g, ab_w_in, dn_conv_w, dn_a_log, dn_dt_bias, dn_norm_g, mla_q_norm_g, mla_w_qb, mla_kv_norm_g, mla_w_kvb, ab_w_out, ffn_w_gate, ffn_w_up, ffn_w_down, gqa_w_qkv, gqa_b_qkv, gqa_sink, gqa_w_out, gqa_b_out, moe_w_router, moe_w_gate, moe_w_up, moe_w_down):
    assert x.shape == (BATCH, SEQ, D_MODEL) and ctx.shape == (BATCH, CTX_LEN, D_MODEL) and ada_w.shape[0] == 2
    h = (x.reshape(N_LAT, D_MODEL), ctx.reshape(N_CTX, D_MODEL))
    cvec = jnp.zeros((8, D_MODEL), F32).at[:BATCH].set(c).at[BATCH].set(c_ctx)
    mod = _ada(cvec, ada_w, ada_b)
    (cos1, sin1), (cos2, sin2) = _rope_tables()

    u = _modulate_in(*h, norm_g[0, 0], mod, 0, 0, 1)
    w_in_t = jnp.swapaxes(ab_w_in, 1, 2)
    p = _matmul(u, w_in_t, None, BF16, MM_ROW_TILE, 1024, n=P_MAIN, w_is_transposed=True)
    p_rest = _matmul(u, _rest_of_w_in_t(w_in_t[0]), None, F32, MM_ROW_TILE, R_WIDTH // 2, w_is_transposed=True)
    o_f, o_b = _deltanet(_dn_conv(p, dn_conv_w[0]), _dn_gates(p_rest, dn_a_log[0], dn_dt_bias[0]))
    q, k, v = _mla_proj(p_rest, mla_q_norm_g[0], mla_kv_norm_g[0], _rearrange_w_qb(mla_w_qb[0]),
                        mla_w_kvb[0].astype(BF16), cos1, sin1)
    a = _ab_mix(o_f, o_b, p, dn_norm_g[0], _mla_attention(q, k, v))
    y = _matmul(a, ab_w_out[0].astype(BF16), None, BF16, MM_ROW_TILE, 1024)
    h, u = _post(h, y, norm_g[0, 1], norm_g[0, 2], mod, (0, 2), (0, 3), (0, 4), BF16)
    y = _ffn(u, ffn_w_gate[0], ffn_w_up[0], ffn_w_down[0])
    h, u = _post(h, y, norm_g[0, 3], norm_g[1, 0], mod, (0, 5), (1, 0), (1, 1), BF16)

    qkv = _gqa_qkv(u, gqa_w_qkv[0].astype(BF16), gqa_b_qkv[0], cos2, sin2)
    o = _gqa_attention(qkv, gqa_sink[0])
    y = _matmul(o, gqa_w_out[0].astype(BF16), gqa_b_out[0], BF16, 1024, 1024)
    h, u, idx, wts = _post(h, y, norm_g[1, 1], norm_g[1, 2], mod, (1, 2), (1, 3), (1, 4), F32, moe_w_router[0])
    pos, slot_tok, slot_valid, blk_e, blk_sub = _moe_routing(idx)
    xs = _moe_gather(slot_tok, slot_valid, u)
    ys = _moe_ffn(blk_e, blk_sub, xs, moe_w_gate[0], moe_w_up[0], moe_w_down[0])
    out = _moe_combine(pos, ys, wts, h, norm_g[1, 3], mod, (1, 5))
    return out.reshape(BATCH, SEQ, D_MODEL)
```

```python
import functools
import math

import jax
import jax.numpy as jnp
from jax import lax
from jax.experimental import pallas as pl
from jax.experimental.pallas import tpu as pltpu

F32 = jnp.float32
BF16 = jnp.bfloat16
I32 = jnp.int32

D_MODEL = 2048
BATCH = 2
SEQ = 4096
CTX_LEN = 256
GRID_W = 64
EPS = 1e-6
ROPE_THETA = 10000.0

DN_HEADS = 8
DN_HEAD_DIM = 128
DN_CONV = 5
DN_CHUNK = 64
MLA_HEADS = 8
MLA_Q_RANK = 768
MLA_KV_RANK = 512
MLA_NOPE = 128
MLA_ROPE = 64
MLA_V = 128
GQA_HEADS = 32
GQA_KV_HEADS = 4
GQA_HEAD_DIM = 64
WINDOW = 128
Q_BLOCK = 128
D_FF = 7168
N_EXPERTS = 8
TOP_K = 2

DN_W = DN_HEADS * DN_HEAD_DIM
DN_QKV = 3 * DN_W
N_LAT = BATCH * SEQ
N_CTX = BATCH * CTX_LEN
N_TOK = N_LAT + N_CTX
GQA_Q = GQA_HEADS * GQA_HEAD_DIM
GQA_KV = GQA_KV_HEADS * GQA_HEAD_DIM

P_QKV = 0
P_Z = DN_QKV
P_MAIN = P_Z + DN_W
R_CKV = 0
R_MISC = MLA_KV_RANK
R_CQ = MLA_Q_RANK
R_WIDTH = R_CQ + MLA_Q_RANK

ROW_TILE = 512
MM_ROW_TILE = 1088
MOE_BLOCK = 1024
MOE_SUB = 128
MOE_SLOTS = N_LAT * TOP_K + N_EXPERTS * MOE_BLOCK
VMEM_LIMIT = 56 * 1024 * 1024

_SDS = jax.ShapeDtypeStruct


def _cparams(sem, vmem=None):
    return pltpu.CompilerParams(dimension_semantics=sem, vmem_limit_bytes=vmem)


def _dot(a, b):
    return jnp.dot(a, b, preferred_element_type=F32)


def _dot_nt(a, b):
    return lax.dot_general(a, b, (((1,), (1,)), ((), ())), preferred_element_type=F32)


def _dot_tn(a, b):
    return lax.dot_general(a, b, (((0,), (0,)), ((), ())), preferred_element_type=F32)


def _silu(x):
    return x * jax.nn.sigmoid(x)


def _rms(x, g):
    return x * lax.rsqrt(jnp.mean(x * x, axis=-1, keepdims=True) + EPS) * g


def _group_of_row(r0):
    return (r0 >= SEQ).astype(I32) + (r0 >= 2 * SEQ).astype(I32)


def _mod_spec(layer, k):
    return pl.BlockSpec((1, 8, D_MODEL), lambda *_: (layer, 0, k))


def _mod_row(ref, grp):
    return ref[0, pl.ds(grp, 1), :]


def _ada_kernel(c_ref, w_ref, b_ref, o_ref):
    s = _silu(c_ref[...]).astype(BF16)
    o_ref[0] = _dot(s, w_ref[0].astype(BF16)) + b_ref[0]


def _ada(cvec, ada_w, ada_b):
    n_layers, _, n = ada_w.shape
    tn = 2048
    return pl.pallas_call(
        _ada_kernel,
        out_shape=_SDS((n_layers, 8, n), F32),
        grid=(n_layers, n // tn),
        in_specs=[pl.BlockSpec((8, D_MODEL), lambda l, j: (0, 0)),
                  pl.BlockSpec((1, D_MODEL, tn), lambda l, j: (l, 0, j)),
                  pl.BlockSpec((1, 1, tn), lambda l, j: (l, 0, j))],
        out_specs=pl.BlockSpec((1, 8, tn), lambda l, j: (l, 0, j)),
        compiler_params=_cparams(("parallel", "parallel")),
        name="ada_mod",
    )(cvec, ada_w, ada_b.reshape(n_layers, 1, n))


_LAT_TILES = N_LAT // ROW_TILE
assert N_CTX == ROW_TILE


def _split_specs(width, col=0):
    lat = pl.BlockSpec((ROW_TILE, width), lambda i, *_: (jnp.minimum(i, _LAT_TILES - 1), col))
    ctx = pl.BlockSpec((ROW_TILE, width), lambda i, *_: (0, col))
    return [lat, ctx]


def _split_read(lat_ref, ctx_ref, sl=slice(None)):
    return jnp.where(pl.program_id(0) == _LAT_TILES, ctx_ref[:, sl], lat_ref[:, sl])


def _modin_kernel(x_ref, c_ref, g_ref, sh_ref, sc_ref, u_ref):
    grp = _group_of_row(pl.program_id(0) * ROW_TILE)
    h = _split_read(x_ref, c_ref)
    u = _rms(h, g_ref[...]) * (1.0 + _mod_row(sc_ref, grp)) + _mod_row(sh_ref, grp)
    u_ref[...] = u.astype(u_ref.dtype)


def _modulate_in(x, ctx, g, mod, layer, k_shift, k_scale):
    row = pl.BlockSpec((ROW_TILE, D_MODEL), lambda i: (i, 0))
    vec = pl.BlockSpec((1, D_MODEL), lambda i: (0, 0))
    return pl.pallas_call(
        _modin_kernel,
        out_shape=_SDS((N_TOK, D_MODEL), BF16),
        grid=(N_TOK // ROW_TILE,),
        in_specs=_split_specs(D_MODEL) + [vec, _mod_spec(layer, k_shift), _mod_spec(layer, k_scale)],
        out_specs=row,
        compiler_params=_cparams(("parallel",)),
        name="modulate_in",
    )(x, ctx, g.reshape(1, D_MODEL), mod, mod)


def _route_top2(logits):
    lane = lax.broadcasted_iota(I32, logits.shape, 1)
    neg = -jnp.inf
    l0 = jnp.where(lane < N_EXPERTS, logits, neg)
    m1 = jnp.max(l0, axis=-1, keepdims=True)
    i1 = jnp.min(jnp.where(l0 == m1, lane, 128), axis=-1, keepdims=True)
    l1 = jnp.where(lane == i1, neg, l0)
    m2 = jnp.max(l1, axis=-1, keepdims=True)
    i2 = jnp.min(jnp.where(l1 == m2, lane, 128), axis=-1, keepdims=True)
    e = jnp.exp(m2 - m1)
    w1 = 1.0 / (1.0 + e)
    idx = jnp.where(lane == 0, i1, jnp.where(lane == 1, i2, 0))
    wts = jnp.where(lane == 0, w1, jnp.where(lane == 1, e * w1, 0.0))
    return idx, wts


U32 = jnp.uint32


def _pack_bf16_halves(x):
    n = x.shape[1] // 2
    lo = lax.bitcast_convert_type(x[:, :n].astype(F32), U32) >> 16
    hi = (lax.bitcast_convert_type(x[:, n:].astype(F32), U32) >> 16) << 16
    return hi | lo


def _unpack_bf16_halves(w):
    lo = lax.bitcast_convert_type(w << 16, F32).astype(BF16)
    hi = lax.bitcast_convert_type((w >> 16) << 16, F32).astype(BF16)
    return lo, hi


def _post_kernel(*refs, split, route):
    refs = list(refs)
    if split:
        h = _split_read(refs.pop(0), refs.pop(0))
    else:
        h = refs.pop(0)[...]
    y_ref, gp_ref, gate_ref, gn_ref, sh_ref, sc_ref = refs[:6]
    refs = refs[6:]
    wr_ref = refs.pop(0) if route else None
    hn_ref, u_ref = refs[:2]
    grp = _group_of_row(pl.program_id(0) * ROW_TILE)
    hn = h + _mod_row(gate_ref, grp) * _rms(y_ref[...].astype(F32), gp_ref[...])
    hn_ref[...] = hn
    u = _rms(hn, gn_ref[...]) * (1.0 + _mod_row(sc_ref, grp)) + _mod_row(sh_ref, grp)
    if route:
        ub = u.astype(BF16)
        u_ref[...] = _pack_bf16_halves(ub)
        idx_ref, wt_ref = refs[2:4]
        idx_ref[...], wt_ref[...] = _route_top2(_dot(ub, wr_ref[...]))
    else:
        u_ref[...] = u.astype(u_ref.dtype)


def _post(h, y, g_post, g_next, mod, gate_lk, shift_lk, scale_lk, u_dtype, w_router=None):
    n = y.shape[0]
    split = isinstance(h, tuple)
    route = w_router is not None
    row = pl.BlockSpec((ROW_TILE, D_MODEL), lambda i: (i, 0))
    vec = pl.BlockSpec((1, D_MODEL), lambda i: (0, 0))
    slab = pl.BlockSpec((ROW_TILE, 128), lambda i: (i, 0))
    h_specs, h_args = (_split_specs(D_MODEL), list(h)) if split else ([row], [h])
    extra_specs, extra_args, extra_out, extra_out_specs = [], [], (), ()
    u_shape, u_spec = _SDS((n, D_MODEL), u_dtype), row
    if route:
        w = jnp.zeros((D_MODEL, 128), BF16).at[:, :N_EXPERTS].set(w_router.astype(BF16))
        extra_specs, extra_args = [pl.BlockSpec((D_MODEL, 128), lambda i: (0, 0))], [w]
        extra_out, extra_out_specs = (_SDS((n, 128), I32), _SDS((n, 128), F32)), (slab, slab)
        u_shape, u_spec = _SDS((n, D_MODEL // 2), U32), pl.BlockSpec((ROW_TILE, D_MODEL // 2), lambda i: (i, 0))
    return pl.pallas_call(
        functools.partial(_post_kernel, split=split, route=route),
        out_shape=(_SDS((n, D_MODEL), F32), u_shape) + extra_out,
        grid=(n // ROW_TILE,),
        in_specs=h_specs + [row, vec, _mod_spec(*gate_lk), vec, _mod_spec(*shift_lk), _mod_spec(*scale_lk)] + extra_specs,
        out_specs=(row, u_spec) + extra_out_specs,
        compiler_params=_cparams(("parallel",)),
        name="residual_post",
    )(*h_args, y, g_post.reshape(1, D_MODEL), mod, g_next.reshape(1, D_MODEL), mod, mod, *extra_args)


def _mm_kernel(a_ref, w_ref, b_ref, o_ref, *, w_is_transposed):
    dot = _dot_nt if w_is_transposed else _dot
    o_ref[...] = (dot(a_ref[...], w_ref[...].astype(BF16)) + b_ref[...]).astype(o_ref.dtype)


def _matmul(a, w, bias, out_dtype, tm, tn, n=None, w_is_transposed=False):
    m, k = a.shape
    if w_is_transposed:
        n = w.shape[-2] if n is None else n
        if w.ndim == 3:
            w_spec = pl.BlockSpec((None, tn, k), lambda i, j: (0, j, 0))
        else:
            w_spec = pl.BlockSpec((tn, k), lambda i, j: (j, 0))
    else:
        n = w.shape[-1] if n is None else n
        w_spec = pl.BlockSpec((k, tn), lambda i, j: (0, j))
    if bias is None:
        bias = jnp.zeros((n,), F32)
    return pl.pallas_call(
        functools.partial(_mm_kernel, w_is_transposed=w_is_transposed),
        out_shape=_SDS((m, n), out_dtype),
        grid=(m // tm, n // tn),
        in_specs=[pl.BlockSpec((tm, k), lambda i, j: (i, 0)),
                  w_spec,
                  pl.BlockSpec((1, tn), lambda i, j: (0, j))],
        out_specs=pl.BlockSpec((tm, tn), lambda i, j: (i, j)),
        compiler_params=_cparams(("parallel", "arbitrary"), VMEM_LIMIT),
        name="matmul",
    )(a, w, bias.reshape(1, n))


FF_TILE = 512
SWIGLU_VMEM_LIMIT = 60 * 1024 * 1024
OUT_TILE = 256
FF_STEPS = D_FF // FF_TILE
OUT_STEPS = D_MODEL // OUT_TILE


def _swiglu_step(s, x_ref, wg_ref, wu_ref, wd_ref, o_ref, h_ref, rows=None, guard=True):
    total = x_ref.shape[0]
    rows = total if rows is None else rows

    @pl.when(guard & (s < FF_STEPS))
    def _():
        x = x_ref[0:rows, :]
        a = _dot(x, wg_ref[...].astype(BF16))
        b = _dot(x, wu_ref[...].astype(BF16))
        h_ref[0:rows, pl.ds(pl.multiple_of(s * FF_TILE, FF_TILE), FF_TILE)] = (_silu(a) * b).astype(BF16)

    @pl.when(guard & (s >= FF_STEPS))
    def _():
        o_ref[0:rows, :] = _dot(h_ref[0:rows, :], wd_ref[...].astype(BF16)).astype(o_ref.dtype)
        if rows < total:
            o_ref[rows:total, :] = jnp.zeros((total - rows, o_ref.shape[1]), o_ref.dtype)


def _ffn_kernel(u_ref, wg_ref, wu_ref, wd_ref, o_ref, h_ref):
    _swiglu_step(pl.program_id(1), u_ref, wg_ref, wu_ref, wd_ref, o_ref, h_ref)


def _ffn(u, w_gate, w_up, w_down):
    m = u.shape[0]
    tm = MM_ROW_TILE

    def ff(i, s):
        return (0, jnp.minimum(s, FF_STEPS - 1))

    def out(s):
        return jnp.maximum(s - FF_STEPS, 0)

    return pl.pallas_call(
        _ffn_kernel,
        out_shape=_SDS((m, D_MODEL), BF16),
        grid=(m // tm, FF_STEPS + OUT_STEPS),
        in_specs=[pl.BlockSpec((tm, D_MODEL), lambda i, s: (i, 0), pipeline_mode=pl.Buffered(1)),
                  pl.BlockSpec((D_MODEL, FF_TILE), ff),
                  pl.BlockSpec((D_MODEL, FF_TILE), ff),
                  pl.BlockSpec((D_FF, OUT_TILE), lambda i, s: (0, out(s)))],
        out_specs=pl.BlockSpec((tm, OUT_TILE), lambda i, s: (i, out(s))),
        scratch_shapes=[pltpu.VMEM((tm, D_FF), BF16)],
        compiler_params=_cparams(("parallel", "arbitrary"), SWIGLU_VMEM_LIMIT),
        name="swiglu_ffn",
    )(u, w_gate, w_up, w_down)


CONV_TILE = 256
CONV_HALO = 16
_SEQ_STARTS = (0, SEQ // CONV_TILE, 2 * SEQ // CONV_TILE, 2 * SEQ // CONV_TILE + 1)
_SEQ_LASTS = (SEQ // CONV_TILE - 1, 2 * SEQ // CONV_TILE - 1, 2 * SEQ // CONV_TILE, 2 * SEQ // CONV_TILE + 1)


def _conv_kernel(prev_ref, cur_ref, next_ref, w_ref, o_ref, buf):
    i = pl.program_id(0)
    j = pl.program_id(1)
    tm = CONV_TILE
    first = functools.reduce(jnp.logical_or, [i == s for s in _SEQ_STARTS])
    last = functools.reduce(jnp.logical_or, [i == s for s in _SEQ_LASTS])
    hb = CONV_HALO
    buf[0:hb, :] = jnp.where(first, 0.0, prev_ref[...].astype(F32))
    buf[hb:hb + tm, :] = cur_ref[...].astype(F32)
    buf[hb + tm:2 * hb + tm, :] = jnp.where(last, 0.0, next_ref[...].astype(F32))
    pad = DN_CONV // 2
    acc = buf[hb - pad:hb - pad + tm, :] * w_ref[0:1, :]
    for d in range(1, DN_CONV):
        acc = acc + buf[hb - pad + d:hb - pad + d + tm, :] * w_ref[d:d + 1, :]
    x = _silu(acc)
    q_scale = jnp.where(j == 0, DN_HEAD_DIM ** -0.5, 1.0)
    for hh in range(DN_HEADS):
        xh = x[:, hh * DN_HEAD_DIM:(hh + 1) * DN_HEAD_DIM]
        inv = lax.rsqrt(jnp.sum(xh * xh, axis=-1, keepdims=True) + EPS) * q_scale
        o_ref[:, hh * DN_HEAD_DIM:(hh + 1) * DN_HEAD_DIM] = xh * jnp.where(j == 2, 1.0, inv)


def _dn_conv(p, conv_w):
    n = p.shape[0]
    tm = CONV_TILE
    hb = CONV_HALO
    per_tile = tm // hb
    n_halo = n // hb
    return pl.pallas_call(
        _conv_kernel,
        out_shape=_SDS((n, DN_QKV), F32),
        grid=(n // tm, 3),
        in_specs=[pl.BlockSpec((hb, DN_W), lambda i, j: (jnp.maximum(i * per_tile - 1, 0), j)),
                  pl.BlockSpec((tm, DN_W), lambda i, j: (i, j)),
                  pl.BlockSpec((hb, DN_W), lambda i, j: (jnp.minimum((i + 1) * per_tile, n_halo - 1), j)),
                  pl.BlockSpec((DN_CONV, DN_W), lambda i, j: (0, j))],
        out_specs=pl.BlockSpec((tm, DN_W), lambda i, j: (i, j)),
        scratch_shapes=[pltpu.VMEM((tm + 2 * hb, DN_W), F32)],
        compiler_params=_cparams(("parallel", "parallel")),
        name="dn_conv",
    )(p, p, p, conv_w)


def _gates_kernel(m_ref, a_ref, dt_ref, o_ref):
    x = pltpu.roll(m_ref[...], 64, 1)
    lane = lax.broadcasted_iota(I32, x.shape, 1)
    z = x + dt_ref[...]
    softplus = jnp.maximum(z, 0.0) + jnp.log1p(jnp.exp(-jnp.abs(z)))
    g = a_ref[...] * softplus
    o_ref[...] = jnp.where(lane < 2 * DN_HEADS, jax.nn.sigmoid(x), jnp.where(lane < 4 * DN_HEADS, g, 0.0))


def _dn_gates(p, a_log, dt_bias):
    n = p.shape[0]
    neg_a = jnp.zeros((1, 128), F32).at[0, 2 * DN_HEADS:4 * DN_HEADS].set(-jnp.exp(a_log.astype(F32)).reshape(-1))
    dtb = jnp.zeros((1, 128), F32).at[0, 2 * DN_HEADS:4 * DN_HEADS].set(dt_bias.astype(F32).reshape(-1))
    tm = ROW_TILE
    vec = pl.BlockSpec((1, 128), lambda i: (0, 0))
    return pl.pallas_call(
        _gates_kernel,
        out_shape=_SDS((n, 128), F32),
        grid=(n // tm,),
        in_specs=[pl.BlockSpec((tm, 128), lambda i: (i, R_MISC // 128)), vec, vec],
        out_specs=pl.BlockSpec((tm, 128), lambda i: (i, 0)),
        compiler_params=_cparams(("parallel",)),
        name="dn_gates",
    )(p, neg_a, dtb)


_CTX_CHUNKS = CTX_LEN // DN_CHUNK
_LAT_CHUNKS = SEQ // DN_CHUNK
_DN_STEPS = _CTX_CHUNKS + _LAT_CHUNKS


def _dn_fwd_block(b, t):
    return jnp.where(t < _CTX_CHUNKS, (2 * SEQ + b * CTX_LEN) // DN_CHUNK + t, b * _LAT_CHUNKS + (t - _CTX_CHUNKS))


def _dn_bwd_block(b, t):
    return jnp.where(t < _CTX_CHUNKS, (2 * SEQ + b * CTX_LEN) // DN_CHUNK + (_CTX_CHUNKS - 1 - t),
                     b * _LAT_CHUNKS + (_DN_STEPS - 1 - t))


DN_GROUP = 2
_GROUP_ROWS = DN_GROUP * DN_CHUNK
_GROUP_STATE = DN_GROUP * DN_HEAD_DIM


_DN_BASE = 8


def _block_diag(x):
    zero = jnp.zeros((DN_CHUNK, DN_HEAD_DIM), x.dtype)
    rows = []
    for hh in range(DN_GROUP):
        piece = x[hh * DN_CHUNK:(hh + 1) * DN_CHUNK]
        rows.append(jnp.concatenate([piece if j == hh else zero for j in range(DN_GROUP)], axis=1))
    return jnp.concatenate(rows, axis=0)


def _dn_groups(items, eye, base_mask):
    it = items
    n = range(len(it))
    gx = [jnp.broadcast_to(it[g]["gc"], (_GROUP_ROWS, _GROUP_ROWS)) for g in n]
    gamma = [jnp.where(it[g]["incl"], jnp.exp(jnp.where(it[g]["incl"], gx[g] - gx[g].T, 0.0)), 0.0) for g in n]
    kb = [it[g]["k"].astype(BF16) for g in n]
    kk = [_dot_nt(kb[g], kb[g]) for g in n]
    a = [jnp.where(it[g]["strict"], it[g]["beta"] * kk[g] * gamma[g], 0.0) for g in n]
    d = [jnp.where(base_mask, a[g], 0.0) for g in n]
    tinv = [eye - d[g] for g in n]
    pw = [d[g].astype(BF16) for g in n]
    for _ in range(int(math.log2(_DN_BASE)) - 1):
        pw = [_dot(pw[g], pw[g]).astype(BF16) for g in n]
        tinv = [tinv[g] + _dot(tinv[g].astype(BF16), pw[g]) for g in n]
    for lvl in range(len(it[0]["levels"])):
        tb = [tinv[g].astype(BF16) for g in n]
        mt = [_dot(jnp.where(it[g]["levels"][lvl], a[g], 0.0).astype(BF16), tb[g]).astype(BF16) for g in n]
        tinv = [tinv[g] - _dot(tb[g], mt[g]) for g in n]
    tb = [tinv[g].astype(BF16) for g in n]
    egc = [jnp.exp(it[g]["gc"]) for g in n]
    rhs = [jnp.concatenate([it[g]["v"] * it[g]["beta"], it[g]["k"] * (it[g]["beta"] * egc[g])], axis=1) for g in n]
    rhs_hi = [rhs[g].astype(BF16) for g in n]
    rhs_lo = [(rhs[g] - rhs_hi[g].astype(F32)).astype(BF16) for g in n]
    sol = [_dot(jnp.concatenate([tb[g], tb[g]], axis=1), jnp.concatenate([rhs_hi[g], rhs_lo[g]], axis=0)) for g in n]
    qk = [(_dot_nt(it[g]["q"].astype(BF16), kb[g]) * gamma[g]).astype(BF16) for g in n]
    s = [it[g]["s_ref"][...] for g in n]
    sb = [s[g].astype(BF16) for g in n]
    v_new = [sol[g][:, :DN_HEAD_DIM] - _dot(_block_diag(sol[g][:, DN_HEAD_DIM:].astype(BF16)), sb[g]) for g in n]
    vb = [v_new[g].astype(BF16) for g in n]
    o = [_dot(_block_diag((it[g]["q"] * egc[g]).astype(BF16)), sb[g]) + _dot(qk[g], vb[g]) for g in n]
    for g in n:
        k_dec = _block_diag((it[g]["k"] * jnp.exp(it[g]["g_last"] - it[g]["gc"])).astype(BF16))
        it[g]["s_ref"][...] = s[g] * jnp.exp(it[g]["g_last_state"]) + _dot_tn(k_dec, vb[g])
    return o


def _stack_cols(x, lanes, rows_each):
    pieces = [x[:, l:l + 1] for l in lanes]
    if rows_each is not None:
        pieces = [jnp.broadcast_to(pc, (rows_each, 1)) for pc in pieces]
    return jnp.concatenate(pieces, axis=0)


def _dn_kernel(*refs):
    n_in = BATCH * 2 * 4
    in_refs, (ofl_ref, ofc_ref, obl_ref, obc_ref), (s_ref, o_scr) = refs[:n_in], refs[n_in:n_in + 4], refs[n_in + 4:]
    t = pl.program_id(0)

    @pl.when(t == 0)
    def _():
        s_ref[...] = jnp.zeros_like(s_ref)

    c = DN_CHUNK
    ri = lax.broadcasted_iota(I32, (_GROUP_ROWS, _GROUP_ROWS), 0)
    ci = lax.broadcasted_iota(I32, (_GROUP_ROWS, _GROUP_ROWS), 1)
    same_head = (ri // c) == (ci // c)
    eye = (ri == ci).astype(F32)
    r1 = lax.broadcasted_iota(I32, (c, c), 0)
    c1 = lax.broadcasted_iota(I32, (c, c), 1)
    base_mask = (ri // _DN_BASE) == (ci // _DN_BASE)
    masks = []
    for d in range(2):
        incl = same_head & ((ri >= ci) if d == 0 else (ri <= ci))
        strict = same_head & ((ri > ci) if d == 0 else (ri < ci))
        levels = []
        size = _DN_BASE
        while size < c:
            lo_blk, hi_blk = ((ci, ri) if d == 0 else (ri, ci))
            levels.append(((ri // (2 * size)) == (ci // (2 * size)))
                          & ((lo_blk // size) % 2 == 0) & ((hi_blk // size) % 2 == 1))
            size *= 2
        masks.append((incl, strict, levels))
    items, where = [], []
    for b, d in [(b, d) for b in range(BATCH) for d in range(2)]:
        q_ref, k_ref, v_ref, g_ref = in_refs[(b * 2 + d) * 4:(b * 2 + d) * 4 + 4]
        incl, strict, levels = masks[d]
        gates = g_ref[...]
        tri = ((r1 >= c1) if d == 0 else (r1 <= c1)).astype(BF16)
        g_hi = gates.astype(BF16)
        g_r1 = gates - g_hi.astype(F32)
        g_mid = g_r1.astype(BF16)
        g_lo = (g_r1 - g_mid.astype(F32)).astype(BF16)
        gc = _dot(tri, g_hi) + _dot(tri, g_mid) + _dot(tri, g_lo)
        last_row = c - 1 if d == 0 else 0
        tot = gc[last_row:last_row + 1, :]
        for grp in range(DN_HEADS // DN_GROUP):
            heads = range(grp * DN_GROUP, (grp + 1) * DN_GROUP)
            lanes_b = [d * DN_HEADS + hh for hh in heads]
            lanes_g = [2 * DN_HEADS + lb for lb in lanes_b]

            def stack(ref):
                return jnp.concatenate([ref[:, hh * DN_HEAD_DIM:(hh + 1) * DN_HEAD_DIM] for hh in heads], axis=0)

            items.append(dict(q=stack(q_ref), k=stack(k_ref), v=stack(v_ref),
                              beta=_stack_cols(gates, lanes_b, None), gc=_stack_cols(gc, lanes_g, None),
                              g_last=_stack_cols(tot, lanes_g, c), g_last_state=_stack_cols(tot, lanes_g, DN_HEAD_DIM),
                              incl=incl, strict=strict, levels=levels, s_ref=s_ref.at[b, d, grp]))
            where.append((d, b, heads))
    outs = _dn_groups(items, eye, base_mask)
    for o, (d, b, heads) in zip(outs, where):
        for j, hh in enumerate(heads):
            o_scr[d, b, :, hh * DN_HEAD_DIM:(hh + 1) * DN_HEAD_DIM] = o[j * c:(j + 1) * c]

    @pl.when(t < _CTX_CHUNKS)
    def _():
        ofc_ref[...] = o_scr[0]
        obc_ref[...] = o_scr[1]

    @pl.when(t >= _CTX_CHUNKS)
    def _():
        ofl_ref[...] = o_scr[0]
        obl_ref[...] = o_scr[1]


def _deltanet(qkv, gates):
    c = DN_CHUNK
    ins, args = [], []
    for b in range(BATCH):
        for fn in (_dn_fwd_block, _dn_bwd_block):
            for width, col, arr in ((DN_W, 0, qkv), (DN_W, 1, qkv), (DN_W, 2, qkv), (128, 0, gates)):
                ins.append(pl.BlockSpec((c, width), functools.partial(lambda t, b, fn, col: (fn(b, t), col),
                                                                      b=b, fn=fn, col=col)))
                args.append(arr)
    last = _DN_STEPS - 1
    lat = _SDS((BATCH, SEQ, DN_W), F32)
    ctx = _SDS((BATCH, CTX_LEN, DN_W), F32)
    blk = (BATCH, c, DN_W)
    outs = pl.pallas_call(
        _dn_kernel,
        out_shape=(lat, ctx, lat, ctx),
        grid=(_DN_STEPS,),
        in_specs=ins,
        out_specs=(pl.BlockSpec(blk, lambda t: (0, jnp.maximum(t - _CTX_CHUNKS, 0), 0)),
                   pl.BlockSpec(blk, lambda t: (0, jnp.minimum(t, _CTX_CHUNKS - 1), 0)),
                   pl.BlockSpec(blk, lambda t: (0, jnp.minimum(last - t, _LAT_CHUNKS - 1), 0)),
                   pl.BlockSpec(blk, lambda t: (0, jnp.maximum(_CTX_CHUNKS - 1 - t, 0), 0))),
        scratch_shapes=[pltpu.VMEM((BATCH, 2, DN_HEADS // DN_GROUP, _GROUP_STATE, DN_HEAD_DIM), F32),
                        pltpu.VMEM((2, BATCH, c, DN_W), F32)],
        compiler_params=_cparams(("arbitrary",)),
        name="deltanet",
    )(*args)
    ofl, ofc, obl, obc = outs
    return ((ofl.reshape(N_LAT, DN_W), ofc.reshape(N_CTX, DN_W)),
            (obl.reshape(N_LAT, DN_W), obc.reshape(N_CTX, DN_W)))


def _rope_tables():
    t = jnp.arange(SEQ)
    row = (t // GRID_W).astype(F32)
    col = (t % GRID_W).astype(F32)
    n_freq = MLA_ROPE // 4
    inv_freq = ROPE_THETA ** (-jnp.arange(n_freq, dtype=F32) / n_freq)
    ang = jnp.concatenate([row[:, None] * inv_freq, col[:, None] * inv_freq], axis=-1)
    cos, sin = jnp.cos(ang), jnp.sin(ang)
    cos64 = jnp.concatenate([cos, cos], axis=-1)
    sin64 = jnp.concatenate([-sin, sin], axis=-1)

    def assemble(lat_cos, lat_sin):
        c = jnp.concatenate([jnp.tile(lat_cos, (BATCH, 1)), jnp.ones((N_CTX, 128), F32)], axis=0)
        s = jnp.concatenate([jnp.tile(lat_sin, (BATCH, 1)), jnp.zeros((N_CTX, 128), F32)], axis=0)
        return c, s

    cos_l = jnp.concatenate([cos64, jnp.ones((SEQ, 64), F32)], axis=-1)
    sin_l = jnp.concatenate([sin64, jnp.zeros((SEQ, 64), F32)], axis=-1)
    one_head = assemble(cos_l, sin_l)
    two_heads = assemble(jnp.tile(cos64, (1, 2)), jnp.tile(sin64, (1, 2)))
    return one_head, two_heads


def _rope128(x, cosv, sinv):
    lane = lax.broadcasted_iota(I32, x.shape, 1)
    swapped = jnp.where((lane % 64) < 32, pltpu.roll(x, 96, 1), pltpu.roll(x, 32, 1))
    return x * cosv + swapped * sinv


MLA_QK = 256
MLA_VX = 256


def _mla_proj_kernel(ckv_ref, cq_ref, misc_ref, gq_ref, gkv_ref, wq_ref, wkv_ref, cos_ref, sin_ref,
                     q_ref, k_ref, v_ref):
    cq = _rms(cq_ref[...].astype(F32), gq_ref[...]).astype(BF16)
    ckv = _rms(ckv_ref[...].astype(F32), gkv_ref[...]).astype(BF16)
    qf = _dot(cq, wq_ref[...])
    kvf = _dot(ckv, wkv_ref[...])
    cosv, sinv = cos_ref[...], sin_ref[...]
    lane = lax.broadcasted_iota(I32, cosv.shape, 1)
    k_rope = _rope128(jnp.where(lane < MLA_ROPE, misc_ref[...], 0.0), cosv, sinv).astype(BF16)
    scale = (MLA_NOPE + MLA_ROPE) ** -0.5
    for hh in range(MLA_HEADS):
        lo, mid, hi = hh * MLA_QK, hh * MLA_QK + 128, (hh + 1) * MLA_QK
        q_ref[:, lo:mid] = (qf[:, lo:mid] * scale).astype(BF16)
        q_ref[:, mid:hi] = (_rope128(qf[:, mid:hi], cosv, sinv) * scale).astype(BF16)
        k_ref[:, lo:mid] = kvf[:, lo:mid].astype(BF16)
        k_ref[:, mid:hi] = k_rope
        v_ref[:, lo:mid] = kvf[:, mid:hi].astype(BF16)
        v_ref[:, mid:hi] = jnp.ones((kvf.shape[0], MLA_VX - MLA_V), BF16)


def _mla_proj(p_rest, q_norm_g, kv_norm_g, w_q, w_kv, cosv, sinv):
    n = p_rest.shape[0]
    tm = ROW_TILE
    wide = MLA_HEADS * MLA_QK

    def rows(width, col):
        return pl.BlockSpec((tm, width), lambda i: (i, col))

    def whole(shape):
        return pl.BlockSpec(shape, lambda i: (0, 0))

    return pl.pallas_call(
        _mla_proj_kernel,
        out_shape=(_SDS((n, wide), BF16), _SDS((n, wide), BF16), _SDS((n, MLA_HEADS * MLA_VX), BF16)),
        grid=(n // tm,),
        in_specs=[rows(MLA_KV_RANK, R_CKV // MLA_KV_RANK), rows(MLA_Q_RANK, R_CQ // MLA_Q_RANK),
                  rows(128, R_MISC // 128), whole((1, MLA_Q_RANK)), whole((1, MLA_KV_RANK)),
                  whole((MLA_Q_RANK, wide)), whole((MLA_KV_RANK, wide)), rows(128, 0), rows(128, 0)],
        out_specs=(rows(wide, 0), rows(wide, 0), rows(MLA_HEADS * MLA_VX, 0)),
        compiler_params=_cparams(("parallel",), VMEM_LIMIT),
        name="mla_proj",
    )(p_rest, p_rest, p_rest, q_norm_g.reshape(1, -1), kv_norm_g.reshape(1, -1), w_q, w_kv, cosv, sinv)


MLA_KEY_CHUNK = 2048


def _flash_kernel(*refs, chunks):
    n_seg = (len(refs) - 2) // 2
    q_ref, o_ref = refs[0], refs[-1]
    k_refs, v_refs = refs[1:1 + n_seg], refs[1 + n_seg:1 + 2 * n_seg]
    q = q_ref[...]

    def scores(c):
        seg, off, size = c
        return _dot_nt(q, k_refs[seg][off:off + size, :])

    m = jnp.full((q.shape[0], 1), -jnp.inf, F32)
    acc = jnp.zeros((q.shape[0], MLA_VX), F32)
    s_next = scores(chunks[0])
    for j, (seg, off, size) in enumerate(chunks):
        s = s_next
        if j + 1 < len(chunks):
            s_next = scores(chunks[j + 1])
        m_new = jnp.maximum(m, jnp.max(s, axis=-1, keepdims=True))
        p = jnp.exp(s - m_new).astype(BF16)
        acc = jnp.exp(m - m_new) * acc + _dot(p, v_refs[seg][off:off + size, :])
        m = m_new
    o_ref[...] = (acc[:, :MLA_V] / acc[:, MLA_V:]).astype(o_ref.dtype)


def _mla_attention(q, k, v):
    tq = 1024
    nq = SEQ // tq
    ctx_blk = 2 * SEQ // CTX_LEN
    lat_chunks = ((0, 0, CTX_LEN),) + tuple((1, off, MLA_KEY_CHUNK) for off in range(0, SEQ, MLA_KEY_CHUNK))
    lat = pl.pallas_call(
        functools.partial(_flash_kernel, chunks=lat_chunks),
        out_shape=_SDS((N_LAT, MLA_HEADS * MLA_V), BF16),
        grid=(BATCH, MLA_HEADS, nq),
        in_specs=[pl.BlockSpec((tq, MLA_QK), lambda b, h, i: (b * nq + i, h)),
                  pl.BlockSpec((CTX_LEN, MLA_QK), lambda b, h, i: (ctx_blk + b, h)),
                  pl.BlockSpec((SEQ, MLA_QK), lambda b, h, i: (b, h)),
                  pl.BlockSpec((CTX_LEN, MLA_VX), lambda b, h, i: (ctx_blk + b, h)),
                  pl.BlockSpec((SEQ, MLA_VX), lambda b, h, i: (b, h))],
        out_specs=pl.BlockSpec((tq, MLA_V), lambda b, h, i: (b * nq + i, h)),
        compiler_params=_cparams(("parallel", "parallel", "arbitrary"), VMEM_LIMIT),
        name="mla_attn_latent",
    )(q, k, k, v, v)
    ctx = pl.pallas_call(
        functools.partial(_flash_kernel, chunks=((0, 0, CTX_LEN),)),
        out_shape=_SDS((N_CTX, MLA_HEADS * MLA_V), BF16),
        grid=(BATCH, MLA_HEADS),
        in_specs=[pl.BlockSpec((CTX_LEN, MLA_QK), lambda b, h: (ctx_blk + b, h)),
                  pl.BlockSpec((CTX_LEN, MLA_QK), lambda b, h: (ctx_blk + b, h)),
                  pl.BlockSpec((CTX_LEN, MLA_VX), lambda b, h: (ctx_blk + b, h))],
        out_specs=pl.BlockSpec((CTX_LEN, MLA_V), lambda b, h: (b, h)),
        compiler_params=_cparams(("parallel", "parallel")),
        name="mla_attn_context",
    )(q, k, v)
    return lat, ctx


def _ab_mix_kernel(ofl_ref, ofc_ref, obl_ref, obc_ref, ml_ref, mc_ref, z_ref, g_ref, a_ref):
    for hh in range(DN_HEADS):
        sl = slice(hh * DN_HEAD_DIM, (hh + 1) * DN_HEAD_DIM)
        o = _split_read(ofl_ref, ofc_ref, sl) + _split_read(obl_ref, obc_ref, sl)
        a_ref[:, sl] = (_rms(o, g_ref[...]) * _silu(z_ref[:, sl].astype(F32))).astype(BF16)
    a_ref[:, DN_W:] = _split_read(ml_ref, mc_ref)


def _ab_mix(o_f, o_b, p, dn_norm_g, mla_o):
    tm = ROW_TILE
    half = DN_W
    return pl.pallas_call(
        _ab_mix_kernel,
        out_shape=_SDS((N_TOK, 2 * half), BF16),
        grid=(N_TOK // tm,),
        in_specs=_split_specs(half) + _split_specs(half) + _split_specs(half) + [
            pl.BlockSpec((tm, half), lambda i: (i, P_Z // half)),
            pl.BlockSpec((1, DN_HEAD_DIM), lambda i: (0, 0))],
        out_specs=pl.BlockSpec((tm, 2 * half), lambda i: (i, 0)),
        compiler_params=_cparams(("parallel",)),
        name="ab_mix",
    )(*o_f, *o_b, *mla_o, p, dn_norm_g.reshape(1, -1))


def _mm_rope_kernel(a_ref, w_ref, b_ref, cos_ref, sin_ref, o_ref, *, tn, rope_cols):
    y = _dot(a_ref[...], w_ref[...].astype(BF16)) + b_ref[...]
    col0 = pl.program_id(1) * tn
    cosv, sinv = cos_ref[...], sin_ref[...]
    for s in range(tn // 128):
        ys = y[:, s * 128:(s + 1) * 128]
        roped = _rope128(ys, cosv, sinv)
        o_ref[:, s * 128:(s + 1) * 128] = jnp.where(col0 + s * 128 < rope_cols, roped, ys).astype(o_ref.dtype)


def _gqa_qkv(u, w, bias, cosv, sinv):
    m, k = u.shape
    n = w.shape[1]
    tm, tn = MM_ROW_TILE, 1280
    return pl.pallas_call(
        functools.partial(_mm_rope_kernel, tn=tn, rope_cols=GQA_Q + GQA_KV),
        out_shape=_SDS((m, n), BF16),
        grid=(m // tm, n // tn),
        in_specs=[pl.BlockSpec((tm, k), lambda i, j: (i, 0)),
                  pl.BlockSpec((k, tn), lambda i, j: (0, j)),
                  pl.BlockSpec((1, tn), lambda i, j: (0, j)),
                  pl.BlockSpec((tm, 128), lambda i, j: (i, 0)),
                  pl.BlockSpec((tm, 128), lambda i, j: (i, 0))],
        out_specs=pl.BlockSpec((tm, tn), lambda i, j: (i, j)),
        compiler_params=_cparams(("parallel", "arbitrary"), VMEM_LIMIT),
        name="gqa_qkv_rope",
    )(u, w, bias.reshape(1, n), cosv, sinv)


def _gqa_kernel(sink_ref, q_ref, kp_ref, kc_ref, kn_ref, vp_ref, vc_ref, vn_ref, kx_ref, vx_ref, o_ref):
    i = pl.program_id(1)
    nb = SEQ // Q_BLOCK
    n_keys = CTX_LEN + 3 * Q_BLOCK
    groups = GQA_HEADS // GQA_KV_HEADS
    slabs = groups // 2
    dh = GQA_HEAD_DIM
    qi = lax.broadcasted_iota(I32, (Q_BLOCK, n_keys), 0)
    kj = lax.broadcasted_iota(I32, (Q_BLOCK, n_keys), 1) - CTX_LEN
    rel = kj - Q_BLOCK - qi
    valid = (kj < 0) | ((jnp.abs(rel) <= WINDOW) & ((kj >= Q_BLOCK) | (i > 0)) & ((kj < 2 * Q_BLOCK) | (i < nb - 1)))
    bias = jnp.where(valid, 0.0, -jnp.inf).astype(F32)
    bias = jnp.concatenate([bias] * slabs, axis=0)
    k_all = jnp.concatenate([kx_ref[...], kp_ref[...], kc_ref[...], kn_ref[...]], axis=0)
    v_all = jnp.concatenate([vx_ref[...], vp_ref[...], vc_ref[...], vn_ref[...]], axis=0)
    low = lax.broadcasted_iota(I32, (n_keys, 128), 1) < dh

    def block_diag(x_all, kvh):
        pair = x_all[:, (kvh // 2) * 128:(kvh // 2 + 1) * 128].astype(F32)
        other = pltpu.roll(pair, dh, 1)
        first, second = (pair, other) if kvh % 2 == 0 else (other, pair)
        return jnp.concatenate([jnp.where(low, first, 0.0), jnp.where(low, 0.0, second)], axis=0).astype(BF16)

    def scores(kvh):
        q = jnp.concatenate([q_ref[:, (kvh * slabs + r) * 128:(kvh * slabs + r + 1) * 128] for r in range(slabs)], axis=0)
        return _dot_nt(q * (dh ** -0.5), block_diag(k_all, kvh))

    row_lo = lax.broadcasted_iota(I32, (2 * n_keys, 128), 0) < n_keys
    lane_lo = lax.broadcasted_iota(I32, (2 * n_keys, 128), 1) < dh
    ones_cols = jnp.where(row_lo == lane_lo, 1.0, 0.0).astype(BF16)

    def softmax(kvh, s):
        ps, sink_terms = [], []
        for half in range(2):
            sh = s[:, half * n_keys:(half + 1) * n_keys] + bias
            sink = jnp.concatenate([jnp.full((Q_BLOCK, 1), sink_ref[(kvh * slabs + r) * 2 + half], F32)
                                    for r in range(slabs)], axis=0)
            m = jnp.maximum(jnp.max(sh, axis=-1, keepdims=True), sink)
            ps.append(jnp.exp(sh - m).astype(BF16))
            sink_terms.append(jnp.exp(sink - m))
        return jnp.concatenate(ps, axis=1), sink_terms

    def finish(kvh, p, sink_terms):
        o = _dot(p, jnp.concatenate([block_diag(v_all, kvh), ones_cols], axis=1))
        num, den = o[:, :128], o[:, 128:]
        lane_low = lax.broadcasted_iota(I32, num.shape, 1) < dh
        o = num / (den + jnp.where(lane_low, sink_terms[0], sink_terms[1]))
        for r in range(slabs):
            o_ref[:, (kvh * slabs + r) * 128:(kvh * slabs + r + 1) * 128] = o[r * Q_BLOCK:(r + 1) * Q_BLOCK].astype(o_ref.dtype)

    s_next = scores(0)
    for kvh in range(GQA_KV_HEADS):
        s_cur = s_next
        if kvh + 1 < GQA_KV_HEADS:
            s_next = scores(kvh + 1)
        p, dens = softmax(kvh, s_cur)
        finish(kvh, p, dens)


def _gqa_attention(qkv, sink):
    nb = SEQ // Q_BLOCK
    kcol, vcol = GQA_Q // GQA_KV, GQA_Q // GQA_KV + 1
    ctx_blk = 2 * SEQ // CTX_LEN

    def win(col, shift):
        return pl.BlockSpec((Q_BLOCK, GQA_KV), lambda b, i: (b * nb + jnp.clip(i + shift, 0, nb - 1), col))

    return pl.pallas_call(
        _gqa_kernel,
        out_shape=_SDS((N_LAT, GQA_Q), BF16),
        grid=(BATCH, nb),
        in_specs=[pl.BlockSpec(memory_space=pltpu.SMEM),
                  pl.BlockSpec((Q_BLOCK, GQA_Q), lambda b, i: (b * nb + i, 0)),
                  win(kcol, -1), win(kcol, 0), win(kcol, 1), win(vcol, -1), win(vcol, 0), win(vcol, 1),
                  pl.BlockSpec((CTX_LEN, GQA_KV), lambda b, i: (ctx_blk + b, kcol)),
                  pl.BlockSpec((CTX_LEN, GQA_KV), lambda b, i: (ctx_blk + b, vcol))],
        out_specs=pl.BlockSpec((Q_BLOCK, GQA_Q), lambda b, i: (b * nb + i, 0)),
        compiler_params=_cparams(("parallel", "arbitrary")),
        name="gqa_window_attn",
    )(sink.astype(F32), qkv, qkv, qkv, qkv, qkv, qkv, qkv, qkv, qkv)


GATHER_TILE = 256
COMBINE_TILE = 512


def _row_copy(src_hbm, row, dst, r, sem):
    return pltpu.make_async_copy(src_hbm.at[pl.ds(row, 1), :], dst.at[pl.ds(r, 1), :], sem)


def _wait_slot(src_hbm, slot_buf, sem):
    def drain(g, carry):
        pltpu.make_async_copy(src_hbm.at[pl.ds(0, 8), :], slot_buf.at[g], sem).wait()
        return carry

    lax.fori_loop(0, slot_buf.shape[0], drain, 0)


def _gather_kernel(tok_ref, tv_ref, u_hbm, valid_ref, o_ref, buf, sem):
    i = pl.program_id(0)
    n = pl.num_programs(0)

    def start_tile(t, slot):
        @pl.when(tv_ref[t] > 0)
        def _():
            def issue(r8, carry):
                for j in range(8):
                    row = tok_ref[t * GATHER_TILE + r8 * 8 + j]
                    _row_copy(u_hbm, row, buf.at[slot, r8], j, sem.at[slot]).start(priority=j % 2)
                return carry

            lax.fori_loop(0, GATHER_TILE // 8, issue, 0)

    @pl.when(i == 0)
    def _():
        start_tile(0, 0)

    @pl.when(i + 1 < n)
    def _():
        start_tile(i + 1, (i + 1) % 2)

    slot = i % 2

    @pl.when(tv_ref[i] > 0)
    def _():
        _wait_slot(u_hbm, buf.at[slot], sem.at[slot])
        lo, hi = _unpack_bf16_halves(buf[slot].reshape(GATHER_TILE, D_MODEL // 2))
        keep = valid_ref[...] > 0.0
        half = D_MODEL // 2
        o_ref[:, :half] = jnp.where(keep, lo, jnp.zeros_like(lo))
        o_ref[:, half:] = jnp.where(keep, hi, jnp.zeros_like(hi))

    @pl.when(tv_ref[i] == 0)
    def _():
        o_ref[...] = jnp.zeros_like(o_ref)


def _moe_gather(slot_tok, slot_valid, u):
    n_slots = slot_tok.shape[0]
    n_tiles = n_slots // GATHER_TILE
    tile_valid = (jnp.max(slot_valid.reshape(n_tiles, GATHER_TILE), axis=1) > 0.0).astype(I32)
    return pl.pallas_call(
        _gather_kernel,
        out_shape=_SDS((n_slots, D_MODEL), BF16),
        grid_spec=pltpu.PrefetchScalarGridSpec(
            num_scalar_prefetch=2,
            grid=(n_tiles,),
            in_specs=[pl.BlockSpec(memory_space=pl.ANY),
                      pl.BlockSpec((GATHER_TILE, 1), lambda i, tok, tv: (i, 0))],
            out_specs=pl.BlockSpec((GATHER_TILE, D_MODEL), lambda i, tok, tv: (i, 0)),
            scratch_shapes=[pltpu.VMEM((2, GATHER_TILE // 8, 8, D_MODEL // 2), U32), pltpu.SemaphoreType.DMA((2,))]),
        compiler_params=_cparams(("arbitrary",)),
        name="moe_gather",
    )(slot_tok, tile_valid, u, slot_valid.reshape(n_slots, 1))


def _moe_ffn_kernel(be_ref, bv_ref, x_ref, wg_ref, wu_ref, wd_ref, o_ref, h_ref):
    i, s = pl.program_id(0), pl.program_id(1)
    n_sub = bv_ref[i]
    for k in range(1, MOE_BLOCK // MOE_SUB + 1):
        _swiglu_step(s, x_ref, wg_ref.at[0], wu_ref.at[0], wd_ref.at[0], o_ref, h_ref,
                     rows=k * MOE_SUB, guard=n_sub == k)

    @pl.when((n_sub == 0) & (s >= FF_STEPS))
    def _():
        o_ref[...] = jnp.zeros_like(o_ref)


def _moe_ffn(block_expert, block_valid, xs, w_gate, w_up, w_down):
    n_slots = xs.shape[0]
    tm = MOE_BLOCK

    def ff(i, s, be, bv):
        return (be[i], 0, jnp.where(bv[i] > 0, jnp.minimum(s, FF_STEPS - 1), FF_STEPS - 1))

    def down(i, s, be, bv):
        return (be[i], 0, jnp.where(bv[i] > 0, jnp.maximum(s - FF_STEPS, 0), OUT_STEPS - 1))

    return pl.pallas_call(
        _moe_ffn_kernel,
        out_shape=_SDS((n_slots, D_MODEL), F32),
        grid_spec=pltpu.PrefetchScalarGridSpec(
            num_scalar_prefetch=2,
            grid=(n_slots // tm, FF_STEPS + OUT_STEPS),
            in_specs=[pl.BlockSpec((tm, D_MODEL), lambda i, s, be, bv: (i, 0), pipeline_mode=pl.Buffered(1)),
                      pl.BlockSpec((1, D_MODEL, FF_TILE), ff),
                      pl.BlockSpec((1, D_MODEL, FF_TILE), ff),
                      pl.BlockSpec((1, D_FF, OUT_TILE), down)],
            out_specs=pl.BlockSpec((tm, OUT_TILE), lambda i, s, be, bv: (i, jnp.maximum(s - FF_STEPS, 0))),
            scratch_shapes=[pltpu.VMEM((tm, D_FF), BF16)]),
        compiler_params=_cparams(("arbitrary", "arbitrary"), SWIGLU_VMEM_LIMIT),
        name="moe_grouped_swiglu",
    )(block_expert, block_valid, xs, w_gate, w_up, w_down)


def _combine_kernel(pos_ref, y_hbm, wt_ref, h_ref, gp_ref, gate_ref, o_ref, buf_a, buf_b, sem):
    i = pl.program_id(0)
    n = pl.num_programs(0)
    base = i * COMBINE_TILE

    def start_tile(t, slot):
        def issue(r8, carry):
            for j in range(8):
                a = 2 * (t * COMBINE_TILE + r8 * 8 + j)
                _row_copy(y_hbm, pos_ref[a], buf_a.at[slot, r8], j, sem.at[slot]).start(priority=0)
                _row_copy(y_hbm, pos_ref[a + 1], buf_b.at[slot, r8], j, sem.at[slot]).start(priority=1)
            return carry

        lax.fori_loop(0, COMBINE_TILE // 8, issue, 0)

    @pl.when(i == 0)
    def _():
        start_tile(0, 0)

    @pl.when(i + 1 < n)
    def _():
        start_tile(i + 1, (i + 1) % 2)

    slot = i % 2

    _wait_slot(y_hbm, buf_a.at[slot], sem.at[slot])
    _wait_slot(y_hbm, buf_b.at[slot], sem.at[slot])
    wt = wt_ref[...]
    rows = (COMBINE_TILE, D_MODEL)
    y = wt[:, 0:1] * buf_a[slot].reshape(rows) + wt[:, 1:2] * buf_b[slot].reshape(rows)
    grp = _group_of_row(base)
    o_ref[...] = h_ref[...] + _mod_row(gate_ref, grp) * _rms(y, gp_ref[...])


def _moe_combine(pos, ys, wts, h, g_post, mod, gate_lk):
    n = wts.shape[0]
    tm = COMBINE_TILE
    return pl.pallas_call(
        _combine_kernel,
        out_shape=_SDS((n, D_MODEL), F32),
        grid_spec=pltpu.PrefetchScalarGridSpec(
            num_scalar_prefetch=1,
            grid=(n // tm,),
            in_specs=[pl.BlockSpec(memory_space=pl.ANY),
                      pl.BlockSpec((tm, 128), lambda i, pos: (i, 0)),
                      pl.BlockSpec((tm, D_MODEL), lambda i, pos: (i, 0)),
                      pl.BlockSpec((1, D_MODEL), lambda i, pos: (0, 0)),
                      pl.BlockSpec((1, 8, D_MODEL), lambda i, pos: (gate_lk[0], 0, gate_lk[1]))],
            out_specs=pl.BlockSpec((tm, D_MODEL), lambda i, pos: (i, 0)),
            scratch_shapes=[pltpu.VMEM((2, tm // 8, 8, D_MODEL), F32), pltpu.VMEM((2, tm // 8, 8, D_MODEL), F32),
                            pltpu.SemaphoreType.DMA((2,))]),
        compiler_params=_cparams(("arbitrary",)),
        name="moe_combine",
    )(pos, ys, wts, h, g_post.reshape(1, D_MODEL), mod)


def _moe_routing(idx):
    flat_e = idx[:, :TOP_K].reshape(-1)
    n_assign = flat_e.shape[0]
    onehot = (flat_e[:, None] == jnp.arange(N_EXPERTS, dtype=I32)[None, :]).astype(I32)
    csum = jnp.cumsum(onehot, axis=0)
    counts = csum[-1]
    padded = (counts + MOE_BLOCK - 1) // MOE_BLOCK * MOE_BLOCK
    pad_end = jnp.cumsum(padded)
    pad_start = pad_end - padded
    pos = jnp.sum(onehot * (pad_start[None, :] + csum - 1), axis=1).astype(I32)
    slot_tok = jnp.zeros((MOE_SLOTS,), I32).at[pos].set(jnp.arange(n_assign, dtype=I32) // TOP_K)
    slot = jnp.arange(MOE_SLOTS, dtype=I32)
    used_end = pad_start + counts
    slot_valid = jnp.any((slot[:, None] >= pad_start[None, :]) & (slot[:, None] < used_end[None, :]), axis=1)
    blk_start = jnp.arange(MOE_SLOTS // MOE_BLOCK, dtype=I32) * MOE_BLOCK
    last_start = jnp.maximum(pad_end[-1] - MOE_BLOCK, 0)
    blk_e = jnp.searchsorted(pad_end, jnp.minimum(blk_start, last_start), side='right').astype(I32)
    blk_e = jnp.minimum(blk_e, N_EXPERTS - 1)
    rows_used = jnp.clip(used_end[blk_e] - blk_start, 0, MOE_BLOCK)
    blk_sub = ((rows_used + MOE_SUB - 1) // MOE_SUB).astype(I32)
    return pos, slot_tok, slot_valid.astype(F32), blk_e, blk_sub


def _rest_of_w_in_t(w_in_t):
    dn_in = P_MAIN + 4 * DN_HEADS
    ba = w_in_t[P_MAIN:dn_in]
    cq = w_in_t[dn_in:dn_in + MLA_Q_RANK]
    ckv = w_in_t[dn_in + MLA_Q_RANK:dn_in + MLA_Q_RANK + MLA_KV_RANK]
    k_rope = w_in_t[dn_in + MLA_Q_RANK + MLA_KV_RANK:]
    pad = jnp.zeros((R_CQ - R_MISC - MLA_ROPE - 4 * DN_HEADS, D_MODEL), w_in_t.dtype)
    return jnp.concatenate([ckv, k_rope, ba, pad, cq], axis=0)


def _rearrange_w_qb(w_qb):
    w = w_qb.reshape(MLA_Q_RANK, MLA_HEADS, MLA_NOPE + MLA_ROPE)
    w = jnp.pad(w, ((0, 0), (0, 0), (0, MLA_QK - MLA_NOPE - MLA_ROPE)))
    return w.reshape(MLA_Q_RANK, MLA_HEADS * MLA_QK).astype(BF16)


def kernel(x, c, ctx, c_ctx, ada_w, ada_b, norm_g, ab_w_in, dn_conv_w, dn_a_log, dn_dt_bias, dn_norm_g, mla_q_norm_g, mla_w_qb, mla_kv_norm_g, mla_w_kvb, ab_w_out, ffn_w_gate, ffn_w_up, ffn_w_down, gqa_w_qkv, gqa_b_qkv, gqa_sink, gqa_w_out, gqa_b_out, moe_w_router, moe_w_gate, moe_w_up, moe_w_down):
    assert x.shape == (BATCH, SEQ, D_MODEL) and ctx.shape == (BATCH, CTX_LEN, D_MODEL) and ada_w.shape[0] == 2
    h = (x.reshape(N_LAT, D_MODEL), ctx.reshape(N_CTX, D_MODEL))
    cvec = jnp.zeros((8, D_MODEL), F32).at[:BATCH].set(c).at[BATCH].set(c_ctx)
    mod = _ada(cvec, ada_w, ada_b)
    (cos1, sin1), (cos2, sin2) = _rope_tables()

    u = _modulate_in(*h, norm_g[0, 0], mod, 0, 0, 1)
    w_in_t = jnp.swapaxes(ab_w_in, 1, 2)
    p = _matmul(u, w_in_t, None, BF16, MM_ROW_TILE, 1024, n=P_MAIN, w_is_transposed=True)
    p_rest = _matmul(u, _rest_of_w_in_t(w_in_t[0]), None, F32, MM_ROW_TILE, R_WIDTH // 2, w_is_transposed=True)
    o_f, o_b = _deltanet(_dn_conv(p, dn_conv_w[0]), _dn_gates(p_rest, dn_a_log[0], dn_dt_bias[0]))
    q, k, v = _mla_proj(p_rest, mla_q_norm_g[0], mla_kv_norm_g[0], _rearrange_w_qb(mla_w_qb[0]),
                        mla_w_kvb[0].astype(BF16), cos1, sin1)
    a = _ab_mix(o_f, o_b, p, dn_norm_g[0], _mla_attention(q, k, v))
    y = _matmul(a, ab_w_out[0].astype(BF16), None, BF16, MM_ROW_TILE, 1024)
    h, u = _post(h, y, norm_g[0, 1], norm_g[0, 2], mod, (0, 2), (0, 3), (0, 4), BF16)
    y = _ffn(u, ffn_w_gate[0], ffn_w_up[0], ffn_w_down[0])
    h, u = _post(h, y, norm_g[0, 3], norm_g[1, 0], mod, (0, 5), (1, 0), (1, 1), BF16)

    qkv = _gqa_qkv(u, gqa_w_qkv[0].astype(BF16), gqa_b_qkv[0], cos2, sin2)
    o = _gqa_attention(qkv, gqa_sink[0])
    y = _matmul(o, gqa_w_out[0].astype(BF16), gqa_b_out[0], BF16, 1024, 1024)
    h, u, idx, wts = _post(h, y, norm_g[1, 1], norm_g[1, 2], mod, (1, 2), (1, 3), (1, 4), F32, moe_w_router[0])
    pos, slot_tok, slot_valid, blk_e, blk_sub = _moe_routing(idx)
    xs = _moe_gather(slot_tok, slot_valid, u)
    ys = _moe_ffn(blk_e, blk_sub, xs, moe_w_gate[0], moe_w_up[0], moe_w_down[0])
    out = _moe_combine(pos, ys, wts, h, norm_g[1, 3], mod, (1, 5))
    return out.reshape(BATCH, SEQ, D_MODEL)
```
